```python
import math
import jax
import jax.numpy as jnp
from jax import lax
import numpy as np

D_MODEL = 1024
BATCH = 4
SEQ = 4096
DEPTH = 1
DEC_BATCH = 8
DEC_SEQ = 16
PAST_LEN = 2048

CHUNK = 64
N_PREV_CHUNKS = 8
HEAD_DIM = 64
A_HEADS = 8
A_WIDTH = A_HEADS * HEAD_DIM
A_MAX_REL = 64
B_HEADS = 8
B_KV_HEADS = 2
B_GROUP = B_HEADS // B_KV_HEADS
B_WIDTH = B_HEADS * HEAD_DIM
B_KV_WIDTH = B_KV_HEADS * HEAD_DIM
MIX_WIDTH = A_WIDTH + B_WIDTH
IDX_HEADS = 8
IDX_DIM = 64
TOPK_MAX = 256
N_BUCKETS = 32
T5_MAX_DIST = 128
MEM_LEN = 256
MEM_HEADS = 4
MEM_HEAD_DIM = 128
MEM_WIDTH = MEM_HEADS * MEM_HEAD_DIM
D_FF = 2816
CONV_W = 3
Q_BLOCK = 128
IN_SIZES = (A_WIDTH, A_WIDTH, A_WIDTH, B_WIDTH, B_KV_WIDTH, B_KV_WIDTH, IDX_HEADS * IDX_DIM, IDX_DIM, IDX_HEADS)
IN_WIDTH = sum(IN_SIZES)
ALPHA = (2 * DEPTH) ** 0.25
BETA = (8 * DEPTH) ** -0.25
LN_EPS = 1e-5
ATTN_SCALE = HEAD_DIM ** -0.5
NEG = -1e30

kernel_name = 'hybrid_streaming_encoder_step'


def layer_norm(x, g, b):
    xf = x.astype(jnp.float32)
    mu = jnp.mean(xf, axis=-1, keepdims=True)
    var = jnp.mean(jnp.square(xf - mu), axis=-1, keepdims=True)
    return ((xf - mu) * lax.rsqrt(var + LN_EPS) * g + b).astype(x.dtype)


def split_in(z):
    b, t = z.shape[:2]
    parts, off = [], 0
    for n in IN_SIZES:
        parts.append(z[..., off:off + n])
        off += n
    qa, ka, va, qb, kb, vb, qi, ki, wi = parts
    hd = lambda a, h, d: a.reshape(b, t, h, d)
    return (hd(qa, A_HEADS, HEAD_DIM), hd(ka, A_HEADS, HEAD_DIM), hd(va, A_HEADS, HEAD_DIM),
            hd(qb, B_HEADS, HEAD_DIM), hd(kb, B_KV_HEADS, HEAD_DIM), hd(vb, B_KV_HEADS, HEAD_DIM),
            hd(qi, IDX_HEADS, IDX_DIM), ki, wi * IDX_HEADS ** -0.5)


def clipped_rel_bias(table, rel):
    return table[jnp.clip(rel, -A_MAX_REL, A_MAX_REL) + A_MAX_REL]


def t5_bucket(rel):
    half = N_BUCKETS // 2
    max_exact = half // 2
    n = jnp.abs(rel)
    log_ratio = jnp.log(jnp.maximum(n, 1).astype(jnp.float32) / max_exact) / math.log(T5_MAX_DIST / max_exact)
    large = jnp.minimum(max_exact + (log_ratio * (half - max_exact)).astype(jnp.int32), half - 1)
    return jnp.where(rel < 0, half, 0) + jnp.where(n < max_exact, n, large)


def band_attend_prompt(q, k, v, table):
    b, t = q.shape[:2]
    n_c = t // CHUNK
    band = (N_PREV_CHUNKS + 1) * CHUNK
    pad = ((0, 0), (N_PREV_CHUNKS * CHUNK, 0), (0, 0), (0, 0))
    kc = jnp.pad(k, pad).reshape(b, n_c + N_PREV_CHUNKS, CHUNK, A_HEADS, HEAD_DIM)
    vc = jnp.pad(v, pad).reshape(b, n_c + N_PREV_CHUNKS, CHUNK, A_HEADS, HEAD_DIM)
    idx = jnp.arange(n_c)[:, None] + jnp.arange(N_PREV_CHUNKS + 1)[None, :]
    kb = kc[:, idx].reshape(b, n_c, band, A_HEADS, HEAD_DIM)
    vb = vc[:, idx].reshape(b, n_c, band, A_HEADS, HEAD_DIM)
    qc = q.reshape(b, n_c, CHUNK, A_HEADS, HEAD_DIM)
    rel = jnp.arange(CHUNK)[:, None] + N_PREV_CHUNKS * CHUNK - jnp.arange(band)[None, :]
    bias = clipped_rel_bias(table, rel).transpose(2, 0, 1)
    kpos = (jnp.arange(n_c)[:, None] - N_PREV_CHUNKS) * CHUNK + jnp.arange(band)[None, :]
    s = jnp.einsum('bcqhd,bckhd->bchqk', qc, kb).astype(jnp.float32) * ATTN_SCALE + bias
    s = jnp.where((kpos >= 0)[None, :, None, None, :], s, NEG)
    p = jax.nn.softmax(s, axis=-1).astype(v.dtype)
    return jnp.einsum('bchqk,bckhd->bcqhd', p, vb).reshape(b, t, A_WIDTH)


def band_attend_sample(q, k, v, table):
    b, s_len = q.shape[:2]
    n_keys = k.shape[1]
    rel = (n_keys - s_len + jnp.arange(s_len))[:, None] - jnp.arange(n_keys)[None, :]
    bias = clipped_rel_bias(table, rel).transpose(2, 0, 1)
    s = jnp.einsum('bqhd,bkhd->bhqk', q, k).astype(jnp.float32) * ATTN_SCALE + bias
    p = jax.nn.softmax(s, axis=-1).astype(v.dtype)
    return jnp.einsum('bhqk,bkhd->bqhd', p, v).reshape(b, s_len, A_WIDTH)


def dsa_attend(q, qi, wi, qpos, k, v, ki, admissible, top_k, t5_table):
    b, nq = q.shape[:2]
    logits = jnp.einsum('bqhe,ble->bqhl', qi, ki).astype(jnp.float32) * IDX_DIM ** -0.5
    score = jnp.einsum('bqh,bqhl->bql', wi.astype(jnp.float32), jax.nn.relu(logits))
    score = jnp.where(admissible[None], score, -jnp.inf)
    top_val, sel = lax.top_k(score, top_k)
    gather = jax.vmap(lambda rows, ids: rows[ids])
    ks, vs = gather(k, sel), gather(v, sel)
    bias = t5_table[t5_bucket(qpos[None, :, None] - sel)]
    bias = bias.reshape(b, nq, top_k, B_KV_HEADS, B_GROUP).transpose(0, 1, 3, 4, 2)
    qg = q.reshape(b, nq, B_KV_HEADS, B_GROUP, HEAD_DIM)
    s = jnp.einsum('bqgrd,bqkgd->bqgrk', qg, ks).astype(jnp.float32) * ATTN_SCALE + bias
    s = jnp.where(jnp.isfinite(top_val)[:, :, None, None, :], s, NEG)
    p = jax.nn.softmax(s, axis=-1).astype(v.dtype)
    return jnp.einsum('bqgrk,bqkgd->bqgrd', p, vs).reshape(b, nq, B_WIDTH)


def dsa_prompt(q, k, v, qi, ki, wi, t5_table):
    b, t = q.shape[:2]
    top_k = min(TOPK_MAX, t // 4)
    key_idx = jnp.arange(t)

    def block(i):
        q0 = i * Q_BLOCK
        sl = lambda a: lax.dynamic_slice_in_dim(a, q0, Q_BLOCK, axis=1)
        qpos = q0 + jnp.arange(Q_BLOCK)
        adm = key_idx[None, :] < ((qpos // CHUNK + 1) * CHUNK)[:, None]
        return dsa_attend(sl(q), sl(qi), sl(wi), qpos, k, v, ki, adm, top_k, t5_table)

    out = lax.map(block, jnp.arange(t // Q_BLOCK))
    return out.transpose(1, 0, 2, 3).reshape(b, t, B_WIDTH)


def mem_kv(mem, w_k, w_v):
    b = mem.shape[0]
    return ((mem @ w_k).reshape(b, MEM_LEN, MEM_HEADS, MEM_HEAD_DIM),
            (mem @ w_v).reshape(b, MEM_LEN, MEM_HEADS, MEM_HEAD_DIM))


def mem_attend(x, mk, mv, w_q, w_o):
    b, t = x.shape[:2]
    q = (x @ w_q).reshape(b, t, MEM_HEADS, MEM_HEAD_DIM)
    s = jnp.einsum('bthd,bmhd->bhtm', q, mk).astype(jnp.float32) * MEM_HEAD_DIM ** -0.5
    p = jax.nn.softmax(s, axis=-1).astype(mv.dtype)
    return jnp.einsum('bhtm,bmhd->bthd', p, mv).reshape(b, t, MEM_WIDTH) @ w_o


def conv_ffn(x, g_hist, w_up, w_conv, b_conv, w_down):
    t = x.shape[1]
    u, g = jnp.split(x @ w_up, 2, axis=-1)
    gp = jnp.concatenate([g_hist, g], axis=1)
    gc = b_conv + sum(w_conv[j] * gp[:, j:j + t] for j in range(CONV_W))
    h = u * jax.nn.gelu(gc)
    return h @ w_down, gp[:, t:]


def setup_inputs(seed: int = 0) -> dict:
    key = jax.random.key(seed)
    ks = iter(jax.random.split(key, 32))
    nrm = lambda shape, scale=1.0: scale * jax.random.normal(next(ks), shape, jnp.float32)
    a_cache = min(N_PREV_CHUNKS * CHUNK, PAST_LEN)
    return {
        'x_prompt': nrm((BATCH, SEQ, D_MODEL)),
        'x_sample': nrm((DEC_BATCH, DEC_SEQ, D_MODEL)),
        'cache_a_k': nrm((DEPTH, DEC_BATCH, a_cache, A_HEADS, HEAD_DIM)),
        'cache_a_v': nrm((DEPTH, DEC_BATCH, a_cache, A_HEADS, HEAD_DIM)),
        'cache_b_k': nrm((DEPTH, DEC_BATCH, PAST_LEN, B_KV_HEADS, HEAD_DIM)),
        'cache_b_v': nrm((DEPTH, DEC_BATCH, PAST_LEN, B_KV_HEADS, HEAD_DIM)),
        'cache_b_kidx': nrm((DEPTH, DEC_BATCH, PAST_LEN, IDX_DIM)),
        'cache_mem_k': nrm((DEPTH, DEC_BATCH, MEM_LEN, MEM_HEADS, MEM_HEAD_DIM)),
        'cache_mem_v': nrm((DEPTH, DEC_BATCH, MEM_LEN, MEM_HEADS, MEM_HEAD_DIM)),
        'state_ffn_conv': nrm((DEPTH, DEC_BATCH, CONV_W - 1, D_FF)),
        'mem_prompt': nrm((BATCH, MEM_LEN, D_MODEL)),
        'w_in': nrm((DEPTH, D_MODEL, IN_WIDTH), D_MODEL ** -0.5),
        'a_rel_bias': nrm((DEPTH, 2 * A_MAX_REL + 1, A_HEADS), 0.1),
        't5_bias': nrm((N_BUCKETS, B_HEADS), 0.1),
        'w_o': nrm((DEPTH, MIX_WIDTH, D_MODEL), BETA * MIX_WIDTH ** -0.5),
        'ln1_g': 1.0 + nrm((DEPTH, D_MODEL), 0.01),
        'ln1_b': nrm((DEPTH, D_MODEL), 0.01),
        'w_mq': nrm((DEPTH, D_MODEL, MEM_WIDTH), D_MODEL ** -0.5),
        'w_mk': nrm((DEPTH, D_MODEL, MEM_WIDTH), D_MODEL ** -0.5),
        'w_mv': nrm((DEPTH, D_MODEL, MEM_WIDTH), D_MODEL ** -0.5),
        'w_mo': nrm((DEPTH, MEM_WIDTH, D_MODEL), BETA * MEM_WIDTH ** -0.5),
        'ln2_g': 1.0 + nrm((DEPTH, D_MODEL), 0.01),
        'ln2_b': nrm((DEPTH, D_MODEL), 0.01),
        'w_up': nrm((DEPTH, D_MODEL, 2 * D_FF), D_MODEL ** -0.5),
        'w_conv': nrm((DEPTH, CONV_W, D_FF), CONV_W ** -0.5),
        'b_conv': nrm((DEPTH, D_FF), 0.01),
        'w_down': nrm((DEPTH, D_FF, D_MODEL), BETA * D_FF ** -0.5),
        'ln3_g': 1.0 + nrm((DEPTH, D_MODEL), 0.01),
        'ln3_b': nrm((DEPTH, D_MODEL), 0.01),
    }


def reference(x_prompt, x_sample, cache_a_k, cache_a_v, cache_b_k, cache_b_v, cache_b_kidx,
              cache_mem_k, cache_mem_v, state_ffn_conv, mem_prompt, w_in, a_rel_bias, t5_bias,
              w_o, ln1_g, ln1_b, w_mq, w_mk, w_mv, w_mo, ln2_g, ln2_b, w_up, w_conv, b_conv,
              w_down, ln3_g, ln3_b):
    xp, xs = x_prompt, x_sample
    b_p, t_p = xp.shape[:2]
    s_len = xs.shape[1]
    a_keep = min(N_PREV_CHUNKS * CHUNK, t_p)
    new_p = [[] for _ in range(8)]
    new_s = [[] for _ in range(6)]
    for l in range(DEPTH):
        qa, ka, va, qb, kb, vb, qi, ki, wi = split_in(xp @ w_in[l])
        mix = jnp.concatenate([band_attend_prompt(qa, ka, va, a_rel_bias[l]),
                               dsa_prompt(qb, kb, vb, qi, ki, wi, t5_bias)], axis=-1)
        h = layer_norm(ALPHA * xp + mix @ w_o[l], ln1_g[l], ln1_b[l])
        mk, mv = mem_kv(mem_prompt, w_mk[l], w_mv[l])
        h = layer_norm(ALPHA * h + mem_attend(h, mk, mv, w_mq[l], w_mo[l]), ln2_g[l], ln2_b[l])
        f, g_tail = conv_ffn(h, jnp.zeros((b_p, CONV_W - 1, D_FF), h.dtype), w_up[l], w_conv[l], b_conv[l], w_down[l])
        xp = layer_norm(ALPHA * h + f, ln3_g[l], ln3_b[l])
        for lst, arr in zip(new_p, (ka[:, t_p - a_keep:], va[:, t_p - a_keep:], kb, vb, ki, mk, mv, g_tail)):
            lst.append(arr)

        qa, ka, va, qb, kb, vb, qi, ki, wi = split_in(xs @ w_in[l])
        oa = band_attend_sample(qa, jnp.concatenate([cache_a_k[l], ka], axis=1),
                                jnp.concatenate([cache_a_v[l], va], axis=1), a_rel_bias[l])
        kb_all = jnp.concatenate([cache_b_k[l], kb], axis=1)
        vb_all = jnp.concatenate([cache_b_v[l], vb], axis=1)
        ki_all = jnp.concatenate([cache_b_kidx[l], ki], axis=1)
        n_keys = kb_all.shape[1]
        qpos = n_keys - s_len + jnp.arange(s_len)
        adm = jnp.ones((s_len, n_keys), dtype=bool)
        ob = dsa_attend(qb, qi, wi, qpos, kb_all, vb_all, ki_all, adm, min(TOPK_MAX, n_keys // 4), t5_bias)
        h = layer_norm(ALPHA * xs + jnp.concatenate([oa, ob], axis=-1) @ w_o[l], ln1_g[l], ln1_b[l])
        h = layer_norm(ALPHA * h + mem_attend(h, cache_mem_k[l], cache_mem_v[l], w_mq[l], w_mo[l]), ln2_g[l], ln2_b[l])
        f, g_tail_s = conv_ffn(h, state_ffn_conv[l], w_up[l], w_conv[l], b_conv[l], w_down[l])
        xs = layer_norm(ALPHA * h + f, ln3_g[l], ln3_b[l])
        for lst, arr in zip(new_s, (ka, va, kb, vb, ki, g_tail_s)):
            lst.append(arr)

    pak, pav, pbk, pbv, pbi, pmk, pmv, pfc = [jnp.stack(a) for a in new_p]
    sak, sav, sbk, sbv, sbi, sfc = [jnp.stack(a) for a in new_s]
    return (xp, xs, pak, pav, pbk, pbv, pbi, pmk, pmv, pfc, sak, sav, sbk, sbv, sbi, sfc)
```

```python
import functools
import math

import jax
import jax.numpy as jnp
from jax import lax
from jax.experimental import pallas as pl
from jax.experimental.pallas import tpu as pltpu

F32 = jnp.float32
BF16 = jnp.bfloat16
I32 = jnp.int32

D_MODEL = 1024
CHUNK = 64
N_PREV_CHUNKS = 8
HEAD_DIM = 64
A_HEADS = 8
A_WIDTH = A_HEADS * HEAD_DIM
A_MAX_REL = 64
B_HEADS = 8
B_KV_HEADS = 2
B_GROUP = B_HEADS // B_KV_HEADS
B_WIDTH = B_HEADS * HEAD_DIM
B_KV_WIDTH = B_KV_HEADS * HEAD_DIM
IDX_HEADS = 8
IDX_DIM = 64
TOPK_MAX = 256
N_BUCKETS = 32
T5_MAX_DIST = 128
MEM_LEN = 256
MEM_HEADS = 4
MEM_HEAD_DIM = 128
MEM_WIDTH = MEM_HEADS * MEM_HEAD_DIM
D_FF = 2816
CONV_W = 3
IN_SIZES = (A_WIDTH, A_WIDTH, A_WIDTH, B_WIDTH, B_KV_WIDTH, B_KV_WIDTH, IDX_HEADS * IDX_DIM, IDX_DIM, IDX_HEADS)
DEPTH = 1
ALPHA = (2 * DEPTH) ** 0.25
LN_EPS = 1e-5
ATTN_SCALE = HEAD_DIM ** -0.5
NEG = -1e30

LANES = 128
TILE = 256
BAND_TILES = 1 + (N_PREV_CHUNKS * CHUNK) // TILE
VMEM_LIMIT = 56 * 1024 * 1024

_C_QA, _C_KA, _C_VA, _C_QB = 0, 512, 1024, 1536
_C_KB, _C_VB, _C_QI, _C_KI, _C_WI = 2048, 2176, 2304, 2816, 2944
IN_PAD = 3072

INT_MIN = -2 ** 31
NEG_INF_KEY = -2139095041

_NT = (((1,), (1,)), ((), ()))


def _params(*sem):
    return pltpu.CompilerParams(dimension_semantics=sem, vmem_limit_bytes=VMEM_LIMIT)


def _layer_norm(z, g, b):
    mu = jnp.mean(z, axis=-1, keepdims=True)
    d = z - mu
    var = jnp.mean(d * d, axis=-1, keepdims=True)
    return d * lax.rsqrt(var + LN_EPS) * g + b


def _in_proj_kernel(x_ref, w_ref, qa_o, ka_o, va_o, qb_o, qi_o, kb_o, vb_o, ki_o, kbb_o, vbb_o, kib_o,
                    wi_o, kat_o, vat_o, *, tiles_per_batch, wi_scale):
    i = pl.program_id(0)
    xb = x_ref[...].astype(BF16)

    def mm(c0, n):
        return jnp.dot(xb, w_ref[:, c0:c0 + n], preferred_element_type=F32)

    qa_o[...] = mm(_C_QA, 512).astype(BF16)
    ka = mm(_C_KA, 512)
    va = mm(_C_VA, 512)
    ka_o[...] = ka.astype(BF16)
    va_o[...] = va.astype(BF16)
    qb_o[...] = mm(_C_QB, 512).astype(BF16)
    qi_o[...] = mm(_C_QI, 512).astype(BF16)
    kb = mm(_C_KB, 128)
    vb = mm(_C_VB, 128)
    kb_o[...] = kb
    vb_o[...] = vb
    kbb_o[...] = kb.astype(BF16)
    vbb_o[...] = vb.astype(BF16)
    ki = mm(_C_KI, 128)[:, :IDX_DIM]
    ki_o[...] = ki
    kib_o[...] = ki.astype(BF16)
    wi_o[...] = mm(_C_WI, 128) * wi_scale

    @pl.when(i % tiles_per_batch == tiles_per_batch - 1)
    def _():
        kat_o[...] = ka
        vat_o[...] = va


def _prep_w_in(w):
    parts, off = [], 0
    for n in IN_SIZES:
        parts.append(w[:, off:off + n])
        off += n
    qa, ka, va, qb, kb, vb, qi, ki, wi = parts
    pad = lambda a, n: jnp.pad(a, ((0, 0), (0, n - a.shape[1])))
    cols = [qa * ATTN_SCALE, ka, va, qb * ATTN_SCALE, kb, vb, qi * IDX_DIM ** -0.5, pad(ki, 128), pad(wi, 128)]
    return jnp.concatenate(cols, axis=1).astype(BF16)


def _in_proj(x2d, w_pad, tm, tiles_per_batch):
    r = x2d.shape[0]
    n_tiles = r // tm
    n_batch = n_tiles // tiles_per_batch
    row = lambda n: pl.BlockSpec((tm, n), lambda i: (i, 0))
    tail = pl.BlockSpec((tm, 512), lambda i: (i // tiles_per_batch, 0))
    sds = jax.ShapeDtypeStruct
    out_shape = [sds((r, 512), BF16)] * 5 + [sds((r, 128), F32), sds((r, 128), F32), sds((r, IDX_DIM), F32),
                                              sds((r, 128), BF16), sds((r, 128), BF16), sds((r, IDX_DIM), BF16),
                                              sds((r, 128), F32),
                                              sds((n_batch * tm, 512), F32), sds((n_batch * tm, 512), F32)]
    out_specs = [row(512)] * 5 + [row(128), row(128), row(IDX_DIM), row(128), row(128), row(IDX_DIM), row(128),
                                  tail, tail]
    return pl.pallas_call(
        functools.partial(_in_proj_kernel, tiles_per_batch=tiles_per_batch, wi_scale=IDX_HEADS ** -0.5),
        grid=(n_tiles,),
        in_specs=[pl.BlockSpec((tm, D_MODEL), lambda i: (i, 0)),
                  pl.BlockSpec((D_MODEL, IN_PAD), lambda i: (0, 0))],
        out_specs=out_specs,
        out_shape=out_shape,
        compiler_params=_params("arbitrary"),
        name="in_proj",
    )(x2d, w_pad)


def _band_bias(table):
    r = jnp.arange(TILE)[:, None]
    c = jnp.arange(BAND_TILES * TILE)[None, :]
    rel = (BAND_TILES - 1) * TILE + r - c
    bias = table[jnp.clip(rel, -A_MAX_REL, A_MAX_REL) + A_MAX_REL].transpose(2, 0, 1)
    ok = (c // CHUNK >= r // CHUNK) & (c // CHUNK <= r // CHUNK + N_PREV_CHUNKS)
    return jnp.where(ok[None], bias, NEG).astype(F32)


def _band_kernel(q_ref, k0, k1, k2, v0, v1, v2, bias_ref, o_ref, *, off, valid_len):
    i = pl.program_id(1)
    base = (i + off - (BAND_TILES - 1)) * TILE
    col = lax.broadcasted_iota(I32, (1, TILE), 1)
    valid = []
    for j in range(BAND_TILES):
        kpos = base + j * TILE + col
        valid.append((kpos >= 0) & (kpos < valid_len))
    krefs, vrefs = (k0, k1, k2), (v0, v1, v2)
    outs = []
    for h in range(A_HEADS):
        sl = slice(h * HEAD_DIM, (h + 1) * HEAD_DIM)
        qh = q_ref[0, :, sl]
        s = []
        for j in range(BAND_TILES):
            sj = lax.dot_general(qh, krefs[j][0, :, sl], _NT, preferred_element_type=F32)
            sj = sj + bias_ref[h, :, j * TILE:(j + 1) * TILE]
            s.append(jnp.where(valid[j], sj, NEG))
        m = jnp.maximum(jnp.maximum(s[0].max(-1, keepdims=True), s[1].max(-1, keepdims=True)),
                        s[2].max(-1, keepdims=True))
        p = [jnp.exp(sj - m) for sj in s]
        l = p[0].sum(-1, keepdims=True) + p[1].sum(-1, keepdims=True) + p[2].sum(-1, keepdims=True)
        o = jnp.dot(p[0].astype(BF16), vrefs[0][0, :, sl], preferred_element_type=F32)
        o = o + jnp.dot(p[1].astype(BF16), vrefs[1][0, :, sl], preferred_element_type=F32)
        o = o + jnp.dot(p[2].astype(BF16), vrefs[2][0, :, sl], preferred_element_type=F32)
        outs.append(o / l)
    o_ref[0] = jnp.concatenate(outs, axis=-1).astype(BF16)


def _band_attn(q, k, v, bias, off, valid_len):
    b, tq = q.shape[:2]
    nq = tq // TILE
    qspec = pl.BlockSpec((1, TILE, A_WIDTH), lambda bb, i: (bb, i, 0))
    kspec = lambda d: pl.BlockSpec((1, TILE, A_WIDTH), lambda bb, i: (bb, jnp.maximum(i + off - d, 0), 0))
    return pl.pallas_call(
        functools.partial(_band_kernel, off=off, valid_len=valid_len),
        grid=(b, nq),
        in_specs=[qspec, kspec(2), kspec(1), kspec(0), kspec(2), kspec(1), kspec(0),
                  pl.BlockSpec((A_HEADS, TILE, BAND_TILES * TILE), lambda bb, i: (0, 0, 0))],
        out_specs=pl.BlockSpec((1, TILE, A_WIDTH), lambda bb, i: (bb, i, 0)),
        out_shape=jax.ShapeDtypeStruct((b, tq, A_WIDTH), BF16),
        compiler_params=_params("arbitrary", "arbitrary"),
        name="band_attn",
    )(q, k, k, k, v, v, v, bias)


def _t5_bucket(rel):
    half = N_BUCKETS // 2
    max_exact = half // 2
    n = jnp.abs(rel)
    log_ratio = jnp.log(jnp.maximum(n, 1).astype(jnp.float32) / max_exact) / math.log(T5_MAX_DIST / max_exact)
    large = jnp.minimum(max_exact + (log_ratio * (half - max_exact)).astype(jnp.int32), half - 1)
    return jnp.where(rel < 0, half, 0) + jnp.where(n < max_exact, n, large)


def _dsa_bias(t5_table):
    kk = jnp.arange(2 * TILE)[:, None]
    r = jnp.arange(TILE)[None, :]
    rel = r + TILE - kk
    near = t5_table[_t5_bucket(rel)]
    far = t5_table[_t5_bucket(jnp.full((1, 1), 2 * TILE + 1, I32))]
    return (near - far).transpose(2, 0, 1).astype(F32)


def _dsa_kernel(qi_ref, wi_ref, qb_ref, ki_ref, kb_ref, vb_ref, bias_ref, o_ref,
                key_ref, vt_ref, m_ref, acc_ref, *, off, valid_len, q_valid, n_kt):
    i = pl.program_id(1)
    qt = i + off
    q0 = qt * TILE
    nk = qt + 1

    @pl.when(i == 0)
    def _():
        ones = jnp.ones((HEAD_DIM, TILE), BF16)

        def body(j, c):
            vt = vb_ref[0, pl.ds(pl.multiple_of(j * TILE, TILE), TILE), :].astype(F32).T
            for g in range(B_KV_HEADS):
                vt_ref[g, j, 0:HEAD_DIM, :] = vt[g * HEAD_DIM:(g + 1) * HEAD_DIM].astype(BF16)
                vt_ref[g, j, HEAD_DIM:2 * HEAD_DIM, :] = ones
            return c

        lax.fori_loop(0, n_kt, body, 0)

    colq = lax.broadcasted_iota(I32, (1, TILE), 1)
    rowk = lax.broadcasted_iota(I32, (TILE, 1), 0)

    qi = qi_ref[0]
    qi_stack = jnp.concatenate([qi[:, h * IDX_DIM:(h + 1) * IDX_DIM] for h in range(IDX_HEADS)], axis=0)
    wi_t = wi_ref[0].T
    lim = jnp.minimum(q0 + (colq // CHUNK + 1) * CHUNK, valid_len)

    def score_tile(j, c):
        kt = ki_ref[0, pl.ds(pl.multiple_of(j * TILE, TILE), TILE), :]
        lg = lax.dot_general(kt, qi_stack, _NT, preferred_element_type=F32)
        sc = wi_t[0:1, :] * jnp.maximum(lg[:, 0:TILE], 0.0)
        for h in range(1, IDX_HEADS):
            sc = sc + wi_t[h:h + 1, :] * jnp.maximum(lg[:, h * TILE:(h + 1) * TILE], 0.0)
        sc = jnp.where(j * TILE + rowk < lim, sc, -jnp.inf)
        bits = pltpu.bitcast(sc, I32)
        key_ref[j] = jnp.where(bits < 0, bits ^ 0x7FFFFFFF, bits)
        return c

    lax.fori_loop(0, nk, score_tile, 0)

    def count_ge(cand):
        def body(j, acc):
            ge = jnp.where(key_ref[j] >= cand, 1, 0)
            return acc + ge.reshape(TILE // 8, 8, TILE).sum(axis=0)
        acc = lax.fori_loop(0, nk, body, jnp.zeros((8, TILE), I32))
        return acc.sum(axis=0, keepdims=True)

    def bis_cond(st):
        bit, _, _, n_open = st
        return (bit >= 0) & (n_open > 0)

    def bis_body(st):
        bit, t, done, _ = st
        cand = t + lax.shift_left(jnp.int32(1), bit)
        c = count_ge(cand)
        t = jnp.where((c >= TOPK_MAX) & (done == 0), cand, t)
        done = jnp.where(c == TOPK_MAX, 1, done)
        return bit - 1, t, done, jnp.sum(1 - done)

    done0 = jnp.where(colq < q_valid, 0, 1)
    _, thr, done, n_open = lax.while_loop(
        bis_cond, bis_body, (jnp.int32(31), jnp.full((1, TILE), INT_MIN, I32), done0, jnp.sum(1 - done0)))

    @pl.when(n_open > 0)
    def _():
        n_gt = count_ge(thr + 1)
        keep = jnp.where(done == 1, 2 ** 30, TOPK_MAX - n_gt)
        lower = (lax.broadcasted_iota(I32, (TILE, TILE), 0) > lax.broadcasted_iota(I32, (TILE, TILE), 1))
        lower = jnp.where(lower, 1.0, 0.0).astype(BF16)

        def body(j, run):
            blk = key_ref[j]
            eq = blk == thr
            eq_f = jnp.where(eq, 1.0, 0.0)
            before = jnp.dot(lower, eq_f.astype(BF16), preferred_element_type=F32)
            rank = run + before.astype(I32)
            key_ref[j] = jnp.where(eq & (rank >= keep), NEG_INF_KEY, blk)
            return run + eq_f.reshape(TILE // 8, 8, TILE).sum(axis=0).sum(axis=0, keepdims=True).astype(I32)

        lax.fori_loop(0, nk, body, jnp.zeros((1, TILE), I32))

    thr = jnp.maximum(thr, NEG_INF_KEY + 1)

    qb = qb_ref[0]
    outs = []
    for g in range(B_KV_HEADS):
        heads = range(g * B_GROUP, (g + 1) * B_GROUP)
        q_stack = jnp.concatenate([qb[:, h * HEAD_DIM:(h + 1) * HEAD_DIM] for h in heads], axis=0)
        m_ref[...] = jnp.full(m_ref.shape, NEG, F32)
        acc_ref[...] = jnp.zeros(acc_ref.shape, F32)

        def tile(j, near):
            kt = kb_ref[0, pl.ds(pl.multiple_of(j * TILE, TILE), TILE), g * HEAD_DIM:(g + 1) * HEAD_DIM]
            st = lax.dot_general(kt, q_stack, _NT, preferred_element_type=F32)
            sel = key_ref[j] >= thr
            parts = []
            for hh, h in enumerate(heads):
                s_h = st[:, hh * TILE:(hh + 1) * TILE]
                if near is not None:
                    s_h = s_h + bias_ref[h, near * TILE:(near + 1) * TILE, :]
                parts.append(jnp.where(sel, s_h, -jnp.inf))
            s = jnp.concatenate(parts, axis=1)
            m_old = m_ref[...]
            m_new = jnp.maximum(m_old, s.max(axis=0, keepdims=True))
            p = jnp.exp(s - m_new).astype(BF16)
            pv = jnp.dot(vt_ref[g, j], p, preferred_element_type=F32)
            acc_ref[...] = jnp.exp(m_old - m_new) * acc_ref[...] + pv
            m_ref[...] = m_new

        def far_tile(j, c):
            tile(j, None)
            return c

        lax.fori_loop(0, nk - 2, far_tile, 0)

        @pl.when(nk >= 2)
        def _():
            tile(nk - 2, 0)

        tile(nk - 1, 1)

        for hh in range(B_GROUP):
            blk = acc_ref[:, hh * TILE:(hh + 1) * TILE].T
            outs.append(blk[:, 0:HEAD_DIM] / blk[:, HEAD_DIM:HEAD_DIM + 1])
    o_ref[0] = jnp.concatenate(outs, axis=-1).astype(BF16)


def _dsa_attn(qi, wi, qb, ki, kb, vb, bias, off, valid_len, q_valid):
    b, tq = qi.shape[:2]
    tk = ki.shape[1]
    nq, n_kt = tq // TILE, tk // TILE
    qspec = lambda n: pl.BlockSpec((1, TILE, n), lambda bb, i: (bb, i, 0))
    kspec = lambda n: pl.BlockSpec((1, tk, n), lambda bb, i: (bb, 0, 0))
    return pl.pallas_call(
        functools.partial(_dsa_kernel, off=off, valid_len=valid_len, q_valid=q_valid, n_kt=n_kt),
        grid=(b, nq),
        in_specs=[qspec(512), qspec(128), qspec(512), kspec(IDX_DIM), kspec(128), kspec(128),
                  pl.BlockSpec((B_HEADS, 2 * TILE, TILE), lambda bb, i: (0, 0, 0))],
        out_specs=pl.BlockSpec((1, TILE, B_WIDTH), lambda bb, i: (bb, i, 0)),
        out_shape=jax.ShapeDtypeStruct((b, tq, B_WIDTH), BF16),
        scratch_shapes=[pltpu.VMEM((n_kt, TILE, TILE), I32),
                        pltpu.VMEM((B_KV_HEADS, n_kt, 2 * HEAD_DIM, TILE), BF16),
                        pltpu.VMEM((1, B_GROUP * TILE), F32),
                        pltpu.VMEM((2 * HEAD_DIM, B_GROUP * TILE), F32)],
        compiler_params=_params("arbitrary", "arbitrary"),
        name="dsa_attn",
    )(qi, wi, qb, ki, kb, vb, bias)


def _out_proj_kernel(x_ref, oa_ref, ob_ref, w_ref, g_ref, b_ref, o_ref):
    mix = jnp.concatenate([oa_ref[...], ob_ref[...]], axis=-1)
    y = jnp.dot(mix, w_ref[...], preferred_element_type=F32)
    o_ref[...] = _layer_norm(ALPHA * x_ref[...] + y, g_ref[...], b_ref[...])


def _out_proj_ln(x2d, oa, ob, w, g, b, tm):
    r = x2d.shape[0]
    row = lambda n: pl.BlockSpec((tm, n), lambda i: (i, 0))
    const = lambda s: pl.BlockSpec(s, lambda i: (0, 0))
    return pl.pallas_call(
        _out_proj_kernel,
        grid=(r // tm,),
        in_specs=[row(D_MODEL), row(A_WIDTH), row(B_WIDTH), const((A_WIDTH + B_WIDTH, D_MODEL)),
                  const((1, D_MODEL)), const((1, D_MODEL))],
        out_specs=row(D_MODEL),
        out_shape=jax.ShapeDtypeStruct((r, D_MODEL), F32),
        compiler_params=_params("arbitrary"),
        name="out_proj_ln",
    )(x2d, oa, ob, w, g, b)


def _mem_kv_kernel(m_ref, wk_ref, wv_ref, k_o, v_o, kb_o, vb_o):
    mb = m_ref[...].astype(BF16)
    k = jnp.dot(mb, wk_ref[...], preferred_element_type=F32)
    v = jnp.dot(mb, wv_ref[...], preferred_element_type=F32)
    k_o[...] = k
    v_o[...] = v
    kb_o[...] = k.astype(BF16)
    vb_o[...] = v.astype(BF16)


def _mem_kv(mem2d, wk, wv):
    r = mem2d.shape[0]
    tm = MEM_LEN
    row = lambda n: pl.BlockSpec((tm, n), lambda i: (i, 0))
    const = lambda s: pl.BlockSpec(s, lambda i: (0, 0))
    sds = jax.ShapeDtypeStruct
    return pl.pallas_call(
        _mem_kv_kernel,
        grid=(r // tm,),
        in_specs=[row(D_MODEL), const((D_MODEL, MEM_WIDTH)), const((D_MODEL, MEM_WIDTH))],
        out_specs=[row(MEM_WIDTH)] * 4,
        out_shape=[sds((r, MEM_WIDTH), F32), sds((r, MEM_WIDTH), F32),
                   sds((r, MEM_WIDTH), BF16), sds((r, MEM_WIDTH), BF16)],
        compiler_params=_params("arbitrary"),
        name="mem_kv",
    )(mem2d, wk, wv)


def _mem_attn_kernel(h_ref, mk_ref, mv_ref, wq_ref, wo_ref, g_ref, b_ref, o_ref):
    h = h_ref[0]
    q = jnp.dot(h.astype(BF16), wq_ref[...], preferred_element_type=F32).astype(BF16)
    outs = []
    for hd in range(MEM_HEADS):
        sl = slice(hd * MEM_HEAD_DIM, (hd + 1) * MEM_HEAD_DIM)
        s = lax.dot_general(q[:, sl], mk_ref[0, :, sl], _NT, preferred_element_type=F32) * MEM_HEAD_DIM ** -0.5
        p = jnp.exp(s - s.max(-1, keepdims=True))
        l = p.sum(-1, keepdims=True)
        outs.append(jnp.dot(p.astype(BF16), mv_ref[0, :, sl], preferred_element_type=F32) / l)
    o = jnp.concatenate(outs, axis=-1).astype(BF16)
    y = jnp.dot(o, wo_ref[...], preferred_element_type=F32)
    o_ref[0] = _layer_norm(ALPHA * h + y, g_ref[...], b_ref[...])


def _mem_attn_ln(h3d, mk, mv, wq, wo, g, b, tm):
    bsz, t = h3d.shape[:2]
    const2 = lambda s: pl.BlockSpec(s, lambda bb, i: (0, 0))
    return pl.pallas_call(
        _mem_attn_kernel,
        grid=(bsz, t // tm),
        in_specs=[pl.BlockSpec((1, tm, D_MODEL), lambda bb, i: (bb, i, 0)),
                  pl.BlockSpec((1, MEM_LEN, MEM_WIDTH), lambda bb, i: (bb, 0, 0)),
                  pl.BlockSpec((1, MEM_LEN, MEM_WIDTH), lambda bb, i: (bb, 0, 0)),
                  const2((D_MODEL, MEM_WIDTH)), const2((MEM_WIDTH, D_MODEL)),
                  const2((1, D_MODEL)), const2((1, D_MODEL))],
        out_specs=pl.BlockSpec((1, tm, D_MODEL), lambda bb, i: (bb, i, 0)),
        out_shape=jax.ShapeDtypeStruct(h3d.shape, F32),
        compiler_params=_params("arbitrary", "arbitrary"),
        name="mem_attn_ln",
    )(h3d, mk, mv, wq, wo, g, b)


FF_CHUNK = 256


def _ffn_kernel(h_ref, hist_ref, wu_ref, wc_ref, bc_ref, wd_ref, g_ref, b_ref, o_ref, tail_ref,
                carry_ref, f_ref, *, tiles_per_batch, seg):
    i = pl.program_id(0)
    tm = h_ref.shape[0]
    nseg = tm // seg
    h = h_ref[...]
    hb = h.astype(BF16)
    row = lax.broadcasted_iota(I32, (tm, 1), 0)
    first = (i % tiles_per_batch) == 0
    f_ref[...] = jnp.zeros(f_ref.shape, F32)
    for c in range(D_FF // FF_CHUNK):
        cs = slice(c * FF_CHUNK, (c + 1) * FF_CHUNK)
        u = jnp.dot(hb, wu_ref[:, cs], preferred_element_type=F32)
        gt = jnp.dot(hb, wu_ref[:, D_FF + c * FF_CHUNK:D_FF + (c + 1) * FF_CHUNK], preferred_element_type=F32)
        p1 = pltpu.roll(gt, 1, 0)
        p2 = pltpu.roll(gt, 2, 0)
        for s in range(nseg):
            hist = hist_ref[s, :, cs]
            if tiles_per_batch > 1:
                hist = jnp.where(first, hist, carry_ref[:, cs])
            p1 = jnp.where(row == s * seg, hist[7:8, :], p1)
            p2 = jnp.where(row == s * seg, hist[6:7, :], p2)
            p2 = jnp.where(row == s * seg + 1, hist[7:8, :], p2)
        gc = bc_ref[:, cs] + ((wc_ref[0:1, cs] * p2 + wc_ref[1:2, cs] * p1) + wc_ref[2:3, cs] * gt)
        act = (u * jax.nn.gelu(gc)).astype(BF16)
        f_ref[...] += jnp.dot(act, wd_ref[cs, :], preferred_element_type=F32)
        for s in range(nseg):
            tail_ref[s, :, cs] = gt[(s + 1) * seg - 8:(s + 1) * seg, :]
        carry_ref[:, cs] = gt[tm - 8:tm, :]
    o_ref[...] = _layer_norm(ALPHA * h + f_ref[...], g_ref[...], b_ref[...])


def _ffn_ln(h2d, hist, wu, wc, bc, wd, g, b, tm, tiles_per_batch, seg):
    r = h2d.shape[0]
    nseg = tm // seg
    n_tail = r // (tm * tiles_per_batch) * nseg
    row = lambda n: pl.BlockSpec((tm, n), lambda i: (i, 0))
    const = lambda s: pl.BlockSpec(s, lambda i: (0,) * len(s))
    return pl.pallas_call(
        functools.partial(_ffn_kernel, tiles_per_batch=tiles_per_batch, seg=seg),
        grid=(r // tm,),
        in_specs=[row(D_MODEL),
                  pl.BlockSpec((nseg, 8, D_FF), lambda i: (i // tiles_per_batch, 0, 0)),
                  const((D_MODEL, 2 * D_FF)), const((CONV_W, D_FF)), const((1, D_FF)), const((D_FF, D_MODEL)),
                  const((1, D_MODEL)), const((1, D_MODEL))],
        out_specs=[row(D_MODEL), pl.BlockSpec((nseg, 8, D_FF), lambda i: (i // tiles_per_batch, 0, 0))],
        out_shape=[jax.ShapeDtypeStruct((r, D_MODEL), F32), jax.ShapeDtypeStruct((n_tail, 8, D_FF), F32)],
        scratch_shapes=[pltpu.VMEM((8, D_FF), F32), pltpu.VMEM((tm, D_MODEL), F32)],
        compiler_params=_params("arbitrary"),
        name="ffn_ln",
    )(h2d, hist, wu, wc, bc, wd, g, b)


def _pad_rows(a, n):
    return jnp.pad(a, ((0, 0), (0, n - a.shape[1])) + ((0, 0),) * (a.ndim - 2))


def _hist8(g_hist):
    return jnp.pad(g_hist, ((0, 0), (8 - g_hist.shape[1], 0), (0, 0)))


def kernel(x_prompt, x_sample, cache_a_k, cache_a_v, cache_b_k, cache_b_v, cache_b_kidx, cache_mem_k, cache_mem_v, state_ffn_conv, mem_prompt, w_in, a_rel_bias, t5_bias, w_o, ln1_g, ln1_b, w_mq, w_mk, w_mv, w_mo, ln2_g, ln2_b, w_up, w_conv, b_conv, w_down, ln3_g, ln3_b):
    bp, tp = x_prompt.shape[:2]
    bs, ts = x_sample.shape[:2]
    l = 0
    vec = lambda a: a[l].reshape(1, -1)
    w_in_p = _prep_w_in(w_in[l])
    w_o_b = w_o[l].astype(BF16)
    w_mq_b, w_mk_b, w_mv_b, w_mo_b = (w[l].astype(BF16) for w in (w_mq, w_mk, w_mv, w_mo))
    w_up_b, w_down_b = w_up[l].astype(BF16), w_down[l].astype(BF16)
    band_bias = _band_bias(a_rel_bias[l])
    dsa_bias = _dsa_bias(t5_bias)
    ffn_w = (w_up_b, w_conv[l], vec(b_conv), w_down_b, vec(ln3_g), vec(ln3_b))

    tm = 512
    a_keep = min(N_PREV_CHUNKS * CHUNK, tp)
    (qa, ka, va, qb, qi, kb, vb, ki, kb_b, vb_b, ki_b, wi, ka_tail, va_tail) = _in_proj(
        x_prompt.reshape(bp * tp, D_MODEL), w_in_p, tm, tp // tm)
    r3 = lambda a: a.reshape(bp, tp, a.shape[-1])
    oa = _band_attn(r3(qa), r3(ka), r3(va), band_bias, 0, tp)
    ob = _dsa_attn(r3(qi), r3(wi), r3(qb), r3(ki_b), r3(kb_b), r3(vb_b), dsa_bias, 0, tp, TILE)
    h = _out_proj_ln(x_prompt.reshape(bp * tp, D_MODEL), oa.reshape(bp * tp, A_WIDTH), ob.reshape(bp * tp, B_WIDTH),
                     w_o_b, vec(ln1_g), vec(ln1_b), tm)
    mk, mv, mk_b, mv_b = _mem_kv(mem_prompt.reshape(bp * MEM_LEN, D_MODEL), w_mk_b, w_mv_b)
    h = _mem_attn_ln(h.reshape(bp, tp, D_MODEL), mk_b.reshape(bp, MEM_LEN, MEM_WIDTH),
                     mv_b.reshape(bp, MEM_LEN, MEM_WIDTH), w_mq_b, w_mo_b, vec(ln2_g), vec(ln2_b), TILE)
    xp, p_tail = _ffn_ln(h.reshape(bp * tp, D_MODEL), jnp.zeros((bp, 8, D_FF), F32), *ffn_w, tm, tp // tm, tm)
    prompt_state = (
        ka_tail.reshape(bp, tm, A_HEADS, HEAD_DIM)[:, tm - a_keep:][None],
        va_tail.reshape(bp, tm, A_HEADS, HEAD_DIM)[:, tm - a_keep:][None],
        kb.reshape(1, bp, tp, B_KV_HEADS, HEAD_DIM), vb.reshape(1, bp, tp, B_KV_HEADS, HEAD_DIM),
        ki.reshape(1, bp, tp, IDX_DIM),
        mk.reshape(1, bp, MEM_LEN, MEM_HEADS, MEM_HEAD_DIM), mv.reshape(1, bp, MEM_LEN, MEM_HEADS, MEM_HEAD_DIM),
        p_tail[:, 8 - (CONV_W - 1):][None])

    rs = bs * ts
    (qa, ka, va, qb, qi, kb, vb, ki, kb_b, vb_b, ki_b, wi, ka_new, va_new) = _in_proj(
        x_sample.reshape(rs, D_MODEL), w_in_p, rs, 1)
    s3 = lambda a: a.reshape(bs, ts, a.shape[-1])
    qpad = lambda a: _pad_rows(s3(a), TILE)

    past_a = cache_a_k.shape[2]
    n_a = past_a + ts
    t_a = -(-n_a // TILE) * TILE
    seq_a = lambda cache, new: _pad_rows(
        jnp.concatenate([cache[l].reshape(bs, past_a, A_WIDTH), s3(new)], axis=1), t_a).astype(BF16)
    oa = _band_attn(qpad(qa), seq_a(cache_a_k, ka_new), seq_a(cache_a_v, va_new), band_bias, past_a // TILE, n_a)

    past_b = cache_b_k.shape[2]
    n_b = past_b + ts
    t_b = -(-n_b // TILE) * TILE
    seq_b = lambda cache, new: _pad_rows(
        jnp.concatenate([cache[l].reshape(bs, past_b, -1), s3(new)], axis=1), t_b).astype(BF16)
    ob = _dsa_attn(qpad(qi), qpad(wi), qpad(qb), seq_b(cache_b_kidx, ki), seq_b(cache_b_k, kb),
                   seq_b(cache_b_v, vb), dsa_bias, past_b // TILE, n_b, ts)

    h = _out_proj_ln(x_sample.reshape(rs, D_MODEL), oa[:, :ts].reshape(rs, A_WIDTH), ob[:, :ts].reshape(rs, B_WIDTH),
                     w_o_b, vec(ln1_g), vec(ln1_b), rs)
    h = _mem_attn_ln(h.reshape(bs, ts, D_MODEL), cache_mem_k[l].reshape(bs, MEM_LEN, MEM_WIDTH).astype(BF16),
                     cache_mem_v[l].reshape(bs, MEM_LEN, MEM_WIDTH).astype(BF16),
                     w_mq_b, w_mo_b, vec(ln2_g), vec(ln2_b), ts)
    xs, s_tail = _ffn_ln(h.reshape(rs, D_MODEL), _hist8(state_ffn_conv[l]), *ffn_w, rs, 1, ts)
    sample_state = (
        ka_new.reshape(1, bs, ts, A_HEADS, HEAD_DIM), va_new.reshape(1, bs, ts, A_HEADS, HEAD_DIM),
        kb.reshape(1, bs, ts, B_KV_HEADS, HEAD_DIM), vb.reshape(1, bs, ts, B_KV_HEADS, HEAD_DIM),
        ki.reshape(1, bs, ts, IDX_DIM))

    return (xp.reshape(bp, tp, D_MODEL), xs.reshape(bs, ts, D_MODEL)) + prompt_state + sample_state + (s_tail[:, 8 - (CONV_W - 1):][None],)
```

```python
import functools
import math

import jax
import jax.numpy as jnp
from jax import lax
from jax.experimental import pallas as pl
from jax.experimental.pallas import tpu as pltpu

F32 = jnp.float32
BF16 = jnp.bfloat16
I32 = jnp.int32

D_MODEL = 1024
CHUNK = 64
N_PREV_CHUNKS = 8
HEAD_DIM = 64
A_HEADS = 8
A_WIDTH = A_HEADS * HEAD_DIM
A_MAX_REL = 64
B_HEADS = 8
B_KV_HEADS = 2
B_GROUP = B_HEADS // B_KV_HEADS
B_WIDTH = B_HEADS * HEAD_DIM
B_KV_WIDTH = B_KV_HEADS * HEAD_DIM
IDX_HEADS = 8
IDX_DIM = 64
TOPK_MAX = 256
N_BUCKETS = 32
T5_MAX_DIST = 128
MEM_LEN = 256
MEM_HEADS = 4
MEM_HEAD_DIM = 128
MEM_WIDTH = MEM_HEADS * MEM_HEAD_DIM
D_FF = 2816
CONV_W = 3
IN_SIZES = (A_WIDTH, A_WIDTH, A_WIDTH, B_WIDTH, B_KV_WIDTH, B_KV_WIDTH, IDX_HEADS * IDX_DIM, IDX_DIM, IDX_HEADS)
DEPTH = 1
ALPHA = (2 * DEPTH) ** 0.25
LN_EPS = 1e-5
ATTN_SCALE = HEAD_DIM ** -0.5
NEG = -1e30

LANES = 128
TILE = 256
BAND_TILES = 1 + (N_PREV_CHUNKS * CHUNK) // TILE
VMEM_LIMIT = 56 * 1024 * 1024

_C_QA, _C_KA, _C_VA, _C_QB = 0, 512, 1024, 1536
_C_KB, _C_VB, _C_QI, _C_KI, _C_WI = 2048, 2176, 2304, 2816, 2944
IN_PAD = 3072

INT_MIN = -2 ** 31
NEG_INF_KEY = -2139095041

_NT = (((1,), (1,)), ((), ()))


def _params(*sem):
    return pltpu.CompilerParams(dimension_semantics=sem, vmem_limit_bytes=VMEM_LIMIT)


def _layer_norm(z, g, b):
    mu = jnp.mean(z, axis=-1, keepdims=True)
    d = z - mu
    var = jnp.mean(d * d, axis=-1, keepdims=True)
    return d * lax.rsqrt(var + LN_EPS) * g + b


def _in_proj_kernel(x_ref, w_ref, qa_o, ka_o, va_o, qb_o, qi_o, kb_o, vb_o, ki_o, kbb_o, vbb_o, kib_o,
                    wi_o, kat_o, vat_o, *, tiles_per_batch, wi_scale):
    i = pl.program_id(0)
    xb = x_ref[...].astype(BF16)

    def mm(c0, n):
        return jnp.dot(xb, w_ref[:, c0:c0 + n], preferred_element_type=F32)

    qa_o[...] = mm(_C_QA, 512).astype(BF16)
    ka = mm(_C_KA, 512)
    va = mm(_C_VA, 512)
    ka_o[...] = ka.astype(BF16)
    va_o[...] = va.astype(BF16)
    qb_o[...] = mm(_C_QB, 512).astype(BF16)
    qi_o[...] = mm(_C_QI, 512).astype(BF16)
    kb = mm(_C_KB, 128)
    vb = mm(_C_VB, 128)
    kb_o[...] = kb
    vb_o[...] = vb
    kbb_o[...] = kb.astype(BF16)
    vbb_o[...] = vb.astype(BF16)
    ki = mm(_C_KI, 128)[:, :IDX_DIM]
    ki_o[...] = ki
    kib_o[...] = ki.astype(BF16)
    wi_o[...] = mm(_C_WI, 128) * wi_scale

    @pl.when(i % tiles_per_batch == tiles_per_batch - 1)
    def _():
        kat_o[...] = ka
        vat_o[...] = va


def _prep_w_in(w):
    parts, off = [], 0
    for n in IN_SIZES:
        parts.append(w[:, off:off + n])
        off += n
    qa, ka, va, qb, kb, vb, qi, ki, wi = parts
    pad = lambda a, n: jnp.pad(a, ((0, 0), (0, n - a.shape[1])))
    cols = [qa * ATTN_SCALE, ka, va, qb * ATTN_SCALE, kb, vb, qi * IDX_DIM ** -0.5, pad(ki, 128), pad(wi, 128)]
    return jnp.concatenate(cols, axis=1).astype(BF16)


def _in_proj(x2d, w_pad, tm, tiles_per_batch):
    r = x2d.shape[0]
    n_tiles = r // tm
    n_batch = n_tiles // tiles_per_batch
    row = lambda n: pl.BlockSpec((tm, n), lambda i: (i, 0))
    tail = pl.BlockSpec((tm, 512), lambda i: (i // tiles_per_batch, 0))
    sds = jax.ShapeDtypeStruct
    out_shape = [sds((r, 512), BF16)] * 5 + [sds((r, 128), F32), sds((r, 128), F32), sds((r, IDX_DIM), F32),
                                              sds((r, 128), BF16), sds((r, 128), BF16), sds((r, IDX_DIM), BF16),
                                              sds((r, 128), F32),
                                              sds((n_batch * tm, 512), F32), sds((n_batch * tm, 512), F32)]
    out_specs = [row(512)] * 5 + [row(128), row(128), row(IDX_DIM), row(128), row(128), row(IDX_DIM), row(128),
                                  tail, tail]
    return pl.pallas_call(
        functools.partial(_in_proj_kernel, tiles_per_batch=tiles_per_batch, wi_scale=IDX_HEADS ** -0.5),
        grid=(n_tiles,),
        in_specs=[pl.BlockSpec((tm, D_MODEL), lambda i: (i, 0)),
                  pl.BlockSpec((D_MODEL, IN_PAD), lambda i: (0, 0))],
        out_specs=out_specs,
        out_shape=out_shape,
        compiler_params=_params("arbitrary"),
        name="in_proj",
    )(x2d, w_pad)


def _hankel(g, rows, cols):
    h, n = g.shape
    return jnp.tile(g, (1, rows + 1))[:, :rows * (n + 1)].reshape(h, rows, n + 1)[:, :, :cols]


def _band_bias(table):
    r = jnp.arange(TILE)[:, None]
    c = jnp.arange(BAND_TILES * TILE)[None, :]
    n_cols = BAND_TILES * TILE
    rel = jnp.arange(TILE + n_cols - 1) - (TILE - 1)
    g = table[jnp.clip(rel, -A_MAX_REL, A_MAX_REL) + A_MAX_REL].T
    bias = _hankel(g, TILE, n_cols)[:, :, ::-1]
    ok = (c // CHUNK >= r // CHUNK) & (c // CHUNK <= r // CHUNK + N_PREV_CHUNKS)
    return jnp.where(ok[None], bias, NEG).astype(F32)


def _band_kernel(q_ref, k0, k1, k2, v0, v1, v2, bias_ref, o_ref, *, off, valid_len):
    i = pl.program_id(1)
    base = (i + off - (BAND_TILES - 1)) * TILE
    col = lax.broadcasted_iota(I32, (1, TILE), 1)
    valid = []
    for j in range(BAND_TILES):
        kpos = base + j * TILE + col
        valid.append((kpos >= 0) & (kpos < valid_len))
    krefs, vrefs = (k0, k1, k2), (v0, v1, v2)
    outs = []
    for h in range(A_HEADS):
        sl = slice(h * HEAD_DIM, (h + 1) * HEAD_DIM)
        qh = q_ref[0, :, sl]
        s = []
        for j in range(BAND_TILES):
            sj = lax.dot_general(qh, krefs[j][0, :, sl], _NT, preferred_element_type=F32)
            sj = sj + bias_ref[h, :, j * TILE:(j + 1) * TILE]
            s.append(jnp.where(valid[j], sj, NEG))
        m = jnp.maximum(jnp.maximum(s[0].max(-1, keepdims=True), s[1].max(-1, keepdims=True)),
                        s[2].max(-1, keepdims=True))
        p = [jnp.exp(sj - m) for sj in s]
        l = p[0].sum(-1, keepdims=True) + p[1].sum(-1, keepdims=True) + p[2].sum(-1, keepdims=True)
        o = jnp.dot(p[0].astype(BF16), vrefs[0][0, :, sl], preferred_element_type=F32)
        o = o + jnp.dot(p[1].astype(BF16), vrefs[1][0, :, sl], preferred_element_type=F32)
        o = o + jnp.dot(p[2].astype(BF16), vrefs[2][0, :, sl], preferred_element_type=F32)
        outs.append(o / l)
    o_ref[0] = jnp.concatenate(outs, axis=-1).astype(BF16)


def _band_attn(q, k, v, bias, off, valid_len):
    b, tq = q.shape[:2]
    nq = tq // TILE
    qspec = pl.BlockSpec((1, TILE, A_WIDTH), lambda bb, i: (bb, i, 0))
    kspec = lambda d: pl.BlockSpec((1, TILE, A_WIDTH), lambda bb, i: (bb, jnp.maximum(i + off - d, 0), 0))
    return pl.pallas_call(
        functools.partial(_band_kernel, off=off, valid_len=valid_len),
        grid=(b, nq),
        in_specs=[qspec, kspec(2), kspec(1), kspec(0), kspec(2), kspec(1), kspec(0),
                  pl.BlockSpec((A_HEADS, TILE, BAND_TILES * TILE), lambda bb, i: (0, 0, 0))],
        out_specs=pl.BlockSpec((1, TILE, A_WIDTH), lambda bb, i: (bb, i, 0)),
        out_shape=jax.ShapeDtypeStruct((b, tq, A_WIDTH), BF16),
        compiler_params=_params("arbitrary", "arbitrary"),
        name="band_attn",
    )(q, k, k, k, v, v, v, bias)


def _t5_bucket(rel):
    half = N_BUCKETS // 2
    max_exact = half // 2
    n = jnp.abs(rel)
    log_ratio = jnp.log(jnp.maximum(n, 1).astype(jnp.float32) / max_exact) / math.log(T5_MAX_DIST / max_exact)
    large = jnp.minimum(max_exact + (log_ratio * (half - max_exact)).astype(jnp.int32), half - 1)
    return jnp.where(rel < 0, half, 0) + jnp.where(n < max_exact, n, large)


def _dsa_bias(t5_table):
    rel = jnp.arange(3 * TILE - 1) - (TILE - 1)
    far = t5_table[_t5_bucket(jnp.full((1,), 2 * TILE + 1, I32))]
    g = (t5_table[_t5_bucket(rel)] - far).T
    return _hankel(g, 2 * TILE, TILE)[:, ::-1, :].astype(F32)


def _dsa_kernel(qi_ref, wi_ref, qb_ref, ki_ref, kb_ref, vb_ref, bias_ref, o_ref,
                key_ref, vt_ref, m_ref, acc_ref, *, off, valid_len, q_valid, n_kt):
    i = pl.program_id(1)
    qt = i + off
    q0 = qt * TILE
    nk = qt + 1

    @pl.when(i == 0)
    def _():
        ones = jnp.ones((HEAD_DIM, TILE), BF16)

        def body(j, c):
            vt = vb_ref[0, pl.ds(pl.multiple_of(j * TILE, TILE), TILE), :].astype(F32).T
            for g in range(B_KV_HEADS):
                vt_ref[g, j, 0:HEAD_DIM, :] = vt[g * HEAD_DIM:(g + 1) * HEAD_DIM].astype(BF16)
                vt_ref[g, j, HEAD_DIM:2 * HEAD_DIM, :] = ones
            return c

        lax.fori_loop(0, n_kt, body, 0)

    colq = lax.broadcasted_iota(I32, (1, TILE), 1)
    rowk = lax.broadcasted_iota(I32, (TILE, 1), 0)

    qi = qi_ref[0]
    qi_stack = jnp.concatenate([qi[:, h * IDX_DIM:(h + 1) * IDX_DIM] for h in range(IDX_HEADS)], axis=0)
    wi_t = wi_ref[0].T
    lim = jnp.minimum(q0 + (colq // CHUNK + 1) * CHUNK, valid_len)

    def score_tile(j, c):
        kt = ki_ref[0, pl.ds(pl.multiple_of(j * TILE, TILE), TILE), :]
        lg = lax.dot_general(kt, qi_stack, _NT, preferred_element_type=F32)
        sc = wi_t[0:1, :] * jnp.maximum(lg[:, 0:TILE], 0.0)
        for h in range(1, IDX_HEADS):
            sc = sc + wi_t[h:h + 1, :] * jnp.maximum(lg[:, h * TILE:(h + 1) * TILE], 0.0)
        sc = jnp.where(j * TILE + rowk < lim, sc, -jnp.inf)
        bits = pltpu.bitcast(sc, I32)
        key_ref[j] = jnp.where(bits < 0, bits ^ 0x7FFFFFFF, bits)
        return c

    lax.fori_loop(0, nk, score_tile, 0)

    def count_ge(cand):
        def body(j, acc):
            ge = jnp.where(key_ref[j] >= cand, 1, 0)
            return acc + ge.reshape(TILE // 8, 8, TILE).sum(axis=0)
        acc = lax.fori_loop(0, nk, body, jnp.zeros((8, TILE), I32))
        return acc.sum(axis=0, keepdims=True)

    def bis_cond(st):
        bit, _, _, n_open = st
        return (bit >= 0) & (n_open > 0)

    def bis_body(st):
        bit, t, done, _ = st
        cand = t + lax.shift_left(jnp.int32(1), bit)
        c = count_ge(cand)
        t = jnp.where((c >= TOPK_MAX) & (done == 0), cand, t)
        done = jnp.where(c == TOPK_MAX, 1, done)
        return bit - 1, t, done, jnp.sum(1 - done)

    done0 = jnp.where(colq < q_valid, 0, 1)
    _, thr, done, n_open = lax.while_loop(
        bis_cond, bis_body, (jnp.int32(31), jnp.full((1, TILE), INT_MIN, I32), done0, jnp.sum(1 - done0)))

    @pl.when(n_open > 0)
    def _():
        n_gt = count_ge(thr + 1)
        keep = jnp.where(done == 1, 2 ** 30, TOPK_MAX - n_gt)
        lower = (lax.broadcasted_iota(I32, (TILE, TILE), 0) > lax.broadcasted_iota(I32, (TILE, TILE), 1))
        lower = jnp.where(lower, 1.0, 0.0).astype(BF16)

        def body(j, run):
            blk = key_ref[j]
            eq = blk == thr
            eq_f = jnp.where(eq, 1.0, 0.0)
            before = jnp.dot(lower, eq_f.astype(BF16), preferred_element_type=F32)
            rank = run + before.astype(I32)
            key_ref[j] = jnp.where(eq & (rank >= keep), NEG_INF_KEY, blk)
            return run + eq_f.reshape(TILE // 8, 8, TILE).sum(axis=0).sum(axis=0, keepdims=True).astype(I32)

        lax.fori_loop(0, nk, body, jnp.zeros((1, TILE), I32))

    thr = jnp.maximum(thr, NEG_INF_KEY + 1)

    qb = qb_ref[0]
    outs = []
    for g in range(B_KV_HEADS):
        heads = range(g * B_GROUP, (g + 1) * B_GROUP)
        q_stack = jnp.concatenate([qb[:, h * HEAD_DIM:(h + 1) * HEAD_DIM] for h in heads], axis=0)
        m_ref[...] = jnp.full(m_ref.shape, NEG, F32)
        acc_ref[...] = jnp.zeros(acc_ref.shape, F32)

        def tile(j, near):
            kt = kb_ref[0, pl.ds(pl.multiple_of(j * TILE, TILE), TILE), g * HEAD_DIM:(g + 1) * HEAD_DIM]
            st = lax.dot_general(kt, q_stack, _NT, preferred_element_type=F32)
            sel = key_ref[j] >= thr
            parts = []
            for hh, h in enumerate(heads):
                s_h = st[:, hh * TILE:(hh + 1) * TILE]
                if near is not None:
                    s_h = s_h + bias_ref[h, near * TILE:(near + 1) * TILE, :]
                parts.append(jnp.where(sel, s_h, -jnp.inf))
            s = jnp.concatenate(parts, axis=1)
            m_old = m_ref[...]
            m_new = jnp.maximum(m_old, s.max(axis=0, keepdims=True))
            p = jnp.exp(s - m_new).astype(BF16)
            pv = jnp.dot(vt_ref[g, j], p, preferred_element_type=F32)
            acc_ref[...] = jnp.exp(m_old - m_new) * acc_ref[...] + pv
            m_ref[...] = m_new

        def far_tile(j, c):
            tile(j, None)
            return c

        lax.fori_loop(0, nk - 2, far_tile, 0)

        @pl.when(nk >= 2)
        def _():
            tile(nk - 2, 0)

        tile(nk - 1, 1)

        for hh in range(B_GROUP):
            blk = acc_ref[:, hh * TILE:(hh + 1) * TILE].T
            outs.append(blk[:, 0:HEAD_DIM] / blk[:, HEAD_DIM:HEAD_DIM + 1])
    o_ref[0] = jnp.concatenate(outs, axis=-1).astype(BF16)


def _dsa_attn(qi, wi, qb, ki, kb, vb, bias, off, valid_len, q_valid):
    b, tq = qi.shape[:2]
    tk = ki.shape[1]
    nq, n_kt = tq // TILE, tk // TILE
    qspec = lambda n: pl.BlockSpec((1, TILE, n), lambda bb, i: (bb, i, 0))
    kspec = lambda n: pl.BlockSpec((1, tk, n), lambda bb, i: (bb, 0, 0))
    return pl.pallas_call(
        functools.partial(_dsa_kernel, off=off, valid_len=valid_len, q_valid=q_valid, n_kt=n_kt),
        grid=(b, nq),
        in_specs=[qspec(512), qspec(128), qspec(512), kspec(IDX_DIM), kspec(128), kspec(128),
                  pl.BlockSpec((B_HEADS, 2 * TILE, TILE), lambda bb, i: (0, 0, 0))],
        out_specs=pl.BlockSpec((1, TILE, B_WIDTH), lambda bb, i: (bb, i, 0)),
        out_shape=jax.ShapeDtypeStruct((b, tq, B_WIDTH), BF16),
        scratch_shapes=[pltpu.VMEM((n_kt, TILE, TILE), I32),
                        pltpu.VMEM((B_KV_HEADS, n_kt, 2 * HEAD_DIM, TILE), BF16),
                        pltpu.VMEM((1, B_GROUP * TILE), F32),
                        pltpu.VMEM((2 * HEAD_DIM, B_GROUP * TILE), F32)],
        compiler_params=_params("arbitrary", "arbitrary"),
        name="dsa_attn",
    )(qi, wi, qb, ki, kb, vb, bias)


def _out_proj_kernel(x_ref, oa_ref, ob_ref, w_ref, g_ref, b_ref, o_ref):
    mix = jnp.concatenate([oa_ref[...], ob_ref[...]], axis=-1)
    y = jnp.dot(mix, w_ref[...], preferred_element_type=F32)
    o_ref[...] = _layer_norm(ALPHA * x_ref[...] + y, g_ref[...], b_ref[...])


def _out_proj_ln(x2d, oa, ob, w, g, b, tm):
    r = x2d.shape[0]
    row = lambda n: pl.BlockSpec((tm, n), lambda i: (i, 0))
    const = lambda s: pl.BlockSpec(s, lambda i: (0, 0))
    return pl.pallas_call(
        _out_proj_kernel,
        grid=(r // tm,),
        in_specs=[row(D_MODEL), row(A_WIDTH), row(B_WIDTH), const((A_WIDTH + B_WIDTH, D_MODEL)),
                  const((1, D_MODEL)), const((1, D_MODEL))],
        out_specs=row(D_MODEL),
        out_shape=jax.ShapeDtypeStruct((r, D_MODEL), F32),
        compiler_params=_params("arbitrary"),
        name="out_proj_ln",
    )(x2d, oa, ob, w, g, b)


def _mem_kv_kernel(m_ref, wk_ref, wv_ref, k_o, v_o, kb_o, vb_o):
    mb = m_ref[...].astype(BF16)
    k = jnp.dot(mb, wk_ref[...], preferred_element_type=F32)
    v = jnp.dot(mb, wv_ref[...], preferred_element_type=F32)
    k_o[...] = k
    v_o[...] = v
    kb_o[...] = k.astype(BF16)
    vb_o[...] = v.astype(BF16)


def _mem_kv(mem2d, wk, wv):
    r = mem2d.shape[0]
    tm = MEM_LEN
    row = lambda n: pl.BlockSpec((tm, n), lambda i: (i, 0))
    const = lambda s: pl.BlockSpec(s, lambda i: (0, 0))
    sds = jax.ShapeDtypeStruct
    return pl.pallas_call(
        _mem_kv_kernel,
        grid=(r // tm,),
        in_specs=[row(D_MODEL), const((D_MODEL, MEM_WIDTH)), const((D_MODEL, MEM_WIDTH))],
        out_specs=[row(MEM_WIDTH)] * 4,
        out_shape=[sds((r, MEM_WIDTH), F32), sds((r, MEM_WIDTH), F32),
                   sds((r, MEM_WIDTH), BF16), sds((r, MEM_WIDTH), BF16)],
        compiler_params=_params("arbitrary"),
        name="mem_kv",
    )(mem2d, wk, wv)


def _mem_attn_kernel(h_ref, mk_ref, mv_ref, wq_ref, wo_ref, g_ref, b_ref, o_ref):
    h = h_ref[0]
    q = jnp.dot(h.astype(BF16), wq_ref[...], preferred_element_type=F32).astype(BF16)
    outs = []
    for hd in range(MEM_HEADS):
        sl = slice(hd * MEM_HEAD_DIM, (hd + 1) * MEM_HEAD_DIM)
        s = lax.dot_general(q[:, sl], mk_ref[0, :, sl], _NT, preferred_element_type=F32) * MEM_HEAD_DIM ** -0.5
        p = jnp.exp(s - s.max(-1, keepdims=True))
        l = p.sum(-1, keepdims=True)
        outs.append(jnp.dot(p.astype(BF16), mv_ref[0, :, sl], preferred_element_type=F32) / l)
    o = jnp.concatenate(outs, axis=-1).astype(BF16)
    y = jnp.dot(o, wo_ref[...], preferred_element_type=F32)
    o_ref[0] = _layer_norm(ALPHA * h + y, g_ref[...], b_ref[...])


def _mem_attn_ln(h3d, mk, mv, wq, wo, g, b, tm):
    bsz, t = h3d.shape[:2]
    const2 = lambda s: pl.BlockSpec(s, lambda bb, i: (0, 0))
    return pl.pallas_call(
        _mem_attn_kernel,
        grid=(bsz, t // tm),
        in_specs=[pl.BlockSpec((1, tm, D_MODEL), lambda bb, i: (bb, i, 0)),
                  pl.BlockSpec((1, MEM_LEN, MEM_WIDTH), lambda bb, i: (bb, 0, 0)),
                  pl.BlockSpec((1, MEM_LEN, MEM_WIDTH), lambda bb, i: (bb, 0, 0)),
                  const2((D_MODEL, MEM_WIDTH)), const2((MEM_WIDTH, D_MODEL)),
                  const2((1, D_MODEL)), const2((1, D_MODEL))],
        out_specs=pl.BlockSpec((1, tm, D_MODEL), lambda bb, i: (bb, i, 0)),
        out_shape=jax.ShapeDtypeStruct(h3d.shape, F32),
        compiler_params=_params("arbitrary", "arbitrary"),
        name="mem_attn_ln",
    )(h3d, mk, mv, wq, wo, g, b)


FF_CHUNK = 256


def _ffn_kernel(h_ref, hist_ref, wu_ref, wc_ref, bc_ref, wd_ref, g_ref, b_ref, o_ref, tail_ref,
                carry_ref, f_ref, *, tiles_per_batch, seg):
    i = pl.program_id(0)
    tm = h_ref.shape[0]
    nseg = tm // seg
    h = h_ref[...]
    hb = h.astype(BF16)
    row = lax.broadcasted_iota(I32, (tm, 1), 0)
    first = (i % tiles_per_batch) == 0
    f_ref[...] = jnp.zeros(f_ref.shape, F32)
    for c in range(D_FF // FF_CHUNK):
        cs = slice(c * FF_CHUNK, (c + 1) * FF_CHUNK)
        u = jnp.dot(hb, wu_ref[:, cs], preferred_element_type=F32)
        gt = jnp.dot(hb, wu_ref[:, D_FF + c * FF_CHUNK:D_FF + (c + 1) * FF_CHUNK], preferred_element_type=F32)
        p1 = pltpu.roll(gt, 1, 0)
        p2 = pltpu.roll(gt, 2, 0)
        for s in range(nseg):
            hist = hist_ref[s, :, cs]
            if tiles_per_batch > 1:
                hist = jnp.where(first, hist, carry_ref[:, cs])
            p1 = jnp.where(row == s * seg, hist[7:8, :], p1)
            p2 = jnp.where(row == s * seg, hist[6:7, :], p2)
            p2 = jnp.where(row == s * seg + 1, hist[7:8, :], p2)
        gc = bc_ref[:, cs] + ((wc_ref[0:1, cs] * p2 + wc_ref[1:2, cs] * p1) + wc_ref[2:3, cs] * gt)
        act = (u * jax.nn.gelu(gc)).astype(BF16)
        f_ref[...] += jnp.dot(act, wd_ref[cs, :], preferred_element_type=F32)
        for s in range(nseg):
            tail_ref[s, :, cs] = gt[(s + 1) * seg - 8:(s + 1) * seg, :]
        carry_ref[:, cs] = gt[tm - 8:tm, :]
    o_ref[...] = _layer_norm(ALPHA * h + f_ref[...], g_ref[...], b_ref[...])


def _ffn_ln(h2d, hist, wu, wc, bc, wd, g, b, tm, tiles_per_batch, seg):
    r = h2d.shape[0]
    nseg = tm // seg
    n_tail = r // (tm * tiles_per_batch) * nseg
    row = lambda n: pl.BlockSpec((tm, n), lambda i: (i, 0))
    const = lambda s: pl.BlockSpec(s, lambda i: (0,) * len(s))
    return pl.pallas_call(
        functools.partial(_ffn_kernel, tiles_per_batch=tiles_per_batch, seg=seg),
        grid=(r // tm,),
        in_specs=[row(D_MODEL),
                  pl.BlockSpec((nseg, 8, D_FF), lambda i: (i // tiles_per_batch, 0, 0)),
                  const((D_MODEL, 2 * D_FF)), const((CONV_W, D_FF)), const((1, D_FF)), const((D_FF, D_MODEL)),
                  const((1, D_MODEL)), const((1, D_MODEL))],
        out_specs=[row(D_MODEL), pl.BlockSpec((nseg, 8, D_FF), lambda i: (i // tiles_per_batch, 0, 0))],
        out_shape=[jax.ShapeDtypeStruct((r, D_MODEL), F32), jax.ShapeDtypeStruct((n_tail, 8, D_FF), F32)],
        scratch_shapes=[pltpu.VMEM((8, D_FF), F32), pltpu.VMEM((tm, D_MODEL), F32)],
        compiler_params=_params("arbitrary"),
        name="ffn_ln",
    )(h2d, hist, wu, wc, bc, wd, g, b)


def _pad_rows(a, n):
    return jnp.pad(a, ((0, 0), (0, n - a.shape[1])) + ((0, 0),) * (a.ndim - 2))


def _hist8(g_hist):
    return jnp.pad(g_hist, ((0, 0), (8 - g_hist.shape[1], 0), (0, 0)))


def kernel(x_prompt, x_sample, cache_a_k, cache_a_v, cache_b_k, cache_b_v, cache_b_kidx, cache_mem_k, cache_mem_v, state_ffn_conv, mem_prompt, w_in, a_rel_bias, t5_bias, w_o, ln1_g, ln1_b, w_mq, w_mk, w_mv, w_mo, ln2_g, ln2_b, w_up, w_conv, b_conv, w_down, ln3_g, ln3_b):
    bp, tp = x_prompt.shape[:2]
    bs, ts = x_sample.shape[:2]
    l = 0
    vec = lambda a: a[l].reshape(1, -1)
    w_in_p = _prep_w_in(w_in[l])
    w_o_b = w_o[l].astype(BF16)
    w_mq_b, w_mk_b, w_mv_b, w_mo_b = (w[l].astype(BF16) for w in (w_mq, w_mk, w_mv, w_mo))
    w_up_b, w_down_b = w_up[l].astype(BF16), w_down[l].astype(BF16)
    band_bias = _band_bias(a_rel_bias[l])
    dsa_bias = _dsa_bias(t5_bias)
    ffn_w = (w_up_b, w_conv[l], vec(b_conv), w_down_b, vec(ln3_g), vec(ln3_b))

    tm = 512
    a_keep = min(N_PREV_CHUNKS * CHUNK, tp)
    (qa, ka, va, qb, qi, kb, vb, ki, kb_b, vb_b, ki_b, wi, ka_tail, va_tail) = _in_proj(
        x_prompt.reshape(bp * tp, D_MODEL), w_in_p, tm, tp // tm)
    r3 = lambda a: a.reshape(bp, tp, a.shape[-1])
    oa = _band_attn(r3(qa), r3(ka), r3(va), band_bias, 0, tp)
    ob = _dsa_attn(r3(qi), r3(wi), r3(qb), r3(ki_b), r3(kb_b), r3(vb_b), dsa_bias, 0, tp, TILE)
    h = _out_proj_ln(x_prompt.reshape(bp * tp, D_MODEL), oa.reshape(bp * tp, A_WIDTH), ob.reshape(bp * tp, B_WIDTH),
                     w_o_b, vec(ln1_g), vec(ln1_b), tm)
    mk, mv, mk_b, mv_b = _mem_kv(mem_prompt.reshape(bp * MEM_LEN, D_MODEL), w_mk_b, w_mv_b)
    h = _mem_attn_ln(h.reshape(bp, tp, D_MODEL), mk_b.reshape(bp, MEM_LEN, MEM_WIDTH),
                     mv_b.reshape(bp, MEM_LEN, MEM_WIDTH), w_mq_b, w_mo_b, vec(ln2_g), vec(ln2_b), TILE)
    xp, p_tail = _ffn_ln(h.reshape(bp * tp, D_MODEL), jnp.zeros((bp, 8, D_FF), F32), *ffn_w, tm, tp // tm, tm)
    prompt_state = (
        ka_tail.reshape(bp, tm, A_HEADS, HEAD_DIM)[:, tm - a_keep:][None],
        va_tail.reshape(bp, tm, A_HEADS, HEAD_DIM)[:, tm - a_keep:][None],
        kb.reshape(1, bp, tp, B_KV_HEADS, HEAD_DIM), vb.reshape(1, bp, tp, B_KV_HEADS, HEAD_DIM),
        ki.reshape(1, bp, tp, IDX_DIM),
        mk.reshape(1, bp, MEM_LEN, MEM_HEADS, MEM_HEAD_DIM), mv.reshape(1, bp, MEM_LEN, MEM_HEADS, MEM_HEAD_DIM),
        p_tail[:, 8 - (CONV_W - 1):][None])

    rs = bs * ts
    (qa, ka, va, qb, qi, kb, vb, ki, kb_b, vb_b, ki_b, wi, ka_new, va_new) = _in_proj(
        x_sample.reshape(rs, D_MODEL), w_in_p, rs, 1)
    s3 = lambda a: a.reshape(bs, ts, a.shape[-1])
    qpad = lambda a: _pad_rows(s3(a), TILE)

    past_a = cache_a_k.shape[2]
    n_a = past_a + ts
    t_a = -(-n_a // TILE) * TILE
    seq_a = lambda cache, new: _pad_rows(
        jnp.concatenate([cache[l].reshape(bs, past_a, A_WIDTH), s3(new)], axis=1), t_a).astype(BF16)
    oa = _band_attn(qpad(qa), seq_a(cache_a_k, ka_new), seq_a(cache_a_v, va_new), band_bias, past_a // TILE, n_a)

    past_b = cache_b_k.shape[2]
    n_b = past_b + ts
    t_b = -(-n_b // TILE) * TILE
    seq_b = lambda cache, new: _pad_rows(
        jnp.concatenate([cache[l].reshape(bs, past_b, -1), s3(new)], axis=1), t_b).astype(BF16)
    ob = _dsa_attn(qpad(qi), qpad(wi), qpad(qb), seq_b(cache_b_kidx, ki), seq_b(cache_b_k, kb),
                   seq_b(cache_b_v, vb), dsa_bias, past_b // TILE, n_b, ts)

    h = _out_proj_ln(x_sample.reshape(rs, D_MODEL), oa[:, :ts].reshape(rs, A_WIDTH), ob[:, :ts].reshape(rs, B_WIDTH),
                     w_o_b, vec(ln1_g), vec(ln1_b), rs)
    h = _mem_attn_ln(h.reshape(bs, ts, D_MODEL), cache_mem_k[l].reshape(bs, MEM_LEN, MEM_WIDTH).astype(BF16),
                     cache_mem_v[l].reshape(bs, MEM_LEN, MEM_WIDTH).astype(BF16),
                     w_mq_b, w_mo_b, vec(ln2_g), vec(ln2_b), ts)
    xs, s_tail = _ffn_ln(h.reshape(rs, D_MODEL), _hist8(state_ffn_conv[l]), *ffn_w, rs, 1, ts)
    sample_state = (
        ka_new.reshape(1, bs, ts, A_HEADS, HEAD_DIM), va_new.reshape(1, bs, ts, A_HEADS, HEAD_DIM),
        kb.reshape(1, bs, ts, B_KV_HEADS, HEAD_DIM), vb.reshape(1, bs, ts, B_KV_HEADS, HEAD_DIM),
        ki.reshape(1, bs, ts, IDX_DIM))

    return (xp.reshape(bp, tp, D_MODEL), xs.reshape(bs, ts, D_MODEL)) + prompt_state + sample_state + (s_tail[:, 8 - (CONV_W - 1):][None],)
```

```python
import functools
import math

import jax
import jax.numpy as jnp
from jax import lax
from jax.experimental import pallas as pl
from jax.experimental.pallas import tpu as pltpu

F32 = jnp.float32
BF16 = jnp.bfloat16
I32 = jnp.int32
I16 = jnp.int16

D_MODEL = 1024
CHUNK = 64
N_PREV_CHUNKS = 8
HEAD_DIM = 64
A_HEADS = 8
A_WIDTH = A_HEADS * HEAD_DIM
A_MAX_REL = 64
B_HEADS = 8
B_KV_HEADS = 2
B_GROUP = B_HEADS // B_KV_HEADS
B_WIDTH = B_HEADS * HEAD_DIM
B_KV_WIDTH = B_KV_HEADS * HEAD_DIM
IDX_HEADS = 8
IDX_DIM = 64
TOPK_MAX = 256
N_BUCKETS = 32
T5_MAX_DIST = 128
MEM_LEN = 256
MEM_HEADS = 4
MEM_HEAD_DIM = 128
MEM_WIDTH = MEM_HEADS * MEM_HEAD_DIM
D_FF = 2816
CONV_W = 3
IN_SIZES = (A_WIDTH, A_WIDTH, A_WIDTH, B_WIDTH, B_KV_WIDTH, B_KV_WIDTH, IDX_HEADS * IDX_DIM, IDX_DIM, IDX_HEADS)
DEPTH = 1
ALPHA = (2 * DEPTH) ** 0.25
LN_EPS = 1e-5
ATTN_SCALE = HEAD_DIM ** -0.5
NEG = -1e30

LANES = 128
TILE = 256
BAND_TILES = 1 + (N_PREV_CHUNKS * CHUNK) // TILE
VMEM_LIMIT = 56 * 1024 * 1024

_C_QA, _C_KA, _C_VA, _C_QB = 0, 512, 1024, 1536
_C_KB, _C_VB, _C_QI, _C_KI, _C_WI = 2048, 2176, 2304, 2816, 2944
IN_PAD = 3072

NEG_INF_KEY = -2139095041
HI_NEG_INF = NEG_INF_KEY >> 16
MIN16, MAX16 = -32768, 32767

_NT = (((1,), (1,)), ((), ()))


def _params(*sem):
    return pltpu.CompilerParams(dimension_semantics=sem, vmem_limit_bytes=VMEM_LIMIT)


def _layer_norm(z, g, b):
    mu = jnp.mean(z, axis=-1, keepdims=True)
    d = z - mu
    var = jnp.mean(d * d, axis=-1, keepdims=True)
    return d * lax.rsqrt(var + LN_EPS) * g + b


def _toeplitz(g_row, rows, width):
    return pltpu.roll(jnp.broadcast_to(g_row, (rows, width)), 0, 1, stride=1, stride_axis=0)


def _in_proj_kernel(x_ref, w_ref, qa_o, ka_o, va_o, qb_o, qi_o, kb_o, vb_o, ki_o, kbb_o, vbb_o, kib_o,
                    wi_o, kat_o, vat_o, *, tiles_per_batch, wi_scale):
    i = pl.program_id(0)
    xb = x_ref[...].astype(BF16)

    def mm(c0, n):
        return jnp.dot(xb, w_ref[:, c0:c0 + n], preferred_element_type=F32)

    qa_o[...] = mm(_C_QA, 512).astype(BF16)
    ka = mm(_C_KA, 512)
    va = mm(_C_VA, 512)
    ka_o[...] = ka.astype(BF16)
    va_o[...] = va.astype(BF16)
    qb_o[...] = mm(_C_QB, 512).astype(BF16)
    qi_o[...] = mm(_C_QI, 512).astype(BF16)
    kb = mm(_C_KB, 128)
    vb = mm(_C_VB, 128)
    kb_o[...] = kb
    vb_o[...] = vb
    kbb_o[...] = kb.astype(BF16)
    vbb_o[...] = vb.astype(BF16)
    ki = mm(_C_KI, 128)[:, :IDX_DIM]
    ki_o[...] = ki
    kib_o[...] = ki.astype(BF16)
    wi_o[...] = mm(_C_WI, 128) * wi_scale

    @pl.when(i % tiles_per_batch == tiles_per_batch - 1)
    def _():
        kat_o[...] = ka
        vat_o[...] = va


def _prep_w_in(w):
    parts, off = [], 0
    for n in IN_SIZES:
        parts.append(w[:, off:off + n])
        off += n
    qa, ka, va, qb, kb, vb, qi, ki, wi = parts
    pad = lambda a, n: jnp.pad(a, ((0, 0), (0, n - a.shape[1])))
    cols = [qa * ATTN_SCALE, ka, va, qb * ATTN_SCALE, kb, vb, qi * IDX_DIM ** -0.5, pad(ki, 128), pad(wi, 128)]
    return jnp.concatenate(cols, axis=1).astype(BF16)


def _in_proj(x2d, w_pad, tm, tiles_per_batch):
    r = x2d.shape[0]
    n_tiles = r // tm
    n_batch = n_tiles // tiles_per_batch
    row = lambda n: pl.BlockSpec((tm, n), lambda i: (i, 0))
    tail = pl.BlockSpec((tm, 512), lambda i: (i // tiles_per_batch, 0))
    sds = jax.ShapeDtypeStruct
    out_shape = [sds((r, 512), BF16)] * 5 + [sds((r, 128), F32), sds((r, 128), F32), sds((r, IDX_DIM), F32),
                                              sds((r, 128), BF16), sds((r, 128), BF16), sds((r, IDX_DIM), BF16),
                                              sds((r, 128), F32),
                                              sds((n_batch * tm, 512), F32), sds((n_batch * tm, 512), F32)]
    out_specs = [row(512)] * 5 + [row(128), row(128), row(IDX_DIM), row(128), row(128), row(IDX_DIM), row(128),
                                  tail, tail]
    return pl.pallas_call(
        functools.partial(_in_proj_kernel, tiles_per_batch=tiles_per_batch, wi_scale=IDX_HEADS ** -0.5),
        grid=(n_tiles,),
        in_specs=[pl.BlockSpec((tm, D_MODEL), lambda i: (i, 0)),
                  pl.BlockSpec((D_MODEL, IN_PAD), lambda i: (0, 0))],
        out_specs=out_specs,
        out_shape=out_shape,
        compiler_params=_params("arbitrary"),
        name="in_proj",
    )(x2d, w_pad)


BAND_COLS = BAND_TILES * TILE
BAND_ROLL = BAND_COLS + TILE


def _band_bias_row(table):
    idx = jnp.arange(BAND_ROLL)
    d = jnp.where(idx < BAND_COLS, idx, idx - BAND_ROLL)
    rel = (BAND_TILES - 1) * TILE - d
    return table[jnp.clip(rel, -A_MAX_REL, A_MAX_REL) + A_MAX_REL].T.astype(F32)


def _band_kernel(g_ref, q_ref, k0, k1, k2, v0, v1, v2, o_ref, bias_ref, *, off, valid_len):
    i = pl.program_id(1)

    @pl.when((pl.program_id(0) == 0) & (i == 0))
    def _():
        r = lax.broadcasted_iota(I32, (TILE, BAND_COLS), 0) // CHUNK
        c = lax.broadcasted_iota(I32, (TILE, BAND_COLS), 1) // CHUNK
        ok = (c >= r) & (c <= r + N_PREV_CHUNKS)
        for h in range(A_HEADS):
            bias_ref[h] = jnp.where(ok, _toeplitz(g_ref[h:h + 1, :], TILE, BAND_ROLL)[:, :BAND_COLS], NEG)

    base = (i + off - (BAND_TILES - 1)) * TILE
    col = lax.broadcasted_iota(I32, (1, TILE), 1)
    valid = []
    for j in range(BAND_TILES):
        kpos = base + j * TILE + col
        valid.append((kpos >= 0) & (kpos < valid_len))
    krefs, vrefs = (k0, k1, k2), (v0, v1, v2)
    outs = []
    for h in range(A_HEADS):
        sl = slice(h * HEAD_DIM, (h + 1) * HEAD_DIM)
        qh = q_ref[0, :, sl]
        s = []
        for j in range(BAND_TILES):
            sj = lax.dot_general(qh, krefs[j][0, :, sl], _NT, preferred_element_type=F32)
            sj = sj + bias_ref[h, :, j * TILE:(j + 1) * TILE]
            s.append(jnp.where(valid[j], sj, NEG))
        m = jnp.maximum(jnp.maximum(s[0].max(-1, keepdims=True), s[1].max(-1, keepdims=True)),
                        s[2].max(-1, keepdims=True))
        p = [jnp.exp(sj - m) for sj in s]
        l = p[0].sum(-1, keepdims=True) + p[1].sum(-1, keepdims=True) + p[2].sum(-1, keepdims=True)
        o = jnp.dot(p[0].astype(BF16), vrefs[0][0, :, sl], preferred_element_type=F32)
        o = o + jnp.dot(p[1].astype(BF16), vrefs[1][0, :, sl], preferred_element_type=F32)
        o = o + jnp.dot(p[2].astype(BF16), vrefs[2][0, :, sl], preferred_element_type=F32)
        outs.append(o / l)
    o_ref[0] = jnp.concatenate(outs, axis=-1).astype(BF16)


def _band_attn(q, k, v, g_row, off, valid_len):
    b, tq = q.shape[:2]
    nq = tq // TILE
    qspec = pl.BlockSpec((1, TILE, A_WIDTH), lambda bb, i: (bb, i, 0))
    kspec = lambda d: pl.BlockSpec((1, TILE, A_WIDTH), lambda bb, i: (bb, jnp.maximum(i + off - d, 0), 0))
    return pl.pallas_call(
        functools.partial(_band_kernel, off=off, valid_len=valid_len),
        grid=(b, nq),
        in_specs=[pl.BlockSpec((A_HEADS, BAND_ROLL), lambda bb, i: (0, 0)),
                  qspec, kspec(2), kspec(1), kspec(0), kspec(2), kspec(1), kspec(0)],
        out_specs=pl.BlockSpec((1, TILE, A_WIDTH), lambda bb, i: (bb, i, 0)),
        out_shape=jax.ShapeDtypeStruct((b, tq, A_WIDTH), BF16),
        scratch_shapes=[pltpu.VMEM((A_HEADS, TILE, BAND_COLS), F32)],
        compiler_params=_params("arbitrary", "arbitrary"),
        name="band_attn",
    )(g_row, q, k, k, k, v, v, v)


DSA_ROLL = 3 * TILE


def _t5_bucket(rel):
    half = N_BUCKETS // 2
    max_exact = half // 2
    n = jnp.abs(rel)
    log_ratio = jnp.log(jnp.maximum(n, 1).astype(jnp.float32) / max_exact) / math.log(T5_MAX_DIST / max_exact)
    large = jnp.minimum(max_exact + (log_ratio * (half - max_exact)).astype(jnp.int32), half - 1)
    return jnp.where(rel < 0, half, 0) + jnp.where(n < max_exact, n, large)


def _dsa_bias_row(t5_table):
    idx = jnp.arange(DSA_ROLL)
    d = jnp.where(idx < TILE, idx, idx - DSA_ROLL)
    far = t5_table[_t5_bucket(jnp.full((1,), 2 * TILE + 1, I32))]
    return (t5_table[_t5_bucket(TILE + d)] - far).T.astype(F32)


def _dsa_kernel(g_ref, qi_ref, wi_ref, qb_ref, ki_ref, kb_ref, vb_ref, o_ref,
                key_ref, hi_ref, lo_ref, vt_ref, qis_ref, qbs_ref, m_ref, acc_ref, bias_ref,
                *, off, valid_len, q_valid, n_kt, tq):
    i = pl.program_id(1)
    qt = i + off
    q0 = qt * TILE
    nk = qt + 1

    @pl.when((pl.program_id(0) == 0) & (i == 0))
    def _():
        for h in range(B_HEADS):
            bias_ref[h] = _toeplitz(g_ref[h:h + 1, :], 2 * TILE, DSA_ROLL)[:, :tq]

    @pl.when(i == 0)
    def _():
        ones = jnp.ones((HEAD_DIM, TILE), BF16)

        def body(j, c):
            vt = vb_ref[0, pl.ds(pl.multiple_of(j * TILE, TILE), TILE), :].astype(F32).T
            for g in range(B_KV_HEADS):
                vt_ref[g, j, 0:HEAD_DIM, :] = vt[g * HEAD_DIM:(g + 1) * HEAD_DIM].astype(BF16)
                vt_ref[g, j, HEAD_DIM:2 * HEAD_DIM, :] = ones
            return c

        lax.fori_loop(0, n_kt, body, 0)

    colq = lax.broadcasted_iota(I32, (1, tq), 1)
    rowk = lax.broadcasted_iota(I32, (TILE, 1), 0)

    qi = qi_ref[0]
    for h in range(IDX_HEADS):
        qis_ref[h * tq:(h + 1) * tq, :] = qi[:, h * IDX_DIM:(h + 1) * IDX_DIM]
    qb = qb_ref[0]
    for g in range(B_KV_HEADS):
        for hh in range(B_GROUP):
            h = g * B_GROUP + hh
            qbs_ref[g, hh * tq:(hh + 1) * tq, :] = qb[:, h * HEAD_DIM:(h + 1) * HEAD_DIM]
    wi_t = wi_ref[0].T
    lim = jnp.minimum(q0 + (colq // CHUNK + 1) * CHUNK, valid_len)

    def score_tile(j):
        kt = ki_ref[0, pl.ds(pl.multiple_of(j * TILE, TILE), TILE), :]
        lg = lax.dot_general(kt, qis_ref[...], _NT, preferred_element_type=F32)
        sc = wi_t[0:1, :] * jnp.maximum(lg[:, 0:tq], 0.0)
        for h in range(1, IDX_HEADS):
            sc = sc + wi_t[h:h + 1, :] * jnp.maximum(lg[:, h * tq:(h + 1) * tq], 0.0)
        sc = jnp.where(j * TILE + rowk < lim, sc, -jnp.inf)
        bits = pltpu.bitcast(sc, I32)
        key = jnp.where(bits < 0, bits ^ 0x7FFFFFFF, bits)
        key_ref[j] = key
        hi_ref[j] = (key >> 16).astype(I16)
        lo_ref[j] = ((key & 0xFFFF) + MIN16).astype(I16)

    def score_pair(jj, c):
        score_tile(2 * jj)
        score_tile(2 * jj + 1)
        return c

    lax.fori_loop(0, (nk + 1) // 2, score_pair, 0)

    def rows16(x, op):
        x = x.reshape(TILE // 16, 16, tq)
        out = x[0]
        for r in range(1, TILE // 16):
            out = op(out, x[r])
        return out

    def count_ge(ref, cand):
        c16 = cand.astype(I16)

        def body(j, acc):
            return acc + rows16(jnp.where(ref[j] >= c16, jnp.int16(1), jnp.int16(0)), lambda a, b: a + b)

        acc = lax.fori_loop(0, nk, body, jnp.zeros((16, tq), I16))
        return acc.astype(I32).sum(axis=0, keepdims=True)

    def bisect_step(ref, need, skip, bit, st):
        t, exact, c_acc, c_rej = st
        cand = t + lax.shift_left(jnp.int32(1), bit)
        c = count_ge(ref, cand)
        live = (exact == 0) & (skip == 0)
        ok = live & (c >= need)
        t = jnp.where(ok, cand, t)
        c_acc = jnp.where(ok, c, c_acc)
        c_rej = jnp.where(live & (c < need), c, c_rej)
        exact = jnp.where(live & (c == need), 1, exact)
        return t, exact, c_acc, c_rej

    def bisect_init():
        return (jnp.full((1, tq), MIN16, I32), jnp.zeros((1, tq), I32),
                jnp.full((1, tq), 1, I32) * (nk * TILE), jnp.zeros((1, tq), I32))

    skip1 = jnp.where(colq < q_valid, 0, 1)
    t1, exact1, c_acc1, c_rej1 = lax.fori_loop(
        0, 16, lambda it, st: bisect_step(hi_ref, TOPK_MAX, skip1, 15 - it, st), bisect_init())

    need2 = TOPK_MAX - c_rej1
    in_bucket = c_acc1 - c_rej1
    settled1 = (skip1 == 1) | (exact1 == 1) | (t1 == HI_NEG_INF) | (in_bucket == need2)

    t1_16 = t1.astype(I16)

    def mask_tile(j, carry):
        e = hi_ref[j] == t1_16
        low = lo_ref[j]
        lom = jnp.where(e, low, jnp.int16(MIN16))
        lo_ref[j] = lom
        tile_max = rows16(lom, lambda a, b: jnp.where(b > a, b, a)).astype(I32)
        tile_min = rows16(jnp.where(e, low, jnp.int16(MAX16)), lambda a, b: jnp.where(b < a, b, a)).astype(I32)
        return (jnp.maximum(carry[0], tile_max.max(axis=0, keepdims=True)),
                jnp.minimum(carry[1], tile_min.min(axis=0, keepdims=True)))

    mx, mn = lax.fori_loop(0, nk, mask_tile, (jnp.full((1, tq), MIN16, I32), jnp.full((1, tq), MAX16, I32)))
    one_value = mx == mn
    skip2 = jnp.where(settled1 | one_value, 1, 0)

    def bis2_cond(st):
        bit, _, n_open = st
        return (bit >= 0) & (n_open > 0)

    def bis2_body(st):
        bit, inner, _ = st
        inner = bisect_step(lo_ref, need2, skip2, bit, inner)
        return bit - 1, inner, jnp.sum(jnp.where((inner[1] == 0) & (skip2 == 0), 1, 0))

    _, (t2, exact2, _, c_rej2), _ = lax.while_loop(
        bis2_cond, bis2_body, (jnp.int32(15), bisect_init(), jnp.sum(1 - skip2)))
    t2 = jnp.where(one_value, mx, t2)
    c_rej2 = jnp.where(one_value, 0, c_rej2)
    thr_hi = lax.shift_left(t1, 16)
    thr = jnp.where(settled1, thr_hi, thr_hi | ((t2 - MIN16) & 0xFFFF))

    tied = (~settled1) & (exact2 == 0)

    @pl.when(jnp.sum(jnp.where(tied, 1, 0)) > 0)
    def _():
        keep = jnp.where(tied, need2 - c_rej2, 2 ** 30).astype(F32)
        lower = (lax.broadcasted_iota(I32, (TILE, TILE), 0) > lax.broadcasted_iota(I32, (TILE, TILE), 1))
        lower = jnp.where(lower, 1.0, 0.0).astype(BF16)

        def body(j, run):
            blk = key_ref[j]
            eq = blk == thr
            eq_f = jnp.where(eq, 1.0, 0.0)
            before = jnp.dot(lower, eq_f.astype(BF16), preferred_element_type=F32)
            key_ref[j] = jnp.where(eq & (run + before >= keep), NEG_INF_KEY, blk)
            return run + eq_f.reshape(TILE // 8, 8, tq).sum(axis=0).sum(axis=0, keepdims=True)

        lax.fori_loop(0, nk, body, jnp.zeros((1, tq), F32))

    thr = jnp.maximum(thr, NEG_INF_KEY + 1)

    for g in range(B_KV_HEADS):
        m_ref[g] = jnp.full((1, B_GROUP * tq), NEG, F32)
        acc_ref[g] = jnp.zeros((2 * HEAD_DIM, B_GROUP * tq), F32)

    def attend(tiles):
        kts = [kb_ref[0, pl.ds(pl.multiple_of(j * TILE, TILE), TILE), :] for j, _ in tiles]
        sels = [key_ref[j] >= thr for j, _ in tiles]
        for g in range(B_KV_HEADS):
            qs = qbs_ref[g]
            ss = []
            for (j, near), kt, sel in zip(tiles, kts, sels):
                st = lax.dot_general(kt[:, g * HEAD_DIM:(g + 1) * HEAD_DIM], qs, _NT,
                                     preferred_element_type=F32)
                parts = []
                for hh in range(B_GROUP):
                    s_h = st[:, hh * tq:(hh + 1) * tq]
                    if near is not None:
                        s_h = s_h + bias_ref[g * B_GROUP + hh, near * TILE:(near + 1) * TILE, :]
                    parts.append(jnp.where(sel, s_h, -jnp.inf))
                ss.append(jnp.concatenate(parts, axis=1))
            m_old = m_ref[g]
            m_new = m_old
            for s in ss:
                m_new = jnp.maximum(m_new, s.max(axis=0, keepdims=True))
            pv = None
            for (j, _), s in zip(tiles, ss):
                p = jnp.exp(s - m_new).astype(BF16)
                d = jnp.dot(vt_ref[g, j], p, preferred_element_type=F32)
                pv = d if pv is None else pv + d
            acc_ref[g] = jnp.exp(m_old - m_new) * acc_ref[g] + pv
            m_ref[g] = m_new

    n_far = jnp.maximum(nk - 2, 0)

    def far_pair(jj, c):
        attend([(2 * jj, None), (2 * jj + 1, None)])
        return c

    lax.fori_loop(0, n_far // 2, far_pair, 0)

    @pl.when(n_far % 2 == 1)
    def _():
        attend([(n_far - 1, None)])

    @pl.when(nk >= 2)
    def _():
        attend([(nk - 2, 0), (nk - 1, 1)])

    @pl.when(nk < 2)
    def _():
        attend([(nk - 1, 1)])

    outs = []
    for g in range(B_KV_HEADS):
        for hh in range(B_GROUP):
            blk = acc_ref[g, :, hh * tq:(hh + 1) * tq].T
            outs.append(blk[:, 0:HEAD_DIM] / blk[:, HEAD_DIM:HEAD_DIM + 1])
    o_ref[0] = jnp.concatenate(outs, axis=-1).astype(BF16)


def _dsa_attn(qi, wi, qb, ki, kb, vb, g_row, off, valid_len, q_valid, tq):
    b, t_q = qi.shape[:2]
    tk = ki.shape[1]
    nq, n_kt = t_q // tq, tk // TILE
    assert tq == TILE or nq == 1
    assert n_kt % 2 == 0 and tk == n_kt * TILE
    qspec = lambda n: pl.BlockSpec((1, tq, n), lambda bb, i: (bb, i, 0))
    kspec = lambda n: pl.BlockSpec((1, tk, n), lambda bb, i: (bb, 0, 0))
    return pl.pallas_call(
        functools.partial(_dsa_kernel, off=off, valid_len=valid_len, q_valid=q_valid, n_kt=n_kt, tq=tq),
        grid=(b, nq),
        in_specs=[pl.BlockSpec((B_HEADS, DSA_ROLL), lambda bb, i: (0, 0)),
                  qspec(512), qspec(128), qspec(512), kspec(IDX_DIM), kspec(128), kspec(128)],
        out_specs=pl.BlockSpec((1, tq, B_WIDTH), lambda bb, i: (bb, i, 0)),
        out_shape=jax.ShapeDtypeStruct((b, t_q, B_WIDTH), BF16),
        scratch_shapes=[pltpu.VMEM((n_kt, TILE, tq), I32),
                        pltpu.VMEM((n_kt, TILE, tq), I16),
                        pltpu.VMEM((n_kt, TILE, tq), I16),
                        pltpu.VMEM((B_KV_HEADS, n_kt, 2 * HEAD_DIM, TILE), BF16),
                        pltpu.VMEM((IDX_HEADS * tq, IDX_DIM), BF16),
                        pltpu.VMEM((B_KV_HEADS, B_GROUP * tq, HEAD_DIM), BF16),
                        pltpu.VMEM((B_KV_HEADS, 1, B_GROUP * tq), F32),
                        pltpu.VMEM((B_KV_HEADS, 2 * HEAD_DIM, B_GROUP * tq), F32),
                        pltpu.VMEM((B_HEADS, 2 * TILE, tq), F32)],
        compiler_params=_params("arbitrary", "arbitrary"),
        name="dsa_attn",
    )(g_row, qi, wi, qb, ki, kb, vb)


def _out_proj_kernel(x_ref, oa_ref, ob_ref, w_ref, g_ref, b_ref, o_ref):
    mix = jnp.concatenate([oa_ref[...], ob_ref[...]], axis=-1)
    y = jnp.dot(mix, w_ref[...], preferred_element_type=F32)
    o_ref[...] = _layer_norm(ALPHA * x_ref[...] + y, g_ref[...], b_ref[...])


def _out_proj_ln(x2d, oa, ob, w, g, b, tm):
    r = x2d.shape[0]
    row = lambda n: pl.BlockSpec((tm, n), lambda i: (i, 0))
    const = lambda s: pl.BlockSpec(s, lambda i: (0, 0))
    return pl.pallas_call(
        _out_proj_kernel,
        grid=(r // tm,),
        in_specs=[row(D_MODEL), row(A_WIDTH), row(B_WIDTH), const((A_WIDTH + B_WIDTH, D_MODEL)),
                  const((1, D_MODEL)), const((1, D_MODEL))],
        out_specs=row(D_MODEL),
        out_shape=jax.ShapeDtypeStruct((r, D_MODEL), F32),
        compiler_params=_params("arbitrary"),
        name="out_proj_ln",
    )(x2d, oa, ob, w, g, b)


def _mem_kv_kernel(m_ref, wk_ref, wv_ref, k_o, v_o, kb_o, vb_o):
    mb = m_ref[...].astype(BF16)
    k = jnp.dot(mb, wk_ref[...], preferred_element_type=F32)
    v = jnp.dot(mb, wv_ref[...], preferred_element_type=F32)
    k_o[...] = k
    v_o[...] = v
    kb_o[...] = k.astype(BF16)
    vb_o[...] = v.astype(BF16)


def _mem_kv(mem2d, wk, wv):
    r = mem2d.shape[0]
    tm = MEM_LEN
    row = lambda n: pl.BlockSpec((tm, n), lambda i: (i, 0))
    const = lambda s: pl.BlockSpec(s, lambda i: (0, 0))
    sds = jax.ShapeDtypeStruct
    return pl.pallas_call(
        _mem_kv_kernel,
        grid=(r // tm,),
        in_specs=[row(D_MODEL), const((D_MODEL, MEM_WIDTH)), const((D_MODEL, MEM_WIDTH))],
        out_specs=[row(MEM_WIDTH)] * 4,
        out_shape=[sds((r, MEM_WIDTH), F32), sds((r, MEM_WIDTH), F32),
                   sds((r, MEM_WIDTH), BF16), sds((r, MEM_WIDTH), BF16)],
        compiler_params=_params("arbitrary"),
        name="mem_kv",
    )(mem2d, wk, wv)


def _mem_attn_kernel(h_ref, mk_ref, mv_ref, wq_ref, wo_ref, g_ref, b_ref, o_ref):
    h = h_ref[0]
    q = jnp.dot(h.astype(BF16), wq_ref[...], preferred_element_type=F32).astype(BF16)
    outs = []
    for hd in range(MEM_HEADS):
        sl = slice(hd * MEM_HEAD_DIM, (hd + 1) * MEM_HEAD_DIM)
        s = lax.dot_general(q[:, sl], mk_ref[0, :, sl], _NT, preferred_element_type=F32) * MEM_HEAD_DIM ** -0.5
        p = jnp.exp(s - s.max(-1, keepdims=True))
        l = p.sum(-1, keepdims=True)
        outs.append(jnp.dot(p.astype(BF16), mv_ref[0, :, sl], preferred_element_type=F32) / l)
    o = jnp.concatenate(outs, axis=-1).astype(BF16)
    y = jnp.dot(o, wo_ref[...], preferred_element_type=F32)
    o_ref[0] = _layer_norm(ALPHA * h + y, g_ref[...], b_ref[...])


def _mem_attn_ln(h3d, mk, mv, wq, wo, g, b, tm):
    bsz, t = h3d.shape[:2]
    const2 = lambda s: pl.BlockSpec(s, lambda bb, i: (0, 0))
    return pl.pallas_call(
        _mem_attn_kernel,
        grid=(bsz, t // tm),
        in_specs=[pl.BlockSpec((1, tm, D_MODEL), lambda bb, i: (bb, i, 0)),
                  pl.BlockSpec((1, MEM_LEN, MEM_WIDTH), lambda bb, i: (bb, 0, 0)),
                  pl.BlockSpec((1, MEM_LEN, MEM_WIDTH), lambda bb, i: (bb, 0, 0)),
                  const2((D_MODEL, MEM_WIDTH)), const2((MEM_WIDTH, D_MODEL)),
                  const2((1, D_MODEL)), const2((1, D_MODEL))],
        out_specs=pl.BlockSpec((1, tm, D_MODEL), lambda bb, i: (bb, i, 0)),
        out_shape=jax.ShapeDtypeStruct(h3d.shape, F32),
        compiler_params=_params("arbitrary", "arbitrary"),
        name="mem_attn_ln",
    )(h3d, mk, mv, wq, wo, g, b)


FF_CHUNK = 256


def _ffn_kernel(h_ref, hist_ref, wu_ref, wc_ref, bc_ref, wd_ref, g_ref, b_ref, o_ref, tail_ref,
                carry_ref, f_ref, *, tiles_per_batch, seg):
    i = pl.program_id(0)
    tm = h_ref.shape[0]
    nseg = tm // seg
    h = h_ref[...]
    hb = h.astype(BF16)
    row = lax.broadcasted_iota(I32, (tm, 1), 0)
    first = (i % tiles_per_batch) == 0
    f_ref[...] = jnp.zeros(f_ref.shape, F32)
    for c in range(D_FF // FF_CHUNK):
        cs = slice(c * FF_CHUNK, (c + 1) * FF_CHUNK)
        u = jnp.dot(hb, wu_ref[:, cs], preferred_element_type=F32)
        gt = jnp.dot(hb, wu_ref[:, D_FF + c * FF_CHUNK:D_FF + (c + 1) * FF_CHUNK], preferred_element_type=F32)
        p1 = pltpu.roll(gt, 1, 0)
        p2 = pltpu.roll(gt, 2, 0)
        for s in range(nseg):
            hist = hist_ref[s, :, cs]
            if tiles_per_batch > 1:
                hist = jnp.where(first, hist, carry_ref[:, cs])
            p1 = jnp.where(row == s * seg, hist[7:8, :], p1)
            p2 = jnp.where(row == s * seg, hist[6:7, :], p2)
            p2 = jnp.where(row == s * seg + 1, hist[7:8, :], p2)
        gc = bc_ref[:, cs] + ((wc_ref[0:1, cs] * p2 + wc_ref[1:2, cs] * p1) + wc_ref[2:3, cs] * gt)
        act = (u * jax.nn.gelu(gc)).astype(BF16)
        f_ref[...] += jnp.dot(act, wd_ref[cs, :], preferred_element_type=F32)
        for s in range(nseg):
            tail_ref[s, :, cs] = gt[(s + 1) * seg - 8:(s + 1) * seg, :]
        carry_ref[:, cs] = gt[tm - 8:tm, :]
    o_ref[...] = _layer_norm(ALPHA * h + f_ref[...], g_ref[...], b_ref[...])


def _ffn_ln(h2d, hist, wu, wc, bc, wd, g, b, tm, tiles_per_batch, seg):
    r = h2d.shape[0]
    nseg = tm // seg
    n_tail = r // (tm * tiles_per_batch) * nseg
    row = lambda n: pl.BlockSpec((tm, n), lambda i: (i, 0))
    const = lambda s: pl.BlockSpec(s, lambda i: (0,) * len(s))
    return pl.pallas_call(
        functools.partial(_ffn_kernel, tiles_per_batch=tiles_per_batch, seg=seg),
        grid=(r // tm,),
        in_specs=[row(D_MODEL),
                  pl.BlockSpec((nseg, 8, D_FF), lambda i: (i // tiles_per_batch, 0, 0)),
                  const((D_MODEL, 2 * D_FF)), const((CONV_W, D_FF)), const((1, D_FF)), const((D_FF, D_MODEL)),
                  const((1, D_MODEL)), const((1, D_MODEL))],
        out_specs=[row(D_MODEL), pl.BlockSpec((nseg, 8, D_FF), lambda i: (i // tiles_per_batch, 0, 0))],
        out_shape=[jax.ShapeDtypeStruct((r, D_MODEL), F32), jax.ShapeDtypeStruct((n_tail, 8, D_FF), F32)],
        scratch_shapes=[pltpu.VMEM((8, D_FF), F32), pltpu.VMEM((tm, D_MODEL), F32)],
        compiler_params=_params("arbitrary"),
        name="ffn_ln",
    )(h2d, hist, wu, wc, bc, wd, g, b)


def _pad_rows(a, n):
    return jnp.pad(a, ((0, 0), (0, n - a.shape[1])) + ((0, 0),) * (a.ndim - 2))


def _hist8(g_hist):
    return jnp.pad(g_hist, ((0, 0), (8 - g_hist.shape[1], 0), (0, 0)))


def kernel(x_prompt, x_sample, cache_a_k, cache_a_v, cache_b_k, cache_b_v, cache_b_kidx, cache_mem_k, cache_mem_v, state_ffn_conv, mem_prompt, w_in, a_rel_bias, t5_bias, w_o, ln1_g, ln1_b, w_mq, w_mk, w_mv, w_mo, ln2_g, ln2_b, w_up, w_conv, b_conv, w_down, ln3_g, ln3_b):
    bp, tp = x_prompt.shape[:2]
    bs, ts = x_sample.shape[:2]
    l = 0
    vec = lambda a: a[l].reshape(1, -1)
    w_in_p = _prep_w_in(w_in[l])
    w_o_b = w_o[l].astype(BF16)
    w_mq_b, w_mk_b, w_mv_b, w_mo_b = (w[l].astype(BF16) for w in (w_mq, w_mk, w_mv, w_mo))
    w_up_b, w_down_b = w_up[l].astype(BF16), w_down[l].astype(BF16)
    band_row = _band_bias_row(a_rel_bias[l])
    dsa_row = _dsa_bias_row(t5_bias)
    ffn_w = (w_up_b, w_conv[l], vec(b_conv), w_down_b, vec(ln3_g), vec(ln3_b))

    tm = 512
    a_keep = min(N_PREV_CHUNKS * CHUNK, tp)
    (qa, ka, va, qb, qi, kb, vb, ki, kb_b, vb_b, ki_b, wi, ka_tail, va_tail) = _in_proj(
        x_prompt.reshape(bp * tp, D_MODEL), w_in_p, tm, tp // tm)
    r3 = lambda a: a.reshape(bp, tp, a.shape[-1])
    oa = _band_attn(r3(qa), r3(ka), r3(va), band_row, 0, tp)
    ob = _dsa_attn(r3(qi), r3(wi), r3(qb), r3(ki_b), r3(kb_b), r3(vb_b), dsa_row, 0, tp, TILE, TILE)
    h = _out_proj_ln(x_prompt.reshape(bp * tp, D_MODEL), oa.reshape(bp * tp, A_WIDTH), ob.reshape(bp * tp, B_WIDTH),
                     w_o_b, vec(ln1_g), vec(ln1_b), tm)
    mk, mv, mk_b, mv_b = _mem_kv(mem_prompt.reshape(bp * MEM_LEN, D_MODEL), w_mk_b, w_mv_b)
    h = _mem_attn_ln(h.reshape(bp, tp, D_MODEL), mk_b.reshape(bp, MEM_LEN, MEM_WIDTH),
                     mv_b.reshape(bp, MEM_LEN, MEM_WIDTH), w_mq_b, w_mo_b, vec(ln2_g), vec(ln2_b), TILE)
    xp, p_tail = _ffn_ln(h.reshape(bp * tp, D_MODEL), jnp.zeros((bp, 8, D_FF), F32), *ffn_w, tm, tp // tm, tm)
    prompt_state = (
        ka_tail.reshape(bp, tm, A_HEADS, HEAD_DIM)[:, tm - a_keep:][None],
        va_tail.reshape(bp, tm, A_HEADS, HEAD_DIM)[:, tm - a_keep:][None],
        kb.reshape(1, bp, tp, B_KV_HEADS, HEAD_DIM), vb.reshape(1, bp, tp, B_KV_HEADS, HEAD_DIM),
        ki.reshape(1, bp, tp, IDX_DIM),
        mk.reshape(1, bp, MEM_LEN, MEM_HEADS, MEM_HEAD_DIM), mv.reshape(1, bp, MEM_LEN, MEM_HEADS, MEM_HEAD_DIM),
        p_tail[:, 8 - (CONV_W - 1):][None])

    rs = bs * ts
    (qa, ka, va, qb, qi, kb, vb, ki, kb_b, vb_b, ki_b, wi, ka_new, va_new) = _in_proj(
        x_sample.reshape(rs, D_MODEL), w_in_p, rs, 1)
    s3 = lambda a: a.reshape(bs, ts, a.shape[-1])
    qpad = lambda a, n: _pad_rows(s3(a), n)

    past_a = cache_a_k.shape[2]
    n_a = past_a + ts
    t_a = -(-n_a // TILE) * TILE
    seq_a = lambda cache, new: _pad_rows(
        jnp.concatenate([cache[l].reshape(bs, past_a, A_WIDTH), s3(new)], axis=1), t_a).astype(BF16)
    oa = _band_attn(qpad(qa, TILE), seq_a(cache_a_k, ka_new), seq_a(cache_a_v, va_new), band_row,
                    past_a // TILE, n_a)

    past_b = cache_b_k.shape[2]
    n_b = past_b + ts
    t_b = -(-n_b // (2 * TILE)) * 2 * TILE
    tq_s = LANES
    seq_b = lambda cache, new: _pad_rows(
        jnp.concatenate([cache[l].reshape(bs, past_b, -1), s3(new)], axis=1), t_b).astype(BF16)
    ob = _dsa_attn(qpad(qi, tq_s), qpad(wi, tq_s), qpad(qb, tq_s), seq_b(cache_b_kidx, ki), seq_b(cache_b_k, kb),
                   seq_b(cache_b_v, vb), dsa_row, past_b // TILE, n_b, ts, tq_s)

    h = _out_proj_ln(x_sample.reshape(rs, D_MODEL), oa[:, :ts].reshape(rs, A_WIDTH), ob[:, :ts].reshape(rs, B_WIDTH),
                     w_o_b, vec(ln1_g), vec(ln1_b), rs)
    h = _mem_attn_ln(h.reshape(bs, ts, D_MODEL), cache_mem_k[l].reshape(bs, MEM_LEN, MEM_WIDTH).astype(BF16),
                     cache_mem_v[l].reshape(bs, MEM_LEN, MEM_WIDTH).astype(BF16),
                     w_mq_b, w_mo_b, vec(ln2_g), vec(ln2_b), ts)
    xs, s_tail = _ffn_ln(h.reshape(rs, D_MODEL), _hist8(state_ffn_conv[l]), *ffn_w, rs, 1, ts)
    sample_state = (
        ka_new.reshape(1, bs, ts, A_HEADS, HEAD_DIM), va_new.reshape(1, bs, ts, A_HEADS, HEAD_DIM),
        kb.reshape(1, bs, ts, B_KV_HEADS, HEAD_DIM), vb.reshape(1, bs, ts, B_KV_HEADS, HEAD_DIM),
        ki.reshape(1, bs, ts, IDX_DIM), s_tail[:, 8 - (CONV_W - 1):][None])

    return (xp.reshape(bp, tp, D_MODEL), xs.reshape(bs, ts, D_MODEL)) + prompt_state + sample_state
```

```python
import functools
import math

import jax
import jax.numpy as jnp
from jax import lax
from jax.experimental import pallas as pl
from jax.experimental.pallas import tpu as pltpu

F32 = jnp.float32
BF16 = jnp.bfloat16
I32 = jnp.int32
I16 = jnp.int16

D_MODEL = 1024
CHUNK = 64
N_PREV_CHUNKS = 8
HEAD_DIM = 64
A_HEADS = 8
A_WIDTH = A_HEADS * HEAD_DIM
A_MAX_REL = 64
B_HEADS = 8
B_KV_HEADS = 2
B_GROUP = B_HEADS // B_KV_HEADS
B_WIDTH = B_HEADS * HEAD_DIM
B_KV_WIDTH = B_KV_HEADS * HEAD_DIM
IDX_HEADS = 8
IDX_DIM = 64
TOPK_MAX = 256
N_BUCKETS = 32
T5_MAX_DIST = 128
MEM_LEN = 256
MEM_HEADS = 4
MEM_HEAD_DIM = 128
MEM_WIDTH = MEM_HEADS * MEM_HEAD_DIM
D_FF = 2816
CONV_W = 3
IN_SIZES = (A_WIDTH, A_WIDTH, A_WIDTH, B_WIDTH, B_KV_WIDTH, B_KV_WIDTH, IDX_HEADS * IDX_DIM, IDX_DIM, IDX_HEADS)
DEPTH = 1
ALPHA = (2 * DEPTH) ** 0.25
LN_EPS = 1e-5
ATTN_SCALE = HEAD_DIM ** -0.5
NEG = -1e30

LANES = 128
TILE = 256
BAND_TILES = 1 + (N_PREV_CHUNKS * CHUNK) // TILE
VMEM_LIMIT = 56 * 1024 * 1024

_C_QA, _C_KA, _C_VA, _C_QB = 0, 512, 1024, 1536
_C_KB, _C_VB, _C_QI, _C_KI, _C_WI = 2048, 2176, 2304, 2816, 2944
IN_PAD = 3072

MIN16 = -32768
HI_NEG_INF = -32641

_NT = (((1,), (1,)), ((), ()))


def _params(*sem):
    return pltpu.CompilerParams(dimension_semantics=sem, vmem_limit_bytes=VMEM_LIMIT)


def _layer_norm(z, g, b):
    mu = jnp.mean(z, axis=-1, keepdims=True)
    d = z - mu
    var = jnp.mean(d * d, axis=-1, keepdims=True)
    return d * lax.rsqrt(var + LN_EPS) * g + b


def _toeplitz(g_row, rows, width):
    return pltpu.roll(jnp.broadcast_to(g_row, (rows, width)), 0, 1, stride=1, stride_axis=0)


def _in_proj_kernel(x_ref, w_ref, qa_o, ka_o, va_o, qb_o, qi_o, kb_o, vb_o, ki_o, kbb_o, vbb_o, kib_o,
                    wi_o, kat_o, vat_o, *, tiles_per_batch, wi_scale):
    i = pl.program_id(0)
    xb = x_ref[...].astype(BF16)

    def mm(c0, n):
        return jnp.dot(xb, w_ref[:, c0:c0 + n], preferred_element_type=F32)

    qa_o[...] = mm(_C_QA, 512).astype(BF16)
    ka = mm(_C_KA, 512)
    va = mm(_C_VA, 512)
    ka_o[...] = ka.astype(BF16)
    va_o[...] = va.astype(BF16)
    qb_o[...] = mm(_C_QB, 512).astype(BF16)
    qi_o[...] = mm(_C_QI, 512).astype(BF16)
    kb = mm(_C_KB, 128)
    vb = mm(_C_VB, 128)
    kb_o[...] = kb
    vb_o[...] = vb
    kbb_o[...] = kb.astype(BF16)
    vbb_o[...] = vb.astype(BF16)
    ki = mm(_C_KI, 128)[:, :IDX_DIM]
    ki_o[...] = ki
    kib_o[...] = ki.astype(BF16)
    wi_o[...] = mm(_C_WI, 128) * wi_scale

    @pl.when(i % tiles_per_batch == tiles_per_batch - 1)
    def _():
        kat_o[...] = ka
        vat_o[...] = va


def _prep_w_in(w):
    parts, off = [], 0
    for n in IN_SIZES:
        parts.append(w[:, off:off + n])
        off += n
    qa, ka, va, qb, kb, vb, qi, ki, wi = parts
    pad = lambda a, n: jnp.pad(a, ((0, 0), (0, n - a.shape[1])))
    cols = [qa * ATTN_SCALE, ka, va, qb * ATTN_SCALE, kb, vb, qi * IDX_DIM ** -0.5, pad(ki, 128), pad(wi, 128)]
    return jnp.concatenate(cols, axis=1).astype(BF16)


def _in_proj(x2d, w_pad, tm, tiles_per_batch):
    r = x2d.shape[0]
    n_tiles = r // tm
    n_batch = n_tiles // tiles_per_batch
    row = lambda n: pl.BlockSpec((tm, n), lambda i: (i, 0))
    tail = pl.BlockSpec((tm, 512), lambda i: (i // tiles_per_batch, 0))
    sds = jax.ShapeDtypeStruct
    out_shape = [sds((r, 512), BF16)] * 5 + [sds((r, 128), F32), sds((r, 128), F32), sds((r, IDX_DIM), F32),
                                              sds((r, 128), BF16), sds((r, 128), BF16), sds((r, IDX_DIM), BF16),
                                              sds((r, 128), F32),
                                              sds((n_batch * tm, 512), F32), sds((n_batch * tm, 512), F32)]
    out_specs = [row(512)] * 5 + [row(128), row(128), row(IDX_DIM), row(128), row(128), row(IDX_DIM), row(128),
                                  tail, tail]
    return pl.pallas_call(
        functools.partial(_in_proj_kernel, tiles_per_batch=tiles_per_batch, wi_scale=IDX_HEADS ** -0.5),
        grid=(n_tiles,),
        in_specs=[pl.BlockSpec((tm, D_MODEL), lambda i: (i, 0)),
                  pl.BlockSpec((D_MODEL, IN_PAD), lambda i: (0, 0))],
        out_specs=out_specs,
        out_shape=out_shape,
        compiler_params=_params("arbitrary"),
        name="in_proj",
    )(x2d, w_pad)


BAND_COLS = BAND_TILES * TILE
BAND_ROLL = BAND_COLS + TILE


def _band_bias_row(table):
    idx = jnp.arange(BAND_ROLL)
    d = jnp.where(idx < TILE, idx, idx - BAND_ROLL)
    rel = (BAND_TILES - 1) * TILE + d
    return table[jnp.clip(rel, -A_MAX_REL, A_MAX_REL) + A_MAX_REL].T.astype(F32)


def _band_kernel(g_ref, q_ref, k0, k1, k2, v0, v1, v2, o_ref, bias_ref, vt_ref, *, off, valid_len):
    i = pl.program_id(1)
    kt = i + off
    krefs, vrefs = (k0, k1, k2), (v0, v1, v2)

    @pl.when((pl.program_id(0) == 0) & (i == 0))
    def _():
        c = lax.broadcasted_iota(I32, (BAND_COLS, TILE), 0) // CHUNK
        r = lax.broadcasted_iota(I32, (BAND_COLS, TILE), 1) // CHUNK
        ok = (c >= r) & (c <= r + N_PREV_CHUNKS)
        for h in range(A_HEADS):
            bias_ref[h] = jnp.where(ok, _toeplitz(g_ref[h:h + 1, :], BAND_COLS, BAND_ROLL)[:, :TILE], NEG)
        ones = jnp.ones((HEAD_DIM, TILE), BF16)
        for s in range(BAND_TILES):
            for h in range(A_HEADS):
                vt_ref[s, h, HEAD_DIM:2 * HEAD_DIM, :] = ones

    def put(slot, vref):
        vt = vref[0].astype(F32).T
        for h in range(A_HEADS):
            vt_ref[slot, h, 0:HEAD_DIM, :] = vt[h * HEAD_DIM:(h + 1) * HEAD_DIM].astype(BF16)

    slots = [(kt + 1 + j) % BAND_TILES for j in range(BAND_TILES)]

    @pl.when(i == 0)
    def _():
        for j in range(BAND_TILES - 1):
            put(slots[j], vrefs[j])

    put(slots[BAND_TILES - 1], vrefs[BAND_TILES - 1])

    base = (kt - (BAND_TILES - 1)) * TILE

    n_slab = 4
    slab_w = n_slab * HEAD_DIM
    lane_head = lax.broadcasted_iota(I32, (TILE, slab_w), 1) // HEAD_DIM

    def attend(masked):
        outs = []
        for g in range(A_HEADS // n_slab):
            gs = slice(g * slab_w, (g + 1) * slab_w)
            q_slab = q_ref[0, :, gs].astype(F32)
            q_bd = jnp.concatenate([jnp.where(lane_head == hh, q_slab, 0.0) for hh in range(n_slab)],
                                   axis=0).astype(BF16)
            s = []
            for j in range(BAND_TILES):
                st = lax.dot_general(krefs[j][0, :, gs], q_bd, _NT, preferred_element_type=F32)
                parts = [st[:, hh * TILE:(hh + 1) * TILE] + bias_ref[g * n_slab + hh, j * TILE:(j + 1) * TILE, :]
                         for hh in range(n_slab)]
                if masked:
                    kpos = base + j * TILE + lax.broadcasted_iota(I32, (TILE, TILE), 0)
                    ok = (kpos >= 0) & (kpos < valid_len)
                    parts = [jnp.where(ok, x, NEG) for x in parts]
                s.append(jnp.concatenate(parts, axis=1))
            m = jnp.maximum(jnp.maximum(s[0], s[1]), s[2]).max(axis=0, keepdims=True)
            p = [jnp.exp(sj - m).astype(BF16) for sj in s]
            for hh in range(n_slab):
                acc = None
                for j in range(BAND_TILES):
                    d = jnp.dot(vt_ref[slots[j], g * n_slab + hh], p[j][:, hh * TILE:(hh + 1) * TILE],
                                preferred_element_type=F32)
                    acc = d if acc is None else acc + d
                blk = acc.T
                outs.append(blk[:, 0:HEAD_DIM] / blk[:, HEAD_DIM:HEAD_DIM + 1])
        o_ref[0] = jnp.concatenate(outs, axis=-1).astype(BF16)

    needs_mask = (base < 0) | (base + BAND_COLS > valid_len)

    @pl.when(needs_mask)
    def _():
        attend(True)

    @pl.when(jnp.logical_not(needs_mask))
    def _():
        attend(False)


def _band_attn(q, k, v, g_row, off, valid_len):
    b, tq = q.shape[:2]
    nq = tq // TILE
    qspec = pl.BlockSpec((1, TILE, A_WIDTH), lambda bb, i: (bb, i, 0))
    kspec = lambda d: pl.BlockSpec((1, TILE, A_WIDTH), lambda bb, i: (bb, jnp.maximum(i + off - d, 0), 0))
    return pl.pallas_call(
        functools.partial(_band_kernel, off=off, valid_len=valid_len),
        grid=(b, nq),
        in_specs=[pl.BlockSpec((A_HEADS, BAND_ROLL), lambda bb, i: (0, 0)),
                  qspec, kspec(2), kspec(1), kspec(0), kspec(2), kspec(1), kspec(0)],
        out_specs=pl.BlockSpec((1, TILE, A_WIDTH), lambda bb, i: (bb, i, 0)),
        out_shape=jax.ShapeDtypeStruct((b, tq, A_WIDTH), BF16),
        scratch_shapes=[pltpu.VMEM((A_HEADS, BAND_COLS, TILE), F32),
                        pltpu.VMEM((BAND_TILES, A_HEADS, 2 * HEAD_DIM, TILE), BF16)],
        compiler_params=_params("arbitrary", "arbitrary"),
        name="band_attn",
    )(g_row, q, k, k, k, v, v, v)


DSA_ROLL = 3 * TILE


def _t5_bucket(rel):
    half = N_BUCKETS // 2
    max_exact = half // 2
    n = jnp.abs(rel)
    log_ratio = jnp.log(jnp.maximum(n, 1).astype(jnp.float32) / max_exact) / math.log(T5_MAX_DIST / max_exact)
    large = jnp.minimum(max_exact + (log_ratio * (half - max_exact)).astype(jnp.int32), half - 1)
    return jnp.where(rel < 0, half, 0) + jnp.where(n < max_exact, n, large)


def _dsa_bias_row(t5_table):
    idx = jnp.arange(DSA_ROLL)
    d = jnp.where(idx < TILE, idx, idx - DSA_ROLL)
    far = t5_table[_t5_bucket(jnp.full((1,), 2 * TILE + 1, I32))]
    return (t5_table[_t5_bucket(TILE + d)] - far).T.astype(F32)


def _dsa_kernel(g_ref, qi_ref, wi_ref, qb_ref, ki_ref, kb_ref, vb_ref, o_ref,
                sc_ref, scb_ref, keep_ref, vt_ref, qis_ref, qbs_ref, m_ref, acc_ref, bias_ref,
                *, off, valid_len, q_valid, n_kt, tq):
    i = pl.program_id(1)
    qt = i + off
    q0 = qt * TILE
    nk = qt + 1

    @pl.when((pl.program_id(0) == 0) & (i == 0))
    def _():
        for h in range(B_HEADS):
            bias_ref[h] = _toeplitz(g_ref[h:h + 1, :], 2 * TILE, DSA_ROLL)[:, :tq]

    @pl.when(i == 0)
    def _():
        ones = jnp.ones((HEAD_DIM, TILE), BF16)

        def body(j, c):
            vt = vb_ref[0, pl.ds(pl.multiple_of(j * TILE, TILE), TILE), :].astype(F32).T
            for g in range(B_KV_HEADS):
                vt_ref[g, j, 0:HEAD_DIM, :] = vt[g * HEAD_DIM:(g + 1) * HEAD_DIM].astype(BF16)
                vt_ref[g, j, HEAD_DIM:2 * HEAD_DIM, :] = ones
            return c

        lax.fori_loop(0, n_kt, body, 0)

    colq = lax.broadcasted_iota(I32, (1, tq), 1)
    rowk = lax.broadcasted_iota(I32, (TILE, 1), 0)

    qi = qi_ref[0]
    for h in range(IDX_HEADS):
        qis_ref[h * tq:(h + 1) * tq, :] = qi[:, h * IDX_DIM:(h + 1) * IDX_DIM]
    qb = qb_ref[0]
    for g in range(B_KV_HEADS):
        for hh in range(B_GROUP):
            h = g * B_GROUP + hh
            qbs_ref[g, hh * tq:(hh + 1) * tq, :] = qb[:, h * HEAD_DIM:(h + 1) * HEAD_DIM]
    wi_t = wi_ref[0].T
    lim = jnp.minimum(q0 + (colq // CHUNK + 1) * CHUNK, valid_len)

    def score_tile(j):
        kt = ki_ref[0, pl.ds(pl.multiple_of(j * TILE, TILE), TILE), :]
        lg = lax.dot_general(kt, qis_ref[...], _NT, preferred_element_type=F32)
        sc = wi_t[0:1, :] * jnp.maximum(lg[:, 0:tq], 0.0)
        for h in range(1, IDX_HEADS):
            sc = sc + wi_t[h:h + 1, :] * jnp.maximum(lg[:, h * tq:(h + 1) * tq], 0.0)
        sc = jnp.where(j * TILE + rowk < lim, sc, -jnp.inf)
        sc_ref[j] = sc
        scb_ref[j] = sc.astype(BF16)

    def score_pair(jj, c):
        score_tile(2 * jj)
        score_tile(2 * jj + 1)
        return c

    lax.fori_loop(0, (nk + 1) // 2, score_pair, 0)

    def f32_of_key(k):
        return pltpu.bitcast(jnp.where(k < 0, k ^ 0x7FFFFFFF, k), F32)

    def bf16_of_key(k):
        bits = jnp.where(k < 0, k ^ 0x7FFF, k) & 0xFFFF
        return pltpu.bitcast(lax.shift_left(bits, 16), F32).astype(BF16)

    def count_bf16(cand):
        def body(j, acc):
            ge = jnp.where(scb_ref[j] >= cand, jnp.int16(1), jnp.int16(0)).reshape(TILE // 16, 16, tq)
            part = ge[0]
            for r in range(1, TILE // 16):
                part = part + ge[r]
            return acc + part

        acc = lax.fori_loop(0, nk, body, jnp.zeros((16, tq), I16))
        return acc.astype(I32).sum(axis=0, keepdims=True)

    def count_f32(cand, strict=False):
        def body(j, acc):
            blk = sc_ref[j]
            hit = (blk > cand) if strict else (blk >= cand)
            return acc + jnp.where(hit, 1, 0).reshape(TILE // 8, 8, tq).sum(axis=0)

        return lax.fori_loop(0, nk, body, jnp.zeros((8, tq), I32)).sum(axis=0, keepdims=True)

    zero = jnp.zeros((1, tq), F32)
    c_pos = count_f32(zero, strict=True)
    zero_tie = (c_pos < TOPK_MAX) & (count_f32(zero) >= TOPK_MAX)
    skip1 = (colq >= q_valid) | zero_tie

    def level1(it, t):
        cand = t + lax.shift_left(jnp.int32(1), 15 - it)
        return jnp.where(count_bf16(bf16_of_key(cand)) >= TOPK_MAX, cand, t)

    t1 = lax.fori_loop(0, 16, level1, jnp.full((1, tq), MIN16, I32))
    settled1 = skip1 | (t1 <= HI_NEG_INF)

    def level2(st):
        lo, hi, thr_key, done = st
        live = (done == 0) & (hi - lo > 1)
        mid = lo + lax.shift_right_arithmetic(hi - lo, 1)
        c = count_f32(f32_of_key(mid))
        hit = live & (c == TOPK_MAX)
        return (jnp.where(live & (c >= TOPK_MAX), mid, lo), jnp.where(live & (c < TOPK_MAX), mid, hi),
                jnp.where(hit, mid, thr_key), jnp.where(hit, 1, done))

    def n_live(st):
        lo, hi, _, done = st
        return jnp.sum(jnp.where((done == 0) & (hi - lo > 1), 1, 0))

    def level2_pair(carry):
        st = level2(level2(carry[0]))
        return st, n_live(st)

    def key32_of_key16(k):
        return lax.shift_left(k, 16) | jnp.where(k < 0, 0xFFFF, 0)

    key_t1 = key32_of_key16(t1)
    st0 = (key_t1 - 0x8000, key32_of_key16(t1 + 1), key_t1, jnp.where(settled1, 1, 0))
    (lo, _, thr_key, done2), _ = lax.while_loop(lambda carry: carry[1] > 0, level2_pair, (st0, n_live(st0)))
    open2 = done2 == 0
    thr = jnp.where(open2, f32_of_key(lo), f32_of_key(thr_key))
    thr = jnp.where(zero_tie, 0.0, thr)
    thr = jnp.where(settled1 & jnp.logical_not(zero_tie), -jnp.inf, thr)
    thr = jnp.maximum(thr, float(jnp.finfo(F32).min))

    keep_ref[...] = jnp.where(zero_tie, TOPK_MAX - c_pos, 2 ** 30).astype(F32)

    @pl.when(jnp.sum(jnp.where(open2, 1, 0)) > 0)
    def _():
        above = count_f32(thr, strict=True)
        keep_ref[...] = jnp.where(open2, (TOPK_MAX - above).astype(F32), keep_ref[...])

    @pl.when(jnp.sum(jnp.where(open2 | zero_tie, 1, 0)) > 0)
    def _():
        keep = keep_ref[...]
        lower = (lax.broadcasted_iota(I32, (TILE, TILE), 0) > lax.broadcasted_iota(I32, (TILE, TILE), 1))
        lower = jnp.where(lower, 1.0, 0.0).astype(BF16)

        def body(jj, run):
            for j in (2 * jj, 2 * jj + 1):
                blk = sc_ref[j]
                eq = blk == thr
                eq_f = jnp.where(eq, 1.0, 0.0)
                before = jnp.dot(lower, eq_f.astype(BF16), preferred_element_type=F32)
                sc_ref[j] = jnp.where(eq & (run + before >= keep), -jnp.inf, blk)
                run = run + eq_f.reshape(TILE // 8, 8, tq).sum(axis=0).sum(axis=0, keepdims=True)
            return run

        lax.fori_loop(0, (nk + 1) // 2, body, jnp.zeros((1, tq), F32))

    for g in range(B_KV_HEADS):
        m_ref[g] = jnp.full((1, B_GROUP * tq), NEG, F32)
        acc_ref[g] = jnp.zeros((2 * HEAD_DIM, B_GROUP * tq), F32)

    def attend(tiles):
        kts = [kb_ref[0, pl.ds(pl.multiple_of(j * TILE, TILE), TILE), :] for j, _ in tiles]
        sels = [sc_ref[j] >= thr for j, _ in tiles]
        for g in range(B_KV_HEADS):
            qs = qbs_ref[g]
            ss = []
            for (j, near), kt, sel in zip(tiles, kts, sels):
                st = lax.dot_general(kt[:, g * HEAD_DIM:(g + 1) * HEAD_DIM], qs, _NT,
                                     preferred_element_type=F32)
                parts = []
                for hh in range(B_GROUP):
                    s_h = st[:, hh * tq:(hh + 1) * tq]
                    if near is not None:
                        s_h = s_h + bias_ref[g * B_GROUP + hh, near * TILE:(near + 1) * TILE, :]
                    parts.append(jnp.where(sel, s_h, -jnp.inf))
                ss.append(jnp.concatenate(parts, axis=1))
            m_old = m_ref[g]
            m_new = m_old
            for s in ss:
                m_new = jnp.maximum(m_new, s.max(axis=0, keepdims=True))
            pv = None
            for (j, _), s in zip(tiles, ss):
                p = jnp.exp(s - m_new).astype(BF16)
                d = jnp.dot(vt_ref[g, j], p, preferred_element_type=F32)
                pv = d if pv is None else pv + d
            acc_ref[g] = jnp.exp(m_old - m_new) * acc_ref[g] + pv
            m_ref[g] = m_new

    n_far = jnp.maximum(nk - 2, 0)

    def far_quad(jj, c):
        attend([(4 * jj + t, None) for t in range(4)])
        return c

    lax.fori_loop(0, n_far // 4, far_quad, 0)
    rem = n_far % 4

    @pl.when(rem >= 2)
    def _():
        attend([(n_far - rem, None), (n_far - rem + 1, None)])

    @pl.when(rem % 2 == 1)
    def _():
        attend([(n_far - 1, None)])

    @pl.when(nk >= 2)
    def _():
        attend([(nk - 2, 0), (nk - 1, 1)])

    @pl.when(nk < 2)
    def _():
        attend([(nk - 1, 1)])

    outs = []
    for g in range(B_KV_HEADS):
        for hh in range(B_GROUP):
            blk = acc_ref[g, :, hh * tq:(hh + 1) * tq].T
            outs.append(blk[:, 0:HEAD_DIM] / blk[:, HEAD_DIM:HEAD_DIM + 1])
    o_ref[0] = jnp.concatenate(outs, axis=-1).astype(BF16)


def _dsa_attn(qi, wi, qb, ki, kb, vb, g_row, off, valid_len, q_valid, tq):
    b, t_q = qi.shape[:2]
    tk = ki.shape[1]
    nq, n_kt = t_q // tq, tk // TILE
    assert tq == TILE or nq == 1
    assert n_kt % 2 == 0 and tk == n_kt * TILE
    qspec = lambda n: pl.BlockSpec((1, tq, n), lambda bb, i: (bb, i, 0))
    kspec = lambda n: pl.BlockSpec((1, tk, n), lambda bb, i: (bb, 0, 0))
    return pl.pallas_call(
        functools.partial(_dsa_kernel, off=off, valid_len=valid_len, q_valid=q_valid, n_kt=n_kt, tq=tq),
        grid=(b, nq),
        in_specs=[pl.BlockSpec((B_HEADS, DSA_ROLL), lambda bb, i: (0, 0)),
                  qspec(512), qspec(128), qspec(512), kspec(IDX_DIM), kspec(128), kspec(128)],
        out_specs=pl.BlockSpec((1, tq, B_WIDTH), lambda bb, i: (bb, i, 0)),
        out_shape=jax.ShapeDtypeStruct((b, t_q, B_WIDTH), BF16),
        scratch_shapes=[pltpu.VMEM((n_kt, TILE, tq), F32),
                        pltpu.VMEM((n_kt, TILE, tq), BF16),
                        pltpu.VMEM((1, tq), F32),
                        pltpu.VMEM((B_KV_HEADS, n_kt, 2 * HEAD_DIM, TILE), BF16),
                        pltpu.VMEM((IDX_HEADS * tq, IDX_DIM), BF16),
                        pltpu.VMEM((B_KV_HEADS, B_GROUP * tq, HEAD_DIM), BF16),
                        pltpu.VMEM((B_KV_HEADS, 1, B_GROUP * tq), F32),
                        pltpu.VMEM((B_KV_HEADS, 2 * HEAD_DIM, B_GROUP * tq), F32),
                        pltpu.VMEM((B_HEADS, 2 * TILE, tq), F32)],
        compiler_params=_params("arbitrary", "arbitrary"),
        name="dsa_attn",
    )(g_row, qi, wi, qb, ki, kb, vb)


def _out_proj_kernel(x_ref, oa_ref, ob_ref, w_ref, g_ref, b_ref, o_ref):
    mix = jnp.concatenate([oa_ref[...], ob_ref[...]], axis=-1)
    y = jnp.dot(mix, w_ref[...], preferred_element_type=F32)
    o_ref[...] = _layer_norm(ALPHA * x_ref[...] + y, g_ref[...], b_ref[...])


def _out_proj_ln(x2d, oa, ob, w, g, b, tm):
    r = x2d.shape[0]
    row = lambda n: pl.BlockSpec((tm, n), lambda i: (i, 0))
    const = lambda s: pl.BlockSpec(s, lambda i: (0, 0))
    return pl.pallas_call(
        _out_proj_kernel,
        grid=(r // tm,),
        in_specs=[row(D_MODEL), row(A_WIDTH), row(B_WIDTH), const((A_WIDTH + B_WIDTH, D_MODEL)),
                  const((1, D_MODEL)), const((1, D_MODEL))],
        out_specs=row(D_MODEL),
        out_shape=jax.ShapeDtypeStruct((r, D_MODEL), F32),
        compiler_params=_params("arbitrary"),
        name="out_proj_ln",
    )(x2d, oa, ob, w, g, b)


def _mem_kv_kernel(m_ref, wk_ref, wv_ref, k_o, v_o, kb_o, vb_o):
    mb = m_ref[...].astype(BF16)
    k = jnp.dot(mb, wk_ref[...], preferred_element_type=F32)
    v = jnp.dot(mb, wv_ref[...], preferred_element_type=F32)
    k_o[...] = k
    v_o[...] = v
    kb_o[...] = k.astype(BF16)
    vb_o[...] = v.astype(BF16)


def _mem_kv(mem2d, wk, wv):
    r = mem2d.shape[0]
    tm = MEM_LEN
    row = lambda n: pl.BlockSpec((tm, n), lambda i: (i, 0))
    const = lambda s: pl.BlockSpec(s, lambda i: (0, 0))
    sds = jax.ShapeDtypeStruct
    return pl.pallas_call(
        _mem_kv_kernel,
        grid=(r // tm,),
        in_specs=[row(D_MODEL), const((D_MODEL, MEM_WIDTH)), const((D_MODEL, MEM_WIDTH))],
        out_specs=[row(MEM_WIDTH)] * 4,
        out_shape=[sds((r, MEM_WIDTH), F32), sds((r, MEM_WIDTH), F32),
                   sds((r, MEM_WIDTH), BF16), sds((r, MEM_WIDTH), BF16)],
        compiler_params=_params("arbitrary"),
        name="mem_kv",
    )(mem2d, wk, wv)


def _mem_attn_kernel(h_ref, mk_ref, mv_ref, wq_ref, wo_ref, g_ref, b_ref, o_ref):
    h = h_ref[0]
    q = jnp.dot(h.astype(BF16), wq_ref[...], preferred_element_type=F32).astype(BF16)
    outs = []
    for hd in range(MEM_HEADS):
        sl = slice(hd * MEM_HEAD_DIM, (hd + 1) * MEM_HEAD_DIM)
        s = lax.dot_general(q[:, sl], mk_ref[0, :, sl], _NT, preferred_element_type=F32) * MEM_HEAD_DIM ** -0.5
        p = jnp.exp(s - s.max(-1, keepdims=True))
        l = p.sum(-1, keepdims=True)
        outs.append(jnp.dot(p.astype(BF16), mv_ref[0, :, sl], preferred_element_type=F32) / l)
    o = jnp.concatenate(outs, axis=-1).astype(BF16)
    y = jnp.dot(o, wo_ref[...], preferred_element_type=F32)
    o_ref[0] = _layer_norm(ALPHA * h + y, g_ref[...], b_ref[...])


def _mem_attn_ln(h3d, mk, mv, wq, wo, g, b, tm):
    bsz, t = h3d.shape[:2]
    const2 = lambda s: pl.BlockSpec(s, lambda bb, i: (0, 0))
    return pl.pallas_call(
        _mem_attn_kernel,
        grid=(bsz, t // tm),
        in_specs=[pl.BlockSpec((1, tm, D_MODEL), lambda bb, i: (bb, i, 0)),
                  pl.BlockSpec((1, MEM_LEN, MEM_WIDTH), lambda bb, i: (bb, 0, 0)),
                  pl.BlockSpec((1, MEM_LEN, MEM_WIDTH), lambda bb, i: (bb, 0, 0)),
                  const2((D_MODEL, MEM_WIDTH)), const2((MEM_WIDTH, D_MODEL)),
                  const2((1, D_MODEL)), const2((1, D_MODEL))],
        out_specs=pl.BlockSpec((1, tm, D_MODEL), lambda bb, i: (bb, i, 0)),
        out_shape=jax.ShapeDtypeStruct(h3d.shape, F32),
        compiler_params=_params("arbitrary", "arbitrary"),
        name="mem_attn_ln",
    )(h3d, mk, mv, wq, wo, g, b)


FF_CHUNK = 256


def _ffn_kernel(h_ref, hist_ref, wu_ref, wc_ref, bc_ref, wd_ref, g_ref, b_ref, o_ref, tail_ref,
                carry_ref, act_ref, *, tiles_per_batch, seg):
    i = pl.program_id(0)
    tm = h_ref.shape[0]
    nseg = tm // seg
    h = h_ref[...]
    hb = h.astype(BF16)
    row = lax.broadcasted_iota(I32, (tm, 1), 0)
    first = (i % tiles_per_batch) == 0
    for c in range(D_FF // FF_CHUNK):
        cs = slice(c * FF_CHUNK, (c + 1) * FF_CHUNK)
        u = jnp.dot(hb, wu_ref[:, cs], preferred_element_type=F32)
        gt = jnp.dot(hb, wu_ref[:, D_FF + c * FF_CHUNK:D_FF + (c + 1) * FF_CHUNK], preferred_element_type=F32)
        p1 = pltpu.roll(gt, 1, 0)
        p2 = pltpu.roll(gt, 2, 0)
        for s in range(nseg):
            hist = hist_ref[s, :, cs]
            if tiles_per_batch > 1:
                hist = jnp.where(first, hist, carry_ref[:, cs])
            p1 = jnp.where(row == s * seg, hist[7:8, :], p1)
            p2 = jnp.where(row == s * seg, hist[6:7, :], p2)
            p2 = jnp.where(row == s * seg + 1, hist[7:8, :], p2)
        gc = bc_ref[:, cs] + ((wc_ref[0:1, cs] * p2 + wc_ref[1:2, cs] * p1) + wc_ref[2:3, cs] * gt)
        act_ref[:, cs] = (u * jax.nn.gelu(gc)).astype(BF16)
        for s in range(nseg):
            tail_ref[s, :, cs] = gt[(s + 1) * seg - 8:(s + 1) * seg, :]
        carry_ref[:, cs] = gt[tm - 8:tm, :]
    f = jnp.dot(act_ref[...], wd_ref[...], preferred_element_type=F32)
    o_ref[...] = _layer_norm(ALPHA * h + f, g_ref[...], b_ref[...])


def _ffn_ln(h2d, hist, wu, wc, bc, wd, g, b, tm, tiles_per_batch, seg):
    r = h2d.shape[0]
    nseg = tm // seg
    n_tail = r // (tm * tiles_per_batch) * nseg
    row = lambda n: pl.BlockSpec((tm, n), lambda i: (i, 0))
    const = lambda s: pl.BlockSpec(s, lambda i: (0,) * len(s))
    return pl.pallas_call(
        functools.partial(_ffn_kernel, tiles_per_batch=tiles_per_batch, seg=seg),
        grid=(r // tm,),
        in_specs=[row(D_MODEL),
                  pl.BlockSpec((nseg, 8, D_FF), lambda i: (i // tiles_per_batch, 0, 0)),
                  const((D_MODEL, 2 * D_FF)), const((CONV_W, D_FF)), const((1, D_FF)), const((D_FF, D_MODEL)),
                  const((1, D_MODEL)), const((1, D_MODEL))],
        out_specs=[row(D_MODEL), pl.BlockSpec((nseg, 8, D_FF), lambda i: (i // tiles_per_batch, 0, 0))],
        out_shape=[jax.ShapeDtypeStruct((r, D_MODEL), F32), jax.ShapeDtypeStruct((n_tail, 8, D_FF), F32)],
        scratch_shapes=[pltpu.VMEM((8, D_FF), F32), pltpu.VMEM((tm, D_FF), BF16)],
        compiler_params=_params("arbitrary"),
        name="ffn_ln",
    )(h2d, hist, wu, wc, bc, wd, g, b)


def _pad_rows(a, n):
    return jnp.pad(a, ((0, 0), (0, n - a.shape[1])) + ((0, 0),) * (a.ndim - 2))


def _hist8(g_hist):
    return jnp.pad(g_hist, ((0, 0), (8 - g_hist.shape[1], 0), (0, 0)))


def kernel(x_prompt, x_sample, cache_a_k, cache_a_v, cache_b_k, cache_b_v, cache_b_kidx, cache_mem_k, cache_mem_v, state_ffn_conv, mem_prompt, w_in, a_rel_bias, t5_bias, w_o, ln1_g, ln1_b, w_mq, w_mk, w_mv, w_mo, ln2_g, ln2_b, w_up, w_conv, b_conv, w_down, ln3_g, ln3_b):
    bp, tp = x_prompt.shape[:2]
    bs, ts = x_sample.shape[:2]
    l = 0
    vec = lambda a: a[l].reshape(1, -1)
    w_in_p = _prep_w_in(w_in[l])
    w_o_b = w_o[l].astype(BF16)
    w_mq_b, w_mk_b, w_mv_b, w_mo_b = (w[l].astype(BF16) for w in (w_mq, w_mk, w_mv, w_mo))
    w_up_b, w_down_b = w_up[l].astype(BF16), w_down[l].astype(BF16)
    band_row = _band_bias_row(a_rel_bias[l])
    dsa_row = _dsa_bias_row(t5_bias)
    ffn_w = (w_up_b, w_conv[l], vec(b_conv), w_down_b, vec(ln3_g), vec(ln3_b))

    tm = 512
    a_keep = min(N_PREV_CHUNKS * CHUNK, tp)
    (qa, ka, va, qb, qi, kb, vb, ki, kb_b, vb_b, ki_b, wi, ka_tail, va_tail) = _in_proj(
        x_prompt.reshape(bp * tp, D_MODEL), w_in_p, tm, tp // tm)
    r3 = lambda a: a.reshape(bp, tp, a.shape[-1])
    oa = _band_attn(r3(qa), r3(ka), r3(va), band_row, 0, tp)
    ob = _dsa_attn(r3(qi), r3(wi), r3(qb), r3(ki_b), r3(kb_b), r3(vb_b), dsa_row, 0, tp, TILE, TILE)
    h = _out_proj_ln(x_prompt.reshape(bp * tp, D_MODEL), oa.reshape(bp * tp, A_WIDTH), ob.reshape(bp * tp, B_WIDTH),
                     w_o_b, vec(ln1_g), vec(ln1_b), tm)
    mk, mv, mk_b, mv_b = _mem_kv(mem_prompt.reshape(bp * MEM_LEN, D_MODEL), w_mk_b, w_mv_b)
    h = _mem_attn_ln(h.reshape(bp, tp, D_MODEL), mk_b.reshape(bp, MEM_LEN, MEM_WIDTH),
                     mv_b.reshape(bp, MEM_LEN, MEM_WIDTH), w_mq_b, w_mo_b, vec(ln2_g), vec(ln2_b), TILE)
    xp, p_tail = _ffn_ln(h.reshape(bp * tp, D_MODEL), jnp.zeros((bp, 8, D_FF), F32), *ffn_w, tm, tp // tm, tm)
    prompt_state = (
        ka_tail.reshape(bp, tm, A_HEADS, HEAD_DIM)[:, tm - a_keep:][None],
        va_tail.reshape(bp, tm, A_HEADS, HEAD_DIM)[:, tm - a_keep:][None],
        kb.reshape(1, bp, tp, B_KV_HEADS, HEAD_DIM), vb.reshape(1, bp, tp, B_KV_HEADS, HEAD_DIM),
        ki.reshape(1, bp, tp, IDX_DIM),
        mk.reshape(1, bp, MEM_LEN, MEM_HEADS, MEM_HEAD_DIM), mv.reshape(1, bp, MEM_LEN, MEM_HEADS, MEM_HEAD_DIM),
        p_tail[:, 8 - (CONV_W - 1):][None])

    rs = bs * ts
    (qa, ka, va, qb, qi, kb, vb, ki, kb_b, vb_b, ki_b, wi, ka_new, va_new) = _in_proj(
        x_sample.reshape(rs, D_MODEL), w_in_p, rs, 1)
    s3 = lambda a: a.reshape(bs, ts, a.shape[-1])
    qpad = lambda a, n: _pad_rows(s3(a), n)

    past_a = cache_a_k.shape[2]
    n_a = past_a + ts
    t_a = -(-n_a // TILE) * TILE
    seq_a = lambda cache, new: _pad_rows(
        jnp.concatenate([cache[l].reshape(bs, past_a, A_WIDTH), s3(new)], axis=1), t_a).astype(BF16)
    oa = _band_attn(qpad(qa, TILE), seq_a(cache_a_k, ka_new), seq_a(cache_a_v, va_new), band_row,
                    past_a // TILE, n_a)

    past_b = cache_b_k.shape[2]
    n_b = past_b + ts
    t_b = -(-n_b // (2 * TILE)) * 2 * TILE
    tq_s = LANES
    seq_b = lambda cache, new: _pad_rows(
        jnp.concatenate([cache[l].reshape(bs, past_b, -1), s3(new)], axis=1), t_b).astype(BF16)
    ob = _dsa_attn(qpad(qi, tq_s), qpad(wi, tq_s), qpad(qb, tq_s), seq_b(cache_b_kidx, ki), seq_b(cache_b_k, kb),
                   seq_b(cache_b_v, vb), dsa_row, past_b // TILE, n_b, ts, tq_s)

    h = _out_proj_ln(x_sample.reshape(rs, D_MODEL), oa[:, :ts].reshape(rs, A_WIDTH), ob[:, :ts].reshape(rs, B_WIDTH),
                     w_o_b, vec(ln1_g), vec(ln1_b), rs)
    h = _mem_attn_ln(h.reshape(bs, ts, D_MODEL), cache_mem_k[l].reshape(bs, MEM_LEN, MEM_WIDTH).astype(BF16),
                     cache_mem_v[l].reshape(bs, MEM_LEN, MEM_WIDTH).astype(BF16),
                     w_mq_b, w_mo_b, vec(ln2_g), vec(ln2_b), ts)
    xs, s_tail = _ffn_ln(h.reshape(rs, D_MODEL), _hist8(state_ffn_conv[l]), *ffn_w, rs, 1, ts)
    sample_state = (
        ka_new.reshape(1, bs, ts, A_HEADS, HEAD_DIM), va_new.reshape(1, bs, ts, A_HEADS, HEAD_DIM),
        kb.reshape(1, bs, ts, B_KV_HEADS, HEAD_DIM), vb.reshape(1, bs, ts, B_KV_HEADS, HEAD_DIM),
        ki.reshape(1, bs, ts, IDX_DIM), s_tail[:, 8 - (CONV_W - 1):][None])

    return (xp.reshape(bp, tp, D_MODEL), xs.reshape(bs, ts, D_MODEL)) + prompt_state + sample_state
```

```python
import functools
import math

import jax
import jax.numpy as jnp
from jax import lax
from jax.experimental import pallas as pl
from jax.experimental.pallas import tpu as pltpu

F32 = jnp.float32
BF16 = jnp.bfloat16
I32 = jnp.int32
I16 = jnp.int16

D_MODEL = 1024
CHUNK = 64
N_PREV_CHUNKS = 8
HEAD_DIM = 64
A_HEADS = 8
A_WIDTH = A_HEADS * HEAD_DIM
A_MAX_REL = 64
B_HEADS = 8
B_KV_HEADS = 2
B_GROUP = B_HEADS // B_KV_HEADS
B_WIDTH = B_HEADS * HEAD_DIM
B_KV_WIDTH = B_KV_HEADS * HEAD_DIM
IDX_HEADS = 8
IDX_DIM = 64
TOPK_MAX = 256
N_BUCKETS = 32
T5_MAX_DIST = 128
MEM_LEN = 256
MEM_HEADS = 4
MEM_HEAD_DIM = 128
MEM_WIDTH = MEM_HEADS * MEM_HEAD_DIM
D_FF = 2816
CONV_W = 3
IN_SIZES = (A_WIDTH, A_WIDTH, A_WIDTH, B_WIDTH, B_KV_WIDTH, B_KV_WIDTH, IDX_HEADS * IDX_DIM, IDX_DIM, IDX_HEADS)
DEPTH = 1
ALPHA = (2 * DEPTH) ** 0.25
LN_EPS = 1e-5
ATTN_SCALE = HEAD_DIM ** -0.5
NEG = -1e30

LANES = 128
TILE = 256
BAND_TILES = 1 + (N_PREV_CHUNKS * CHUNK) // TILE
VMEM_LIMIT = 56 * 1024 * 1024

_C_QA, _C_KA, _C_VA, _C_QB = 0, 512, 1024, 1536
_C_KB, _C_VB, _C_QI, _C_KI, _C_WI = 2048, 2176, 2304, 2816, 2944
IN_PAD = 3072

MIN16 = -32768
HI_NEG_INF = -32641

_NT = (((1,), (1,)), ((), ()))


def _params(*sem):
    return pltpu.CompilerParams(dimension_semantics=sem, vmem_limit_bytes=VMEM_LIMIT)


def _layer_norm(z, g, b):
    mu = jnp.mean(z, axis=-1, keepdims=True)
    d = z - mu
    var = jnp.mean(d * d, axis=-1, keepdims=True)
    return d * lax.rsqrt(var + LN_EPS) * g + b


def _toeplitz(g_row, rows, width):
    return pltpu.roll(jnp.broadcast_to(g_row, (rows, width)), 0, 1, stride=1, stride_axis=0)


def _in_proj_kernel(x_ref, w_ref, qa_o, ka_o, va_o, qb_o, qi_o, kb_o, vb_o, ki_o, kbb_o, vbb_o, kib_o,
                    wi_o, kat_o, vat_o, *, tiles_per_batch, wi_scale):
    i = pl.program_id(0)
    xb = x_ref[...].astype(BF16)

    def mm(c0, n):
        return jnp.dot(xb, w_ref[:, c0:c0 + n], preferred_element_type=F32)

    qa_o[...] = mm(_C_QA, 512).astype(BF16)
    ka = mm(_C_KA, 512)
    va = mm(_C_VA, 512)
    ka_o[...] = ka.astype(BF16)
    va_o[...] = va.astype(BF16)
    qb_o[...] = mm(_C_QB, 512).astype(BF16)
    qi_o[...] = mm(_C_QI, 512).astype(BF16)
    kb = mm(_C_KB, 128)
    vb = mm(_C_VB, 128)
    kb_o[...] = kb
    vb_o[...] = vb
    kbb_o[...] = kb.astype(BF16)
    vbb_o[...] = vb.astype(BF16)
    ki = mm(_C_KI, 128)[:, :IDX_DIM]
    ki_o[...] = ki
    kib_o[...] = ki.astype(BF16)
    wi_o[...] = mm(_C_WI, 128) * wi_scale

    @pl.when(i % tiles_per_batch == tiles_per_batch - 1)
    def _():
        kat_o[...] = ka
        vat_o[...] = va


def _prep_w_in(w):
    parts, off = [], 0
    for n in IN_SIZES:
        parts.append(w[:, off:off + n])
        off += n
    qa, ka, va, qb, kb, vb, qi, ki, wi = parts
    pad = lambda a, n: jnp.pad(a, ((0, 0), (0, n - a.shape[1])))
    cols = [qa * ATTN_SCALE, ka, va, qb * ATTN_SCALE, kb, vb, qi * IDX_DIM ** -0.5, pad(ki, 128), pad(wi, 128)]
    return jnp.concatenate(cols, axis=1).astype(BF16)


def _in_proj(x2d, w_pad, tm, tiles_per_batch):
    r = x2d.shape[0]
    n_tiles = r // tm
    n_batch = n_tiles // tiles_per_batch
    row = lambda n: pl.BlockSpec((tm, n), lambda i: (i, 0))
    tail = pl.BlockSpec((tm, 512), lambda i: (i // tiles_per_batch, 0))
    sds = jax.ShapeDtypeStruct
    out_shape = [sds((r, 512), BF16)] * 5 + [sds((r, 128), F32), sds((r, 128), F32), sds((r, IDX_DIM), F32),
                                              sds((r, 128), BF16), sds((r, 128), BF16), sds((r, IDX_DIM), BF16),
                                              sds((r, 128), F32),
                                              sds((n_batch * tm, 512), F32), sds((n_batch * tm, 512), F32)]
    out_specs = [row(512)] * 5 + [row(128), row(128), row(IDX_DIM), row(128), row(128), row(IDX_DIM), row(128),
                                  tail, tail]
    return pl.pallas_call(
        functools.partial(_in_proj_kernel, tiles_per_batch=tiles_per_batch, wi_scale=IDX_HEADS ** -0.5),
        grid=(n_tiles,),
        in_specs=[pl.BlockSpec((tm, D_MODEL), lambda i: (i, 0)),
                  pl.BlockSpec((D_MODEL, IN_PAD), lambda i: (0, 0))],
        out_specs=out_specs,
        out_shape=out_shape,
        compiler_params=_params("arbitrary"),
        name="in_proj",
    )(x2d, w_pad)


BAND_COLS = BAND_TILES * TILE
BAND_ROLL = BAND_COLS + TILE


def _band_bias_row(table):
    idx = jnp.arange(BAND_ROLL)
    d = jnp.where(idx < TILE, idx, idx - BAND_ROLL)
    rel = (BAND_TILES - 1) * TILE + d
    return table[jnp.clip(rel, -A_MAX_REL, A_MAX_REL) + A_MAX_REL].T.astype(F32)


def _band_kernel(g_ref, q_ref, k0, k1, k2, v0, v1, v2, o_ref, bias_ref, vt_ref, *, off, valid_len):
    i = pl.program_id(1)
    kt = i + off
    krefs, vrefs = (k0, k1, k2), (v0, v1, v2)

    @pl.when((pl.program_id(0) == 0) & (i == 0))
    def _():
        c = lax.broadcasted_iota(I32, (BAND_COLS, TILE), 0) // CHUNK
        r = lax.broadcasted_iota(I32, (BAND_COLS, TILE), 1) // CHUNK
        ok = (c >= r) & (c <= r + N_PREV_CHUNKS)
        for h in range(A_HEADS):
            bias_ref[h] = jnp.where(ok, _toeplitz(g_ref[h:h + 1, :], BAND_COLS, BAND_ROLL)[:, :TILE], NEG)
        ones = jnp.ones((HEAD_DIM, TILE), BF16)
        for s in range(BAND_TILES):
            for h in range(A_HEADS):
                vt_ref[s, h, HEAD_DIM:2 * HEAD_DIM, :] = ones

    def put(slot, vref):
        vt = vref[0].astype(F32).T
        for h in range(A_HEADS):
            vt_ref[slot, h, 0:HEAD_DIM, :] = vt[h * HEAD_DIM:(h + 1) * HEAD_DIM].astype(BF16)

    slots = [(kt + 1 + j) % BAND_TILES for j in range(BAND_TILES)]

    @pl.when(i == 0)
    def _():
        for j in range(BAND_TILES - 1):
            put(slots[j], vrefs[j])

    put(slots[BAND_TILES - 1], vrefs[BAND_TILES - 1])

    base = (kt - (BAND_TILES - 1)) * TILE

    n_slab = 4
    slab_w = n_slab * HEAD_DIM
    lane_head = lax.broadcasted_iota(I32, (TILE, slab_w), 1) // HEAD_DIM

    def attend(masked):
        outs = []
        for g in range(A_HEADS // n_slab):
            gs = slice(g * slab_w, (g + 1) * slab_w)
            q_slab = q_ref[0, :, gs].astype(F32)
            q_bd = jnp.concatenate([jnp.where(lane_head == hh, q_slab, 0.0) for hh in range(n_slab)],
                                   axis=0).astype(BF16)
            s = []
            for j in range(BAND_TILES):
                st = lax.dot_general(krefs[j][0, :, gs], q_bd, _NT, preferred_element_type=F32)
                parts = [st[:, hh * TILE:(hh + 1) * TILE] + bias_ref[g * n_slab + hh, j * TILE:(j + 1) * TILE, :]
                         for hh in range(n_slab)]
                if masked:
                    kpos = base + j * TILE + lax.broadcasted_iota(I32, (TILE, TILE), 0)
                    ok = (kpos >= 0) & (kpos < valid_len)
                    parts = [jnp.where(ok, x, NEG) for x in parts]
                s.append(jnp.concatenate(parts, axis=1))
            m = jnp.maximum(jnp.maximum(s[0], s[1]), s[2]).max(axis=0, keepdims=True)
            p = [jnp.exp(sj - m).astype(BF16) for sj in s]
            for hh in range(n_slab):
                acc = None
                for j in range(BAND_TILES):
                    d = jnp.dot(vt_ref[slots[j], g * n_slab + hh], p[j][:, hh * TILE:(hh + 1) * TILE],
                                preferred_element_type=F32)
                    acc = d if acc is None else acc + d
                blk = acc.T
                outs.append(blk[:, 0:HEAD_DIM] / blk[:, HEAD_DIM:HEAD_DIM + 1])
        o_ref[0] = jnp.concatenate(outs, axis=-1).astype(BF16)

    needs_mask = (base < 0) | (base + BAND_COLS > valid_len)

    @pl.when(needs_mask)
    def _():
        attend(True)

    @pl.when(jnp.logical_not(needs_mask))
    def _():
        attend(False)


def _band_attn(q, k, v, g_row, off, valid_len):
    b, tq = q.shape[:2]
    nq = tq // TILE
    qspec = pl.BlockSpec((1, TILE, A_WIDTH), lambda bb, i: (bb, i, 0))
    kspec = lambda d: pl.BlockSpec((1, TILE, A_WIDTH), lambda bb, i: (bb, jnp.maximum(i + off - d, 0), 0))
    return pl.pallas_call(
        functools.partial(_band_kernel, off=off, valid_len=valid_len),
        grid=(b, nq),
        in_specs=[pl.BlockSpec((A_HEADS, BAND_ROLL), lambda bb, i: (0, 0)),
                  qspec, kspec(2), kspec(1), kspec(0), kspec(2), kspec(1), kspec(0)],
        out_specs=pl.BlockSpec((1, TILE, A_WIDTH), lambda bb, i: (bb, i, 0)),
        out_shape=jax.ShapeDtypeStruct((b, tq, A_WIDTH), BF16),
        scratch_shapes=[pltpu.VMEM((A_HEADS, BAND_COLS, TILE), F32),
                        pltpu.VMEM((BAND_TILES, A_HEADS, 2 * HEAD_DIM, TILE), BF16)],
        compiler_params=_params("arbitrary", "arbitrary"),
        name="band_attn",
    )(g_row, q, k, k, k, v, v, v)


DSA_ROLL = 3 * TILE


def _t5_bucket(rel):
    half = N_BUCKETS // 2
    max_exact = half // 2
    n = jnp.abs(rel)
    log_ratio = jnp.log(jnp.maximum(n, 1).astype(jnp.float32) / max_exact) / math.log(T5_MAX_DIST / max_exact)
    large = jnp.minimum(max_exact + (log_ratio * (half - max_exact)).astype(jnp.int32), half - 1)
    return jnp.where(rel < 0, half, 0) + jnp.where(n < max_exact, n, large)


def _dsa_bias_row(t5_table):
    idx = jnp.arange(DSA_ROLL)
    d = jnp.where(idx < TILE, idx, idx - DSA_ROLL)
    far = t5_table[_t5_bucket(jnp.full((1,), 2 * TILE + 1, I32))]
    return (t5_table[_t5_bucket(TILE + d)] - far).T.astype(F32)


def _dsa_kernel(g_ref, qi_ref, wi_ref, qb_ref, ki_ref, kb_ref, vb_ref, o_ref,
                sc_ref, scb_ref, keep_ref, vt_ref, qis_ref, qbs_ref, m_ref, acc_ref, bias_ref,
                *, off, valid_len, q_valid, n_kt, tq):
    i = pl.program_id(1)
    qt = i + off
    q0 = qt * TILE
    nk = qt + 1

    @pl.when((pl.program_id(0) == 0) & (i == 0))
    def _():
        for h in range(B_HEADS):
            bias_ref[h] = _toeplitz(g_ref[h:h + 1, :], 2 * TILE, DSA_ROLL)[:, :tq]

    @pl.when(i == 0)
    def _():
        ones = jnp.ones((HEAD_DIM, TILE), BF16)

        def body(j, c):
            vt = vb_ref[0, pl.ds(pl.multiple_of(j * TILE, TILE), TILE), :].astype(F32).T
            for g in range(B_KV_HEADS):
                vt_ref[g, j, 0:HEAD_DIM, :] = vt[g * HEAD_DIM:(g + 1) * HEAD_DIM].astype(BF16)
                vt_ref[g, j, HEAD_DIM:2 * HEAD_DIM, :] = ones
            return c

        lax.fori_loop(0, n_kt, body, 0)

    colq = lax.broadcasted_iota(I32, (1, tq), 1)
    rowk = lax.broadcasted_iota(I32, (TILE, 1), 0)

    qi = qi_ref[0]
    for h in range(IDX_HEADS):
        qis_ref[h * tq:(h + 1) * tq, :] = qi[:, h * IDX_DIM:(h + 1) * IDX_DIM]
    qb = qb_ref[0]
    for g in range(B_KV_HEADS):
        for hh in range(B_GROUP):
            h = g * B_GROUP + hh
            qbs_ref[g, hh * tq:(hh + 1) * tq, :] = qb[:, h * HEAD_DIM:(h + 1) * HEAD_DIM]
    wi_t = wi_ref[0].T
    lim = jnp.minimum(q0 + (colq // CHUNK + 1) * CHUNK, valid_len)

    def score_tile(j):
        kt = ki_ref[0, pl.ds(pl.multiple_of(j * TILE, TILE), TILE), :]
        lg = lax.dot_general(kt, qis_ref[...], _NT, preferred_element_type=F32)
        sc = wi_t[0:1, :] * jnp.maximum(lg[:, 0:tq], 0.0)
        for h in range(1, IDX_HEADS):
            sc = sc + wi_t[h:h + 1, :] * jnp.maximum(lg[:, h * tq:(h + 1) * tq], 0.0)
        sc = jnp.where(j * TILE + rowk < lim, sc, -jnp.inf)
        sc_ref[j] = sc
        scb_ref[j] = sc.astype(BF16)

    def score_pair(jj, c):
        score_tile(2 * jj)
        score_tile(2 * jj + 1)
        return c

    lax.fori_loop(0, (nk + 1) // 2, score_pair, 0)

    def f32_of_key(k):
        return pltpu.bitcast(jnp.where(k < 0, k ^ 0x7FFFFFFF, k), F32)

    def bf16_of_key(k):
        bits = jnp.where(k < 0, k ^ 0x7FFF, k) & 0xFFFF
        return pltpu.bitcast(lax.shift_left(bits, 16), F32).astype(BF16)

    def count_bf16(cand):
        def body(j, acc):
            ge = jnp.where(scb_ref[j] >= cand, jnp.int16(1), jnp.int16(0)).reshape(TILE // 16, 16, tq)
            part = ge[0]
            for r in range(1, TILE // 16):
                part = part + ge[r]
            return acc + part

        acc = lax.fori_loop(0, nk, body, jnp.zeros((16, tq), I16))
        return acc.astype(I32).sum(axis=0, keepdims=True)

    def count_f32(cand, strict=False):
        def body(j, acc):
            blk = sc_ref[j]
            hit = (blk > cand) if strict else (blk >= cand)
            return acc + jnp.where(hit, 1, 0).reshape(TILE // 8, 8, tq).sum(axis=0)

        return lax.fori_loop(0, nk, body, jnp.zeros((8, tq), I32)).sum(axis=0, keepdims=True)

    zero = jnp.zeros((1, tq), F32)
    c_pos = count_f32(zero, strict=True)
    zero_tie = (c_pos < TOPK_MAX) & (count_f32(zero) >= TOPK_MAX)
    skip1 = (colq >= q_valid) | zero_tie

    def level1(it, t):
        cand = t + lax.shift_left(jnp.int32(1), 15 - it)
        return jnp.where(count_bf16(bf16_of_key(cand)) >= TOPK_MAX, cand, t)

    t1 = lax.fori_loop(0, 16, level1, jnp.full((1, tq), MIN16, I32))
    settled1 = skip1 | (t1 <= HI_NEG_INF)

    def level2(st):
        lo, hi, thr_key, done = st
        live = (done == 0) & (hi - lo > 1)
        mid = lo + lax.shift_right_arithmetic(hi - lo, 1)
        c = count_f32(f32_of_key(mid))
        hit = live & (c == TOPK_MAX)
        return (jnp.where(live & (c >= TOPK_MAX), mid, lo), jnp.where(live & (c < TOPK_MAX), mid, hi),
                jnp.where(hit, mid, thr_key), jnp.where(hit, 1, done))

    def n_live(st):
        lo, hi, _, done = st
        return jnp.sum(jnp.where((done == 0) & (hi - lo > 1), 1, 0))

    def level2_pair(carry):
        st = level2(level2(carry[0]))
        return st, n_live(st)

    def key32_of_key16(k):
        return lax.shift_left(k, 16) | jnp.where(k < 0, 0xFFFF, 0)

    key_t1 = key32_of_key16(t1)
    st0 = (key_t1 - 0x8000, key32_of_key16(t1 + 1), key_t1, jnp.where(settled1, 1, 0))
    (lo, _, thr_key, done2), _ = lax.while_loop(lambda carry: carry[1] > 0, level2_pair, (st0, n_live(st0)))
    open2 = done2 == 0
    thr = jnp.where(open2, f32_of_key(lo), f32_of_key(thr_key))
    thr = jnp.where(zero_tie, 0.0, thr)
    thr = jnp.where(settled1 & jnp.logical_not(zero_tie), -jnp.inf, thr)
    thr = jnp.maximum(thr, float(jnp.finfo(F32).min))

    keep_ref[...] = jnp.where(zero_tie, TOPK_MAX - c_pos, 2 ** 30).astype(F32)

    @pl.when(jnp.sum(jnp.where(open2, 1, 0)) > 0)
    def _():
        above = count_f32(thr, strict=True)
        keep_ref[...] = jnp.where(open2, (TOPK_MAX - above).astype(F32), keep_ref[...])

    @pl.when(jnp.sum(jnp.where(open2 | zero_tie, 1, 0)) > 0)
    def _():
        keep = keep_ref[...]
        lower = (lax.broadcasted_iota(I32, (TILE, TILE), 0) > lax.broadcasted_iota(I32, (TILE, TILE), 1))
        lower = jnp.where(lower, 1.0, 0.0).astype(BF16)

        def body(jj, run):
            for j in (2 * jj, 2 * jj + 1):
                blk = sc_ref[j]
                eq = blk == thr
                eq_f = jnp.where(eq, 1.0, 0.0)
                before = jnp.dot(lower, eq_f.astype(BF16), preferred_element_type=F32)
                sc_ref[j] = jnp.where(eq & (run + before >= keep), -jnp.inf, blk)
                run = run + eq_f.reshape(TILE // 8, 8, tq).sum(axis=0).sum(axis=0, keepdims=True)
            return run

        lax.fori_loop(0, (nk + 1) // 2, body, jnp.zeros((1, tq), F32))

    for g in range(B_KV_HEADS):
        m_ref[g] = jnp.full((1, B_GROUP * tq), NEG, F32)
        acc_ref[g] = jnp.zeros((2 * HEAD_DIM, B_GROUP * tq), F32)

    def attend(tiles):
        kts = [kb_ref[0, pl.ds(pl.multiple_of(j * TILE, TILE), TILE), :] for j, _ in tiles]
        sels = [sc_ref[j] >= thr for j, _ in tiles]
        for g in range(B_KV_HEADS):
            qs = qbs_ref[g]
            ss = []
            for (j, near), kt, sel in zip(tiles, kts, sels):
                st = lax.dot_general(kt[:, g * HEAD_DIM:(g + 1) * HEAD_DIM], qs, _NT,
                                     preferred_element_type=F32)
                parts = []
                for hh in range(B_GROUP):
                    s_h = st[:, hh * tq:(hh + 1) * tq]
                    if near is not None:
                        s_h = s_h + bias_ref[g * B_GROUP + hh, near * TILE:(near + 1) * TILE, :]
                    parts.append(jnp.where(sel, s_h, -jnp.inf))
                ss.append(jnp.concatenate(parts, axis=1))
            m_old = m_ref[g]
            m_new = m_old
            for s in ss:
                m_new = jnp.maximum(m_new, s.max(axis=0, keepdims=True))
            pv = None
            for (j, _), s in zip(tiles, ss):
                p = jnp.exp(s - m_new).astype(BF16)
                d = jnp.dot(vt_ref[g, j], p, preferred_element_type=F32)
                pv = d if pv is None else pv + d
            acc_ref[g] = jnp.exp(m_old - m_new) * acc_ref[g] + pv
            m_ref[g] = m_new

    n_far = jnp.maximum(nk - 2, 0)

    def far_quad(jj, c):
        attend([(4 * jj + t, None) for t in range(4)])
        return c

    lax.fori_loop(0, n_far // 4, far_quad, 0)
    rem = n_far % 4

    @pl.when(rem >= 2)
    def _():
        attend([(n_far - rem, None), (n_far - rem + 1, None)])

    @pl.when(rem % 2 == 1)
    def _():
        attend([(n_far - 1, None)])

    @pl.when(nk >= 2)
    def _():
        attend([(nk - 2, 0), (nk - 1, 1)])

    @pl.when(nk < 2)
    def _():
        attend([(nk - 1, 1)])

    outs = []
    for g in range(B_KV_HEADS):
        for hh in range(B_GROUP):
            blk = acc_ref[g, :, hh * tq:(hh + 1) * tq].T
            outs.append(blk[:, 0:HEAD_DIM] / blk[:, HEAD_DIM:HEAD_DIM + 1])
    o_ref[0] = jnp.concatenate(outs, axis=-1).astype(BF16)


def _dsa_attn(qi, wi, qb, ki, kb, vb, g_row, off, valid_len, q_valid, tq):
    b, t_q = qi.shape[:2]
    tk = ki.shape[1]
    nq, n_kt = t_q // tq, tk // TILE
    assert tq == TILE or nq == 1
    assert n_kt % 2 == 0 and tk == n_kt * TILE
    qspec = lambda n: pl.BlockSpec((1, tq, n), lambda bb, i: (bb, i, 0))
    kspec = lambda n: pl.BlockSpec((1, tk, n), lambda bb, i: (bb, 0, 0))
    return pl.pallas_call(
        functools.partial(_dsa_kernel, off=off, valid_len=valid_len, q_valid=q_valid, n_kt=n_kt, tq=tq),
        grid=(b, nq),
        in_specs=[pl.BlockSpec((B_HEADS, DSA_ROLL), lambda bb, i: (0, 0)),
                  qspec(512), qspec(128), qspec(512), kspec(IDX_DIM), kspec(128), kspec(128)],
        out_specs=pl.BlockSpec((1, tq, B_WIDTH), lambda bb, i: (bb, i, 0)),
        out_shape=jax.ShapeDtypeStruct((b, t_q, B_WIDTH), BF16),
        scratch_shapes=[pltpu.VMEM((n_kt, TILE, tq), F32),
                        pltpu.VMEM((n_kt, TILE, tq), BF16),
                        pltpu.VMEM((1, tq), F32),
                        pltpu.VMEM((B_KV_HEADS, n_kt, 2 * HEAD_DIM, TILE), BF16),
                        pltpu.VMEM((IDX_HEADS * tq, IDX_DIM), BF16),
                        pltpu.VMEM((B_KV_HEADS, B_GROUP * tq, HEAD_DIM), BF16),
                        pltpu.VMEM((B_KV_HEADS, 1, B_GROUP * tq), F32),
                        pltpu.VMEM((B_KV_HEADS, 2 * HEAD_DIM, B_GROUP * tq), F32),
                        pltpu.VMEM((B_HEADS, 2 * TILE, tq), F32)],
        compiler_params=_params("arbitrary", "arbitrary"),
        name="dsa_attn",
    )(g_row, qi, wi, qb, ki, kb, vb)


def _mem_kv_kernel(m_ref, wk_ref, wv_ref, k_o, v_o, kb_o, vb_o):
    mb = m_ref[...].astype(BF16)
    k = jnp.dot(mb, wk_ref[...], preferred_element_type=F32)
    v = jnp.dot(mb, wv_ref[...], preferred_element_type=F32)
    k_o[...] = k
    v_o[...] = v
    kb_o[...] = k.astype(BF16)
    vb_o[...] = v.astype(BF16)


def _mem_kv(mem2d, wk, wv):
    r = mem2d.shape[0]
    tm = MEM_LEN
    row = lambda n: pl.BlockSpec((tm, n), lambda i: (i, 0))
    const = lambda s: pl.BlockSpec(s, lambda i: (0, 0))
    sds = jax.ShapeDtypeStruct
    return pl.pallas_call(
        _mem_kv_kernel,
        grid=(r // tm,),
        in_specs=[row(D_MODEL), const((D_MODEL, MEM_WIDTH)), const((D_MODEL, MEM_WIDTH))],
        out_specs=[row(MEM_WIDTH)] * 4,
        out_shape=[sds((r, MEM_WIDTH), F32), sds((r, MEM_WIDTH), F32),
                   sds((r, MEM_WIDTH), BF16), sds((r, MEM_WIDTH), BF16)],
        compiler_params=_params("arbitrary"),
        name="mem_kv",
    )(mem2d, wk, wv)


FF_CHUNK = 256


def _tail_kernel(x_ref, oa_ref, ob_ref, mk_ref, mv_ref, hist_ref,
                 wo_ref, g1_ref, b1_ref, wq_ref, wmo_ref, g2_ref, b2_ref,
                 wu_ref, wc_ref, bc_ref, wd_ref, g3_ref, b3_ref,
                 o_ref, tail_ref, carry_ref, act_ref, *, tiles_per_batch, seg):
    i = pl.program_id(0)
    tm = x_ref.shape[0]
    nseg = tm // seg

    mix = jnp.concatenate([oa_ref[...], ob_ref[...]], axis=-1)
    h = _layer_norm(ALPHA * x_ref[...] + jnp.dot(mix, wo_ref[...], preferred_element_type=F32),
                    g1_ref[...], b1_ref[...])

    q = jnp.dot(h.astype(BF16), wq_ref[...], preferred_element_type=F32).astype(BF16)
    segs = []
    for s in range(nseg):
        qs = q[s * seg:(s + 1) * seg]
        heads = []
        for hd in range(MEM_HEADS):
            sl = slice(hd * MEM_HEAD_DIM, (hd + 1) * MEM_HEAD_DIM)
            sc = lax.dot_general(qs[:, sl], mk_ref[s, :, sl], _NT, preferred_element_type=F32) * MEM_HEAD_DIM ** -0.5
            p = jnp.exp(sc - sc.max(-1, keepdims=True))
            l = p.sum(-1, keepdims=True)
            heads.append(jnp.dot(p.astype(BF16), mv_ref[s, :, sl], preferred_element_type=F32) / l)
        segs.append(jnp.concatenate(heads, axis=-1))
    att = jnp.concatenate(segs, axis=0).astype(BF16)
    h = _layer_norm(ALPHA * h + jnp.dot(att, wmo_ref[...], preferred_element_type=F32), g2_ref[...], b2_ref[...])

    hb = h.astype(BF16)
    row = lax.broadcasted_iota(I32, (tm, 1), 0)
    first = (i % tiles_per_batch) == 0
    for c in range(D_FF // FF_CHUNK):
        cs = slice(c * FF_CHUNK, (c + 1) * FF_CHUNK)
        u = jnp.dot(hb, wu_ref[:, cs], preferred_element_type=F32)
        gt = jnp.dot(hb, wu_ref[:, D_FF + c * FF_CHUNK:D_FF + (c + 1) * FF_CHUNK], preferred_element_type=F32)
        p1 = pltpu.roll(gt, 1, 0)
        p2 = pltpu.roll(gt, 2, 0)
        for s in range(nseg):
            hist = hist_ref[s, :, cs]
            if tiles_per_batch > 1:
                hist = jnp.where(first, hist, carry_ref[:, cs])
            p1 = jnp.where(row == s * seg, hist[7:8, :], p1)
            p2 = jnp.where(row == s * seg, hist[6:7, :], p2)
            p2 = jnp.where(row == s * seg + 1, hist[7:8, :], p2)
        gc = bc_ref[:, cs] + ((wc_ref[0:1, cs] * p2 + wc_ref[1:2, cs] * p1) + wc_ref[2:3, cs] * gt)
        act_ref[:, cs] = (u * jax.nn.gelu(gc)).astype(BF16)
        for s in range(nseg):
            tail_ref[s, :, cs] = gt[(s + 1) * seg - 8:(s + 1) * seg, :]
        carry_ref[:, cs] = gt[tm - 8:tm, :]
    f = jnp.dot(act_ref[...], wd_ref[...], preferred_element_type=F32)
    o_ref[...] = _layer_norm(ALPHA * h + f, g3_ref[...], b3_ref[...])


def _layer_tail(x2d, oa, ob, mk, mv, hist, weights, tm, tiles_per_batch, seg):
    r = x2d.shape[0]
    nseg = tm // seg
    n_stream = r // (tm * tiles_per_batch) * nseg
    row = lambda n: pl.BlockSpec((tm, n), lambda i: (i, 0))
    per_stream = lambda a, b: pl.BlockSpec((nseg, a, b), lambda i: (i // tiles_per_batch, 0, 0))
    const = lambda a: pl.BlockSpec(a.shape, lambda i: (0,) * a.ndim, pipeline_mode=pl.Buffered(1))
    return pl.pallas_call(
        functools.partial(_tail_kernel, tiles_per_batch=tiles_per_batch, seg=seg),
        grid=(r // tm,),
        in_specs=[row(D_MODEL), row(A_WIDTH), row(B_WIDTH), per_stream(MEM_LEN, MEM_WIDTH),
                  per_stream(MEM_LEN, MEM_WIDTH), per_stream(8, D_FF)] + [const(w) for w in weights],
        out_specs=[row(D_MODEL), per_stream(8, D_FF)],
        out_shape=[jax.ShapeDtypeStruct((r, D_MODEL), F32), jax.ShapeDtypeStruct((n_stream, 8, D_FF), F32)],
        scratch_shapes=[pltpu.VMEM((8, D_FF), F32), pltpu.VMEM((tm, D_FF), BF16)],
        compiler_params=_params("arbitrary"),
        name="layer_tail",
    )(x2d, oa, ob, mk, mv, hist, *weights)


def _pad_rows(a, n):
    return jnp.pad(a, ((0, 0), (0, n - a.shape[1])) + ((0, 0),) * (a.ndim - 2))


def _hist8(g_hist):
    return jnp.pad(g_hist, ((0, 0), (8 - g_hist.shape[1], 0), (0, 0)))


def kernel(x_prompt, x_sample, cache_a_k, cache_a_v, cache_b_k, cache_b_v, cache_b_kidx, cache_mem_k, cache_mem_v, state_ffn_conv, mem_prompt, w_in, a_rel_bias, t5_bias, w_o, ln1_g, ln1_b, w_mq, w_mk, w_mv, w_mo, ln2_g, ln2_b, w_up, w_conv, b_conv, w_down, ln3_g, ln3_b):
    bp, tp = x_prompt.shape[:2]
    bs, ts = x_sample.shape[:2]
    l = 0
    vec = lambda a: a[l].reshape(1, -1)
    w_in_p = _prep_w_in(w_in[l])
    w_o_b = w_o[l].astype(BF16)
    w_mq_b, w_mk_b, w_mv_b, w_mo_b = (w[l].astype(BF16) for w in (w_mq, w_mk, w_mv, w_mo))
    w_up_b, w_down_b = w_up[l].astype(BF16), w_down[l].astype(BF16)
    band_row = _band_bias_row(a_rel_bias[l])
    dsa_row = _dsa_bias_row(t5_bias)
    tail_w = (w_o_b, vec(ln1_g), vec(ln1_b), w_mq_b, w_mo_b, vec(ln2_g), vec(ln2_b),
              w_up_b, w_conv[l], vec(b_conv), w_down_b, vec(ln3_g), vec(ln3_b))

    tm = 512
    a_keep = min(N_PREV_CHUNKS * CHUNK, tp)
    (qa, ka, va, qb, qi, kb, vb, ki, kb_b, vb_b, ki_b, wi, ka_tail, va_tail) = _in_proj(
        x_prompt.reshape(bp * tp, D_MODEL), w_in_p, tm, tp // tm)
    r3 = lambda a: a.reshape(bp, tp, a.shape[-1])
    oa = _band_attn(r3(qa), r3(ka), r3(va), band_row, 0, tp)
    ob = _dsa_attn(r3(qi), r3(wi), r3(qb), r3(ki_b), r3(kb_b), r3(vb_b), dsa_row, 0, tp, TILE, TILE)
    mk, mv, mk_b, mv_b = _mem_kv(mem_prompt.reshape(bp * MEM_LEN, D_MODEL), w_mk_b, w_mv_b)
    xp, p_tail = _layer_tail(x_prompt.reshape(bp * tp, D_MODEL), oa.reshape(bp * tp, A_WIDTH),
                             ob.reshape(bp * tp, B_WIDTH), mk_b.reshape(bp, MEM_LEN, MEM_WIDTH),
                             mv_b.reshape(bp, MEM_LEN, MEM_WIDTH), jnp.zeros((bp, 8, D_FF), F32), tail_w,
                             tm, tp // tm, tm)
    prompt_state = (
        ka_tail.reshape(bp, tm, A_HEADS, HEAD_DIM)[:, tm - a_keep:][None],
        va_tail.reshape(bp, tm, A_HEADS, HEAD_DIM)[:, tm - a_keep:][None],
        kb.reshape(1, bp, tp, B_KV_HEADS, HEAD_DIM), vb.reshape(1, bp, tp, B_KV_HEADS, HEAD_DIM),
        ki.reshape(1, bp, tp, IDX_DIM),
        mk.reshape(1, bp, MEM_LEN, MEM_HEADS, MEM_HEAD_DIM), mv.reshape(1, bp, MEM_LEN, MEM_HEADS, MEM_HEAD_DIM),
        p_tail[:, 8 - (CONV_W - 1):][None])

    rs = bs * ts
    (qa, ka, va, qb, qi, kb, vb, ki, kb_b, vb_b, ki_b, wi, ka_new, va_new) = _in_proj(
        x_sample.reshape(rs, D_MODEL), w_in_p, rs, 1)
    s3 = lambda a: a.reshape(bs, ts, a.shape[-1])
    qpad = lambda a, n: _pad_rows(s3(a), n)

    past_a = cache_a_k.shape[2]
    n_a = past_a + ts
    t_a = -(-n_a // TILE) * TILE
    seq_a = lambda cache, new: _pad_rows(
        jnp.concatenate([cache[l].reshape(bs, past_a, A_WIDTH), s3(new)], axis=1), t_a).astype(BF16)
    oa = _band_attn(qpad(qa, TILE), seq_a(cache_a_k, ka_new), seq_a(cache_a_v, va_new), band_row,
                    past_a // TILE, n_a)

    past_b = cache_b_k.shape[2]
    n_b = past_b + ts
    t_b = -(-n_b // (2 * TILE)) * 2 * TILE
    tq_s = LANES
    seq_b = lambda cache, new: _pad_rows(
        jnp.concatenate([cache[l].reshape(bs, past_b, -1), s3(new)], axis=1), t_b).astype(BF16)
    ob = _dsa_attn(qpad(qi, tq_s), qpad(wi, tq_s), qpad(qb, tq_s), seq_b(cache_b_kidx, ki), seq_b(cache_b_k, kb),
                   seq_b(cache_b_v, vb), dsa_row, past_b // TILE, n_b, ts, tq_s)

    xs, s_tail = _layer_tail(x_sample.reshape(rs, D_MODEL), oa[:, :ts].reshape(rs, A_WIDTH),
                             ob[:, :ts].reshape(rs, B_WIDTH),
                             cache_mem_k[l].reshape(bs, MEM_LEN, MEM_WIDTH).astype(BF16),
                             cache_mem_v[l].reshape(bs, MEM_LEN, MEM_WIDTH).astype(BF16),
                             _hist8(state_ffn_conv[l]), tail_w, rs, 1, ts)
    sample_state = (
        ka_new.reshape(1, bs, ts, A_HEADS, HEAD_DIM), va_new.reshape(1, bs, ts, A_HEADS, HEAD_DIM),
        kb.reshape(1, bs, ts, B_KV_HEADS, HEAD_DIM), vb.reshape(1, bs, ts, B_KV_HEADS, HEAD_DIM),
        ki.reshape(1, bs, ts, IDX_DIM), s_tail[:, 8 - (CONV_W - 1):][None])

    return (xp.reshape(bp, tp, D_MODEL), xs.reshape(bs, ts, D_MODEL)) + prompt_state + sample_state
```

```python
import functools
import math

import jax
import jax.numpy as jnp
from jax import lax
from jax.experimental import pallas as pl
from jax.experimental.pallas import tpu as pltpu

F32 = jnp.float32
BF16 = jnp.bfloat16
I32 = jnp.int32
I16 = jnp.int16

D_MODEL = 1024
CHUNK = 64
N_PREV_CHUNKS = 8
HEAD_DIM = 64
A_HEADS = 8
A_WIDTH = A_HEADS * HEAD_DIM
A_MAX_REL = 64
B_HEADS = 8
B_KV_HEADS = 2
B_GROUP = B_HEADS // B_KV_HEADS
B_WIDTH = B_HEADS * HEAD_DIM
B_KV_WIDTH = B_KV_HEADS * HEAD_DIM
IDX_HEADS = 8
IDX_DIM = 64
TOPK_MAX = 256
N_BUCKETS = 32
T5_MAX_DIST = 128
MEM_LEN = 256
MEM_HEADS = 4
MEM_HEAD_DIM = 128
MEM_WIDTH = MEM_HEADS * MEM_HEAD_DIM
D_FF = 2816
CONV_W = 3
IN_SIZES = (A_WIDTH, A_WIDTH, A_WIDTH, B_WIDTH, B_KV_WIDTH, B_KV_WIDTH, IDX_HEADS * IDX_DIM, IDX_DIM, IDX_HEADS)
DEPTH = 1
ALPHA = (2 * DEPTH) ** 0.25
LN_EPS = 1e-5
ATTN_SCALE = HEAD_DIM ** -0.5
NEG = -1e30

LANES = 128
TILE = 256
BAND_TILES = 1 + (N_PREV_CHUNKS * CHUNK) // TILE
VMEM_LIMIT = 56 * 1024 * 1024

_C_QA, _C_KA, _C_VA, _C_QB = 0, 512, 1024, 1536
_C_KB, _C_VB, _C_QI, _C_KI, _C_WI = 2048, 2176, 2304, 2816, 2944
IN_PAD = 3072

MIN16 = -32768
HI_NEG_INF = -32641

_NT = (((1,), (1,)), ((), ()))


def _params(*sem):
    return pltpu.CompilerParams(dimension_semantics=sem, vmem_limit_bytes=VMEM_LIMIT)


def _layer_norm(z, g, b):
    mu = jnp.mean(z, axis=-1, keepdims=True)
    d = z - mu
    var = jnp.mean(d * d, axis=-1, keepdims=True)
    return d * lax.rsqrt(var + LN_EPS) * g + b


def _toeplitz(g_row, rows, width):
    return pltpu.roll(jnp.broadcast_to(g_row, (rows, width)), 0, 1, stride=1, stride_axis=0)


def _in_proj_kernel(x_ref, w_ref, qa_o, ka_o, va_o, qb_o, qi_o, kb_o, vb_o, ki_o, kbb_o, vbb_o, kib_o,
                    wi_o, kat_o, vat_o, *, tiles_per_batch, wi_scale):
    i = pl.program_id(0)
    xb = x_ref[...].astype(BF16)

    def mm(c0, n):
        return jnp.dot(xb, w_ref[:, c0:c0 + n], preferred_element_type=F32)

    qa_o[...] = mm(_C_QA, 512).astype(BF16)
    ka = mm(_C_KA, 512)
    va = mm(_C_VA, 512)
    ka_o[...] = ka.astype(BF16)
    va_o[...] = va.astype(BF16)
    qb_o[...] = mm(_C_QB, 512).astype(BF16)
    qi_o[...] = mm(_C_QI, 512).astype(BF16)
    kb = mm(_C_KB, 128)
    vb = mm(_C_VB, 128)
    for g in range(B_KV_HEADS):
        kb_o[:, g, :] = kb[:, g * HEAD_DIM:(g + 1) * HEAD_DIM]
        vb_o[:, g, :] = vb[:, g * HEAD_DIM:(g + 1) * HEAD_DIM]
    kbb_o[...] = kb.astype(BF16)
    vbb_o[...] = vb.astype(BF16)
    ki = mm(_C_KI, 128)[:, :IDX_DIM]
    ki_o[...] = ki
    kib_o[...] = ki.astype(BF16)
    wi_o[...] = mm(_C_WI, 128) * wi_scale

    @pl.when(i % tiles_per_batch == tiles_per_batch - 1)
    def _():
        kat_o[...] = ka
        vat_o[...] = va


def _prep_w_in(w):
    parts, off = [], 0
    for n in IN_SIZES:
        parts.append(w[:, off:off + n])
        off += n
    qa, ka, va, qb, kb, vb, qi, ki, wi = parts
    pad = lambda a, n: jnp.pad(a, ((0, 0), (0, n - a.shape[1])))
    cols = [qa * ATTN_SCALE, ka, va, qb * ATTN_SCALE, kb, vb, qi * IDX_DIM ** -0.5, pad(ki, 128), pad(wi, 128)]
    return jnp.concatenate(cols, axis=1).astype(BF16)


def _in_proj(x2d, w_pad, tm, tiles_per_batch):
    r = x2d.shape[0]
    n_tiles = r // tm
    n_batch = n_tiles // tiles_per_batch
    row = lambda n: pl.BlockSpec((tm, n), lambda i: (i, 0))
    tail = pl.BlockSpec((tm, 512), lambda i: (i // tiles_per_batch, 0))
    sds = jax.ShapeDtypeStruct
    kv_state = pl.BlockSpec((tm, B_KV_HEADS, HEAD_DIM), lambda i: (i, 0, 0))
    kv_shape = sds((r, B_KV_HEADS, HEAD_DIM), F32)
    out_shape = [sds((r, 512), BF16)] * 5 + [kv_shape, kv_shape, sds((r, IDX_DIM), F32),
                                              sds((r, 128), BF16), sds((r, 128), BF16), sds((r, IDX_DIM), BF16),
                                              sds((r, 128), F32),
                                              sds((n_batch * tm, 512), F32), sds((n_batch * tm, 512), F32)]
    out_specs = [row(512)] * 5 + [kv_state, kv_state, row(IDX_DIM), row(128), row(128), row(IDX_DIM), row(128),
                                  tail, tail]
    return pl.pallas_call(
        functools.partial(_in_proj_kernel, tiles_per_batch=tiles_per_batch, wi_scale=IDX_HEADS ** -0.5),
        grid=(n_tiles,),
        in_specs=[pl.BlockSpec((tm, D_MODEL), lambda i: (i, 0)),
                  pl.BlockSpec((D_MODEL, IN_PAD), lambda i: (0, 0))],
        out_specs=out_specs,
        out_shape=out_shape,
        compiler_params=_params("arbitrary"),
        name="in_proj",
    )(x2d, w_pad)


BAND_COLS = BAND_TILES * TILE
BAND_ROLL = BAND_COLS + TILE


def _band_bias_row(table):
    idx = jnp.arange(BAND_ROLL)
    d = jnp.where(idx < TILE, idx, idx - BAND_ROLL)
    rel = (BAND_TILES - 1) * TILE + d
    return table[jnp.clip(rel, -A_MAX_REL, A_MAX_REL) + A_MAX_REL].T.astype(F32)


def _band_kernel(g_ref, q_ref, k0, k1, k2, v0, v1, v2, o_ref, bias_ref, vt_ref, *, off, valid_len):
    i = pl.program_id(1)
    kt = i + off
    krefs, vrefs = (k0, k1, k2), (v0, v1, v2)

    @pl.when((pl.program_id(0) == 0) & (i == 0))
    def _():
        c = lax.broadcasted_iota(I32, (BAND_COLS, TILE), 0) // CHUNK
        r = lax.broadcasted_iota(I32, (BAND_COLS, TILE), 1) // CHUNK
        ok = (c >= r) & (c <= r + N_PREV_CHUNKS)
        for h in range(A_HEADS):
            bias_ref[h] = jnp.where(ok, _toeplitz(g_ref[h:h + 1, :], BAND_COLS, BAND_ROLL)[:, :TILE], NEG)
        ones = jnp.ones((HEAD_DIM, TILE), BF16)
        for s in range(BAND_TILES):
            for h in range(A_HEADS):
                vt_ref[s, h, HEAD_DIM:2 * HEAD_DIM, :] = ones

    def put(slot, vref):
        vt = vref[0].astype(F32).T
        for h in range(A_HEADS):
            vt_ref[slot, h, 0:HEAD_DIM, :] = vt[h * HEAD_DIM:(h + 1) * HEAD_DIM].astype(BF16)

    slots = [(kt + 1 + j) % BAND_TILES for j in range(BAND_TILES)]

    @pl.when(i == 0)
    def _():
        for j in range(BAND_TILES - 1):
            put(slots[j], vrefs[j])

    put(slots[BAND_TILES - 1], vrefs[BAND_TILES - 1])

    base = (kt - (BAND_TILES - 1)) * TILE

    n_slab = 4
    slab_w = n_slab * HEAD_DIM
    lane_head = lax.broadcasted_iota(I32, (TILE, slab_w), 1) // HEAD_DIM

    def attend(masked):
        outs = []
        for g in range(A_HEADS // n_slab):
            gs = slice(g * slab_w, (g + 1) * slab_w)
            q_slab = q_ref[0, :, gs].astype(F32)
            q_bd = jnp.concatenate([jnp.where(lane_head == hh, q_slab, 0.0) for hh in range(n_slab)],
                                   axis=0).astype(BF16)
            s = []
            for j in range(BAND_TILES):
                st = lax.dot_general(krefs[j][0, :, gs], q_bd, _NT, preferred_element_type=F32)
                parts = [st[:, hh * TILE:(hh + 1) * TILE] + bias_ref[g * n_slab + hh, j * TILE:(j + 1) * TILE, :]
                         for hh in range(n_slab)]
                if masked:
                    kpos = base + j * TILE + lax.broadcasted_iota(I32, (TILE, TILE), 0)
                    ok = (kpos >= 0) & (kpos < valid_len)
                    parts = [jnp.where(ok, x, NEG) for x in parts]
                s.append(jnp.concatenate(parts, axis=1))
            m = jnp.maximum(jnp.maximum(s[0], s[1]), s[2]).max(axis=0, keepdims=True)
            p = [jnp.exp(sj - m).astype(BF16) for sj in s]
            for hh in range(n_slab):
                acc = None
                for j in range(BAND_TILES):
                    d = jnp.dot(vt_ref[slots[j], g * n_slab + hh], p[j][:, hh * TILE:(hh + 1) * TILE],
                                preferred_element_type=F32)
                    acc = d if acc is None else acc + d
                blk = acc.T
                outs.append(blk[:, 0:HEAD_DIM] / blk[:, HEAD_DIM:HEAD_DIM + 1])
        o_ref[0] = jnp.concatenate(outs, axis=-1).astype(BF16)

    needs_mask = (base < 0) | (base + BAND_COLS > valid_len)

    @pl.when(needs_mask)
    def _():
        attend(True)

    @pl.when(jnp.logical_not(needs_mask))
    def _():
        attend(False)


def _band_attn(q, k, v, g_row, off, valid_len):
    b, tq = q.shape[:2]
    nq = tq // TILE
    qspec = pl.BlockSpec((1, TILE, A_WIDTH), lambda bb, i: (bb, i, 0))
    kspec = lambda d: pl.BlockSpec((1, TILE, A_WIDTH), lambda bb, i: (bb, jnp.maximum(i + off - d, 0), 0))
    return pl.pallas_call(
        functools.partial(_band_kernel, off=off, valid_len=valid_len),
        grid=(b, nq),
        in_specs=[pl.BlockSpec((A_HEADS, BAND_ROLL), lambda bb, i: (0, 0)),
                  qspec, kspec(2), kspec(1), kspec(0), kspec(2), kspec(1), kspec(0)],
        out_specs=pl.BlockSpec((1, TILE, A_WIDTH), lambda bb, i: (bb, i, 0)),
        out_shape=jax.ShapeDtypeStruct((b, tq, A_WIDTH), BF16),
        scratch_shapes=[pltpu.VMEM((A_HEADS, BAND_COLS, TILE), F32),
                        pltpu.VMEM((BAND_TILES, A_HEADS, 2 * HEAD_DIM, TILE), BF16)],
        compiler_params=_params("arbitrary", "arbitrary"),
        name="band_attn",
    )(g_row, q, k, k, k, v, v, v)


DSA_ROLL = 3 * TILE


def _t5_bucket(rel):
    half = N_BUCKETS // 2
    max_exact = half // 2
    n = jnp.abs(rel)
    log_ratio = jnp.log(jnp.maximum(n, 1).astype(jnp.float32) / max_exact) / math.log(T5_MAX_DIST / max_exact)
    large = jnp.minimum(max_exact + (log_ratio * (half - max_exact)).astype(jnp.int32), half - 1)
    return jnp.where(rel < 0, half, 0) + jnp.where(n < max_exact, n, large)


def _dsa_bias_row(t5_table):
    idx = jnp.arange(DSA_ROLL)
    d = jnp.where(idx < TILE, idx, idx - DSA_ROLL)
    far = t5_table[_t5_bucket(jnp.full((1,), 2 * TILE + 1, I32))]
    return (t5_table[_t5_bucket(TILE + d)] - far).T.astype(F32)


def _dsa_kernel(g_ref, qi_ref, wi_ref, qb_ref, ki_ref, kb_ref, vb_ref, o_ref,
                sc_ref, scb_ref, keep_ref, vt_ref, qis_ref, qbs_ref, m_ref, acc_ref, bias_ref,
                *, off, valid_len, q_valid, n_kt, tq, n_merge):
    i = pl.program_id(1)
    qt = i + off
    q0 = qt * TILE
    nk = qt + 1
    key_rows = lambda j: pl.ds(pl.multiple_of(j * TILE, TILE), TILE)

    @pl.when((pl.program_id(0) == 0) & (i == 0))
    def _():
        for h in range(B_HEADS):
            if n_merge:
                bias_ref[h] = g_ref[h]
            else:
                bias_ref[h] = _toeplitz(g_ref[h:h + 1, :], 2 * TILE, DSA_ROLL)[:, :tq]

    @pl.when(i == 0)
    def _():
        def body(j, c):
            for b in range(max(n_merge, 1)):
                vt = vb_ref[b, key_rows(j), :].astype(F32).T
                for g in range(B_KV_HEADS):
                    vt_ref[g, j, 0:HEAD_DIM, b * TILE:(b + 1) * TILE] = vt[g * HEAD_DIM:(g + 1) * HEAD_DIM].astype(BF16)
            for g in range(B_KV_HEADS):
                vt_ref[g, j, HEAD_DIM:2 * HEAD_DIM, :] = jnp.ones((HEAD_DIM, vt_ref.shape[-1]), BF16)
            return c

        lax.fori_loop(0, n_kt, body, 0)

    colq = lax.broadcasted_iota(I32, (1, tq), 1)
    rowk = lax.broadcasted_iota(I32, (TILE, 1), 0)

    if n_merge:
        wi_t = wi_ref[...]
        lim = jnp.full((1, tq), valid_len, I32)
    else:
        qi = qi_ref[0]
        for h in range(IDX_HEADS):
            qis_ref[h * tq:(h + 1) * tq, :] = qi[:, h * IDX_DIM:(h + 1) * IDX_DIM]
        qb = qb_ref[0]
        for g in range(B_KV_HEADS):
            for hh in range(B_GROUP):
                h = g * B_GROUP + hh
                qbs_ref[g, hh * tq:(hh + 1) * tq, :] = qb[:, h * HEAD_DIM:(h + 1) * HEAD_DIM]
        wi_t = wi_ref[0].T
        lim = jnp.minimum(q0 + (colq // CHUNK + 1) * CHUNK, valid_len)

    def score_tile(j):
        if n_merge:
            kt = ki_ref[key_rows(j), :]
            lg = jnp.concatenate([lax.dot_general(kt, qi_ref[h], _NT, preferred_element_type=F32)
                                  for h in range(IDX_HEADS)], axis=1)
        else:
            kt = ki_ref[0, key_rows(j), :]
            lg = lax.dot_general(kt, qis_ref[...], _NT, preferred_element_type=F32)
        sc = wi_t[0:1, :] * jnp.maximum(lg[:, 0:tq], 0.0)
        for h in range(1, IDX_HEADS):
            sc = sc + wi_t[h:h + 1, :] * jnp.maximum(lg[:, h * tq:(h + 1) * tq], 0.0)
        sc = jnp.where(j * TILE + rowk < lim, sc, -jnp.inf)
        sc_ref[j] = sc
        scb_ref[j] = sc.astype(BF16)

    def score_pair(jj, c):
        score_tile(2 * jj)
        score_tile(2 * jj + 1)
        return c

    lax.fori_loop(0, (nk + 1) // 2, score_pair, 0)

    def f32_of_key(k):
        return pltpu.bitcast(jnp.where(k < 0, k ^ 0x7FFFFFFF, k), F32)

    def bf16_of_key(k):
        bits = jnp.where(k < 0, k ^ 0x7FFF, k) & 0xFFFF
        return pltpu.bitcast(lax.shift_left(bits, 16), F32).astype(BF16)

    def count_bf16(cand):
        def body(j, acc):
            ge = jnp.where(scb_ref[j] >= cand, jnp.int16(1), jnp.int16(0)).reshape(TILE // 16, 16, tq)
            part = ge[0]
            for r in range(1, TILE // 16):
                part = part + ge[r]
            return acc + part

        acc = lax.fori_loop(0, nk, body, jnp.zeros((16, tq), I16))
        return acc.astype(I32).sum(axis=0, keepdims=True)

    def count_f32(cand, strict=False):
        def body(j, acc):
            blk = sc_ref[j]
            hit = (blk > cand) if strict else (blk >= cand)
            return acc + jnp.where(hit, 1, 0).reshape(TILE // 8, 8, tq).sum(axis=0)

        return lax.fori_loop(0, nk, body, jnp.zeros((8, tq), I32)).sum(axis=0, keepdims=True)

    zero = jnp.zeros((1, tq), F32)
    c_pos = count_f32(zero, strict=True)
    zero_tie = (c_pos < TOPK_MAX) & (count_f32(zero) >= TOPK_MAX)
    skip1 = (colq >= q_valid) | zero_tie

    def level1(it, t):
        cand = t + lax.shift_left(jnp.int32(1), 15 - it)
        return jnp.where(count_bf16(bf16_of_key(cand)) >= TOPK_MAX, cand, t)

    t1 = lax.fori_loop(0, 16, level1, jnp.full((1, tq), MIN16, I32))
    settled1 = skip1 | (t1 <= HI_NEG_INF)

    def level2(st):
        lo, hi, thr_key, done = st
        live = (done == 0) & (hi - lo > 1)
        mid = lo + lax.shift_right_arithmetic(hi - lo, 1)
        c = count_f32(f32_of_key(mid))
        hit = live & (c == TOPK_MAX)
        return (jnp.where(live & (c >= TOPK_MAX), mid, lo), jnp.where(live & (c < TOPK_MAX), mid, hi),
                jnp.where(hit, mid, thr_key), jnp.where(hit, 1, done))

    def n_live(st):
        lo, hi, _, done = st
        return jnp.sum(jnp.where((done == 0) & (hi - lo > 1), 1, 0))

    def level2_pair(carry):
        st = level2(level2(carry[0]))
        return st, n_live(st)

    def key32_of_key16(k):
        return lax.shift_left(k, 16) | jnp.where(k < 0, 0xFFFF, 0)

    key_t1 = key32_of_key16(t1)
    st0 = (key_t1 - 0x8000, key32_of_key16(t1 + 1), key_t1, jnp.where(settled1, 1, 0))
    (lo, _, thr_key, done2), _ = lax.while_loop(lambda carry: carry[1] > 0, level2_pair, (st0, n_live(st0)))
    open2 = done2 == 0
    thr = jnp.where(open2, f32_of_key(lo), f32_of_key(thr_key))
    thr = jnp.where(zero_tie, 0.0, thr)
    thr = jnp.where(settled1 & jnp.logical_not(zero_tie), -jnp.inf, thr)
    thr = jnp.maximum(thr, float(jnp.finfo(F32).min))

    keep_ref[...] = jnp.where(zero_tie, TOPK_MAX - c_pos, 2 ** 30).astype(F32)

    @pl.when(jnp.sum(jnp.where(open2, 1, 0)) > 0)
    def _():
        above = count_f32(thr, strict=True)
        keep_ref[...] = jnp.where(open2, (TOPK_MAX - above).astype(F32), keep_ref[...])

    @pl.when(jnp.sum(jnp.where(open2 | zero_tie, 1, 0)) > 0)
    def _():
        keep = keep_ref[...]
        lower = (lax.broadcasted_iota(I32, (TILE, TILE), 0) > lax.broadcasted_iota(I32, (TILE, TILE), 1))
        lower = jnp.where(lower, 1.0, 0.0).astype(BF16)

        def body(jj, run):
            for j in (2 * jj, 2 * jj + 1):
                blk = sc_ref[j]
                eq = blk == thr
                eq_f = jnp.where(eq, 1.0, 0.0)
                before = jnp.dot(lower, eq_f.astype(BF16), preferred_element_type=F32)
                sc_ref[j] = jnp.where(eq & (run + before >= keep), -jnp.inf, blk)
                run = run + eq_f.reshape(TILE // 8, 8, tq).sum(axis=0).sum(axis=0, keepdims=True)
            return run

        lax.fori_loop(0, (nk + 1) // 2, body, jnp.zeros((1, tq), F32))

    for g in range(B_KV_HEADS):
        m_ref[g] = jnp.full((1, B_GROUP * tq), NEG, F32)
        acc_ref[g] = jnp.zeros((2 * HEAD_DIM, B_GROUP * tq), F32)

    def attend(tiles):
        sels = [sc_ref[j] >= thr for j, _ in tiles]
        for g in range(B_KV_HEADS):
            ss = []
            for (j, near), sel in zip(tiles, sels):
                if n_merge:
                    kt = kb_ref[g, key_rows(j), :]
                    st = jnp.concatenate([lax.dot_general(kt, qb_ref[g * B_GROUP + hh], _NT,
                                                          preferred_element_type=F32)
                                          for hh in range(B_GROUP)], axis=1)
                else:
                    kt = kb_ref[0, key_rows(j), g * HEAD_DIM:(g + 1) * HEAD_DIM]
                    st = lax.dot_general(kt, qbs_ref[g], _NT, preferred_element_type=F32)
                parts = []
                for hh in range(B_GROUP):
                    s_h = st[:, hh * tq:(hh + 1) * tq]
                    if near is not None:
                        s_h = s_h + bias_ref[g * B_GROUP + hh, near * TILE:(near + 1) * TILE, :]
                    parts.append(jnp.where(sel, s_h, -jnp.inf))
                ss.append(jnp.concatenate(parts, axis=1))
            m_old = m_ref[g]
            m_new = m_old
            for s in ss:
                m_new = jnp.maximum(m_new, s.max(axis=0, keepdims=True))
            pv = None
            for (j, _), s in zip(tiles, ss):
                p = jnp.exp(s - m_new)
                if n_merge:
                    lane_stream = colq // (tq // n_merge)
                    d = jnp.concatenate(
                        [jnp.dot(vt_ref[g, j],
                                 jnp.concatenate([jnp.where(lane_stream == b, p[:, hh * tq:(hh + 1) * tq], 0.0)
                                                  for b in range(n_merge)], axis=0).astype(BF16),
                                 preferred_element_type=F32) for hh in range(B_GROUP)], axis=1)
                else:
                    d = jnp.dot(vt_ref[g, j], p.astype(BF16), preferred_element_type=F32)
                pv = d if pv is None else pv + d
            acc_ref[g] = jnp.exp(m_old - m_new) * acc_ref[g] + pv
            m_ref[g] = m_new

    n_far = jnp.maximum(nk - 2, 0)

    def far_quad(jj, c):
        attend([(4 * jj + t, None) for t in range(4)])
        return c

    lax.fori_loop(0, n_far // 4, far_quad, 0)
    rem = n_far % 4

    @pl.when(rem >= 2)
    def _():
        attend([(n_far - rem, None), (n_far - rem + 1, None)])

    @pl.when(rem % 2 == 1)
    def _():
        attend([(n_far - 1, None)])

    @pl.when(nk >= 2)
    def _():
        attend([(nk - 2, 0), (nk - 1, 1)])

    @pl.when(nk < 2)
    def _():
        attend([(nk - 1, 1)])

    outs = []
    for g in range(B_KV_HEADS):
        for hh in range(B_GROUP):
            blk = acc_ref[g, :, hh * tq:(hh + 1) * tq].T
            outs.append(blk[:, 0:HEAD_DIM] / blk[:, HEAD_DIM:HEAD_DIM + 1])
    o_ref[0] = jnp.concatenate(outs, axis=-1).astype(BF16)


def _dsa_attn(qi, wi, qb, ki, kb, vb, g_row, off, valid_len, q_valid, tq):
    b, t_q = qi.shape[:2]
    tk = ki.shape[1]
    nq, n_kt = t_q // tq, tk // TILE
    assert tq == TILE or nq == 1
    assert n_kt % 2 == 0 and tk == n_kt * TILE
    qspec = lambda n: pl.BlockSpec((1, tq, n), lambda bb, i: (bb, i, 0))
    kspec = lambda n: pl.BlockSpec((1, tk, n), lambda bb, i: (bb, 0, 0))
    in_specs = [pl.BlockSpec((B_HEADS, DSA_ROLL), lambda bb, i: (0, 0)),
                qspec(512), qspec(128), qspec(512), kspec(IDX_DIM), kspec(128), kspec(128)]
    return _dsa_call((g_row, qi, wi, qb, ki, kb, vb), in_specs, (b, nq), t_q, tq, n_kt,
                     dict(off=off, valid_len=valid_len, q_valid=q_valid, n_merge=0))


def _dsa_attn_merged(qi, wi, qb, ki, kb, vb, t5_table, off, valid_len):
    s, ts = qi.shape[:2]
    tk = ki.shape[1]
    tq, n_kt = s * ts, tk // TILE
    assert tq % LANES == 0 and ts <= CHUNK and n_kt % 2 == 0 and tk == n_kt * TILE
    eye = jnp.eye(s, dtype=qi.dtype)

    def block_diag(q):
        q5 = q.reshape(s, ts, B_HEADS, 1, HEAD_DIM) * eye[:, None, None, :, None]
        return q5.transpose(2, 0, 1, 3, 4).reshape(B_HEADS, tq, s * HEAD_DIM)

    ki_cat = ki.transpose(1, 0, 2).reshape(tk, s * IDX_DIM)
    kb_cat = kb.reshape(s, tk, B_KV_HEADS, HEAD_DIM).transpose(2, 1, 0, 3).reshape(B_KV_HEADS, tk, s * HEAD_DIM)
    wi_t = wi[:, :, :IDX_HEADS].transpose(2, 0, 1).reshape(IDX_HEADS, tq)
    rel = jnp.arange(ts)[None, :] + TILE - jnp.arange(2 * TILE)[:, None]
    far = t5_table[_t5_bucket(jnp.full((1, 1), 2 * TILE + 1, I32))]
    bias = jnp.tile((t5_table[_t5_bucket(rel)] - far).transpose(2, 0, 1), (1, 1, s)).astype(F32)
    operands = (bias, block_diag(qi), wi_t, block_diag(qb), ki_cat, kb_cat, vb)
    whole = lambda a: pl.BlockSpec(a.shape, lambda bb, i: (0,) * a.ndim, pipeline_mode=pl.Buffered(1))
    out = _dsa_call(operands, [whole(a) for a in operands], (1, 1), tq, tq, n_kt,
                    dict(off=off, valid_len=valid_len, q_valid=tq, n_merge=s))
    return out.reshape(s, ts, B_WIDTH)


def _dsa_call(operands, in_specs, grid, t_q, tq, n_kt, static):
    n_merge = static["n_merge"]
    return pl.pallas_call(
        functools.partial(_dsa_kernel, n_kt=n_kt, tq=tq, **static),
        grid=grid,
        in_specs=in_specs,
        out_specs=pl.BlockSpec((1, tq, B_WIDTH), lambda bb, i: (bb, i, 0)),
        out_shape=jax.ShapeDtypeStruct((grid[0], t_q, B_WIDTH), BF16),
        scratch_shapes=[pltpu.VMEM((n_kt, TILE, tq), F32),
                        pltpu.VMEM((n_kt, TILE, tq), BF16),
                        pltpu.VMEM((1, tq), F32),
                        pltpu.VMEM((B_KV_HEADS, n_kt, 2 * HEAD_DIM, max(n_merge, 1) * TILE), BF16),
                        pltpu.VMEM((IDX_HEADS * tq, IDX_DIM), BF16),
                        pltpu.VMEM((B_KV_HEADS, B_GROUP * tq, HEAD_DIM), BF16),
                        pltpu.VMEM((B_KV_HEADS, 1, B_GROUP * tq), F32),
                        pltpu.VMEM((B_KV_HEADS, 2 * HEAD_DIM, B_GROUP * tq), F32),
                        pltpu.VMEM((B_HEADS, 2 * TILE, tq), F32)],
        compiler_params=_params("arbitrary", "arbitrary"),
        name="dsa_attn",
    )(*operands)


def _mem_kv_kernel(m_ref, wk_ref, wv_ref, k_o, v_o, kb_o, vb_o):
    mb = m_ref[...].astype(BF16)
    k = jnp.dot(mb, wk_ref[...], preferred_element_type=F32)
    v = jnp.dot(mb, wv_ref[...], preferred_element_type=F32)
    k_o[...] = k
    v_o[...] = v
    kb_o[...] = k.astype(BF16)
    vb_o[...] = v.astype(BF16)


def _mem_kv(mem2d, wk, wv):
    r = mem2d.shape[0]
    tm = MEM_LEN
    row = lambda n: pl.BlockSpec((tm, n), lambda i: (i, 0))
    const = lambda s: pl.BlockSpec(s, lambda i: (0, 0))
    sds = jax.ShapeDtypeStruct
    return pl.pallas_call(
        _mem_kv_kernel,
        grid=(r // tm,),
        in_specs=[row(D_MODEL), const((D_MODEL, MEM_WIDTH)), const((D_MODEL, MEM_WIDTH))],
        out_specs=[row(MEM_WIDTH)] * 4,
        out_shape=[sds((r, MEM_WIDTH), F32), sds((r, MEM_WIDTH), F32),
                   sds((r, MEM_WIDTH), BF16), sds((r, MEM_WIDTH), BF16)],
        compiler_params=_params("arbitrary"),
        name="mem_kv",
    )(mem2d, wk, wv)


FF_CHUNK = 256


def _tail_kernel(x_ref, oa_ref, ob_ref, mk_ref, mv_ref, hist_ref,
                 wo_ref, g1_ref, b1_ref, wq_ref, wmo_ref, g2_ref, b2_ref,
                 wu_ref, wc_ref, bc_ref, wd_ref, g3_ref, b3_ref,
                 o_ref, tail_ref, carry_ref, act_ref, *, tiles_per_batch, seg):
    i = pl.program_id(0)
    tm = x_ref.shape[0]
    nseg = tm // seg

    mix = jnp.concatenate([oa_ref[...], ob_ref[...]], axis=-1)
    h = _layer_norm(ALPHA * x_ref[...] + jnp.dot(mix, wo_ref[...], preferred_element_type=F32),
                    g1_ref[...], b1_ref[...])

    q = jnp.dot(h.astype(BF16), wq_ref[...], preferred_element_type=F32).astype(BF16)
    segs = []
    for s in range(nseg):
        qs = q[s * seg:(s + 1) * seg]
        heads = []
        for hd in range(MEM_HEADS):
            sl = slice(hd * MEM_HEAD_DIM, (hd + 1) * MEM_HEAD_DIM)
            sc = lax.dot_general(qs[:, sl], mk_ref[s, :, sl], _NT, preferred_element_type=F32) * MEM_HEAD_DIM ** -0.5
            p = jnp.exp(sc - sc.max(-1, keepdims=True))
            l = p.sum(-1, keepdims=True)
            heads.append(jnp.dot(p.astype(BF16), mv_ref[s, :, sl], preferred_element_type=F32) / l)
        segs.append(jnp.concatenate(heads, axis=-1))
    att = jnp.concatenate(segs, axis=0).astype(BF16)
    h = _layer_norm(ALPHA * h + jnp.dot(att, wmo_ref[...], preferred_element_type=F32), g2_ref[...], b2_ref[...])

    hb = h.astype(BF16)
    row = lax.broadcasted_iota(I32, (tm, 1), 0)
    first = (i % tiles_per_batch) == 0
    for c in range(D_FF // FF_CHUNK):
        cs = slice(c * FF_CHUNK, (c + 1) * FF_CHUNK)
        u = jnp.dot(hb, wu_ref[:, cs], preferred_element_type=F32)
        gt = jnp.dot(hb, wu_ref[:, D_FF + c * FF_CHUNK:D_FF + (c + 1) * FF_CHUNK], preferred_element_type=F32)
        p1 = pltpu.roll(gt, 1, 0)
        p2 = pltpu.roll(gt, 2, 0)
        for s in range(nseg):
            hist = hist_ref[s, :, cs]
            if tiles_per_batch > 1:
                hist = jnp.where(first, hist, carry_ref[:, cs])
            p1 = jnp.where(row == s * seg, hist[7:8, :], p1)
            p2 = jnp.where(row == s * seg, hist[6:7, :], p2)
            p2 = jnp.where(row == s * seg + 1, hist[7:8, :], p2)
        gc = bc_ref[:, cs] + ((wc_ref[0:1, cs] * p2 + wc_ref[1:2, cs] * p1) + wc_ref[2:3, cs] * gt)
        act_ref[:, cs] = (u * jax.nn.gelu(gc)).astype(BF16)
        for s in range(nseg):
            tail_ref[s, :, cs] = gt[(s + 1) * seg - 8:(s + 1) * seg, :]
        carry_ref[:, cs] = gt[tm - 8:tm, :]
    f = jnp.dot(act_ref[...], wd_ref[...], preferred_element_type=F32)
    o_ref[...] = _layer_norm(ALPHA * h + f, g3_ref[...], b3_ref[...])


def _layer_tail(x2d, oa, ob, mk, mv, hist, weights, tm, tiles_per_batch, seg):
    r = x2d.shape[0]
    nseg = tm // seg
    n_stream = r // (tm * tiles_per_batch) * nseg
    row = lambda n: pl.BlockSpec((tm, n), lambda i: (i, 0))
    per_stream = lambda a, b: pl.BlockSpec((nseg, a, b), lambda i: (i // tiles_per_batch, 0, 0))
    const = lambda a: pl.BlockSpec(a.shape, lambda i: (0,) * a.ndim, pipeline_mode=pl.Buffered(1))
    return pl.pallas_call(
        functools.partial(_tail_kernel, tiles_per_batch=tiles_per_batch, seg=seg),
        grid=(r // tm,),
        in_specs=[row(D_MODEL), row(A_WIDTH), row(B_WIDTH), per_stream(MEM_LEN, MEM_WIDTH),
                  per_stream(MEM_LEN, MEM_WIDTH), per_stream(8, D_FF)] + [const(w) for w in weights],
        out_specs=[row(D_MODEL), per_stream(8, D_FF)],
        out_shape=[jax.ShapeDtypeStruct((r, D_MODEL), F32), jax.ShapeDtypeStruct((n_stream, 8, D_FF), F32)],
        scratch_shapes=[pltpu.VMEM((8, D_FF), F32), pltpu.VMEM((tm, D_FF), BF16)],
        compiler_params=_params("arbitrary"),
        name="layer_tail",
    )(x2d, oa, ob, mk, mv, hist, *weights)


def _pad_rows(a, n):
    return jnp.pad(a, ((0, 0), (0, n - a.shape[1])) + ((0, 0),) * (a.ndim - 2))


def _hist8(g_hist):
    return jnp.pad(g_hist, ((0, 0), (8 - g_hist.shape[1], 0), (0, 0)))


def kernel(x_prompt, x_sample, cache_a_k, cache_a_v, cache_b_k, cache_b_v, cache_b_kidx, cache_mem_k, cache_mem_v, state_ffn_conv, mem_prompt, w_in, a_rel_bias, t5_bias, w_o, ln1_g, ln1_b, w_mq, w_mk, w_mv, w_mo, ln2_g, ln2_b, w_up, w_conv, b_conv, w_down, ln3_g, ln3_b):
    bp, tp = x_prompt.shape[:2]
    bs, ts = x_sample.shape[:2]
    l = 0
    vec = lambda a: a[l].reshape(1, -1)
    w_in_p = _prep_w_in(w_in[l])
    w_o_b = w_o[l].astype(BF16)
    w_mq_b, w_mk_b, w_mv_b, w_mo_b = (w[l].astype(BF16) for w in (w_mq, w_mk, w_mv, w_mo))
    w_up_b, w_down_b = w_up[l].astype(BF16), w_down[l].astype(BF16)
    band_row = _band_bias_row(a_rel_bias[l])
    dsa_row = _dsa_bias_row(t5_bias)
    tail_w = (w_o_b, vec(ln1_g), vec(ln1_b), w_mq_b, w_mo_b, vec(ln2_g), vec(ln2_b),
              w_up_b, w_conv[l], vec(b_conv), w_down_b, vec(ln3_g), vec(ln3_b))

    tm = 512
    a_keep = min(N_PREV_CHUNKS * CHUNK, tp)
    (qa, ka, va, qb, qi, kb, vb, ki, kb_b, vb_b, ki_b, wi, ka_tail, va_tail) = _in_proj(
        x_prompt.reshape(bp * tp, D_MODEL), w_in_p, tm, tp // tm)
    r3 = lambda a: a.reshape(bp, tp, a.shape[-1])
    oa = _band_attn(r3(qa), r3(ka), r3(va), band_row, 0, tp)
    ob = _dsa_attn(r3(qi), r3(wi), r3(qb), r3(ki_b), r3(kb_b), r3(vb_b), dsa_row, 0, tp, TILE, TILE)
    mk, mv, mk_b, mv_b = _mem_kv(mem_prompt.reshape(bp * MEM_LEN, D_MODEL), w_mk_b, w_mv_b)
    xp, p_tail = _layer_tail(x_prompt.reshape(bp * tp, D_MODEL), oa.reshape(bp * tp, A_WIDTH),
                             ob.reshape(bp * tp, B_WIDTH), mk_b.reshape(bp, MEM_LEN, MEM_WIDTH),
                             mv_b.reshape(bp, MEM_LEN, MEM_WIDTH), jnp.zeros((bp, 8, D_FF), F32), tail_w,
                             tm, tp // tm, tm)
    prompt_state = (
        ka_tail.reshape(bp, tm, A_HEADS, HEAD_DIM)[:, tm - a_keep:][None],
        va_tail.reshape(bp, tm, A_HEADS, HEAD_DIM)[:, tm - a_keep:][None],
        kb.reshape(1, bp, tp, B_KV_HEADS, HEAD_DIM), vb.reshape(1, bp, tp, B_KV_HEADS, HEAD_DIM),
        ki.reshape(1, bp, tp, IDX_DIM),
        mk.reshape(1, bp, MEM_LEN, MEM_HEADS, MEM_HEAD_DIM), mv.reshape(1, bp, MEM_LEN, MEM_HEADS, MEM_HEAD_DIM),
        p_tail[:, 8 - (CONV_W - 1):][None])

    rs = bs * ts
    (qa, ka, va, qb, qi, kb, vb, ki, kb_b, vb_b, ki_b, wi, ka_new, va_new) = _in_proj(
        x_sample.reshape(rs, D_MODEL), w_in_p, rs, 1)
    s3 = lambda a: a.reshape(bs, ts, -1)
    qpad = lambda a, n: _pad_rows(s3(a), n)

    past_a = cache_a_k.shape[2]
    n_a = past_a + ts
    t_a = -(-n_a // TILE) * TILE
    seq_a = lambda cache, new: _pad_rows(
        jnp.concatenate([cache[l].reshape(bs, past_a, A_WIDTH), s3(new)], axis=1), t_a).astype(BF16)
    oa = _band_attn(qpad(qa, TILE), seq_a(cache_a_k, ka_new), seq_a(cache_a_v, va_new), band_row,
                    past_a // TILE, n_a)

    past_b = cache_b_k.shape[2]
    n_b = past_b + ts
    t_b = -(-n_b // (2 * TILE)) * 2 * TILE
    seq_b = lambda cache, new: _pad_rows(
        jnp.concatenate([cache[l].reshape(bs, past_b, -1), s3(new)], axis=1), t_b).astype(BF16)
    ob = _dsa_attn_merged(s3(qi), s3(wi), s3(qb), seq_b(cache_b_kidx, ki), seq_b(cache_b_k, kb),
                          seq_b(cache_b_v, vb), t5_bias, past_b // TILE, n_b)

    xs, s_tail = _layer_tail(x_sample.reshape(rs, D_MODEL), oa[:, :ts].reshape(rs, A_WIDTH),
                             ob[:, :ts].reshape(rs, B_WIDTH),
                             cache_mem_k[l].reshape(bs, MEM_LEN, MEM_WIDTH).astype(BF16),
                             cache_mem_v[l].reshape(bs, MEM_LEN, MEM_WIDTH).astype(BF16),
                             _hist8(state_ffn_conv[l]), tail_w, rs, 1, ts)
    sample_state = (
        ka_new.reshape(1, bs, ts, A_HEADS, HEAD_DIM), va_new.reshape(1, bs, ts, A_HEADS, HEAD_DIM),
        kb.reshape(1, bs, ts, B_KV_HEADS, HEAD_DIM), vb.reshape(1, bs, ts, B_KV_HEADS, HEAD_DIM),
        ki.reshape(1, bs, ts, IDX_DIM), s_tail[:, 8 - (CONV_W - 1):][None])

    return (xp.reshape(bp, tp, D_MODEL), xs.reshape(bs, ts, D_MODEL)) + prompt_state + sample_state
```

```python
import functools
import math

import jax
import jax.numpy as jnp
from jax import lax
from jax.experimental import pallas as pl
from jax.experimental.pallas import tpu as pltpu

F32 = jnp.float32
BF16 = jnp.bfloat16
I32 = jnp.int32
I16 = jnp.int16

D_MODEL = 1024
CHUNK = 64
N_PREV_CHUNKS = 8
HEAD_DIM = 64
A_HEADS = 8
A_WIDTH = A_HEADS * HEAD_DIM
A_MAX_REL = 64
B_HEADS = 8
B_KV_HEADS = 2
B_GROUP = B_HEADS // B_KV_HEADS
B_WIDTH = B_HEADS * HEAD_DIM
B_KV_WIDTH = B_KV_HEADS * HEAD_DIM
IDX_HEADS = 8
IDX_DIM = 64
TOPK_MAX = 256
N_BUCKETS = 32
T5_MAX_DIST = 128
MEM_LEN = 256
MEM_HEADS = 4
MEM_HEAD_DIM = 128
MEM_WIDTH = MEM_HEADS * MEM_HEAD_DIM
D_FF = 2816
CONV_W = 3
IN_SIZES = (A_WIDTH, A_WIDTH, A_WIDTH, B_WIDTH, B_KV_WIDTH, B_KV_WIDTH, IDX_HEADS * IDX_DIM, IDX_DIM, IDX_HEADS)
DEPTH = 1
ALPHA = (2 * DEPTH) ** 0.25
LN_EPS = 1e-5
ATTN_SCALE = HEAD_DIM ** -0.5
NEG = -1e30

LANES = 128
TILE = 256
BAND_TILES = 1 + (N_PREV_CHUNKS * CHUNK) // TILE
VMEM_LIMIT = 56 * 1024 * 1024

_C_QA, _C_KA, _C_VA, _C_QB = 0, 512, 1024, 1536
_C_KB, _C_VB, _C_QI, _C_KI, _C_WI = 2048, 2176, 2304, 2816, 2944
IN_PAD = 3072

MIN16 = -32768
HI_NEG_INF = -32641

_NT = (((1,), (1,)), ((), ()))


def _params(*sem):
    return pltpu.CompilerParams(dimension_semantics=sem, vmem_limit_bytes=VMEM_LIMIT)


def _layer_norm(z, g, b):
    mu = jnp.mean(z, axis=-1, keepdims=True)
    d = z - mu
    var = jnp.mean(d * d, axis=-1, keepdims=True)
    return d * lax.rsqrt(var + LN_EPS) * g + b


def _toeplitz(g_row, rows, width):
    return pltpu.roll(jnp.broadcast_to(g_row, (rows, width)), 0, 1, stride=1, stride_axis=0)


def _in_proj_kernel(x_ref, w_ref, qa_o, ka_o, va_o, qb_o, qi_o, kb_o, vb_o, ki_o, kbb_o, vbb_o, kib_o,
                    wi_o, kat_o, vat_o, *, tiles_per_batch, wi_scale):
    i = pl.program_id(0)
    xb = x_ref[...].astype(BF16)

    def mm(c0, n):
        return jnp.dot(xb, w_ref[:, c0:c0 + n], preferred_element_type=F32)

    qa_o[...] = mm(_C_QA, 512).astype(BF16)
    ka = mm(_C_KA, 512)
    va = mm(_C_VA, 512)
    ka_o[...] = ka.astype(BF16)
    va_o[...] = va.astype(BF16)
    qb_o[...] = mm(_C_QB, 512).astype(BF16)
    qi_o[...] = mm(_C_QI, 512).astype(BF16)
    kb = mm(_C_KB, 128)
    vb = mm(_C_VB, 128)
    for g in range(B_KV_HEADS):
        kb_o[:, g, :] = kb[:, g * HEAD_DIM:(g + 1) * HEAD_DIM]
        vb_o[:, g, :] = vb[:, g * HEAD_DIM:(g + 1) * HEAD_DIM]
    kbb_o[...] = kb.astype(BF16)
    vbb_o[...] = vb.astype(BF16)
    ki = mm(_C_KI, 128)[:, :IDX_DIM]
    ki_o[...] = ki
    kib_o[...] = ki.astype(BF16)
    wi_o[...] = mm(_C_WI, 128) * wi_scale

    @pl.when(i % tiles_per_batch == tiles_per_batch - 1)
    def _():
        kat_o[...] = ka
        vat_o[...] = va


def _prep_w_in(w):
    parts, off = [], 0
    for n in IN_SIZES:
        parts.append(w[:, off:off + n])
        off += n
    qa, ka, va, qb, kb, vb, qi, ki, wi = parts
    pad = lambda a, n: jnp.pad(a, ((0, 0), (0, n - a.shape[1])))
    cols = [qa * ATTN_SCALE, ka, va, qb * ATTN_SCALE, kb, vb, qi * IDX_DIM ** -0.5, pad(ki, 128), pad(wi, 128)]
    return jnp.concatenate(cols, axis=1).astype(BF16)


def _in_proj(x2d, w_pad, tm, tiles_per_batch):
    r = x2d.shape[0]
    n_tiles = r // tm
    n_batch = n_tiles // tiles_per_batch
    row = lambda n: pl.BlockSpec((tm, n), lambda i: (i, 0))
    tail = pl.BlockSpec((tm, 512), lambda i: (i // tiles_per_batch, 0))
    sds = jax.ShapeDtypeStruct
    kv_state = pl.BlockSpec((tm, B_KV_HEADS, HEAD_DIM), lambda i: (i, 0, 0))
    kv_shape = sds((r, B_KV_HEADS, HEAD_DIM), F32)
    out_shape = [sds((r, 512), BF16)] * 5 + [kv_shape, kv_shape, sds((r, IDX_DIM), F32),
                                              sds((r, 128), BF16), sds((r, 128), BF16), sds((r, IDX_DIM), BF16),
                                              sds((r, 128), F32),
                                              sds((n_batch * tm, 512), F32), sds((n_batch * tm, 512), F32)]
    out_specs = [row(512)] * 5 + [kv_state, kv_state, row(IDX_DIM), row(128), row(128), row(IDX_DIM), row(128),
                                  tail, tail]
    return pl.pallas_call(
        functools.partial(_in_proj_kernel, tiles_per_batch=tiles_per_batch, wi_scale=IDX_HEADS ** -0.5),
        grid=(n_tiles,),
        in_specs=[pl.BlockSpec((tm, D_MODEL), lambda i: (i, 0)),
                  pl.BlockSpec((D_MODEL, IN_PAD), lambda i: (0, 0))],
        out_specs=out_specs,
        out_shape=out_shape,
        compiler_params=_params("arbitrary"),
        name="in_proj",
    )(x2d, w_pad)


BAND_COLS = BAND_TILES * TILE
BAND_ROLL = BAND_COLS + TILE


def _band_bias_row(table):
    idx = jnp.arange(BAND_ROLL)
    d = jnp.where(idx < TILE, idx, idx - BAND_ROLL)
    rel = (BAND_TILES - 1) * TILE + d
    return table[jnp.clip(rel, -A_MAX_REL, A_MAX_REL) + A_MAX_REL].T.astype(F32)


def _band_kernel(g_ref, q_ref, k0, k1, k2, v0, v1, v2, o_ref, bias_ref, vt_ref, *, off, valid_len):
    i = pl.program_id(1)
    kt = i + off
    krefs, vrefs = (k0, k1, k2), (v0, v1, v2)

    @pl.when((pl.program_id(0) == 0) & (i == 0))
    def _():
        c = lax.broadcasted_iota(I32, (BAND_COLS, TILE), 0) // CHUNK
        r = lax.broadcasted_iota(I32, (BAND_COLS, TILE), 1) // CHUNK
        ok = (c >= r) & (c <= r + N_PREV_CHUNKS)
        for h in range(A_HEADS):
            bias_ref[h] = jnp.where(ok, _toeplitz(g_ref[h:h + 1, :], BAND_COLS, BAND_ROLL)[:, :TILE], NEG)
        ones = jnp.ones((HEAD_DIM, TILE), BF16)
        for s in range(BAND_TILES):
            for h in range(A_HEADS):
                vt_ref[s, h, HEAD_DIM:2 * HEAD_DIM, :] = ones

    def put(slot, vref):
        vt = vref[0].astype(F32).T
        for h in range(A_HEADS):
            vt_ref[slot, h, 0:HEAD_DIM, :] = vt[h * HEAD_DIM:(h + 1) * HEAD_DIM].astype(BF16)

    slots = [(kt + 1 + j) % BAND_TILES for j in range(BAND_TILES)]

    @pl.when(i == 0)
    def _():
        for j in range(BAND_TILES - 1):
            put(slots[j], vrefs[j])

    put(slots[BAND_TILES - 1], vrefs[BAND_TILES - 1])

    base = (kt - (BAND_TILES - 1)) * TILE

    n_slab = 4
    slab_w = n_slab * HEAD_DIM
    lane_head = lax.broadcasted_iota(I32, (TILE, slab_w), 1) // HEAD_DIM

    def attend(masked):
        outs = []
        for g in range(A_HEADS // n_slab):
            gs = slice(g * slab_w, (g + 1) * slab_w)
            q_slab = q_ref[0, :, gs].astype(F32)
            q_bd = jnp.concatenate([jnp.where(lane_head == hh, q_slab, 0.0) for hh in range(n_slab)],
                                   axis=0).astype(BF16)
            s = []
            for j in range(BAND_TILES):
                st = lax.dot_general(krefs[j][0, :, gs], q_bd, _NT, preferred_element_type=F32)
                parts = [st[:, hh * TILE:(hh + 1) * TILE] + bias_ref[g * n_slab + hh, j * TILE:(j + 1) * TILE, :]
                         for hh in range(n_slab)]
                if masked:
                    kpos = base + j * TILE + lax.broadcasted_iota(I32, (TILE, TILE), 0)
                    ok = (kpos >= 0) & (kpos < valid_len)
                    parts = [jnp.where(ok, x, NEG) for x in parts]
                s.append(jnp.concatenate(parts, axis=1))
            m = jnp.maximum(jnp.maximum(s[0], s[1]), s[2]).max(axis=0, keepdims=True)
            p = [jnp.exp(sj - m).astype(BF16) for sj in s]
            for hh in range(n_slab):
                acc = None
                for j in range(BAND_TILES):
                    d = jnp.dot(vt_ref[slots[j], g * n_slab + hh], p[j][:, hh * TILE:(hh + 1) * TILE],
                                preferred_element_type=F32)
                    acc = d if acc is None else acc + d
                blk = acc.T
                outs.append(blk[:, 0:HEAD_DIM] / blk[:, HEAD_DIM:HEAD_DIM + 1])
        o_ref[0] = jnp.concatenate(outs, axis=-1).astype(BF16)

    needs_mask = (base < 0) | (base + BAND_COLS > valid_len)

    @pl.when(needs_mask)
    def _():
        attend(True)

    @pl.when(jnp.logical_not(needs_mask))
    def _():
        attend(False)


def _band_attn(q, k, v, g_row, off, valid_len):
    b, tq = q.shape[:2]
    nq = tq // TILE
    qspec = pl.BlockSpec((1, TILE, A_WIDTH), lambda bb, i: (bb, i, 0))
    kspec = lambda d: pl.BlockSpec((1, TILE, A_WIDTH), lambda bb, i: (bb, jnp.maximum(i + off - d, 0), 0))
    return pl.pallas_call(
        functools.partial(_band_kernel, off=off, valid_len=valid_len),
        grid=(b, nq),
        in_specs=[pl.BlockSpec((A_HEADS, BAND_ROLL), lambda bb, i: (0, 0)),
                  qspec, kspec(2), kspec(1), kspec(0), kspec(2), kspec(1), kspec(0)],
        out_specs=pl.BlockSpec((1, TILE, A_WIDTH), lambda bb, i: (bb, i, 0)),
        out_shape=jax.ShapeDtypeStruct((b, tq, A_WIDTH), BF16),
        scratch_shapes=[pltpu.VMEM((A_HEADS, BAND_COLS, TILE), F32),
                        pltpu.VMEM((BAND_TILES, A_HEADS, 2 * HEAD_DIM, TILE), BF16)],
        compiler_params=_params("arbitrary", "arbitrary"),
        name="band_attn",
    )(g_row, q, k, k, k, v, v, v)


DSA_ROLL = 3 * TILE


def _t5_bucket(rel):
    half = N_BUCKETS // 2
    max_exact = half // 2
    n = jnp.abs(rel)
    log_ratio = jnp.log(jnp.maximum(n, 1).astype(jnp.float32) / max_exact) / math.log(T5_MAX_DIST / max_exact)
    large = jnp.minimum(max_exact + (log_ratio * (half - max_exact)).astype(jnp.int32), half - 1)
    return jnp.where(rel < 0, half, 0) + jnp.where(n < max_exact, n, large)


def _dsa_bias_row(t5_table):
    idx = jnp.arange(DSA_ROLL)
    d = jnp.where(idx < TILE, idx, idx - DSA_ROLL)
    far = t5_table[_t5_bucket(jnp.full((1,), 2 * TILE + 1, I32))]
    return (t5_table[_t5_bucket(TILE + d)] - far).T.astype(F32)


def _dsa_kernel(g_ref, qi_ref, wi_ref, qb_ref, ki_ref, kb_ref, vb_ref, o_ref,
                sc_ref, scb_ref, keep_ref, vt_ref, qis_ref, qbs_ref, m_ref, acc_ref, bias_ref,
                *, off, valid_len, q_valid, n_kt, tq, n_merge):
    i = pl.program_id(1)
    qt = i + off
    q0 = qt * TILE
    nk = qt + 1
    key_rows = lambda j: pl.ds(pl.multiple_of(j * TILE, TILE), TILE)

    @pl.when((pl.program_id(0) == 0) & (i == 0))
    def _():
        for h in range(B_HEADS):
            if n_merge:
                bias_ref[h] = g_ref[h]
            else:
                bias_ref[h] = _toeplitz(g_ref[h:h + 1, :], 2 * TILE, DSA_ROLL)[:, :tq]

    @pl.when(i == 0)
    def _():
        def body(j, c):
            for b in range(max(n_merge, 1)):
                vt = vb_ref[b, key_rows(j), :].astype(F32).T
                for g in range(B_KV_HEADS):
                    vt_ref[g, j, 0:HEAD_DIM, b * TILE:(b + 1) * TILE] = vt[g * HEAD_DIM:(g + 1) * HEAD_DIM].astype(BF16)
            for g in range(B_KV_HEADS):
                vt_ref[g, j, HEAD_DIM:2 * HEAD_DIM, :] = jnp.ones((HEAD_DIM, vt_ref.shape[-1]), BF16)
            return c

        lax.fori_loop(0, n_kt, body, 0)

    colq = lax.broadcasted_iota(I32, (1, tq), 1)
    rowk = lax.broadcasted_iota(I32, (TILE, 1), 0)

    if n_merge:
        wi_t = wi_ref[...]
        lim = jnp.full((1, tq), valid_len, I32)
    else:
        qi = qi_ref[0]
        for h in range(IDX_HEADS):
            qis_ref[h * tq:(h + 1) * tq, :] = qi[:, h * IDX_DIM:(h + 1) * IDX_DIM]
        qb = qb_ref[0]
        for g in range(B_KV_HEADS):
            for hh in range(B_GROUP):
                h = g * B_GROUP + hh
                qbs_ref[g, hh * tq:(hh + 1) * tq, :] = qb[:, h * HEAD_DIM:(h + 1) * HEAD_DIM]
        wi_t = wi_ref[0].T
        lim = jnp.minimum(q0 + (colq // CHUNK + 1) * CHUNK, valid_len)

    def score_tile(j):
        if n_merge:
            kt = ki_ref[key_rows(j), :]
            lg = jnp.concatenate([lax.dot_general(kt, qi_ref[h], _NT, preferred_element_type=F32)
                                  for h in range(IDX_HEADS)], axis=1)
        else:
            kt = ki_ref[0, key_rows(j), :]
            lg = lax.dot_general(kt, qis_ref[...], _NT, preferred_element_type=F32)
        sc = wi_t[0:1, :] * jnp.maximum(lg[:, 0:tq], 0.0)
        for h in range(1, IDX_HEADS):
            sc = sc + wi_t[h:h + 1, :] * jnp.maximum(lg[:, h * tq:(h + 1) * tq], 0.0)
        sc = jnp.where(j * TILE + rowk < lim, sc, -jnp.inf)
        sc_ref[j] = sc
        scb_ref[j] = sc.astype(BF16)
        rows8 = lambda hit: jnp.where(hit, 1, 0).reshape(TILE // 8, 8, tq).sum(axis=0)
        return rows8(sc > 0.0), rows8(sc >= 0.0)

    def score_pair(jj, c):
        pos_a, nn_a = score_tile(2 * jj)
        pos_b, nn_b = score_tile(2 * jj + 1)
        return c[0] + pos_a + pos_b, c[1] + nn_a + nn_b

    n_pos, n_nonneg = lax.fori_loop(0, (nk + 1) // 2, score_pair,
                                    (jnp.zeros((8, tq), I32), jnp.zeros((8, tq), I32)))

    def f32_of_key(k):
        return pltpu.bitcast(jnp.where(k < 0, k ^ 0x7FFFFFFF, k), F32)

    def bf16_of_key(k):
        bits = jnp.where(k < 0, k ^ 0x7FFF, k) & 0xFFFF
        return pltpu.bitcast(lax.shift_left(bits, 16), F32).astype(BF16)

    def count_bf16(cand):
        def body(jj, acc):
            for j in (2 * jj, 2 * jj + 1):
                ge = jnp.where(scb_ref[j] >= cand, jnp.int16(1), jnp.int16(0)).reshape(TILE // 16, 16, tq)
                part = ge[0]
                for r in range(1, TILE // 16):
                    part = part + ge[r]
                acc = acc + part
            return acc

        acc = lax.fori_loop(0, (nk + 1) // 2, body, jnp.zeros((16, tq), I16))
        return acc.astype(I32).sum(axis=0, keepdims=True)

    def count_f32(cand, strict=False):
        def body(j, acc):
            blk = sc_ref[j]
            hit = (blk > cand) if strict else (blk >= cand)
            return acc + jnp.where(hit, 1, 0).reshape(TILE // 8, 8, tq).sum(axis=0)

        return lax.fori_loop(0, nk, body, jnp.zeros((8, tq), I32)).sum(axis=0, keepdims=True)

    c_pos = n_pos.sum(axis=0, keepdims=True)
    zero_tie = (c_pos < TOPK_MAX) & (n_nonneg.sum(axis=0, keepdims=True) >= TOPK_MAX)
    skip1 = (colq >= q_valid) | zero_tie

    def level1(it, t):
        cand = t + lax.shift_left(jnp.int32(1), 15 - it)
        return jnp.where(count_bf16(bf16_of_key(cand)) >= TOPK_MAX, cand, t)

    t1 = lax.fori_loop(0, 16, level1, jnp.full((1, tq), MIN16, I32))
    settled1 = skip1 | (t1 <= HI_NEG_INF)

    def level2(st):
        lo, hi, thr_key, done = st
        live = (done == 0) & (hi - lo > 1)
        mid = lo + lax.shift_right_arithmetic(hi - lo, 1)
        c = count_f32(f32_of_key(mid))
        hit = live & (c == TOPK_MAX)
        return (jnp.where(live & (c >= TOPK_MAX), mid, lo), jnp.where(live & (c < TOPK_MAX), mid, hi),
                jnp.where(hit, mid, thr_key), jnp.where(hit, 1, done))

    def n_live(st):
        lo, hi, _, done = st
        return jnp.sum(jnp.where((done == 0) & (hi - lo > 1), 1, 0))

    def level2_pair(carry):
        st = level2(level2(carry[0]))
        return st, n_live(st)

    def key32_of_key16(k):
        return lax.shift_left(k, 16) | jnp.where(k < 0, 0xFFFF, 0)

    key_t1 = key32_of_key16(t1)
    st0 = (key_t1 - 0x8000, key32_of_key16(t1 + 1), key_t1, jnp.where(settled1, 1, 0))
    (lo, _, thr_key, done2), _ = lax.while_loop(lambda carry: carry[1] > 0, level2_pair, (st0, n_live(st0)))
    open2 = done2 == 0
    thr = jnp.where(open2, f32_of_key(lo), f32_of_key(thr_key))
    thr = jnp.where(zero_tie, 0.0, thr)
    thr = jnp.where(settled1 & jnp.logical_not(zero_tie), -jnp.inf, thr)
    thr = jnp.maximum(thr, float(jnp.finfo(F32).min))

    keep_ref[...] = jnp.where(zero_tie, TOPK_MAX - c_pos, 2 ** 30).astype(F32)

    @pl.when(jnp.sum(jnp.where(open2, 1, 0)) > 0)
    def _():
        above = count_f32(thr, strict=True)
        keep_ref[...] = jnp.where(open2, (TOPK_MAX - above).astype(F32), keep_ref[...])

    @pl.when(jnp.sum(jnp.where(open2 | zero_tie, 1, 0)) > 0)
    def _():
        keep = keep_ref[...]
        lower = (lax.broadcasted_iota(I32, (TILE, TILE), 0) > lax.broadcasted_iota(I32, (TILE, TILE), 1))
        lower = jnp.where(lower, 1.0, 0.0).astype(BF16)

        def body(jj, run):
            for j in (2 * jj, 2 * jj + 1):
                blk = sc_ref[j]
                eq = blk == thr
                eq_f = jnp.where(eq, 1.0, 0.0)
                before = jnp.dot(lower, eq_f.astype(BF16), preferred_element_type=F32)
                sc_ref[j] = jnp.where(eq & (run + before >= keep), -jnp.inf, blk)
                run = run + eq_f.reshape(TILE // 8, 8, tq).sum(axis=0).sum(axis=0, keepdims=True)
            return run

        lax.fori_loop(0, (nk + 1) // 2, body, jnp.zeros((1, tq), F32))

    for g in range(B_KV_HEADS):
        m_ref[g] = jnp.full((1, B_GROUP * tq), NEG, F32)
        acc_ref[g] = jnp.zeros((2 * HEAD_DIM, B_GROUP * tq), F32)

    def attend(tiles):
        sels = [sc_ref[j] >= thr for j, _ in tiles]
        for g in range(B_KV_HEADS):
            ss = []
            for (j, near), sel in zip(tiles, sels):
                if n_merge:
                    kt = kb_ref[g, key_rows(j), :]
                    st = jnp.concatenate([lax.dot_general(kt, qb_ref[g * B_GROUP + hh], _NT,
                                                          preferred_element_type=F32)
                                          for hh in range(B_GROUP)], axis=1)
                else:
                    kt = kb_ref[0, key_rows(j), g * HEAD_DIM:(g + 1) * HEAD_DIM]
                    st = lax.dot_general(kt, qbs_ref[g], _NT, preferred_element_type=F32)
                parts = []
                for hh in range(B_GROUP):
                    s_h = st[:, hh * tq:(hh + 1) * tq]
                    if near is not None:
                        s_h = s_h + bias_ref[g * B_GROUP + hh, near * TILE:(near + 1) * TILE, :]
                    parts.append(jnp.where(sel, s_h, -jnp.inf))
                ss.append(jnp.concatenate(parts, axis=1))
            m_old = m_ref[g]
            m_new = m_old
            for s in ss:
                m_new = jnp.maximum(m_new, s.max(axis=0, keepdims=True))
            pv = None
            for (j, _), s in zip(tiles, ss):
                p = jnp.exp(s - m_new)
                if n_merge:
                    lane_stream = colq // (tq // n_merge)
                    d = jnp.concatenate(
                        [jnp.dot(vt_ref[g, j],
                                 jnp.concatenate([jnp.where(lane_stream == b, p[:, hh * tq:(hh + 1) * tq], 0.0)
                                                  for b in range(n_merge)], axis=0).astype(BF16),
                                 preferred_element_type=F32) for hh in range(B_GROUP)], axis=1)
                else:
                    d = jnp.dot(vt_ref[g, j], p.astype(BF16), preferred_element_type=F32)
                pv = d if pv is None else pv + d
            acc_ref[g] = jnp.exp(m_old - m_new) * acc_ref[g] + pv
            m_ref[g] = m_new

    n_far = jnp.maximum(nk - 2, 0)

    def far_quad(jj, c):
        attend([(4 * jj + t, None) for t in range(4)])
        return c

    lax.fori_loop(0, n_far // 4, far_quad, 0)
    rem = n_far % 4

    @pl.when(rem >= 2)
    def _():
        attend([(n_far - rem, None), (n_far - rem + 1, None)])

    @pl.when(rem % 2 == 1)
    def _():
        attend([(n_far - 1, None)])

    @pl.when(nk >= 2)
    def _():
        attend([(nk - 2, 0), (nk - 1, 1)])

    @pl.when(nk < 2)
    def _():
        attend([(nk - 1, 1)])

    outs = []
    for g in range(B_KV_HEADS):
        for hh in range(B_GROUP):
            blk = acc_ref[g, :, hh * tq:(hh + 1) * tq].T
            outs.append(blk[:, 0:HEAD_DIM] / blk[:, HEAD_DIM:HEAD_DIM + 1])
    o_ref[0] = jnp.concatenate(outs, axis=-1).astype(BF16)


def _dsa_attn(qi, wi, qb, ki, kb, vb, g_row, off, valid_len, q_valid, tq):
    b, t_q = qi.shape[:2]
    tk = ki.shape[1]
    nq, n_kt = t_q // tq, tk // TILE
    assert tq == TILE or nq == 1
    assert n_kt % 2 == 0 and tk == n_kt * TILE
    qspec = lambda n: pl.BlockSpec((1, tq, n), lambda bb, i: (bb, i, 0))
    kspec = lambda n: pl.BlockSpec((1, tk, n), lambda bb, i: (bb, 0, 0))
    in_specs = [pl.BlockSpec((B_HEADS, DSA_ROLL), lambda bb, i: (0, 0)),
                qspec(512), qspec(128), qspec(512), kspec(IDX_DIM), kspec(128), kspec(128)]
    return _dsa_call((g_row, qi, wi, qb, ki, kb, vb), in_specs, (b, nq), t_q, tq, n_kt,
                     dict(off=off, valid_len=valid_len, q_valid=q_valid, n_merge=0))


def _dsa_attn_merged(qi, wi, qb, ki, kb, vb, t5_table, off, valid_len):
    s, ts = qi.shape[:2]
    tk = ki.shape[1]
    tq, n_kt = s * ts, tk // TILE
    assert tq % LANES == 0 and ts <= CHUNK and n_kt % 2 == 0 and tk == n_kt * TILE
    eye = jnp.eye(s, dtype=qi.dtype)

    def block_diag(q):
        q5 = q.reshape(s, ts, B_HEADS, 1, HEAD_DIM) * eye[:, None, None, :, None]
        return q5.transpose(2, 0, 1, 3, 4).reshape(B_HEADS, tq, s * HEAD_DIM)

    ki_cat = ki.transpose(1, 0, 2).reshape(tk, s * IDX_DIM)
    kb_cat = kb.reshape(s, tk, B_KV_HEADS, HEAD_DIM).transpose(2, 1, 0, 3).reshape(B_KV_HEADS, tk, s * HEAD_DIM)
    wi_t = wi[:, :, :IDX_HEADS].transpose(2, 0, 1).reshape(IDX_HEADS, tq)
    n_rel = 2 * TILE + ts - 1
    far = t5_table[_t5_bucket(jnp.full((1,), 2 * TILE + 1, I32))]
    f_row = (t5_table[_t5_bucket(jnp.arange(n_rel) - (TILE - 1))] - far).T
    hank = jnp.tile(f_row, (1, 2 * TILE + 1))[:, :2 * TILE * (n_rel + 1)].reshape(B_HEADS, 2 * TILE, n_rel + 1)
    bias = jnp.tile(hank[:, ::-1, :ts], (1, 1, s)).astype(F32)
    operands = (bias, block_diag(qi), wi_t, block_diag(qb), ki_cat, kb_cat, vb)
    whole = lambda a: pl.BlockSpec(a.shape, lambda bb, i: (0,) * a.ndim, pipeline_mode=pl.Buffered(1))
    out = _dsa_call(operands, [whole(a) for a in operands], (1, 1), tq, tq, n_kt,
                    dict(off=off, valid_len=valid_len, q_valid=tq, n_merge=s))
    return out.reshape(s, ts, B_WIDTH)


def _dsa_call(operands, in_specs, grid, t_q, tq, n_kt, static):
    n_merge = static["n_merge"]
    return pl.pallas_call(
        functools.partial(_dsa_kernel, n_kt=n_kt, tq=tq, **static),
        grid=grid,
        in_specs=in_specs,
        out_specs=pl.BlockSpec((1, tq, B_WIDTH), lambda bb, i: (bb, i, 0)),
        out_shape=jax.ShapeDtypeStruct((grid[0], t_q, B_WIDTH), BF16),
        scratch_shapes=[pltpu.VMEM((n_kt, TILE, tq), F32),
                        pltpu.VMEM((n_kt, TILE, tq), BF16),
                        pltpu.VMEM((1, tq), F32),
                        pltpu.VMEM((B_KV_HEADS, n_kt, 2 * HEAD_DIM, max(n_merge, 1) * TILE), BF16),
                        pltpu.VMEM((IDX_HEADS * tq, IDX_DIM), BF16),
                        pltpu.VMEM((B_KV_HEADS, B_GROUP * tq, HEAD_DIM), BF16),
                        pltpu.VMEM((B_KV_HEADS, 1, B_GROUP * tq), F32),
                        pltpu.VMEM((B_KV_HEADS, 2 * HEAD_DIM, B_GROUP * tq), F32),
                        pltpu.VMEM((B_HEADS, 2 * TILE, tq), F32)],
        compiler_params=_params("arbitrary", "arbitrary"),
        name="dsa_attn",
    )(*operands)


def _mem_kv_kernel(m_ref, wk_ref, wv_ref, k_o, v_o, kb_o, vb_o):
    mb = m_ref[...].astype(BF16)
    k = jnp.dot(mb, wk_ref[...], preferred_element_type=F32)
    v = jnp.dot(mb, wv_ref[...], preferred_element_type=F32)
    k_o[...] = k
    v_o[...] = v
    kb_o[...] = k.astype(BF16)
    vb_o[...] = v.astype(BF16)


def _mem_kv(mem2d, wk, wv):
    r = mem2d.shape[0]
    tm = MEM_LEN
    row = lambda n: pl.BlockSpec((tm, n), lambda i: (i, 0))
    const = lambda s: pl.BlockSpec(s, lambda i: (0, 0))
    sds = jax.ShapeDtypeStruct
    return pl.pallas_call(
        _mem_kv_kernel,
        grid=(r // tm,),
        in_specs=[row(D_MODEL), const((D_MODEL, MEM_WIDTH)), const((D_MODEL, MEM_WIDTH))],
        out_specs=[row(MEM_WIDTH)] * 4,
        out_shape=[sds((r, MEM_WIDTH), F32), sds((r, MEM_WIDTH), F32),
                   sds((r, MEM_WIDTH), BF16), sds((r, MEM_WIDTH), BF16)],
        compiler_params=_params("arbitrary"),
        name="mem_kv",
    )(mem2d, wk, wv)


FF_CHUNK = 256


def _tail_kernel(x_ref, oa_ref, ob_ref, mk_ref, mv_ref, hist_ref,
                 wo_ref, g1_ref, b1_ref, wq_ref, wmo_ref, g2_ref, b2_ref,
                 wu_ref, wc_ref, bc_ref, wd_ref, g3_ref, b3_ref,
                 o_ref, tail_ref, carry_ref, act_ref, *, tiles_per_batch, seg):
    i = pl.program_id(0)
    tm = x_ref.shape[0]
    nseg = tm // seg

    mix = jnp.concatenate([oa_ref[...], ob_ref[...]], axis=-1)
    h = _layer_norm(ALPHA * x_ref[...] + jnp.dot(mix, wo_ref[...], preferred_element_type=F32),
                    g1_ref[...], b1_ref[...])

    q = jnp.dot(h.astype(BF16), wq_ref[...], preferred_element_type=F32).astype(BF16)
    segs = []
    for s in range(nseg):
        qs = q[s * seg:(s + 1) * seg]
        heads = []
        for hd in range(MEM_HEADS):
            sl = slice(hd * MEM_HEAD_DIM, (hd + 1) * MEM_HEAD_DIM)
            sc = lax.dot_general(qs[:, sl], mk_ref[s, :, sl], _NT, preferred_element_type=F32) * MEM_HEAD_DIM ** -0.5
            p = jnp.exp(sc - sc.max(-1, keepdims=True))
            l = p.sum(-1, keepdims=True)
            heads.append(jnp.dot(p.astype(BF16), mv_ref[s, :, sl], preferred_element_type=F32) / l)
        segs.append(jnp.concatenate(heads, axis=-1))
    att = jnp.concatenate(segs, axis=0).astype(BF16)
    h = _layer_norm(ALPHA * h + jnp.dot(att, wmo_ref[...], preferred_element_type=F32), g2_ref[...], b2_ref[...])

    hb = h.astype(BF16)
    row = lax.broadcasted_iota(I32, (tm, 1), 0)
    first = (i % tiles_per_batch) == 0
    for c in range(D_FF // FF_CHUNK):
        cs = slice(c * FF_CHUNK, (c + 1) * FF_CHUNK)
        u = jnp.dot(hb, wu_ref[:, cs], preferred_element_type=F32)
        gt = jnp.dot(hb, wu_ref[:, D_FF + c * FF_CHUNK:D_FF + (c + 1) * FF_CHUNK], preferred_element_type=F32)
        p1 = pltpu.roll(gt, 1, 0)
        p2 = pltpu.roll(gt, 2, 0)
        for s in range(nseg):
            hist = hist_ref[s, :, cs]
            if tiles_per_batch > 1:
                hist = jnp.where(first, hist, carry_ref[:, cs])
            p1 = jnp.where(row == s * seg, hist[7:8, :], p1)
            p2 = jnp.where(row == s * seg, hist[6:7, :], p2)
            p2 = jnp.where(row == s * seg + 1, hist[7:8, :], p2)
        gc = bc_ref[:, cs] + ((wc_ref[0:1, cs] * p2 + wc_ref[1:2, cs] * p1) + wc_ref[2:3, cs] * gt)
        act_ref[:, cs] = (u * jax.nn.gelu(gc)).astype(BF16)
        for s in range(nseg):
            tail_ref[s, :, cs] = gt[(s + 1) * seg - 8:(s + 1) * seg, :]
        carry_ref[:, cs] = gt[tm - 8:tm, :]
    f = jnp.dot(act_ref[...], wd_ref[...], preferred_element_type=F32)
    o_ref[...] = _layer_norm(ALPHA * h + f, g3_ref[...], b3_ref[...])


def _layer_tail(x2d, oa, ob, mk, mv, hist, weights, tm, tiles_per_batch, seg):
    r = x2d.shape[0]
    nseg = tm // seg
    n_stream = r // (tm * tiles_per_batch) * nseg
    row = lambda n: pl.BlockSpec((tm, n), lambda i: (i, 0))
    per_stream = lambda a, b: pl.BlockSpec((nseg, a, b), lambda i: (i // tiles_per_batch, 0, 0))
    const = lambda a: pl.BlockSpec(a.shape, lambda i: (0,) * a.ndim, pipeline_mode=pl.Buffered(1))
    return pl.pallas_call(
        functools.partial(_tail_kernel, tiles_per_batch=tiles_per_batch, seg=seg),
        grid=(r // tm,),
        in_specs=[row(D_MODEL), row(A_WIDTH), row(B_WIDTH), per_stream(MEM_LEN, MEM_WIDTH),
                  per_stream(MEM_LEN, MEM_WIDTH), per_stream(8, D_FF)] + [const(w) for w in weights],
        out_specs=[row(D_MODEL), per_stream(8, D_FF)],
        out_shape=[jax.ShapeDtypeStruct((r, D_MODEL), F32), jax.ShapeDtypeStruct((n_stream, 8, D_FF), F32)],
        scratch_shapes=[pltpu.VMEM((8, D_FF), F32), pltpu.VMEM((tm, D_FF), BF16)],
        compiler_params=_params("arbitrary"),
        name="layer_tail",
    )(x2d, oa, ob, mk, mv, hist, *weights)


def _pad_rows(a, n):
    return jnp.pad(a, ((0, 0), (0, n - a.shape[1])) + ((0, 0),) * (a.ndim - 2))


def _hist8(g_hist):
    return jnp.pad(g_hist, ((0, 0), (8 - g_hist.shape[1], 0), (0, 0)))


def kernel(x_prompt, x_sample, cache_a_k, cache_a_v, cache_b_k, cache_b_v, cache_b_kidx, cache_mem_k, cache_mem_v, state_ffn_conv, mem_prompt, w_in, a_rel_bias, t5_bias, w_o, ln1_g, ln1_b, w_mq, w_mk, w_mv, w_mo, ln2_g, ln2_b, w_up, w_conv, b_conv, w_down, ln3_g, ln3_b):
    bp, tp = x_prompt.shape[:2]
    bs, ts = x_sample.shape[:2]
    l = 0
    vec = lambda a: a[l].reshape(1, -1)
    w_in_p = _prep_w_in(w_in[l])
    w_o_b = w_o[l].astype(BF16)
    w_mq_b, w_mk_b, w_mv_b, w_mo_b = (w[l].astype(BF16) for w in (w_mq, w_mk, w_mv, w_mo))
    w_up_b, w_down_b = w_up[l].astype(BF16), w_down[l].astype(BF16)
    band_row = _band_bias_row(a_rel_bias[l])
    dsa_row = _dsa_bias_row(t5_bias)
    tail_w = (w_o_b, vec(ln1_g), vec(ln1_b), w_mq_b, w_mo_b, vec(ln2_g), vec(ln2_b),
              w_up_b, w_conv[l], vec(b_conv), w_down_b, vec(ln3_g), vec(ln3_b))

    tm = 512
    a_keep = min(N_PREV_CHUNKS * CHUNK, tp)
    (qa, ka, va, qb, qi, kb, vb, ki, kb_b, vb_b, ki_b, wi, ka_tail, va_tail) = _in_proj(
        x_prompt.reshape(bp * tp, D_MODEL), w_in_p, tm, tp // tm)
    r3 = lambda a: a.reshape(bp, tp, a.shape[-1])
    oa = _band_attn(r3(qa), r3(ka), r3(va), band_row, 0, tp)
    ob = _dsa_attn(r3(qi), r3(wi), r3(qb), r3(ki_b), r3(kb_b), r3(vb_b), dsa_row, 0, tp, TILE, TILE)
    mk, mv, mk_b, mv_b = _mem_kv(mem_prompt.reshape(bp * MEM_LEN, D_MODEL), w_mk_b, w_mv_b)
    xp, p_tail = _layer_tail(x_prompt.reshape(bp * tp, D_MODEL), oa.reshape(bp * tp, A_WIDTH),
                             ob.reshape(bp * tp, B_WIDTH), mk_b.reshape(bp, MEM_LEN, MEM_WIDTH),
                             mv_b.reshape(bp, MEM_LEN, MEM_WIDTH), jnp.zeros((bp, 8, D_FF), F32), tail_w,
                             tm, tp // tm, tm)
    prompt_state = (
        ka_tail.reshape(bp, tm, A_HEADS, HEAD_DIM)[:, tm - a_keep:][None],
        va_tail.reshape(bp, tm, A_HEADS, HEAD_DIM)[:, tm - a_keep:][None],
        kb.reshape(1, bp, tp, B_KV_HEADS, HEAD_DIM), vb.reshape(1, bp, tp, B_KV_HEADS, HEAD_DIM),
        ki.reshape(1, bp, tp, IDX_DIM),
        mk.reshape(1, bp, MEM_LEN, MEM_HEADS, MEM_HEAD_DIM), mv.reshape(1, bp, MEM_LEN, MEM_HEADS, MEM_HEAD_DIM),
        p_tail[:, 8 - (CONV_W - 1):][None])

    rs = bs * ts
    (qa, ka, va, qb, qi, kb, vb, ki, kb_b, vb_b, ki_b, wi, ka_new, va_new) = _in_proj(
        x_sample.reshape(rs, D_MODEL), w_in_p, rs, 1)
    s3 = lambda a: a.reshape(bs, ts, -1)
    qpad = lambda a, n: _pad_rows(s3(a), n)

    past_a = cache_a_k.shape[2]
    n_a = past_a + ts
    t_a = -(-n_a // TILE) * TILE
    seq_a = lambda cache, new: _pad_rows(
        jnp.concatenate([cache[l].reshape(bs, past_a, A_WIDTH), s3(new)], axis=1), t_a).astype(BF16)
    oa = _band_attn(qpad(qa, TILE), seq_a(cache_a_k, ka_new), seq_a(cache_a_v, va_new), band_row,
                    past_a // TILE, n_a)

    past_b = cache_b_k.shape[2]
    n_b = past_b + ts
    t_b = -(-n_b // (2 * TILE)) * 2 * TILE
    seq_b = lambda cache, new: _pad_rows(
        jnp.concatenate([cache[l].reshape(bs, past_b, -1), s3(new)], axis=1), t_b).astype(BF16)
    ob = _dsa_attn_merged(s3(qi), s3(wi), s3(qb), seq_b(cache_b_kidx, ki), seq_b(cache_b_k, kb),
                          seq_b(cache_b_v, vb), t5_bias, past_b // TILE, n_b)

    xs, s_tail = _layer_tail(x_sample.reshape(rs, D_MODEL), oa[:, :ts].reshape(rs, A_WIDTH),
                             ob[:, :ts].reshape(rs, B_WIDTH),
                             cache_mem_k[l].reshape(bs, MEM_LEN, MEM_WIDTH).astype(BF16),
                             cache_mem_v[l].reshape(bs, MEM_LEN, MEM_WIDTH).astype(BF16),
                             _hist8(state_ffn_conv[l]), tail_w, rs, 1, ts)
    sample_state = (
        ka_new.reshape(1, bs, ts, A_HEADS, HEAD_DIM), va_new.reshape(1, bs, ts, A_HEADS, HEAD_DIM),
        kb.reshape(1, bs, ts, B_KV_HEADS, HEAD_DIM), vb.reshape(1, bs, ts, B_KV_HEADS, HEAD_DIM),
        ki.reshape(1, bs, ts, IDX_DIM), s_tail[:, 8 - (CONV_W - 1):][None])

    return (xp.reshape(bp, tp, D_MODEL), xs.reshape(bs, ts, D_MODEL)) + prompt_state + sample_state
```

```python
import functools
import math

import jax
import jax.numpy as jnp
from jax import lax
from jax.experimental import pallas as pl
from jax.experimental.pallas import tpu as pltpu

F32 = jnp.float32
BF16 = jnp.bfloat16
I32 = jnp.int32
I16 = jnp.int16

D_MODEL = 1024
CHUNK = 64
N_PREV_CHUNKS = 8
HEAD_DIM = 64
A_HEADS = 8
A_WIDTH = A_HEADS * HEAD_DIM
A_MAX_REL = 64
B_HEADS = 8
B_KV_HEADS = 2
B_GROUP = B_HEADS // B_KV_HEADS
B_WIDTH = B_HEADS * HEAD_DIM
B_KV_WIDTH = B_KV_HEADS * HEAD_DIM
IDX_HEADS = 8
IDX_DIM = 64
TOPK_MAX = 256
N_BUCKETS = 32
T5_MAX_DIST = 128
MEM_LEN = 256
MEM_HEADS = 4
MEM_HEAD_DIM = 128
MEM_WIDTH = MEM_HEADS * MEM_HEAD_DIM
D_FF = 2816
CONV_W = 3
IN_SIZES = (A_WIDTH, A_WIDTH, A_WIDTH, B_WIDTH, B_KV_WIDTH, B_KV_WIDTH, IDX_HEADS * IDX_DIM, IDX_DIM, IDX_HEADS)
DEPTH = 1
ALPHA = (2 * DEPTH) ** 0.25
LN_EPS = 1e-5
ATTN_SCALE = HEAD_DIM ** -0.5
NEG = -1e30

LANES = 128
TILE = 256
BAND_TILES = 1 + (N_PREV_CHUNKS * CHUNK) // TILE
VMEM_LIMIT = 56 * 1024 * 1024

_C_QA, _C_KA, _C_VA, _C_QB = 0, 512, 1024, 1536
_C_KB, _C_VB, _C_QI, _C_KI, _C_WI = 2048, 2176, 2304, 2816, 2944
IN_PAD = 3072

MIN16 = -32768
HI_NEG_INF = -32641

_NT = (((1,), (1,)), ((), ()))


def _params(*sem):
    return pltpu.CompilerParams(dimension_semantics=sem, vmem_limit_bytes=VMEM_LIMIT)


def _layer_norm(z, g, b):
    mu = jnp.mean(z, axis=-1, keepdims=True)
    d = z - mu
    var = jnp.mean(d * d, axis=-1, keepdims=True)
    return d * lax.rsqrt(var + LN_EPS) * g + b


def _toeplitz(g_row, rows, width):
    return pltpu.roll(jnp.broadcast_to(g_row, (rows, width)), 0, 1, stride=1, stride_axis=0)


def _in_proj_kernel(x_ref, w_ref, qa_o, ka_o, va_o, qb_o, qi_o, kb_o, vb_o, ki_o, kbb_o, vbb_o, kib_o,
                    wi_o, kat_o, vat_o, *, tiles_per_batch, wi_scale):
    i = pl.program_id(0)
    xb = x_ref[...].astype(BF16)

    def mm(c0, n):
        return jnp.dot(xb, w_ref[:, c0:c0 + n], preferred_element_type=F32)

    qa_o[...] = mm(_C_QA, 512).astype(BF16)
    ka = mm(_C_KA, 512)
    va = mm(_C_VA, 512)
    ka_o[...] = ka.astype(BF16)
    va_o[...] = va.astype(BF16)
    qb_o[...] = mm(_C_QB, 512).astype(BF16)
    qi_o[...] = mm(_C_QI, 512).astype(BF16)
    kb = mm(_C_KB, 128)
    vb = mm(_C_VB, 128)
    for g in range(B_KV_HEADS):
        kb_o[:, g, :] = kb[:, g * HEAD_DIM:(g + 1) * HEAD_DIM]
        vb_o[:, g, :] = vb[:, g * HEAD_DIM:(g + 1) * HEAD_DIM]
    kbb_o[...] = kb.astype(BF16)
    vbb_o[...] = vb.astype(BF16)
    ki = mm(_C_KI, 128)[:, :IDX_DIM]
    ki_o[...] = ki
    kib_o[...] = ki.astype(BF16)
    wi_o[...] = mm(_C_WI, 128) * wi_scale

    @pl.when(i % tiles_per_batch == tiles_per_batch - 1)
    def _():
        kat_o[...] = ka
        vat_o[...] = va


def _prep_w_in(w):
    parts, off = [], 0
    for n in IN_SIZES:
        parts.append(w[:, off:off + n])
        off += n
    qa, ka, va, qb, kb, vb, qi, ki, wi = parts
    pad = lambda a, n: jnp.pad(a, ((0, 0), (0, n - a.shape[1])))
    cols = [qa * ATTN_SCALE, ka, va, qb * ATTN_SCALE, kb, vb, qi * IDX_DIM ** -0.5, pad(ki, 128), pad(wi, 128)]
    return jnp.concatenate(cols, axis=1).astype(BF16)


def _in_proj(x2d, w_pad, tm, tiles_per_batch):
    r = x2d.shape[0]
    n_tiles = r // tm
    n_batch = n_tiles // tiles_per_batch
    row = lambda n: pl.BlockSpec((tm, n), lambda i: (i, 0))
    tail = pl.BlockSpec((tm, 512), lambda i: (i // tiles_per_batch, 0))
    sds = jax.ShapeDtypeStruct
    kv_state = pl.BlockSpec((tm, B_KV_HEADS, HEAD_DIM), lambda i: (i, 0, 0))
    kv_shape = sds((r, B_KV_HEADS, HEAD_DIM), F32)
    out_shape = [sds((r, 512), BF16)] * 5 + [kv_shape, kv_shape, sds((r, IDX_DIM), F32),
                                              sds((r, 128), BF16), sds((r, 128), BF16), sds((r, IDX_DIM), BF16),
                                              sds((r, 128), F32),
                                              sds((n_batch * tm, 512), F32), sds((n_batch * tm, 512), F32)]
    out_specs = [row(512)] * 5 + [kv_state, kv_state, row(IDX_DIM), row(128), row(128), row(IDX_DIM), row(128),
                                  tail, tail]
    return pl.pallas_call(
        functools.partial(_in_proj_kernel, tiles_per_batch=tiles_per_batch, wi_scale=IDX_HEADS ** -0.5),
        grid=(n_tiles,),
        in_specs=[pl.BlockSpec((tm, D_MODEL), lambda i: (i, 0)),
                  pl.BlockSpec((D_MODEL, IN_PAD), lambda i: (0, 0))],
        out_specs=out_specs,
        out_shape=out_shape,
        compiler_params=_params("arbitrary"),
        name="in_proj",
    )(x2d, w_pad)


BAND_COLS = BAND_TILES * TILE
BAND_ROLL = BAND_COLS + TILE


def _band_bias_row(table):
    idx = jnp.arange(BAND_ROLL)
    d = jnp.where(idx < TILE, idx, idx - BAND_ROLL)
    rel = (BAND_TILES - 1) * TILE + d
    return table[jnp.clip(rel, -A_MAX_REL, A_MAX_REL) + A_MAX_REL].T.astype(F32)


def _band_kernel(g_ref, q_ref, k0, k1, k2, v0, v1, v2, o_ref, bias_ref, vt_ref, *, off, valid_len):
    i = pl.program_id(1)
    kt = i + off
    krefs, vrefs = (k0, k1, k2), (v0, v1, v2)

    @pl.when((pl.program_id(0) == 0) & (i == 0))
    def _():
        c = lax.broadcasted_iota(I32, (BAND_COLS, TILE), 0) // CHUNK
        r = lax.broadcasted_iota(I32, (BAND_COLS, TILE), 1) // CHUNK
        ok = (c >= r) & (c <= r + N_PREV_CHUNKS)
        for h in range(A_HEADS):
            bias_ref[h] = jnp.where(ok, _toeplitz(g_ref[h:h + 1, :], BAND_COLS, BAND_ROLL)[:, :TILE], NEG)
        ones = jnp.ones((HEAD_DIM, TILE), BF16)
        for s in range(BAND_TILES):
            for h in range(A_HEADS):
                vt_ref[s, h, HEAD_DIM:2 * HEAD_DIM, :] = ones

    def put(slot, vref):
        vt = vref[0].astype(F32).T
        for h in range(A_HEADS):
            vt_ref[slot, h, 0:HEAD_DIM, :] = vt[h * HEAD_DIM:(h + 1) * HEAD_DIM].astype(BF16)

    slots = [(kt + 1 + j) % BAND_TILES for j in range(BAND_TILES)]

    @pl.when(i == 0)
    def _():
        for j in range(BAND_TILES - 1):
            put(slots[j], vrefs[j])

    put(slots[BAND_TILES - 1], vrefs[BAND_TILES - 1])

    base = (kt - (BAND_TILES - 1)) * TILE

    n_slab = 4
    slab_w = n_slab * HEAD_DIM
    lane_head = lax.broadcasted_iota(I32, (TILE, slab_w), 1) // HEAD_DIM

    def attend(masked):
        outs = []
        for g in range(A_HEADS // n_slab):
            gs = slice(g * slab_w, (g + 1) * slab_w)
            q_slab = q_ref[0, :, gs].astype(F32)
            q_bd = jnp.concatenate([jnp.where(lane_head == hh, q_slab, 0.0) for hh in range(n_slab)],
                                   axis=0).astype(BF16)
            s = []
            for j in range(BAND_TILES):
                st = lax.dot_general(krefs[j][0, :, gs], q_bd, _NT, preferred_element_type=F32)
                parts = [st[:, hh * TILE:(hh + 1) * TILE] + bias_ref[g * n_slab + hh, j * TILE:(j + 1) * TILE, :]
                         for hh in range(n_slab)]
                if masked:
                    kpos = base + j * TILE + lax.broadcasted_iota(I32, (TILE, TILE), 0)
                    ok = (kpos >= 0) & (kpos < valid_len)
                    parts = [jnp.where(ok, x, NEG) for x in parts]
                s.append(jnp.concatenate(parts, axis=1))
            m = jnp.maximum(jnp.maximum(s[0], s[1]), s[2]).max(axis=0, keepdims=True)
            p = [jnp.exp(sj - m).astype(BF16) for sj in s]
            for hh in range(n_slab):
                acc = None
                for j in range(BAND_TILES):
                    d = jnp.dot(vt_ref[slots[j], g * n_slab + hh], p[j][:, hh * TILE:(hh + 1) * TILE],
                                preferred_element_type=F32)
                    acc = d if acc is None else acc + d
                blk = acc.T
                outs.append(blk[:, 0:HEAD_DIM] / blk[:, HEAD_DIM:HEAD_DIM + 1])
        o_ref[0] = jnp.concatenate(outs, axis=-1).astype(BF16)

    needs_mask = (base < 0) | (base + BAND_COLS > valid_len)

    @pl.when(needs_mask)
    def _():
        attend(True)

    @pl.when(jnp.logical_not(needs_mask))
    def _():
        attend(False)


def _band_attn(q, k, v, g_row, off, valid_len):
    b, tq = q.shape[:2]
    nq = tq // TILE
    qspec = pl.BlockSpec((1, TILE, A_WIDTH), lambda bb, i: (bb, i, 0))
    kspec = lambda d: pl.BlockSpec((1, TILE, A_WIDTH), lambda bb, i: (bb, jnp.maximum(i + off - d, 0), 0))
    return pl.pallas_call(
        functools.partial(_band_kernel, off=off, valid_len=valid_len),
        grid=(b, nq),
        in_specs=[pl.BlockSpec((A_HEADS, BAND_ROLL), lambda bb, i: (0, 0)),
                  qspec, kspec(2), kspec(1), kspec(0), kspec(2), kspec(1), kspec(0)],
        out_specs=pl.BlockSpec((1, TILE, A_WIDTH), lambda bb, i: (bb, i, 0)),
        out_shape=jax.ShapeDtypeStruct((b, tq, A_WIDTH), BF16),
        scratch_shapes=[pltpu.VMEM((A_HEADS, BAND_COLS, TILE), F32),
                        pltpu.VMEM((BAND_TILES, A_HEADS, 2 * HEAD_DIM, TILE), BF16)],
        compiler_params=_params("arbitrary", "arbitrary"),
        name="band_attn",
    )(g_row, q, k, k, k, v, v, v)


DSA_ROLL = 3 * TILE


def _t5_bucket(rel):
    half = N_BUCKETS // 2
    max_exact = half // 2
    n = jnp.abs(rel)
    log_ratio = jnp.log(jnp.maximum(n, 1).astype(jnp.float32) / max_exact) / math.log(T5_MAX_DIST / max_exact)
    large = jnp.minimum(max_exact + (log_ratio * (half - max_exact)).astype(jnp.int32), half - 1)
    return jnp.where(rel < 0, half, 0) + jnp.where(n < max_exact, n, large)


def _dsa_bias_row(t5_table):
    idx = jnp.arange(DSA_ROLL)
    d = jnp.where(idx < TILE, idx, idx - DSA_ROLL)
    far = t5_table[_t5_bucket(jnp.full((1,), 2 * TILE + 1, I32))]
    return (t5_table[_t5_bucket(TILE + d)] - far).T.astype(F32)


def _dsa_kernel(g_ref, qi_ref, wi_ref, qb_ref, ki_ref, kb_ref, vb_ref, o_ref,
                sc_ref, scb_ref, keep_ref, vt_ref, qis_ref, qbs_ref, m_ref, acc_ref, bias_ref,
                *, off, valid_len, q_valid, n_kt, tq, n_merge):
    i = pl.program_id(1)
    qt = i + off
    q0 = qt * TILE
    nk = qt + 1
    key_rows = lambda j: pl.ds(pl.multiple_of(j * TILE, TILE), TILE)

    @pl.when((pl.program_id(0) == 0) & (i == 0))
    def _():
        for h in range(B_HEADS):
            tile = _toeplitz(g_ref[h:h + 1, :], 2 * TILE, DSA_ROLL)[:, :tq]
            if n_merge:
                ts = tq // n_merge
                stream = lax.broadcasted_iota(I32, (1, tq), 1) // ts
                first = tile
                for b in range(1, n_merge):
                    tile = jnp.where(stream == b, pltpu.roll(first, b * ts, 1), tile)
            bias_ref[h] = tile

    @pl.when(i == 0)
    def _():
        def body(j, c):
            for b in range(max(n_merge, 1)):
                vt = vb_ref[b, key_rows(j), :].astype(F32).T
                for g in range(B_KV_HEADS):
                    vt_ref[g, j, 0:HEAD_DIM, b * TILE:(b + 1) * TILE] = vt[g * HEAD_DIM:(g + 1) * HEAD_DIM].astype(BF16)
            for g in range(B_KV_HEADS):
                vt_ref[g, j, HEAD_DIM:2 * HEAD_DIM, :] = jnp.ones((HEAD_DIM, vt_ref.shape[-1]), BF16)
            return c

        lax.fori_loop(0, n_kt, body, 0)

    colq = lax.broadcasted_iota(I32, (1, tq), 1)
    rowk = lax.broadcasted_iota(I32, (TILE, 1), 0)

    if n_merge:
        width = n_merge * HEAD_DIM
        own = (lax.broadcasted_iota(I32, (tq, width), 0) // (tq // n_merge)
               == lax.broadcasted_iota(I32, (tq, width), 1) // HEAD_DIM)

        def block_diag(x):
            return jnp.where(own, jnp.concatenate([x.astype(F32)] * n_merge, axis=1), 0.0).astype(BF16)

        qi, qb = qi_ref[...], qb_ref[...]
        for h in range(B_HEADS):
            qis_ref[h] = block_diag(qi[:, h * IDX_DIM:(h + 1) * IDX_DIM])
            qbs_ref[h] = block_diag(qb[:, h * HEAD_DIM:(h + 1) * HEAD_DIM])
        side_by_side = lambda ref, j, cols: jnp.concatenate(
            [ref[b, key_rows(j), cols] for b in range(n_merge)], axis=1)
        wi_t = wi_ref[...].T
        lim = jnp.full((1, tq), valid_len, I32)
    else:
        qi = qi_ref[0]
        for h in range(IDX_HEADS):
            qis_ref[h * tq:(h + 1) * tq, :] = qi[:, h * IDX_DIM:(h + 1) * IDX_DIM]
        qb = qb_ref[0]
        for g in range(B_KV_HEADS):
            for hh in range(B_GROUP):
                h = g * B_GROUP + hh
                qbs_ref[g, hh * tq:(hh + 1) * tq, :] = qb[:, h * HEAD_DIM:(h + 1) * HEAD_DIM]
        wi_t = wi_ref[0].T
        lim = jnp.minimum(q0 + (colq // CHUNK + 1) * CHUNK, valid_len)

    def score_tile(j):
        if n_merge:
            kt = side_by_side(ki_ref, j, slice(None))
            lg = jnp.concatenate([lax.dot_general(kt, qis_ref[h], _NT, preferred_element_type=F32)
                                  for h in range(IDX_HEADS)], axis=1)
        else:
            kt = ki_ref[0, key_rows(j), :]
            lg = lax.dot_general(kt, qis_ref[...], _NT, preferred_element_type=F32)
        sc = wi_t[0:1, :] * jnp.maximum(lg[:, 0:tq], 0.0)
        for h in range(1, IDX_HEADS):
            sc = sc + wi_t[h:h + 1, :] * jnp.maximum(lg[:, h * tq:(h + 1) * tq], 0.0)
        sc = jnp.where(j * TILE + rowk < lim, sc, -jnp.inf)
        sc_ref[j] = sc
        scb_ref[j] = sc.astype(BF16)
        rows8 = lambda hit: jnp.where(hit, 1, 0).reshape(TILE // 8, 8, tq).sum(axis=0)
        return rows8(sc > 0.0), rows8(sc >= 0.0)

    def score_pair(jj, c):
        pos_a, nn_a = score_tile(2 * jj)
        pos_b, nn_b = score_tile(2 * jj + 1)
        return c[0] + pos_a + pos_b, c[1] + nn_a + nn_b

    n_pos, n_nonneg = lax.fori_loop(0, (nk + 1) // 2, score_pair,
                                    (jnp.zeros((8, tq), I32), jnp.zeros((8, tq), I32)))

    def f32_of_key(k):
        return pltpu.bitcast(jnp.where(k < 0, k ^ 0x7FFFFFFF, k), F32)

    def bf16_of_key(k):
        bits = jnp.where(k < 0, k ^ 0x7FFF, k) & 0xFFFF
        return pltpu.bitcast(lax.shift_left(bits, 16), F32).astype(BF16)

    def count_bf16(cand):
        def body(jj, acc):
            for j in (2 * jj, 2 * jj + 1):
                ge = jnp.where(scb_ref[j] >= cand, jnp.int16(1), jnp.int16(0)).reshape(TILE // 16, 16, tq)
                part = ge[0]
                for r in range(1, TILE // 16):
                    part = part + ge[r]
                acc = acc + part
            return acc

        acc = lax.fori_loop(0, (nk + 1) // 2, body, jnp.zeros((16, tq), I16))
        return acc.astype(I32).sum(axis=0, keepdims=True)

    def count_f32(cand, strict=False):
        def body(j, acc):
            blk = sc_ref[j]
            hit = (blk > cand) if strict else (blk >= cand)
            return acc + jnp.where(hit, 1, 0).reshape(TILE // 8, 8, tq).sum(axis=0)

        return lax.fori_loop(0, nk, body, jnp.zeros((8, tq), I32)).sum(axis=0, keepdims=True)

    c_pos = n_pos.sum(axis=0, keepdims=True)
    zero_tie = (c_pos < TOPK_MAX) & (n_nonneg.sum(axis=0, keepdims=True) >= TOPK_MAX)
    skip1 = (colq >= q_valid) | zero_tie

    def level1(it, t):
        cand = t + lax.shift_left(jnp.int32(1), 15 - it)
        return jnp.where(count_bf16(bf16_of_key(cand)) >= TOPK_MAX, cand, t)

    t1 = lax.fori_loop(0, 16, level1, jnp.full((1, tq), MIN16, I32))
    settled1 = skip1 | (t1 <= HI_NEG_INF)

    def level2(st):
        lo, hi, thr_key, done = st
        live = (done == 0) & (hi - lo > 1)
        mid = lo + lax.shift_right_arithmetic(hi - lo, 1)
        c = count_f32(f32_of_key(mid))
        hit = live & (c == TOPK_MAX)
        return (jnp.where(live & (c >= TOPK_MAX), mid, lo), jnp.where(live & (c < TOPK_MAX), mid, hi),
                jnp.where(hit, mid, thr_key), jnp.where(hit, 1, done))

    def n_live(st):
        lo, hi, _, done = st
        return jnp.sum(jnp.where((done == 0) & (hi - lo > 1), 1, 0))

    def level2_pair(carry):
        st = level2(level2(carry[0]))
        return st, n_live(st)

    def key32_of_key16(k):
        return lax.shift_left(k, 16) | jnp.where(k < 0, 0xFFFF, 0)

    key_t1 = key32_of_key16(t1)
    st0 = (key_t1 - 0x8000, key32_of_key16(t1 + 1), key_t1, jnp.where(settled1, 1, 0))
    (lo, _, thr_key, done2), _ = lax.while_loop(lambda carry: carry[1] > 0, level2_pair, (st0, n_live(st0)))
    open2 = done2 == 0
    thr = jnp.where(open2, f32_of_key(lo), f32_of_key(thr_key))
    thr = jnp.where(zero_tie, 0.0, thr)
    thr = jnp.where(settled1 & jnp.logical_not(zero_tie), -jnp.inf, thr)
    thr = jnp.maximum(thr, float(jnp.finfo(F32).min))

    keep_ref[...] = jnp.where(zero_tie, TOPK_MAX - c_pos, 2 ** 30).astype(F32)

    @pl.when(jnp.sum(jnp.where(open2, 1, 0)) > 0)
    def _():
        above = count_f32(thr, strict=True)
        keep_ref[...] = jnp.where(open2, (TOPK_MAX - above).astype(F32), keep_ref[...])

    @pl.when(jnp.sum(jnp.where(open2 | zero_tie, 1, 0)) > 0)
    def _():
        keep = keep_ref[...]
        lower = (lax.broadcasted_iota(I32, (TILE, TILE), 0) > lax.broadcasted_iota(I32, (TILE, TILE), 1))
        lower = jnp.where(lower, 1.0, 0.0).astype(BF16)

        def body(jj, run):
            for j in (2 * jj, 2 * jj + 1):
                blk = sc_ref[j]
                eq = blk == thr
                eq_f = jnp.where(eq, 1.0, 0.0)
                before = jnp.dot(lower, eq_f.astype(BF16), preferred_element_type=F32)
                sc_ref[j] = jnp.where(eq & (run + before >= keep), -jnp.inf, blk)
                run = run + eq_f.reshape(TILE // 8, 8, tq).sum(axis=0).sum(axis=0, keepdims=True)
            return run

        lax.fori_loop(0, (nk + 1) // 2, body, jnp.zeros((1, tq), F32))

    for g in range(B_KV_HEADS):
        m_ref[g] = jnp.full((1, B_GROUP * tq), NEG, F32)
        acc_ref[g] = jnp.zeros((2 * HEAD_DIM, B_GROUP * tq), F32)

    def attend(tiles):
        sels = [sc_ref[j] >= thr for j, _ in tiles]
        for g in range(B_KV_HEADS):
            ss = []
            for (j, near), sel in zip(tiles, sels):
                if n_merge:
                    kt = side_by_side(kb_ref, j, slice(g * HEAD_DIM, (g + 1) * HEAD_DIM))
                    st = jnp.concatenate([lax.dot_general(kt, qbs_ref[g * B_GROUP + hh], _NT,
                                                          preferred_element_type=F32)
                                          for hh in range(B_GROUP)], axis=1)
                else:
                    kt = kb_ref[0, key_rows(j), g * HEAD_DIM:(g + 1) * HEAD_DIM]
                    st = lax.dot_general(kt, qbs_ref[g], _NT, preferred_element_type=F32)
                parts = []
                for hh in range(B_GROUP):
                    s_h = st[:, hh * tq:(hh + 1) * tq]
                    if near is not None:
                        s_h = s_h + bias_ref[g * B_GROUP + hh, near * TILE:(near + 1) * TILE, :]
                    parts.append(jnp.where(sel, s_h, -jnp.inf))
                ss.append(jnp.concatenate(parts, axis=1))
            m_old = m_ref[g]
            m_new = m_old
            for s in ss:
                m_new = jnp.maximum(m_new, s.max(axis=0, keepdims=True))
            pv = None
            for (j, _), s in zip(tiles, ss):
                p = jnp.exp(s - m_new)
                if n_merge:
                    lane_stream = colq // (tq // n_merge)
                    d = jnp.concatenate(
                        [jnp.dot(vt_ref[g, j],
                                 jnp.concatenate([jnp.where(lane_stream == b, p[:, hh * tq:(hh + 1) * tq], 0.0)
                                                  for b in range(n_merge)], axis=0).astype(BF16),
                                 preferred_element_type=F32) for hh in range(B_GROUP)], axis=1)
                else:
                    d = jnp.dot(vt_ref[g, j], p.astype(BF16), preferred_element_type=F32)
                pv = d if pv is None else pv + d
            acc_ref[g] = jnp.exp(m_old - m_new) * acc_ref[g] + pv
            m_ref[g] = m_new

    n_far = jnp.maximum(nk - 2, 0)

    def far_quad(jj, c):
        attend([(4 * jj + t, None) for t in range(4)])
        return c

    lax.fori_loop(0, n_far // 4, far_quad, 0)
    rem = n_far % 4

    @pl.when(rem >= 2)
    def _():
        attend([(n_far - rem, None), (n_far - rem + 1, None)])

    @pl.when(rem % 2 == 1)
    def _():
        attend([(n_far - 1, None)])

    @pl.when(nk >= 2)
    def _():
        attend([(nk - 2, 0), (nk - 1, 1)])

    @pl.when(nk < 2)
    def _():
        attend([(nk - 1, 1)])

    outs = []
    for g in range(B_KV_HEADS):
        for hh in range(B_GROUP):
            blk = acc_ref[g, :, hh * tq:(hh + 1) * tq].T
            outs.append(blk[:, 0:HEAD_DIM] / blk[:, HEAD_DIM:HEAD_DIM + 1])
    o_ref[0] = jnp.concatenate(outs, axis=-1).astype(BF16)


def _dsa_attn(qi, wi, qb, ki, kb, vb, g_row, off, valid_len, q_valid, tq):
    b, t_q = qi.shape[:2]
    tk = ki.shape[1]
    nq, n_kt = t_q // tq, tk // TILE
    assert tq == TILE or nq == 1
    assert n_kt % 2 == 0 and tk == n_kt * TILE
    qspec = lambda n: pl.BlockSpec((1, tq, n), lambda bb, i: (bb, i, 0))
    kspec = lambda n: pl.BlockSpec((1, tk, n), lambda bb, i: (bb, 0, 0))
    in_specs = [pl.BlockSpec((B_HEADS, DSA_ROLL), lambda bb, i: (0, 0)),
                qspec(512), qspec(128), qspec(512), kspec(IDX_DIM), kspec(128), kspec(128)]
    return _dsa_call((g_row, qi, wi, qb, ki, kb, vb), in_specs, (b, nq), t_q, tq, n_kt,
                     dict(off=off, valid_len=valid_len, q_valid=q_valid, n_merge=0))


def _dsa_attn_merged(qi, wi, qb, ki, kb, vb, g_row, off, valid_len):
    s, ts = qi.shape[:2]
    tk = ki.shape[1]
    tq, n_kt = s * ts, tk // TILE
    assert tq % LANES == 0 and ts <= CHUNK and n_kt % 2 == 0 and tk == n_kt * TILE
    flat = lambda a: a.reshape(tq, a.shape[-1])
    operands = (g_row, flat(qi), flat(wi), flat(qb), ki, kb, vb)
    whole = lambda a: pl.BlockSpec(a.shape, lambda bb, i: (0,) * a.ndim, pipeline_mode=pl.Buffered(1))
    out = _dsa_call(operands, [whole(a) for a in operands], (1, 1), tq, tq, n_kt,
                    dict(off=off, valid_len=valid_len, q_valid=tq, n_merge=s))
    return out.reshape(s, ts, B_WIDTH)


def _dsa_call(operands, in_specs, grid, t_q, tq, n_kt, static):
    n_merge = static["n_merge"]
    if n_merge:
        q_scratch = [pltpu.VMEM((IDX_HEADS, tq, n_merge * IDX_DIM), BF16),
                     pltpu.VMEM((B_HEADS, tq, n_merge * HEAD_DIM), BF16)]
    else:
        q_scratch = [pltpu.VMEM((IDX_HEADS * tq, IDX_DIM), BF16),
                     pltpu.VMEM((B_KV_HEADS, B_GROUP * tq, HEAD_DIM), BF16)]
    return pl.pallas_call(
        functools.partial(_dsa_kernel, n_kt=n_kt, tq=tq, **static),
        grid=grid,
        in_specs=in_specs,
        out_specs=pl.BlockSpec((1, tq, B_WIDTH), lambda bb, i: (bb, i, 0)),
        out_shape=jax.ShapeDtypeStruct((grid[0], t_q, B_WIDTH), BF16),
        scratch_shapes=[pltpu.VMEM((n_kt, TILE, tq), F32),
                        pltpu.VMEM((n_kt, TILE, tq), BF16),
                        pltpu.VMEM((1, tq), F32),
                        pltpu.VMEM((B_KV_HEADS, n_kt, 2 * HEAD_DIM, max(n_merge, 1) * TILE), BF16),
                        *q_scratch,
                        pltpu.VMEM((B_KV_HEADS, 1, B_GROUP * tq), F32),
                        pltpu.VMEM((B_KV_HEADS, 2 * HEAD_DIM, B_GROUP * tq), F32),
                        pltpu.VMEM((B_HEADS, 2 * TILE, tq), F32)],
        compiler_params=_params("arbitrary", "arbitrary"),
        name="dsa_attn",
    )(*operands)


def _mem_kv_kernel(m_ref, wk_ref, wv_ref, k_o, v_o, kb_o, vb_o):
    mb = m_ref[...].astype(BF16)
    k = jnp.dot(mb, wk_ref[...], preferred_element_type=F32)
    v = jnp.dot(mb, wv_ref[...], preferred_element_type=F32)
    k_o[...] = k
    v_o[...] = v
    kb_o[...] = k.astype(BF16)
    vb_o[...] = v.astype(BF16)


def _mem_kv(mem2d, wk, wv):
    r = mem2d.shape[0]
    tm = MEM_LEN
    row = lambda n: pl.BlockSpec((tm, n), lambda i: (i, 0))
    const = lambda s: pl.BlockSpec(s, lambda i: (0, 0))
    sds = jax.ShapeDtypeStruct
    return pl.pallas_call(
        _mem_kv_kernel,
        grid=(r // tm,),
        in_specs=[row(D_MODEL), const((D_MODEL, MEM_WIDTH)), const((D_MODEL, MEM_WIDTH))],
        out_specs=[row(MEM_WIDTH)] * 4,
        out_shape=[sds((r, MEM_WIDTH), F32), sds((r, MEM_WIDTH), F32),
                   sds((r, MEM_WIDTH), BF16), sds((r, MEM_WIDTH), BF16)],
        compiler_params=_params("arbitrary"),
        name="mem_kv",
    )(mem2d, wk, wv)


FF_CHUNK = 256


def _tail_kernel(x_ref, oa_ref, ob_ref, mk_ref, mv_ref, hist_ref,
                 wo_ref, g1_ref, b1_ref, wq_ref, wmo_ref, g2_ref, b2_ref,
                 wu_ref, wc_ref, bc_ref, wd_ref, g3_ref, b3_ref,
                 o_ref, tail_ref, carry_ref, act_ref, *, tiles_per_batch, seg):
    i = pl.program_id(0)
    tm = x_ref.shape[0]
    nseg = tm // seg

    mix = jnp.concatenate([oa_ref[...], ob_ref[...]], axis=-1)
    h = _layer_norm(ALPHA * x_ref[...] + jnp.dot(mix, wo_ref[...], preferred_element_type=F32),
                    g1_ref[...], b1_ref[...])

    q = jnp.dot(h.astype(BF16), wq_ref[...], preferred_element_type=F32).astype(BF16)
    segs = []
    for s in range(nseg):
        qs = q[s * seg:(s + 1) * seg]
        heads = []
        for hd in range(MEM_HEADS):
            sl = slice(hd * MEM_HEAD_DIM, (hd + 1) * MEM_HEAD_DIM)
            sc = lax.dot_general(qs[:, sl], mk_ref[s, :, sl], _NT, preferred_element_type=F32) * MEM_HEAD_DIM ** -0.5
            p = jnp.exp(sc - sc.max(-1, keepdims=True))
            l = p.sum(-1, keepdims=True)
            heads.append(jnp.dot(p.astype(BF16), mv_ref[s, :, sl], preferred_element_type=F32) / l)
        segs.append(jnp.concatenate(heads, axis=-1))
    att = jnp.concatenate(segs, axis=0).astype(BF16)
    h = _layer_norm(ALPHA * h + jnp.dot(att, wmo_ref[...], preferred_element_type=F32), g2_ref[...], b2_ref[...])

    hb = h.astype(BF16)
    row = lax.broadcasted_iota(I32, (tm, 1), 0)
    first = (i % tiles_per_batch) == 0
    for c in range(D_FF // FF_CHUNK):
        cs = slice(c * FF_CHUNK, (c + 1) * FF_CHUNK)
        u = jnp.dot(hb, wu_ref[:, cs], preferred_element_type=F32)
        gt = jnp.dot(hb, wu_ref[:, D_FF + c * FF_CHUNK:D_FF + (c + 1) * FF_CHUNK], preferred_element_type=F32)
        p1 = pltpu.roll(gt, 1, 0)
        p2 = pltpu.roll(gt, 2, 0)
        for s in range(nseg):
            hist = hist_ref[s, :, cs]
            if tiles_per_batch > 1:
                hist = jnp.where(first, hist, carry_ref[:, cs])
            p1 = jnp.where(row == s * seg, hist[7:8, :], p1)
            p2 = jnp.where(row == s * seg, hist[6:7, :], p2)
            p2 = jnp.where(row == s * seg + 1, hist[7:8, :], p2)
        gc = bc_ref[:, cs] + ((wc_ref[0:1, cs] * p2 + wc_ref[1:2, cs] * p1) + wc_ref[2:3, cs] * gt)
        act_ref[:, cs] = (u * jax.nn.gelu(gc)).astype(BF16)
        for s in range(nseg):
            tail_ref[s, :, cs] = gt[(s + 1) * seg - 8:(s + 1) * seg, :]
        carry_ref[:, cs] = gt[tm - 8:tm, :]
    f = jnp.dot(act_ref[...], wd_ref[...], preferred_element_type=F32)
    o_ref[...] = _layer_norm(ALPHA * h + f, g3_ref[...], b3_ref[...])


def _layer_tail(x2d, oa, ob, mk, mv, hist, weights, tm, tiles_per_batch, seg):
    r = x2d.shape[0]
    nseg = tm // seg
    n_stream = r // (tm * tiles_per_batch) * nseg
    row = lambda n: pl.BlockSpec((tm, n), lambda i: (i, 0))
    per_stream = lambda a, b: pl.BlockSpec((nseg, a, b), lambda i: (i // tiles_per_batch, 0, 0))
    const = lambda a: pl.BlockSpec(a.shape, lambda i: (0,) * a.ndim, pipeline_mode=pl.Buffered(1))
    return pl.pallas_call(
        functools.partial(_tail_kernel, tiles_per_batch=tiles_per_batch, seg=seg),
        grid=(r // tm,),
        in_specs=[row(D_MODEL), row(A_WIDTH), row(B_WIDTH), per_stream(MEM_LEN, MEM_WIDTH),
                  per_stream(MEM_LEN, MEM_WIDTH), per_stream(8, D_FF)] + [const(w) for w in weights],
        out_specs=[row(D_MODEL), per_stream(8, D_FF)],
        out_shape=[jax.ShapeDtypeStruct((r, D_MODEL), F32), jax.ShapeDtypeStruct((n_stream, 8, D_FF), F32)],
        scratch_shapes=[pltpu.VMEM((8, D_FF), F32), pltpu.VMEM((tm, D_FF), BF16)],
        compiler_params=_params("arbitrary"),
        name="layer_tail",
    )(x2d, oa, ob, mk, mv, hist, *weights)


def _pad_rows(a, n):
    return jnp.pad(a, ((0, 0), (0, n - a.shape[1])) + ((0, 0),) * (a.ndim - 2))


def _hist8(g_hist):
    return jnp.pad(g_hist, ((0, 0), (8 - g_hist.shape[1], 0), (0, 0)))


def kernel(x_prompt, x_sample, cache_a_k, cache_a_v, cache_b_k, cache_b_v, cache_b_kidx, cache_mem_k, cache_mem_v, state_ffn_conv, mem_prompt, w_in, a_rel_bias, t5_bias, w_o, ln1_g, ln1_b, w_mq, w_mk, w_mv, w_mo, ln2_g, ln2_b, w_up, w_conv, b_conv, w_down, ln3_g, ln3_b):
    bp, tp = x_prompt.shape[:2]
    bs, ts = x_sample.shape[:2]
    l = 0
    vec = lambda a: a[l].reshape(1, -1)
    w_in_p = _prep_w_in(w_in[l])
    w_o_b = w_o[l].astype(BF16)
    w_mq_b, w_mk_b, w_mv_b, w_mo_b = (w[l].astype(BF16) for w in (w_mq, w_mk, w_mv, w_mo))
    w_up_b, w_down_b = w_up[l].astype(BF16), w_down[l].astype(BF16)
    band_row = _band_bias_row(a_rel_bias[l])
    dsa_row = _dsa_bias_row(t5_bias)
    tail_w = (w_o_b, vec(ln1_g), vec(ln1_b), w_mq_b, w_mo_b, vec(ln2_g), vec(ln2_b),
              w_up_b, w_conv[l], vec(b_conv), w_down_b, vec(ln3_g), vec(ln3_b))

    tm = 512
    a_keep = min(N_PREV_CHUNKS * CHUNK, tp)
    (qa, ka, va, qb, qi, kb, vb, ki, kb_b, vb_b, ki_b, wi, ka_tail, va_tail) = _in_proj(
        x_prompt.reshape(bp * tp, D_MODEL), w_in_p, tm, tp // tm)
    r3 = lambda a: a.reshape(bp, tp, a.shape[-1])
    oa = _band_attn(r3(qa), r3(ka), r3(va), band_row, 0, tp)
    ob = _dsa_attn(r3(qi), r3(wi), r3(qb), r3(ki_b), r3(kb_b), r3(vb_b), dsa_row, 0, tp, TILE, TILE)
    mk, mv, mk_b, mv_b = _mem_kv(mem_prompt.reshape(bp * MEM_LEN, D_MODEL), w_mk_b, w_mv_b)
    xp, p_tail = _layer_tail(x_prompt.reshape(bp * tp, D_MODEL), oa.reshape(bp * tp, A_WIDTH),
                             ob.reshape(bp * tp, B_WIDTH), mk_b.reshape(bp, MEM_LEN, MEM_WIDTH),
                             mv_b.reshape(bp, MEM_LEN, MEM_WIDTH), jnp.zeros((bp, 8, D_FF), F32), tail_w,
                             tm, tp // tm, tm)
    prompt_state = (
        ka_tail.reshape(bp, tm, A_HEADS, HEAD_DIM)[:, tm - a_keep:][None],
        va_tail.reshape(bp, tm, A_HEADS, HEAD_DIM)[:, tm - a_keep:][None],
        kb.reshape(1, bp, tp, B_KV_HEADS, HEAD_DIM), vb.reshape(1, bp, tp, B_KV_HEADS, HEAD_DIM),
        ki.reshape(1, bp, tp, IDX_DIM),
        mk.reshape(1, bp, MEM_LEN, MEM_HEADS, MEM_HEAD_DIM), mv.reshape(1, bp, MEM_LEN, MEM_HEADS, MEM_HEAD_DIM),
        p_tail[:, 8 - (CONV_W - 1):][None])

    rs = bs * ts
    (qa, ka, va, qb, qi, kb, vb, ki, kb_b, vb_b, ki_b, wi, ka_new, va_new) = _in_proj(
        x_sample.reshape(rs, D_MODEL), w_in_p, rs, 1)
    s3 = lambda a: a.reshape(bs, ts, -1)
    qpad = lambda a, n: _pad_rows(s3(a), n)

    past_a = cache_a_k.shape[2]
    n_a = past_a + ts
    t_a = -(-n_a // TILE) * TILE
    seq_a = lambda cache, new: _pad_rows(
        jnp.concatenate([cache[l].reshape(bs, past_a, A_WIDTH), s3(new)], axis=1), t_a).astype(BF16)
    oa = _band_attn(qpad(qa, TILE), seq_a(cache_a_k, ka_new), seq_a(cache_a_v, va_new), band_row,
                    past_a // TILE, n_a)

    past_b = cache_b_k.shape[2]
    n_b = past_b + ts
    t_b = -(-n_b // (2 * TILE)) * 2 * TILE
    seq_b = lambda cache, new: _pad_rows(
        jnp.concatenate([cache[l].reshape(bs, past_b, -1), s3(new)], axis=1), t_b).astype(BF16)
    ob = _dsa_attn_merged(s3(qi), s3(wi), s3(qb), seq_b(cache_b_kidx, ki), seq_b(cache_b_k, kb),
                          seq_b(cache_b_v, vb), dsa_row, past_b // TILE, n_b)

    xs, s_tail = _layer_tail(x_sample.reshape(rs, D_MODEL), oa[:, :ts].reshape(rs, A_WIDTH),
                             ob[:, :ts].reshape(rs, B_WIDTH),
                             cache_mem_k[l].reshape(bs, MEM_LEN, MEM_WIDTH).astype(BF16),
                             cache_mem_v[l].reshape(bs, MEM_LEN, MEM_WIDTH).astype(BF16),
                             _hist8(state_ffn_conv[l]), tail_w, rs, 1, ts)
    sample_state = (
        ka_new.reshape(1, bs, ts, A_HEADS, HEAD_DIM), va_new.reshape(1, bs, ts, A_HEADS, HEAD_DIM),
        kb.reshape(1, bs, ts, B_KV_HEADS, HEAD_DIM), vb.reshape(1, bs, ts, B_KV_HEADS, HEAD_DIM),
        ki.reshape(1, bs, ts, IDX_DIM), s_tail[:, 8 - (CONV_W - 1):][None])

    return (xp.reshape(bp, tp, D_MODEL), xs.reshape(bs, ts, D_MODEL)) + prompt_state + sample_state
```

```python
import functools
import math

import jax
import jax.numpy as jnp
from jax import lax
from jax.experimental import pallas as pl
from jax.experimental.pallas import tpu as pltpu

F32 = jnp.float32
BF16 = jnp.bfloat16
I32 = jnp.int32
I16 = jnp.int16

D_MODEL = 1024
CHUNK = 64
N_PREV_CHUNKS = 8
HEAD_DIM = 64
A_HEADS = 8
A_WIDTH = A_HEADS * HEAD_DIM
A_MAX_REL = 64
B_HEADS = 8
B_KV_HEADS = 2
B_GROUP = B_HEADS // B_KV_HEADS
B_WIDTH = B_HEADS * HEAD_DIM
B_KV_WIDTH = B_KV_HEADS * HEAD_DIM
IDX_HEADS = 8
IDX_DIM = 64
TOPK_MAX = 256
N_BUCKETS = 32
T5_MAX_DIST = 128
MEM_LEN = 256
MEM_HEADS = 4
MEM_HEAD_DIM = 128
MEM_WIDTH = MEM_HEADS * MEM_HEAD_DIM
D_FF = 2816
CONV_W = 3
IN_SIZES = (A_WIDTH, A_WIDTH, A_WIDTH, B_WIDTH, B_KV_WIDTH, B_KV_WIDTH, IDX_HEADS * IDX_DIM, IDX_DIM, IDX_HEADS)
DEPTH = 1
ALPHA = (2 * DEPTH) ** 0.25
LN_EPS = 1e-5
ATTN_SCALE = HEAD_DIM ** -0.5
NEG = -1e30

LANES = 128
TILE = 256
BAND_TILES = 1 + (N_PREV_CHUNKS * CHUNK) // TILE
VMEM_LIMIT = 56 * 1024 * 1024

_C_QA, _C_KA, _C_VA, _C_QB = 0, 512, 1024, 1536
_C_KB, _C_VB, _C_QI, _C_KI, _C_WI = 2048, 2176, 2304, 2816, 2944
IN_PAD = 3072

MIN16 = -32768
HI_NEG_INF = -32641

_NT = (((1,), (1,)), ((), ()))


def _params(*sem):
    return pltpu.CompilerParams(dimension_semantics=sem, vmem_limit_bytes=VMEM_LIMIT)


def _layer_norm(z, g, b):
    mu = jnp.mean(z, axis=-1, keepdims=True)
    d = z - mu
    var = jnp.mean(d * d, axis=-1, keepdims=True)
    return d * lax.rsqrt(var + LN_EPS) * g + b


def _toeplitz(g_row, rows, width):
    return pltpu.roll(jnp.broadcast_to(g_row, (rows, width)), 0, 1, stride=1, stride_axis=0)


def _in_proj_kernel(x_ref, w_ref, qa_o, ka_o, va_o, qb_o, qi_o, kb_o, vb_o, ki_o, kbb_o, vbb_o, kib_o,
                    wi_o, kat_o, vat_o, *, tiles_per_batch, wi_scale):
    i = pl.program_id(0)
    xb = x_ref[...].astype(BF16)

    def mm(c0, n):
        return jnp.dot(xb, w_ref[:, c0:c0 + n], preferred_element_type=F32)

    qa_o[...] = mm(_C_QA, 512).astype(BF16)
    ka = mm(_C_KA, 512)
    va = mm(_C_VA, 512)
    ka_o[...] = ka.astype(BF16)
    va_o[...] = va.astype(BF16)
    qb_o[...] = mm(_C_QB, 512).astype(BF16)
    qi_o[...] = mm(_C_QI, 512).astype(BF16)
    kb = mm(_C_KB, 128)
    vb = mm(_C_VB, 128)
    for g in range(B_KV_HEADS):
        kb_o[:, g, :] = kb[:, g * HEAD_DIM:(g + 1) * HEAD_DIM]
        vb_o[:, g, :] = vb[:, g * HEAD_DIM:(g + 1) * HEAD_DIM]
    kbb_o[...] = kb.astype(BF16)
    vbb_o[...] = vb.astype(BF16)
    ki = mm(_C_KI, 128)[:, :IDX_DIM]
    ki_o[...] = ki
    kib_o[...] = ki.astype(BF16)
    wi_o[...] = mm(_C_WI, 128) * wi_scale

    @pl.when(i % tiles_per_batch == tiles_per_batch - 1)
    def _():
        kat_o[...] = ka
        vat_o[...] = va


def _prep_w_in(w):
    parts, off = [], 0
    for n in IN_SIZES:
        parts.append(w[:, off:off + n])
        off += n
    qa, ka, va, qb, kb, vb, qi, ki, wi = parts
    pad = lambda a, n: jnp.pad(a, ((0, 0), (0, n - a.shape[1])))
    cols = [qa * ATTN_SCALE, ka, va, qb * ATTN_SCALE, kb, vb, qi * IDX_DIM ** -0.5, pad(ki, 128), pad(wi, 128)]
    return jnp.concatenate(cols, axis=1).astype(BF16)


def _in_proj(x2d, w_pad, tm, tiles_per_batch):
    r = x2d.shape[0]
    n_tiles = r // tm
    n_batch = n_tiles // tiles_per_batch
    row = lambda n: pl.BlockSpec((tm, n), lambda i: (i, 0))
    tail = pl.BlockSpec((tm, 512), lambda i: (i // tiles_per_batch, 0))
    sds = jax.ShapeDtypeStruct
    kv_state = pl.BlockSpec((tm, B_KV_HEADS, HEAD_DIM), lambda i: (i, 0, 0))
    kv_shape = sds((r, B_KV_HEADS, HEAD_DIM), F32)
    out_shape = [sds((r, 512), BF16)] * 5 + [kv_shape, kv_shape, sds((r, IDX_DIM), F32),
                                              sds((r, 128), BF16), sds((r, 128), BF16), sds((r, IDX_DIM), BF16),
                                              sds((r, 128), F32),
                                              sds((n_batch * tm, 512), F32), sds((n_batch * tm, 512), F32)]
    out_specs = [row(512)] * 5 + [kv_state, kv_state, row(IDX_DIM), row(128), row(128), row(IDX_DIM), row(128),
                                  tail, tail]
    return pl.pallas_call(
        functools.partial(_in_proj_kernel, tiles_per_batch=tiles_per_batch, wi_scale=IDX_HEADS ** -0.5),
        grid=(n_tiles,),
        in_specs=[pl.BlockSpec((tm, D_MODEL), lambda i: (i, 0)),
                  pl.BlockSpec((D_MODEL, IN_PAD), lambda i: (0, 0))],
        out_specs=out_specs,
        out_shape=out_shape,
        compiler_params=_params("arbitrary"),
        name="in_proj",
    )(x2d, w_pad)


BAND_COLS = BAND_TILES * TILE
BAND_ROLL = BAND_COLS + TILE


def _band_bias_row(table):
    idx = jnp.arange(BAND_ROLL)
    d = jnp.where(idx < TILE, idx, idx - BAND_ROLL)
    rel = (BAND_TILES - 1) * TILE + d
    return table[jnp.clip(rel, -A_MAX_REL, A_MAX_REL) + A_MAX_REL].T.astype(F32)


def _band_kernel(g_ref, q_ref, k0, k1, k2, v0, v1, v2, o_ref, bias_ref, vt_ref, *, off, valid_len):
    i = pl.program_id(1)
    kt = i + off
    krefs, vrefs = (k0, k1, k2), (v0, v1, v2)

    @pl.when((pl.program_id(0) == 0) & (i == 0))
    def _():
        c = lax.broadcasted_iota(I32, (BAND_COLS, TILE), 0) // CHUNK
        r = lax.broadcasted_iota(I32, (BAND_COLS, TILE), 1) // CHUNK
        ok = (c >= r) & (c <= r + N_PREV_CHUNKS)
        for h in range(A_HEADS):
            bias_ref[h] = jnp.where(ok, _toeplitz(g_ref[h:h + 1, :], BAND_COLS, BAND_ROLL)[:, :TILE], NEG)
        ones = jnp.ones((HEAD_DIM, TILE), BF16)
        for s in range(BAND_TILES):
            for h in range(A_HEADS):
                vt_ref[s, h, HEAD_DIM:2 * HEAD_DIM, :] = ones

    def put(slot, vref):
        vt = vref[0].astype(F32).T
        for h in range(A_HEADS):
            vt_ref[slot, h, 0:HEAD_DIM, :] = vt[h * HEAD_DIM:(h + 1) * HEAD_DIM].astype(BF16)

    slots = [(kt + 1 + j) % BAND_TILES for j in range(BAND_TILES)]

    @pl.when(i == 0)
    def _():
        for j in range(BAND_TILES - 1):
            put(slots[j], vrefs[j])

    put(slots[BAND_TILES - 1], vrefs[BAND_TILES - 1])

    base = (kt - (BAND_TILES - 1)) * TILE

    n_slab = 4
    slab_w = n_slab * HEAD_DIM
    lane_head = lax.broadcasted_iota(I32, (TILE, slab_w), 1) // HEAD_DIM

    def attend(masked):
        outs = []
        for g in range(A_HEADS // n_slab):
            gs = slice(g * slab_w, (g + 1) * slab_w)
            q_slab = q_ref[0, :, gs].astype(F32)
            q_bd = jnp.concatenate([jnp.where(lane_head == hh, q_slab, 0.0) for hh in range(n_slab)],
                                   axis=0).astype(BF16)
            s = []
            for j in range(BAND_TILES):
                st = lax.dot_general(krefs[j][0, :, gs], q_bd, _NT, preferred_element_type=F32)
                parts = [st[:, hh * TILE:(hh + 1) * TILE] + bias_ref[g * n_slab + hh, j * TILE:(j + 1) * TILE, :]
                         for hh in range(n_slab)]
                if masked:
                    kpos = base + j * TILE + lax.broadcasted_iota(I32, (TILE, TILE), 0)
                    ok = (kpos >= 0) & (kpos < valid_len)
                    parts = [jnp.where(ok, x, NEG) for x in parts]
                s.append(jnp.concatenate(parts, axis=1))
            m = jnp.maximum(jnp.maximum(s[0], s[1]), s[2]).max(axis=0, keepdims=True)
            p = [jnp.exp(sj - m).astype(BF16) for sj in s]
            for hh in range(n_slab):
                acc = None
                for j in range(BAND_TILES):
                    d = jnp.dot(vt_ref[slots[j], g * n_slab + hh], p[j][:, hh * TILE:(hh + 1) * TILE],
                                preferred_element_type=F32)
                    acc = d if acc is None else acc + d
                blk = acc.T
                outs.append(blk[:, 0:HEAD_DIM] / blk[:, HEAD_DIM:HEAD_DIM + 1])
        o_ref[0] = jnp.concatenate(outs, axis=-1).astype(BF16)

    needs_mask = (base < 0) | (base + BAND_COLS > valid_len)

    @pl.when(needs_mask)
    def _():
        attend(True)

    @pl.when(jnp.logical_not(needs_mask))
    def _():
        attend(False)


def _band_attn(q, k, v, g_row, off, valid_len):
    b, tq = q.shape[:2]
    nq = tq // TILE
    qspec = pl.BlockSpec((1, TILE, A_WIDTH), lambda bb, i: (bb, i, 0))
    kspec = lambda d: pl.BlockSpec((1, TILE, A_WIDTH), lambda bb, i: (bb, jnp.maximum(i + off - d, 0), 0))
    return pl.pallas_call(
        functools.partial(_band_kernel, off=off, valid_len=valid_len),
        grid=(b, nq),
        in_specs=[pl.BlockSpec((A_HEADS, BAND_ROLL), lambda bb, i: (0, 0)),
                  qspec, kspec(2), kspec(1), kspec(0), kspec(2), kspec(1), kspec(0)],
        out_specs=pl.BlockSpec((1, TILE, A_WIDTH), lambda bb, i: (bb, i, 0)),
        out_shape=jax.ShapeDtypeStruct((b, tq, A_WIDTH), BF16),
        scratch_shapes=[pltpu.VMEM((A_HEADS, BAND_COLS, TILE), F32),
                        pltpu.VMEM((BAND_TILES, A_HEADS, 2 * HEAD_DIM, TILE), BF16)],
        compiler_params=_params("arbitrary", "arbitrary"),
        name="band_attn",
    )(g_row, q, k, k, k, v, v, v)


DSA_ROLL = 3 * TILE


def _t5_bucket(rel):
    half = N_BUCKETS // 2
    max_exact = half // 2
    n = jnp.abs(rel)
    log_ratio = jnp.log(jnp.maximum(n, 1).astype(jnp.float32) / max_exact) / math.log(T5_MAX_DIST / max_exact)
    large = jnp.minimum(max_exact + (log_ratio * (half - max_exact)).astype(jnp.int32), half - 1)
    return jnp.where(rel < 0, half, 0) + jnp.where(n < max_exact, n, large)


def _dsa_bias_row(t5_table):
    idx = jnp.arange(DSA_ROLL)
    d = jnp.where(idx < TILE, idx, idx - DSA_ROLL)
    far = t5_table[_t5_bucket(jnp.full((1,), 2 * TILE + 1, I32))]
    return (t5_table[_t5_bucket(TILE + d)] - far).T.astype(F32)


def _dsa_kernel(g_ref, qi_ref, wi_ref, qb_ref, ki_ref, kb_ref, vb_ref, o_ref,
                sc_ref, scb_ref, keep_ref, vt_ref, qis_ref, qbs_ref, m_ref, acc_ref, bias_ref,
                *, off, valid_len, q_valid, n_kt, tq, n_merge):
    i = pl.program_id(1)
    qt = i + off
    q0 = qt * TILE
    nk = qt + 1
    key_rows = lambda j: pl.ds(pl.multiple_of(j * TILE, TILE), TILE)

    @pl.when((pl.program_id(0) == 0) & (i == 0))
    def _():
        for h in range(B_HEADS):
            tile = _toeplitz(g_ref[h:h + 1, :], 2 * TILE, DSA_ROLL)[:, :tq]
            if n_merge:
                ts = tq // n_merge
                stream = lax.broadcasted_iota(I32, (1, tq), 1) // ts
                first = tile
                for b in range(1, n_merge):
                    tile = jnp.where(stream == b, pltpu.roll(first, b * ts, 1), tile)
            bias_ref[h] = tile

    @pl.when(i == 0)
    def _():
        def body(j, c):
            for b in range(max(n_merge, 1)):
                vt = vb_ref[b, key_rows(j), :].astype(F32).T
                for g in range(B_KV_HEADS):
                    vt_ref[g, j, 0:HEAD_DIM, b * TILE:(b + 1) * TILE] = vt[g * HEAD_DIM:(g + 1) * HEAD_DIM].astype(BF16)
            for g in range(B_KV_HEADS):
                vt_ref[g, j, HEAD_DIM:2 * HEAD_DIM, :] = jnp.ones((HEAD_DIM, vt_ref.shape[-1]), BF16)
            return c

        lax.fori_loop(0, n_kt, body, 0)

    colq = lax.broadcasted_iota(I32, (1, tq), 1)
    rowk = lax.broadcasted_iota(I32, (TILE, 1), 0)

    if n_merge:
        width = n_merge * HEAD_DIM
        own = (lax.broadcasted_iota(I32, (tq, width), 0) // (tq // n_merge)
               == lax.broadcasted_iota(I32, (tq, width), 1) // HEAD_DIM)

        def block_diag(x):
            return jnp.where(own, jnp.concatenate([x.astype(F32)] * n_merge, axis=1), 0.0).astype(BF16)

        qi, qb = qi_ref[...], qb_ref[...]
        for h in range(B_HEADS):
            qis_ref[h] = block_diag(qi[:, h * IDX_DIM:(h + 1) * IDX_DIM])
            qbs_ref[h] = block_diag(qb[:, h * HEAD_DIM:(h + 1) * HEAD_DIM])
        side_by_side = lambda ref, j, cols: jnp.concatenate(
            [ref[b, key_rows(j), cols] for b in range(n_merge)], axis=1)
        wi_t = wi_ref[...].T
        lim = jnp.full((1, tq), valid_len, I32)
    else:
        qi = qi_ref[0]
        for h in range(IDX_HEADS):
            qis_ref[h * tq:(h + 1) * tq, :] = qi[:, h * IDX_DIM:(h + 1) * IDX_DIM]
        qb = qb_ref[0]
        for g in range(B_KV_HEADS):
            for hh in range(B_GROUP):
                h = g * B_GROUP + hh
                qbs_ref[g, hh * tq:(hh + 1) * tq, :] = qb[:, h * HEAD_DIM:(h + 1) * HEAD_DIM]
        wi_t = wi_ref[0].T
        lim = jnp.minimum(q0 + (colq // CHUNK + 1) * CHUNK, valid_len)

    def score_tile(j, masked):
        if n_merge:
            kt = side_by_side(ki_ref, j, slice(None))
            lg = jnp.concatenate([lax.dot_general(kt, qis_ref[h], _NT, preferred_element_type=F32)
                                  for h in range(IDX_HEADS)], axis=1)
        else:
            kt = ki_ref[0, key_rows(j), :]
            lg = lax.dot_general(kt, qis_ref[...], _NT, preferred_element_type=F32)
        sc = wi_t[0:1, :] * jnp.maximum(lg[:, 0:tq], 0.0)
        for h in range(1, IDX_HEADS):
            sc = sc + wi_t[h:h + 1, :] * jnp.maximum(lg[:, h * tq:(h + 1) * tq], 0.0)
        if masked:
            sc = jnp.where(j * TILE + rowk < lim, sc, -jnp.inf)
        sc_ref[j] = sc
        scb_ref[j] = sc.astype(BF16)
        rows8 = lambda hit: jnp.where(hit, 1, 0).reshape(TILE // 8, 8, tq).sum(axis=0)
        return rows8(sc > 0.0), rows8(sc >= 0.0)

    def score_pair(jj, c, masked):
        pos_a, nn_a = score_tile(2 * jj, masked)
        pos_b, nn_b = score_tile(2 * jj + 1, masked)
        return c[0] + pos_a + pos_b, c[1] + nn_a + nn_b

    n_open_pairs = (nk - 1) // 2
    counts = lax.fori_loop(0, n_open_pairs, functools.partial(score_pair, masked=False),
                           (jnp.zeros((8, tq), I32), jnp.zeros((8, tq), I32)))
    n_pos, n_nonneg = lax.fori_loop(n_open_pairs, (nk + 1) // 2, functools.partial(score_pair, masked=True), counts)

    def f32_of_key(k):
        return pltpu.bitcast(jnp.where(k < 0, k ^ 0x7FFFFFFF, k), F32)

    def bf16_of_key(k):
        bits = jnp.where(k < 0, k ^ 0x7FFF, k) & 0xFFFF
        return pltpu.bitcast(lax.shift_left(bits, 16), F32).astype(BF16)

    def count_bf16(cand):
        def body(jj, acc):
            for j in (2 * jj, 2 * jj + 1):
                ge = jnp.where(scb_ref[j] >= cand, jnp.int16(1), jnp.int16(0)).reshape(TILE // 16, 16, tq)
                part = ge[0]
                for r in range(1, TILE // 16):
                    part = part + ge[r]
                acc = acc + part
            return acc

        acc = lax.fori_loop(0, (nk + 1) // 2, body, jnp.zeros((16, tq), I16))
        return acc.astype(I32).sum(axis=0, keepdims=True)

    def count_f32(cand, strict=False):
        def body(j, acc):
            blk = sc_ref[j]
            hit = (blk > cand) if strict else (blk >= cand)
            return acc + jnp.where(hit, 1, 0).reshape(TILE // 8, 8, tq).sum(axis=0)

        return lax.fori_loop(0, nk, body, jnp.zeros((8, tq), I32)).sum(axis=0, keepdims=True)

    c_pos = n_pos.sum(axis=0, keepdims=True)
    zero_tie = (c_pos < TOPK_MAX) & (n_nonneg.sum(axis=0, keepdims=True) >= TOPK_MAX)
    skip1 = (colq >= q_valid) | zero_tie

    def level1(it, t):
        cand = t + lax.shift_left(jnp.int32(1), 15 - it)
        return jnp.where(count_bf16(bf16_of_key(cand)) >= TOPK_MAX, cand, t)

    t1 = lax.fori_loop(0, 16, level1, jnp.full((1, tq), MIN16, I32))
    settled1 = skip1 | (t1 <= HI_NEG_INF)

    def level2(st):
        lo, hi, thr_key, done = st
        live = (done == 0) & (hi - lo > 1)
        mid = lo + lax.shift_right_arithmetic(hi - lo, 1)
        c = count_f32(f32_of_key(mid))
        hit = live & (c == TOPK_MAX)
        return (jnp.where(live & (c >= TOPK_MAX), mid, lo), jnp.where(live & (c < TOPK_MAX), mid, hi),
                jnp.where(hit, mid, thr_key), jnp.where(hit, 1, done))

    def n_live(st):
        lo, hi, _, done = st
        return jnp.sum(jnp.where((done == 0) & (hi - lo > 1), 1, 0))

    def level2_pair(carry):
        st = level2(level2(carry[0]))
        return st, n_live(st)

    def key32_of_key16(k):
        return lax.shift_left(k, 16) | jnp.where(k < 0, 0xFFFF, 0)

    key_t1 = key32_of_key16(t1)
    st0 = (key_t1 - 0x8000, key32_of_key16(t1 + 1), key_t1, jnp.where(settled1, 1, 0))
    (lo, _, thr_key, done2), _ = lax.while_loop(lambda carry: carry[1] > 0, level2_pair, (st0, n_live(st0)))
    open2 = done2 == 0
    thr = jnp.where(open2, f32_of_key(lo), f32_of_key(thr_key))
    thr = jnp.where(zero_tie, 0.0, thr)
    thr = jnp.where(settled1 & jnp.logical_not(zero_tie), -jnp.inf, thr)
    thr = jnp.maximum(thr, float(jnp.finfo(F32).min))

    keep_ref[...] = jnp.where(zero_tie, TOPK_MAX - c_pos, 2 ** 30).astype(F32)

    @pl.when(jnp.sum(jnp.where(open2, 1, 0)) > 0)
    def _():
        above = count_f32(thr, strict=True)
        keep_ref[...] = jnp.where(open2, (TOPK_MAX - above).astype(F32), keep_ref[...])

    @pl.when(jnp.sum(jnp.where(open2 | zero_tie, 1, 0)) > 0)
    def _():
        keep = keep_ref[...]
        lower = (lax.broadcasted_iota(I32, (TILE, TILE), 0) > lax.broadcasted_iota(I32, (TILE, TILE), 1))
        lower = jnp.where(lower, 1.0, 0.0).astype(BF16)

        def body(jj, run):
            for j in (2 * jj, 2 * jj + 1):
                blk = sc_ref[j]
                eq = blk == thr
                eq_f = jnp.where(eq, 1.0, 0.0)
                before = jnp.dot(lower, eq_f.astype(BF16), preferred_element_type=F32)
                sc_ref[j] = jnp.where(eq & (run + before >= keep), -jnp.inf, blk)
                run = run + eq_f.reshape(TILE // 8, 8, tq).sum(axis=0).sum(axis=0, keepdims=True)
            return run

        lax.fori_loop(0, (nk + 1) // 2, body, jnp.zeros((1, tq), F32))

    for g in range(B_KV_HEADS):
        m_ref[g] = jnp.full((1, B_GROUP * tq), NEG, F32)
        acc_ref[g] = jnp.zeros((2 * HEAD_DIM, B_GROUP * tq), F32)

    def attend(tiles):
        sels = [sc_ref[j] >= thr for j, _ in tiles]
        masked_scores = []
        for g in range(B_KV_HEADS):
            ss = []
            for (j, near), sel in zip(tiles, sels):
                if n_merge:
                    kt = side_by_side(kb_ref, j, slice(g * HEAD_DIM, (g + 1) * HEAD_DIM))
                    st = jnp.concatenate([lax.dot_general(kt, qbs_ref[g * B_GROUP + hh], _NT,
                                                          preferred_element_type=F32)
                                          for hh in range(B_GROUP)], axis=1)
                else:
                    kt = kb_ref[0, key_rows(j), g * HEAD_DIM:(g + 1) * HEAD_DIM]
                    st = lax.dot_general(kt, qbs_ref[g], _NT, preferred_element_type=F32)
                parts = []
                for hh in range(B_GROUP):
                    s_h = st[:, hh * tq:(hh + 1) * tq]
                    if near is not None:
                        s_h = s_h + bias_ref[g * B_GROUP + hh, near * TILE:(near + 1) * TILE, :]
                    parts.append(jnp.where(sel, s_h, -jnp.inf))
                ss.append(jnp.concatenate(parts, axis=1))
            masked_scores.append(ss)
        for g, ss in enumerate(masked_scores):
            m_old = m_ref[g]
            m_new = m_old
            for s in ss:
                m_new = jnp.maximum(m_new, s.max(axis=0, keepdims=True))
            pv = None
            for (j, _), s in zip(tiles, ss):
                p = jnp.exp(s - m_new)
                if n_merge:
                    lane_stream = colq // (tq // n_merge)
                    d = jnp.concatenate(
                        [jnp.dot(vt_ref[g, j],
                                 jnp.concatenate([jnp.where(lane_stream == b, p[:, hh * tq:(hh + 1) * tq], 0.0)
                                                  for b in range(n_merge)], axis=0).astype(BF16),
                                 preferred_element_type=F32) for hh in range(B_GROUP)], axis=1)
                else:
                    d = jnp.dot(vt_ref[g, j], p.astype(BF16), preferred_element_type=F32)
                pv = d if pv is None else pv + d
            acc_ref[g] = jnp.exp(m_old - m_new) * acc_ref[g] + pv
            m_ref[g] = m_new

    n_far = jnp.maximum(nk - 2, 0)

    def far_quad(jj, c):
        attend([(4 * jj + t, None) for t in range(4)])
        return c

    lax.fori_loop(0, n_far // 4, far_quad, 0)
    rem = n_far % 4

    @pl.when(rem >= 2)
    def _():
        attend([(n_far - rem, None), (n_far - rem + 1, None)])

    @pl.when(rem % 2 == 1)
    def _():
        attend([(n_far - 1, None)])

    @pl.when(nk >= 2)
    def _():
        attend([(nk - 2, 0), (nk - 1, 1)])

    @pl.when(nk < 2)
    def _():
        attend([(nk - 1, 1)])

    outs = []
    for g in range(B_KV_HEADS):
        for hh in range(B_GROUP):
            blk = acc_ref[g, :, hh * tq:(hh + 1) * tq].T
            outs.append(blk[:, 0:HEAD_DIM] / blk[:, HEAD_DIM:HEAD_DIM + 1])
    o_ref[0] = jnp.concatenate(outs, axis=-1).astype(BF16)


def _dsa_attn(qi, wi, qb, ki, kb, vb, g_row, off, valid_len, q_valid, tq):
    b, t_q = qi.shape[:2]
    tk = ki.shape[1]
    nq, n_kt = t_q // tq, tk // TILE
    assert tq == TILE or nq == 1
    assert n_kt % 2 == 0 and tk == n_kt * TILE
    qspec = lambda n: pl.BlockSpec((1, tq, n), lambda bb, i: (bb, i, 0))
    kspec = lambda n: pl.BlockSpec((1, tk, n), lambda bb, i: (bb, 0, 0))
    in_specs = [pl.BlockSpec((B_HEADS, DSA_ROLL), lambda bb, i: (0, 0)),
                qspec(512), qspec(128), qspec(512), kspec(IDX_DIM), kspec(128), kspec(128)]
    return _dsa_call((g_row, qi, wi, qb, ki, kb, vb), in_specs, (b, nq), t_q, tq, n_kt,
                     dict(off=off, valid_len=valid_len, q_valid=q_valid, n_merge=0))


def _dsa_attn_merged(qi, wi, qb, ki, kb, vb, g_row, off, valid_len):
    s, ts = qi.shape[:2]
    tk = ki.shape[1]
    tq, n_kt = s * ts, tk // TILE
    assert tq % LANES == 0 and ts <= CHUNK and n_kt % 2 == 0 and tk == n_kt * TILE
    flat = lambda a: a.reshape(tq, a.shape[-1])
    operands = (g_row, flat(qi), flat(wi), flat(qb), ki, kb, vb)
    whole = lambda a: pl.BlockSpec(a.shape, lambda bb, i: (0,) * a.ndim, pipeline_mode=pl.Buffered(1))
    out = _dsa_call(operands, [whole(a) for a in operands], (1, 1), tq, tq, n_kt,
                    dict(off=off, valid_len=valid_len, q_valid=tq, n_merge=s))
    return out.reshape(s, ts, B_WIDTH)


def _dsa_call(operands, in_specs, grid, t_q, tq, n_kt, static):
    n_merge = static["n_merge"]
    assert static["valid_len"] >= (static["off"] + grid[1] - 1) * TILE
    if n_merge:
        q_scratch = [pltpu.VMEM((IDX_HEADS, tq, n_merge * IDX_DIM), BF16),
                     pltpu.VMEM((B_HEADS, tq, n_merge * HEAD_DIM), BF16)]
    else:
        q_scratch = [pltpu.VMEM((IDX_HEADS * tq, IDX_DIM), BF16),
                     pltpu.VMEM((B_KV_HEADS, B_GROUP * tq, HEAD_DIM), BF16)]
    return pl.pallas_call(
        functools.partial(_dsa_kernel, n_kt=n_kt, tq=tq, **static),
        grid=grid,
        in_specs=in_specs,
        out_specs=pl.BlockSpec((1, tq, B_WIDTH), lambda bb, i: (bb, i, 0)),
        out_shape=jax.ShapeDtypeStruct((grid[0], t_q, B_WIDTH), BF16),
        scratch_shapes=[pltpu.VMEM((n_kt, TILE, tq), F32),
                        pltpu.VMEM((n_kt, TILE, tq), BF16),
                        pltpu.VMEM((1, tq), F32),
                        pltpu.VMEM((B_KV_HEADS, n_kt, 2 * HEAD_DIM, max(n_merge, 1) * TILE), BF16),
                        *q_scratch,
                        pltpu.VMEM((B_KV_HEADS, 1, B_GROUP * tq), F32),
                        pltpu.VMEM((B_KV_HEADS, 2 * HEAD_DIM, B_GROUP * tq), F32),
                        pltpu.VMEM((B_HEADS, 2 * TILE, tq), F32)],
        compiler_params=_params("arbitrary", "arbitrary"),
        name="dsa_attn",
    )(*operands)


def _mem_kv_kernel(m_ref, wk_ref, wv_ref, k_o, v_o, kb_o, vb_o):
    mb = m_ref[...].astype(BF16)
    k = jnp.dot(mb, wk_ref[...], preferred_element_type=F32)
    v = jnp.dot(mb, wv_ref[...], preferred_element_type=F32)
    k_o[...] = k
    v_o[...] = v
    kb_o[...] = k.astype(BF16)
    vb_o[...] = v.astype(BF16)


def _mem_kv(mem2d, wk, wv):
    r = mem2d.shape[0]
    tm = MEM_LEN
    row = lambda n: pl.BlockSpec((tm, n), lambda i: (i, 0))
    const = lambda s: pl.BlockSpec(s, lambda i: (0, 0))
    sds = jax.ShapeDtypeStruct
    return pl.pallas_call(
        _mem_kv_kernel,
        grid=(r // tm,),
        in_specs=[row(D_MODEL), const((D_MODEL, MEM_WIDTH)), const((D_MODEL, MEM_WIDTH))],
        out_specs=[row(MEM_WIDTH)] * 4,
        out_shape=[sds((r, MEM_WIDTH), F32), sds((r, MEM_WIDTH), F32),
                   sds((r, MEM_WIDTH), BF16), sds((r, MEM_WIDTH), BF16)],
        compiler_params=_params("arbitrary"),
        name="mem_kv",
    )(mem2d, wk, wv)


FF_CHUNK = 256


def _tail_kernel(x_ref, oa_ref, ob_ref, mk_ref, mv_ref, hist_ref,
                 wo_ref, g1_ref, b1_ref, wq_ref, wmo_ref, g2_ref, b2_ref,
                 wu_ref, wc_ref, bc_ref, wd_ref, g3_ref, b3_ref,
                 o_ref, tail_ref, carry_ref, act_ref, *, tiles_per_batch, seg):
    i = pl.program_id(0)
    tm = x_ref.shape[0]
    nseg = tm // seg

    mix = jnp.concatenate([oa_ref[...], ob_ref[...]], axis=-1)
    h = _layer_norm(ALPHA * x_ref[...] + jnp.dot(mix, wo_ref[...], preferred_element_type=F32),
                    g1_ref[...], b1_ref[...])

    q = jnp.dot(h.astype(BF16), wq_ref[...], preferred_element_type=F32).astype(BF16)
    segs = []
    for s in range(nseg):
        qs = q[s * seg:(s + 1) * seg]
        heads = []
        for hd in range(MEM_HEADS):
            sl = slice(hd * MEM_HEAD_DIM, (hd + 1) * MEM_HEAD_DIM)
            sc = lax.dot_general(qs[:, sl], mk_ref[s, :, sl], _NT, preferred_element_type=F32) * MEM_HEAD_DIM ** -0.5
            p = jnp.exp(sc - sc.max(-1, keepdims=True))
            l = p.sum(-1, keepdims=True)
            heads.append(jnp.dot(p.astype(BF16), mv_ref[s, :, sl], preferred_element_type=F32) / l)
        segs.append(jnp.concatenate(heads, axis=-1))
    att = jnp.concatenate(segs, axis=0).astype(BF16)
    h = _layer_norm(ALPHA * h + jnp.dot(att, wmo_ref[...], preferred_element_type=F32), g2_ref[...], b2_ref[...])

    hb = h.astype(BF16)
    row = lax.broadcasted_iota(I32, (tm, 1), 0)
    first = (i % tiles_per_batch) == 0
    for c in range(D_FF // FF_CHUNK):
        cs = slice(c * FF_CHUNK, (c + 1) * FF_CHUNK)
        u = jnp.dot(hb, wu_ref[:, cs], preferred_element_type=F32)
        gt = jnp.dot(hb, wu_ref[:, D_FF + c * FF_CHUNK:D_FF + (c + 1) * FF_CHUNK], preferred_element_type=F32)
        p1 = pltpu.roll(gt, 1, 0)
        p2 = pltpu.roll(gt, 2, 0)
        for s in range(nseg):
            hist = hist_ref[s, :, cs]
            if tiles_per_batch > 1:
                hist = jnp.where(first, hist, carry_ref[:, cs])
            p1 = jnp.where(row == s * seg, hist[7:8, :], p1)
            p2 = jnp.where(row == s * seg, hist[6:7, :], p2)
            p2 = jnp.where(row == s * seg + 1, hist[7:8, :], p2)
        gc = bc_ref[:, cs] + ((wc_ref[0:1, cs] * p2 + wc_ref[1:2, cs] * p1) + wc_ref[2:3, cs] * gt)
        act_ref[:, cs] = (u * jax.nn.gelu(gc)).astype(BF16)
        for s in range(nseg):
            tail_ref[s, :, cs] = gt[(s + 1) * seg - 8:(s + 1) * seg, :]
        carry_ref[:, cs] = gt[tm - 8:tm, :]
    f = jnp.dot(act_ref[...], wd_ref[...], preferred_element_type=F32)
    o_ref[...] = _layer_norm(ALPHA * h + f, g3_ref[...], b3_ref[...])


def _layer_tail(x2d, oa, ob, mk, mv, hist, weights, tm, tiles_per_batch, seg):
    r = x2d.shape[0]
    nseg = tm // seg
    n_stream = r // (tm * tiles_per_batch) * nseg
    row = lambda n: pl.BlockSpec((tm, n), lambda i: (i, 0))
    per_stream = lambda a, b: pl.BlockSpec((nseg, a, b), lambda i: (i // tiles_per_batch, 0, 0))
    const = lambda a: pl.BlockSpec(a.shape, lambda i: (0,) * a.ndim, pipeline_mode=pl.Buffered(1))
    return pl.pallas_call(
        functools.partial(_tail_kernel, tiles_per_batch=tiles_per_batch, seg=seg),
        grid=(r // tm,),
        in_specs=[row(D_MODEL), row(A_WIDTH), row(B_WIDTH), per_stream(MEM_LEN, MEM_WIDTH),
                  per_stream(MEM_LEN, MEM_WIDTH), per_stream(8, D_FF)] + [const(w) for w in weights],
        out_specs=[row(D_MODEL), per_stream(8, D_FF)],
        out_shape=[jax.ShapeDtypeStruct((r, D_MODEL), F32), jax.ShapeDtypeStruct((n_stream, 8, D_FF), F32)],
        scratch_shapes=[pltpu.VMEM((8, D_FF), F32), pltpu.VMEM((tm, D_FF), BF16)],
        compiler_params=_params("arbitrary"),
        name="layer_tail",
    )(x2d, oa, ob, mk, mv, hist, *weights)


def _pad_rows(a, n):
    return jnp.pad(a, ((0, 0), (0, n - a.shape[1])) + ((0, 0),) * (a.ndim - 2))


def _hist8(g_hist):
    return jnp.pad(g_hist, ((0, 0), (8 - g_hist.shape[1], 0), (0, 0)))


def kernel(x_prompt, x_sample, cache_a_k, cache_a_v, cache_b_k, cache_b_v, cache_b_kidx, cache_mem_k, cache_mem_v, state_ffn_conv, mem_prompt, w_in, a_rel_bias, t5_bias, w_o, ln1_g, ln1_b, w_mq, w_mk, w_mv, w_mo, ln2_g, ln2_b, w_up, w_conv, b_conv, w_down, ln3_g, ln3_b):
    bp, tp = x_prompt.shape[:2]
    bs, ts = x_sample.shape[:2]
    l = 0
    vec = lambda a: a[l].reshape(1, -1)
    w_in_p = _prep_w_in(w_in[l])
    w_o_b = w_o[l].astype(BF16)
    w_mq_b, w_mk_b, w_mv_b, w_mo_b = (w[l].astype(BF16) for w in (w_mq, w_mk, w_mv, w_mo))
    w_up_b, w_down_b = w_up[l].astype(BF16), w_down[l].astype(BF16)
    band_row = _band_bias_row(a_rel_bias[l])
    dsa_row = _dsa_bias_row(t5_bias)
    tail_w = (w_o_b, vec(ln1_g), vec(ln1_b), w_mq_b, w_mo_b, vec(ln2_g), vec(ln2_b),
              w_up_b, w_conv[l], vec(b_conv), w_down_b, vec(ln3_g), vec(ln3_b))

    tm = 512
    a_keep = min(N_PREV_CHUNKS * CHUNK, tp)
    (qa, ka, va, qb, qi, kb, vb, ki, kb_b, vb_b, ki_b, wi, ka_tail, va_tail) = _in_proj(
        x_prompt.reshape(bp * tp, D_MODEL), w_in_p, tm, tp // tm)
    r3 = lambda a: a.reshape(bp, tp, a.shape[-1])
    oa = _band_attn(r3(qa), r3(ka), r3(va), band_row, 0, tp)
    ob = _dsa_attn(r3(qi), r3(wi), r3(qb), r3(ki_b), r3(kb_b), r3(vb_b), dsa_row, 0, tp, TILE, TILE)
    mk, mv, mk_b, mv_b = _mem_kv(mem_prompt.reshape(bp * MEM_LEN, D_MODEL), w_mk_b, w_mv_b)
    xp, p_tail = _layer_tail(x_prompt.reshape(bp * tp, D_MODEL), oa.reshape(bp * tp, A_WIDTH),
                             ob.reshape(bp * tp, B_WIDTH), mk_b.reshape(bp, MEM_LEN, MEM_WIDTH),
                             mv_b.reshape(bp, MEM_LEN, MEM_WIDTH), jnp.zeros((bp, 8, D_FF), F32), tail_w,
                             tm, tp // tm, tm)
    prompt_state = (
        ka_tail.reshape(bp, tm, A_HEADS, HEAD_DIM)[:, tm - a_keep:][None],
        va_tail.reshape(bp, tm, A_HEADS, HEAD_DIM)[:, tm - a_keep:][None],
        kb.reshape(1, bp, tp, B_KV_HEADS, HEAD_DIM), vb.reshape(1, bp, tp, B_KV_HEADS, HEAD_DIM),
        ki.reshape(1, bp, tp, IDX_DIM),
        mk.reshape(1, bp, MEM_LEN, MEM_HEADS, MEM_HEAD_DIM), mv.reshape(1, bp, MEM_LEN, MEM_HEADS, MEM_HEAD_DIM),
        p_tail[:, 8 - (CONV_W - 1):][None])

    rs = bs * ts
    (qa, ka, va, qb, qi, kb, vb, ki, kb_b, vb_b, ki_b, wi, ka_new, va_new) = _in_proj(
        x_sample.reshape(rs, D_MODEL), w_in_p, rs, 1)
    s3 = lambda a: a.reshape(bs, ts, -1)
    qpad = lambda a, n: _pad_rows(s3(a), n)

    past_a = cache_a_k.shape[2]
    n_a = past_a + ts
    t_a = -(-n_a // TILE) * TILE
    seq_a = lambda cache, new: _pad_rows(
        jnp.concatenate([cache[l].reshape(bs, past_a, A_WIDTH), s3(new)], axis=1), t_a).astype(BF16)
    oa = _band_attn(qpad(qa, TILE), seq_a(cache_a_k, ka_new), seq_a(cache_a_v, va_new), band_row,
                    past_a // TILE, n_a)

    past_b = cache_b_k.shape[2]
    n_b = past_b + ts
    t_b = -(-n_b // (2 * TILE)) * 2 * TILE
    seq_b = lambda cache, new: _pad_rows(
        jnp.concatenate([cache[l].reshape(bs, past_b, -1), s3(new)], axis=1), t_b).astype(BF16)
    ob = _dsa_attn_merged(s3(qi), s3(wi), s3(qb), seq_b(cache_b_kidx, ki), seq_b(cache_b_k, kb),
                          seq_b(cache_b_v, vb), dsa_row, past_b // TILE, n_b)

    xs, s_tail = _layer_tail(x_sample.reshape(rs, D_MODEL), oa[:, :ts].reshape(rs, A_WIDTH),
                             ob[:, :ts].reshape(rs, B_WIDTH),
                             cache_mem_k[l].reshape(bs, MEM_LEN, MEM_WIDTH).astype(BF16),
                             cache_mem_v[l].reshape(bs, MEM_LEN, MEM_WIDTH).astype(BF16),
                             _hist8(state_ffn_conv[l]), tail_w, rs, 1, ts)
    sample_state = (
        ka_new.reshape(1, bs, ts, A_HEADS, HEAD_DIM), va_new.reshape(1, bs, ts, A_HEADS, HEAD_DIM),
        kb.reshape(1, bs, ts, B_KV_HEADS, HEAD_DIM), vb.reshape(1, bs, ts, B_KV_HEADS, HEAD_DIM),
        ki.reshape(1, bs, ts, IDX_DIM), s_tail[:, 8 - (CONV_W - 1):][None])

    return (xp.reshape(bp, tp, D_MODEL), xs.reshape(bs, ts, D_MODEL)) + prompt_state + sample_state
```

```python
import functools
import math

import jax
import jax.numpy as jnp
from jax import lax
from jax.experimental import pallas as pl
from jax.experimental.pallas import tpu as pltpu

F32 = jnp.float32
BF16 = jnp.bfloat16
I32 = jnp.int32
I16 = jnp.int16

D_MODEL = 1024
CHUNK = 64
N_PREV_CHUNKS = 8
HEAD_DIM = 64
A_HEADS = 8
A_WIDTH = A_HEADS * HEAD_DIM
A_MAX_REL = 64
B_HEADS = 8
B_KV_HEADS = 2
B_GROUP = B_HEADS // B_KV_HEADS
B_WIDTH = B_HEADS * HEAD_DIM
B_KV_WIDTH = B_KV_HEADS * HEAD_DIM
IDX_HEADS = 8
IDX_DIM = 64
TOPK_MAX = 256
N_BUCKETS = 32
T5_MAX_DIST = 128
MEM_LEN = 256
MEM_HEADS = 4
MEM_HEAD_DIM = 128
MEM_WIDTH = MEM_HEADS * MEM_HEAD_DIM
D_FF = 2816
CONV_W = 3
IN_SIZES = (A_WIDTH, A_WIDTH, A_WIDTH, B_WIDTH, B_KV_WIDTH, B_KV_WIDTH, IDX_HEADS * IDX_DIM, IDX_DIM, IDX_HEADS)
DEPTH = 1
ALPHA = (2 * DEPTH) ** 0.25
LN_EPS = 1e-5
ATTN_SCALE = HEAD_DIM ** -0.5
NEG = -1e30

LANES = 128
TILE = 256
BAND_TILES = 1 + (N_PREV_CHUNKS * CHUNK) // TILE
VMEM_LIMIT = 56 * 1024 * 1024

_C_QA, _C_KA, _C_VA, _C_QB = 0, 512, 1024, 1536
_C_KB, _C_VB, _C_QI, _C_KI, _C_WI = 2048, 2176, 2304, 2816, 2944
IN_PAD = 3072

MIN16 = -32768
HI_NEG_INF = -32641

_NT = (((1,), (1,)), ((), ()))


def _params(*sem):
    return pltpu.CompilerParams(dimension_semantics=sem, vmem_limit_bytes=VMEM_LIMIT)


def _layer_norm(z, g, b):
    mu = jnp.mean(z, axis=-1, keepdims=True)
    d = z - mu
    var = jnp.mean(d * d, axis=-1, keepdims=True)
    return d * lax.rsqrt(var + LN_EPS) * g + b


def _toeplitz(g_row, rows, width):
    return pltpu.roll(jnp.broadcast_to(g_row, (rows, width)), 0, 1, stride=1, stride_axis=0)


def _in_proj_kernel(x_ref, w_ref, qa_o, ka_o, va_o, qb_o, qi_o, kb_o, vb_o, ki_o, kbb_o, vbb_o, kib_o,
                    wi_o, kat_o, vat_o, *, tiles_per_batch, wi_scale):
    i = pl.program_id(0)
    xb = x_ref[...].astype(BF16)

    def mm(c0, n):
        return jnp.dot(xb, w_ref[:, c0:c0 + n], preferred_element_type=F32)

    qa_o[...] = mm(_C_QA, 512).astype(BF16)
    ka = mm(_C_KA, 512)
    va = mm(_C_VA, 512)
    ka_o[...] = ka.astype(BF16)
    va_o[...] = va.astype(BF16)
    qb_o[...] = mm(_C_QB, 512).astype(BF16)
    qi_o[...] = mm(_C_QI, 512).astype(BF16)
    kb = mm(_C_KB, 128)
    vb = mm(_C_VB, 128)
    for g in range(B_KV_HEADS):
        kb_o[:, g, :] = kb[:, g * HEAD_DIM:(g + 1) * HEAD_DIM]
        vb_o[:, g, :] = vb[:, g * HEAD_DIM:(g + 1) * HEAD_DIM]
    kbb_o[...] = kb.astype(BF16)
    vbb_o[...] = vb.astype(BF16)
    ki = mm(_C_KI, 128)[:, :IDX_DIM]
    ki_o[...] = ki
    kib_o[...] = ki.astype(BF16)
    wi_o[...] = mm(_C_WI, 128) * wi_scale

    @pl.when(i % tiles_per_batch == tiles_per_batch - 1)
    def _():
        kat_o[...] = ka
        vat_o[...] = va


def _prep_w_in(w):
    parts, off = [], 0
    for n in IN_SIZES:
        parts.append(w[:, off:off + n])
        off += n
    qa, ka, va, qb, kb, vb, qi, ki, wi = parts
    pad = lambda a, n: jnp.pad(a, ((0, 0), (0, n - a.shape[1])))
    cols = [qa * ATTN_SCALE, ka, va, qb * ATTN_SCALE, kb, vb, qi * IDX_DIM ** -0.5, pad(ki, 128), pad(wi, 128)]
    return jnp.concatenate(cols, axis=1).astype(BF16)


def _in_proj(x2d, w_pad, tm, tiles_per_batch):
    r = x2d.shape[0]
    n_tiles = r // tm
    n_batch = n_tiles // tiles_per_batch
    row = lambda n: pl.BlockSpec((tm, n), lambda i: (i, 0))
    tail = pl.BlockSpec((tm, 512), lambda i: (i // tiles_per_batch, 0))
    sds = jax.ShapeDtypeStruct
    kv_state = pl.BlockSpec((tm, B_KV_HEADS, HEAD_DIM), lambda i: (i, 0, 0))
    kv_shape = sds((r, B_KV_HEADS, HEAD_DIM), F32)
    out_shape = [sds((r, 512), BF16)] * 5 + [kv_shape, kv_shape, sds((r, IDX_DIM), F32),
                                              sds((r, 128), BF16), sds((r, 128), BF16), sds((r, IDX_DIM), BF16),
                                              sds((r, 128), F32),
                                              sds((n_batch * tm, 512), F32), sds((n_batch * tm, 512), F32)]
    out_specs = [row(512)] * 5 + [kv_state, kv_state, row(IDX_DIM), row(128), row(128), row(IDX_DIM), row(128),
                                  tail, tail]
    return pl.pallas_call(
        functools.partial(_in_proj_kernel, tiles_per_batch=tiles_per_batch, wi_scale=IDX_HEADS ** -0.5),
        grid=(n_tiles,),
        in_specs=[pl.BlockSpec((tm, D_MODEL), lambda i: (i, 0)),
                  pl.BlockSpec((D_MODEL, IN_PAD), lambda i: (0, 0))],
        out_specs=out_specs,
        out_shape=out_shape,
        compiler_params=_params("arbitrary"),
        name="in_proj",
    )(x2d, w_pad)


BAND_COLS = BAND_TILES * TILE
BAND_ROLL = BAND_COLS + TILE
BAND_SLAB = 4


def _band_bias_row(table):
    idx = jnp.arange(BAND_ROLL)
    d = jnp.where(idx < TILE, idx, idx - BAND_ROLL)
    rel = (BAND_TILES - 1) * TILE + d
    return table[jnp.clip(rel, -A_MAX_REL, A_MAX_REL) + A_MAX_REL].T.astype(F32)


def _band_kernel(g_ref, q_ref, k0, k1, k2, v0, v1, v2, o_ref, bias_ref, vt_ref, s_ref, *, off, valid_len):
    i = pl.program_id(1)
    kt = i + off
    krefs, vrefs = (k0, k1, k2), (v0, v1, v2)

    @pl.when((pl.program_id(0) == 0) & (i == 0))
    def _():
        c = lax.broadcasted_iota(I32, (BAND_COLS, TILE), 0) // CHUNK
        r = lax.broadcasted_iota(I32, (BAND_COLS, TILE), 1) // CHUNK
        ok = (c >= r) & (c <= r + N_PREV_CHUNKS)
        for h in range(A_HEADS):
            bias_ref[h] = jnp.where(ok, _toeplitz(g_ref[h:h + 1, :], BAND_COLS, BAND_ROLL)[:, :TILE], NEG)
        ones = jnp.ones((HEAD_DIM, TILE), BF16)
        for s in range(BAND_TILES):
            for h in range(A_HEADS):
                vt_ref[s, h, HEAD_DIM:2 * HEAD_DIM, :] = ones

    def put(slot, vref):
        vt = vref[0].astype(F32).T
        for h in range(A_HEADS):
            vt_ref[slot, h, 0:HEAD_DIM, :] = vt[h * HEAD_DIM:(h + 1) * HEAD_DIM].astype(BF16)

    slots = [(kt + 1 + j) % BAND_TILES for j in range(BAND_TILES)]

    @pl.when(i == 0)
    def _():
        for j in range(BAND_TILES - 1):
            put(slots[j], vrefs[j])

    put(slots[BAND_TILES - 1], vrefs[BAND_TILES - 1])

    base = (kt - (BAND_TILES - 1)) * TILE

    n_slab = BAND_SLAB
    slab_w = n_slab * HEAD_DIM
    lane_head = lax.broadcasted_iota(I32, (TILE, slab_w), 1) // HEAD_DIM

    def attend(masked):
        outs = []
        for g in range(A_HEADS // n_slab):
            gs = slice(g * slab_w, (g + 1) * slab_w)
            q_slab = q_ref[0, :, gs].astype(F32)
            q_bd = jnp.concatenate([jnp.where(lane_head == hh, q_slab, 0.0) for hh in range(n_slab)],
                                   axis=0).astype(BF16)
            pm = None
            for j in range(BAND_TILES):
                st = lax.dot_general(krefs[j][0, :, gs], q_bd, _NT, preferred_element_type=F32)
                parts = [st[:, hh * TILE:(hh + 1) * TILE] + bias_ref[g * n_slab + hh, j * TILE:(j + 1) * TILE, :]
                         for hh in range(n_slab)]
                if masked:
                    kpos = base + j * TILE + lax.broadcasted_iota(I32, (TILE, TILE), 0)
                    ok = (kpos >= 0) & (kpos < valid_len)
                    parts = [jnp.where(ok, x, NEG) for x in parts]
                sj = jnp.concatenate(parts, axis=1)
                s_ref[g * BAND_TILES + j] = sj
                m8 = sj.reshape(TILE // 8, 8, n_slab * TILE).max(axis=0)
                pm = m8 if pm is None else jnp.maximum(pm, m8)
            m = pm.max(axis=0, keepdims=True)
            p = [jnp.exp(s_ref[g * BAND_TILES + j] - m).astype(BF16) for j in range(BAND_TILES)]
            for hh in range(n_slab):
                acc = None
                for j in range(BAND_TILES):
                    d = jnp.dot(vt_ref[slots[j], g * n_slab + hh], p[j][:, hh * TILE:(hh + 1) * TILE],
                                preferred_element_type=F32)
                    acc = d if acc is None else acc + d
                blk = acc.T
                outs.append(blk[:, 0:HEAD_DIM] / blk[:, HEAD_DIM:HEAD_DIM + 1])
        o_ref[0] = jnp.concatenate(outs, axis=-1).astype(BF16)

    needs_mask = (base < 0) | (base + BAND_COLS > valid_len)

    @pl.when(needs_mask)
    def _():
        attend(True)

    @pl.when(jnp.logical_not(needs_mask))
    def _():
        attend(False)


def _band_attn(q, k, v, g_row, off, valid_len):
    b, tq = q.shape[:2]
    nq = tq // TILE
    qspec = pl.BlockSpec((1, TILE, A_WIDTH), lambda bb, i: (bb, i, 0))
    kspec = lambda d: pl.BlockSpec((1, TILE, A_WIDTH), lambda bb, i: (bb, jnp.maximum(i + off - d, 0), 0))
    return pl.pallas_call(
        functools.partial(_band_kernel, off=off, valid_len=valid_len),
        grid=(b, nq),
        in_specs=[pl.BlockSpec((A_HEADS, BAND_ROLL), lambda bb, i: (0, 0)),
                  qspec, kspec(2), kspec(1), kspec(0), kspec(2), kspec(1), kspec(0)],
        out_specs=pl.BlockSpec((1, TILE, A_WIDTH), lambda bb, i: (bb, i, 0)),
        out_shape=jax.ShapeDtypeStruct((b, tq, A_WIDTH), BF16),
        scratch_shapes=[pltpu.VMEM((A_HEADS, BAND_COLS, TILE), F32),
                        pltpu.VMEM((BAND_TILES, A_HEADS, 2 * HEAD_DIM, TILE), BF16),
                        pltpu.VMEM((A_HEADS // BAND_SLAB * BAND_TILES, TILE, BAND_SLAB * TILE), F32)],
        compiler_params=_params("arbitrary", "arbitrary"),
        name="band_attn",
    )(g_row, q, k, k, k, v, v, v)


DSA_ROLL = 3 * TILE
FAR_STEP = 4


def _t5_bucket(rel):
    half = N_BUCKETS // 2
    max_exact = half // 2
    n = jnp.abs(rel)
    log_ratio = jnp.log(jnp.maximum(n, 1).astype(jnp.float32) / max_exact) / math.log(T5_MAX_DIST / max_exact)
    large = jnp.minimum(max_exact + (log_ratio * (half - max_exact)).astype(jnp.int32), half - 1)
    return jnp.where(rel < 0, half, 0) + jnp.where(n < max_exact, n, large)


def _dsa_bias_row(t5_table):
    idx = jnp.arange(DSA_ROLL)
    d = jnp.where(idx < TILE, idx, idx - DSA_ROLL)
    far = t5_table[_t5_bucket(jnp.full((1,), 2 * TILE + 1, I32))]
    return (t5_table[_t5_bucket(TILE + d)] - far).T.astype(F32)


def _dsa_kernel(g_ref, qi_ref, wi_ref, qb_ref, ki_ref, kb_ref, vb_ref, o_ref,
                sc_ref, scb_ref, keep_ref, vt_ref, qis_ref, qbs_ref, m_ref, acc_ref, bias_ref, s_ref,
                *, off, valid_len, q_valid, n_kt, tq, n_merge):
    i = pl.program_id(1)
    qt = i + off
    q0 = qt * TILE
    nk = qt + 1
    key_rows = lambda j: pl.ds(pl.multiple_of(j * TILE, TILE), TILE)

    @pl.when((pl.program_id(0) == 0) & (i == 0))
    def _():
        for h in range(B_HEADS):
            tile = _toeplitz(g_ref[h:h + 1, :], 2 * TILE, DSA_ROLL)[:, :tq]
            if n_merge:
                ts = tq // n_merge
                stream = lax.broadcasted_iota(I32, (1, tq), 1) // ts
                first = tile
                for b in range(1, n_merge):
                    tile = jnp.where(stream == b, pltpu.roll(first, b * ts, 1), tile)
            bias_ref[h] = tile

    @pl.when(i == 0)
    def _():
        def body(j, c):
            for b in range(max(n_merge, 1)):
                vt = vb_ref[b, key_rows(j), :].astype(F32).T
                for g in range(B_KV_HEADS):
                    vt_ref[g, j, 0:HEAD_DIM, b * TILE:(b + 1) * TILE] = vt[g * HEAD_DIM:(g + 1) * HEAD_DIM].astype(BF16)
            for g in range(B_KV_HEADS):
                vt_ref[g, j, HEAD_DIM:2 * HEAD_DIM, :] = jnp.ones((HEAD_DIM, vt_ref.shape[-1]), BF16)
            return c

        lax.fori_loop(0, n_kt, body, 0)

    colq = lax.broadcasted_iota(I32, (1, tq), 1)
    rowk = lax.broadcasted_iota(I32, (TILE, 1), 0)

    if n_merge:
        width = n_merge * HEAD_DIM
        own = (lax.broadcasted_iota(I32, (tq, width), 0) // (tq // n_merge)
               == lax.broadcasted_iota(I32, (tq, width), 1) // HEAD_DIM)

        def block_diag(x):
            return jnp.where(own, jnp.concatenate([x.astype(F32)] * n_merge, axis=1), 0.0).astype(BF16)

        qi, qb = qi_ref[...], qb_ref[...]
        for h in range(B_HEADS):
            qis_ref[h] = block_diag(qi[:, h * IDX_DIM:(h + 1) * IDX_DIM])
            qbs_ref[h] = block_diag(qb[:, h * HEAD_DIM:(h + 1) * HEAD_DIM])
        side_by_side = lambda ref, j, cols: jnp.concatenate(
            [ref[b, key_rows(j), cols] for b in range(n_merge)], axis=1)
        wi_t = wi_ref[...].T
        lim = jnp.full((1, tq), valid_len, I32)
    else:
        qi = qi_ref[0]
        for h in range(IDX_HEADS):
            qis_ref[h * tq:(h + 1) * tq, :] = qi[:, h * IDX_DIM:(h + 1) * IDX_DIM]
        qb = qb_ref[0]
        for g in range(B_KV_HEADS):
            for hh in range(B_GROUP):
                h = g * B_GROUP + hh
                qbs_ref[g, hh * tq:(hh + 1) * tq, :] = qb[:, h * HEAD_DIM:(h + 1) * HEAD_DIM]
        wi_t = wi_ref[0].T
        lim = jnp.minimum(q0 + (colq // CHUNK + 1) * CHUNK, valid_len)

    def score_tile(j, masked):
        if n_merge:
            kt = side_by_side(ki_ref, j, slice(None))
            lg = jnp.concatenate([lax.dot_general(kt, qis_ref[h], _NT, preferred_element_type=F32)
                                  for h in range(IDX_HEADS)], axis=1)
        else:
            kt = ki_ref[0, key_rows(j), :]
            lg = lax.dot_general(kt, qis_ref[...], _NT, preferred_element_type=F32)
        sc = wi_t[0:1, :] * jnp.maximum(lg[:, 0:tq], 0.0)
        for h in range(1, IDX_HEADS):
            sc = sc + wi_t[h:h + 1, :] * jnp.maximum(lg[:, h * tq:(h + 1) * tq], 0.0)
        if masked:
            sc = jnp.where(j * TILE + rowk < lim, sc, -jnp.inf)
        sc_ref[j] = sc
        scb_ref[j] = sc.astype(BF16)
        rows8 = lambda hit: jnp.where(hit, 1, 0).reshape(TILE // 8, 8, tq).sum(axis=0)
        return rows8(sc > 0.0), rows8(sc >= 0.0)

    def score_pair(jj, c, masked):
        pos_a, nn_a = score_tile(2 * jj, masked)
        pos_b, nn_b = score_tile(2 * jj + 1, masked)
        return c[0] + pos_a + pos_b, c[1] + nn_a + nn_b

    n_open_pairs = (nk - 1) // 2
    counts = lax.fori_loop(0, n_open_pairs, functools.partial(score_pair, masked=False),
                           (jnp.zeros((8, tq), I32), jnp.zeros((8, tq), I32)))
    n_pos, n_nonneg = lax.fori_loop(n_open_pairs, (nk + 1) // 2, functools.partial(score_pair, masked=True), counts)

    def f32_of_key(k):
        return pltpu.bitcast(jnp.where(k < 0, k ^ 0x7FFFFFFF, k), F32)

    def bf16_of_key(k):
        bits = jnp.where(k < 0, k ^ 0x7FFF, k) & 0xFFFF
        return pltpu.bitcast(lax.shift_left(bits, 16), F32).astype(BF16)

    def count_bf16(cand):
        def body(jj, acc):
            for j in (2 * jj, 2 * jj + 1):
                ge = jnp.where(scb_ref[j] >= cand, jnp.int16(1), jnp.int16(0)).reshape(TILE // 16, 16, tq)
                part = ge[0]
                for r in range(1, TILE // 16):
                    part = part + ge[r]
                acc = acc + part
            return acc

        acc = lax.fori_loop(0, (nk + 1) // 2, body, jnp.zeros((16, tq), I16))
        return acc.astype(I32).sum(axis=0, keepdims=True)

    def count_f32(cand, strict=False):
        def body(j, acc):
            blk = sc_ref[j]
            hit = (blk > cand) if strict else (blk >= cand)
            return acc + jnp.where(hit, 1, 0).reshape(TILE // 8, 8, tq).sum(axis=0)

        return lax.fori_loop(0, nk, body, jnp.zeros((8, tq), I32)).sum(axis=0, keepdims=True)

    c_pos = n_pos.sum(axis=0, keepdims=True)
    zero_tie = (c_pos < TOPK_MAX) & (n_nonneg.sum(axis=0, keepdims=True) >= TOPK_MAX)
    skip1 = (colq >= q_valid) | zero_tie

    def level1(it, t):
        cand = t + lax.shift_left(jnp.int32(1), 15 - it)
        return jnp.where(count_bf16(bf16_of_key(cand)) >= TOPK_MAX, cand, t)

    t1 = lax.fori_loop(0, 16, level1, jnp.full((1, tq), MIN16, I32))
    settled1 = skip1 | (t1 <= HI_NEG_INF)

    def level2(st):
        lo, hi, thr_key, done = st
        live = (done == 0) & (hi - lo > 1)
        mid = lo + lax.shift_right_arithmetic(hi - lo, 1)
        c = count_f32(f32_of_key(mid))
        hit = live & (c == TOPK_MAX)
        return (jnp.where(live & (c >= TOPK_MAX), mid, lo), jnp.where(live & (c < TOPK_MAX), mid, hi),
                jnp.where(hit, mid, thr_key), jnp.where(hit, 1, done))

    def n_live(st):
        lo, hi, _, done = st
        return jnp.sum(jnp.where((done == 0) & (hi - lo > 1), 1, 0))

    def level2_pair(carry):
        st = level2(level2(carry[0]))
        return st, n_live(st)

    def key32_of_key16(k):
        return lax.shift_left(k, 16) | jnp.where(k < 0, 0xFFFF, 0)

    key_t1 = key32_of_key16(t1)
    st0 = (key_t1 - 0x8000, key32_of_key16(t1 + 1), key_t1, jnp.where(settled1, 1, 0))
    (lo, _, thr_key, done2), _ = lax.while_loop(lambda carry: carry[1] > 0, level2_pair, (st0, n_live(st0)))
    open2 = done2 == 0
    thr = jnp.where(open2, f32_of_key(lo), f32_of_key(thr_key))
    thr = jnp.where(zero_tie, 0.0, thr)
    thr = jnp.where(settled1 & jnp.logical_not(zero_tie), -jnp.inf, thr)
    thr = jnp.maximum(thr, float(jnp.finfo(F32).min))

    keep_ref[...] = jnp.where(zero_tie, TOPK_MAX - c_pos, 2 ** 30).astype(F32)

    @pl.when(jnp.sum(jnp.where(open2, 1, 0)) > 0)
    def _():
        above = count_f32(thr, strict=True)
        keep_ref[...] = jnp.where(open2, (TOPK_MAX - above).astype(F32), keep_ref[...])

    @pl.when(jnp.sum(jnp.where(open2 | zero_tie, 1, 0)) > 0)
    def _():
        keep = keep_ref[...]
        lower = (lax.broadcasted_iota(I32, (TILE, TILE), 0) > lax.broadcasted_iota(I32, (TILE, TILE), 1))
        lower = jnp.where(lower, 1.0, 0.0).astype(BF16)

        def body(jj, run):
            for j in (2 * jj, 2 * jj + 1):
                blk = sc_ref[j]
                eq = blk == thr
                eq_f = jnp.where(eq, 1.0, 0.0)
                before = jnp.dot(lower, eq_f.astype(BF16), preferred_element_type=F32)
                sc_ref[j] = jnp.where(eq & (run + before >= keep), -jnp.inf, blk)
                run = run + eq_f.reshape(TILE // 8, 8, tq).sum(axis=0).sum(axis=0, keepdims=True)
            return run

        lax.fori_loop(0, (nk + 1) // 2, body, jnp.zeros((1, tq), F32))

    for g in range(B_KV_HEADS):
        m_ref[g] = jnp.full((1, B_GROUP * tq), NEG, F32)
        acc_ref[g] = jnp.zeros((2 * HEAD_DIM, B_GROUP * tq), F32)

    def attend(tiles):
        sels = [sc_ref[j] >= thr for j, _ in tiles]
        part_max = []
        for g in range(B_KV_HEADS):
            pm = None
            for t, ((j, near), sel) in enumerate(zip(tiles, sels)):
                if n_merge:
                    kt = side_by_side(kb_ref, j, slice(g * HEAD_DIM, (g + 1) * HEAD_DIM))
                    st = jnp.concatenate([lax.dot_general(kt, qbs_ref[g * B_GROUP + hh], _NT,
                                                          preferred_element_type=F32)
                                          for hh in range(B_GROUP)], axis=1)
                else:
                    kt = kb_ref[0, key_rows(j), g * HEAD_DIM:(g + 1) * HEAD_DIM]
                    st = lax.dot_general(kt, qbs_ref[g], _NT, preferred_element_type=F32)
                parts = []
                for hh in range(B_GROUP):
                    s_h = st[:, hh * tq:(hh + 1) * tq]
                    if near is not None:
                        s_h = s_h + bias_ref[g * B_GROUP + hh, near * TILE:(near + 1) * TILE, :]
                    parts.append(jnp.where(sel, s_h, -jnp.inf))
                s = jnp.concatenate(parts, axis=1)
                s_ref[g * len(tiles) + t] = s
                m8 = s.reshape(TILE // 8, 8, B_GROUP * tq).max(axis=0)
                pm = m8 if pm is None else jnp.maximum(pm, m8)
            part_max.append(pm)
        for g in range(B_KV_HEADS):
            m_old = m_ref[g]
            m_new = jnp.maximum(m_old, part_max[g].max(axis=0, keepdims=True))
            pv = None
            for t, (j, _) in enumerate(tiles):
                p = jnp.exp(s_ref[g * len(tiles) + t] - m_new)
                if n_merge:
                    lane_stream = colq // (tq // n_merge)
                    d = jnp.concatenate(
                        [jnp.dot(vt_ref[g, j],
                                 jnp.concatenate([jnp.where(lane_stream == b, p[:, hh * tq:(hh + 1) * tq], 0.0)
                                                  for b in range(n_merge)], axis=0).astype(BF16),
                                 preferred_element_type=F32) for hh in range(B_GROUP)], axis=1)
                else:
                    d = jnp.dot(vt_ref[g, j], p.astype(BF16), preferred_element_type=F32)
                pv = d if pv is None else pv + d
            acc_ref[g] = jnp.exp(m_old - m_new) * acc_ref[g] + pv
            m_ref[g] = m_new

    n_far = jnp.maximum(nk - 2, 0)

    def far_step(jj, c):
        attend([(FAR_STEP * jj + t, None) for t in range(FAR_STEP)])
        return c

    lax.fori_loop(0, n_far // FAR_STEP, far_step, 0)
    rem = n_far % FAR_STEP

    @pl.when(rem >= 2)
    def _():
        attend([(n_far - rem, None), (n_far - rem + 1, None)])

    @pl.when(rem % 2 == 1)
    def _():
        attend([(n_far - 1, None)])

    @pl.when(nk >= 2)
    def _():
        attend([(nk - 2, 0), (nk - 1, 1)])

    @pl.when(nk < 2)
    def _():
        attend([(nk - 1, 1)])

    outs = []
    for g in range(B_KV_HEADS):
        for hh in range(B_GROUP):
            blk = acc_ref[g, :, hh * tq:(hh + 1) * tq].T
            outs.append(blk[:, 0:HEAD_DIM] / blk[:, HEAD_DIM:HEAD_DIM + 1])
    o_ref[0] = jnp.concatenate(outs, axis=-1).astype(BF16)


def _dsa_attn(qi, wi, qb, ki, kb, vb, g_row, off, valid_len, q_valid, tq):
    b, t_q = qi.shape[:2]
    tk = ki.shape[1]
    nq, n_kt = t_q // tq, tk // TILE
    assert tq == TILE or nq == 1
    assert n_kt % 2 == 0 and tk == n_kt * TILE
    qspec = lambda n: pl.BlockSpec((1, tq, n), lambda bb, i: (bb, i, 0))
    kspec = lambda n: pl.BlockSpec((1, tk, n), lambda bb, i: (bb, 0, 0))
    in_specs = [pl.BlockSpec((B_HEADS, DSA_ROLL), lambda bb, i: (0, 0)),
                qspec(512), qspec(128), qspec(512), kspec(IDX_DIM), kspec(128), kspec(128)]
    return _dsa_call((g_row, qi, wi, qb, ki, kb, vb), in_specs, (b, nq), t_q, tq, n_kt,
                     dict(off=off, valid_len=valid_len, q_valid=q_valid, n_merge=0))


def _dsa_attn_merged(qi, wi, qb, ki, kb, vb, g_row, off, valid_len):
    s, ts = qi.shape[:2]
    tk = ki.shape[1]
    tq, n_kt = s * ts, tk // TILE
    assert tq % LANES == 0 and ts <= CHUNK and n_kt % 2 == 0 and tk == n_kt * TILE
    flat = lambda a: a.reshape(tq, a.shape[-1])
    operands = (g_row, flat(qi), flat(wi), flat(qb), ki, kb, vb)
    whole = lambda a: pl.BlockSpec(a.shape, lambda bb, i: (0,) * a.ndim, pipeline_mode=pl.Buffered(1))
    out = _dsa_call(operands, [whole(a) for a in operands], (1, 1), tq, tq, n_kt,
                    dict(off=off, valid_len=valid_len, q_valid=tq, n_merge=s))
    return out.reshape(s, ts, B_WIDTH)


def _dsa_call(operands, in_specs, grid, t_q, tq, n_kt, static):
    n_merge = static["n_merge"]
    assert static["valid_len"] >= (static["off"] + grid[1] - 1) * TILE
    if n_merge:
        q_scratch = [pltpu.VMEM((IDX_HEADS, tq, n_merge * IDX_DIM), BF16),
                     pltpu.VMEM((B_HEADS, tq, n_merge * HEAD_DIM), BF16)]
    else:
        q_scratch = [pltpu.VMEM((IDX_HEADS * tq, IDX_DIM), BF16),
                     pltpu.VMEM((B_KV_HEADS, B_GROUP * tq, HEAD_DIM), BF16)]
    return pl.pallas_call(
        functools.partial(_dsa_kernel, n_kt=n_kt, tq=tq, **static),
        grid=grid,
        in_specs=in_specs,
        out_specs=pl.BlockSpec((1, tq, B_WIDTH), lambda bb, i: (bb, i, 0)),
        out_shape=jax.ShapeDtypeStruct((grid[0], t_q, B_WIDTH), BF16),
        scratch_shapes=[pltpu.VMEM((n_kt, TILE, tq), F32),
                        pltpu.VMEM((n_kt, TILE, tq), BF16),
                        pltpu.VMEM((1, tq), F32),
                        pltpu.VMEM((B_KV_HEADS, n_kt, 2 * HEAD_DIM, max(n_merge, 1) * TILE), BF16),
                        *q_scratch,
                        pltpu.VMEM((B_KV_HEADS, 1, B_GROUP * tq), F32),
                        pltpu.VMEM((B_KV_HEADS, 2 * HEAD_DIM, B_GROUP * tq), F32),
                        pltpu.VMEM((B_HEADS, 2 * TILE, tq), F32),
                        pltpu.VMEM((B_KV_HEADS * FAR_STEP, TILE, B_GROUP * tq), F32)],
        compiler_params=_params("arbitrary", "arbitrary"),
        name="dsa_attn",
    )(*operands)


def _mem_kv_kernel(m_ref, wk_ref, wv_ref, k_o, v_o, kb_o, vb_o):
    mb = m_ref[...].astype(BF16)
    k = jnp.dot(mb, wk_ref[...], preferred_element_type=F32)
    v = jnp.dot(mb, wv_ref[...], preferred_element_type=F32)
    k_o[...] = k
    v_o[...] = v
    kb_o[...] = k.astype(BF16)
    vb_o[...] = v.astype(BF16)


def _mem_kv(mem2d, wk, wv):
    r = mem2d.shape[0]
    tm = MEM_LEN
    row = lambda n: pl.BlockSpec((tm, n), lambda i: (i, 0))
    const = lambda s: pl.BlockSpec(s, lambda i: (0, 0))
    sds = jax.ShapeDtypeStruct
    return pl.pallas_call(
        _mem_kv_kernel,
        grid=(r // tm,),
        in_specs=[row(D_MODEL), const((D_MODEL, MEM_WIDTH)), const((D_MODEL, MEM_WIDTH))],
        out_specs=[row(MEM_WIDTH)] * 4,
        out_shape=[sds((r, MEM_WIDTH), F32), sds((r, MEM_WIDTH), F32),
                   sds((r, MEM_WIDTH), BF16), sds((r, MEM_WIDTH), BF16)],
        compiler_params=_params("arbitrary"),
        name="mem_kv",
    )(mem2d, wk, wv)


FF_CHUNK = 256


def _tail_kernel(x_ref, oa_ref, ob_ref, mk_ref, mv_ref, hist_ref,
                 wo_ref, g1_ref, b1_ref, wq_ref, wmo_ref, g2_ref, b2_ref,
                 wu_ref, wc_ref, bc_ref, wd_ref, g3_ref, b3_ref,
                 o_ref, tail_ref, carry_ref, act_ref, *, tiles_per_batch, seg):
    i = pl.program_id(0)
    tm = x_ref.shape[0]
    nseg = tm // seg

    mix = jnp.concatenate([oa_ref[...], ob_ref[...]], axis=-1)
    h = _layer_norm(ALPHA * x_ref[...] + jnp.dot(mix, wo_ref[...], preferred_element_type=F32),
                    g1_ref[...], b1_ref[...])

    q = jnp.dot(h.astype(BF16), wq_ref[...], preferred_element_type=F32).astype(BF16)
    segs = []
    for s in range(nseg):
        qs = q[s * seg:(s + 1) * seg]
        heads = []
        for hd in range(MEM_HEADS):
            sl = slice(hd * MEM_HEAD_DIM, (hd + 1) * MEM_HEAD_DIM)
            sc = lax.dot_general(qs[:, sl], mk_ref[s, :, sl], _NT, preferred_element_type=F32) * MEM_HEAD_DIM ** -0.5
            p = jnp.exp(sc - sc.max(-1, keepdims=True))
            l = p.sum(-1, keepdims=True)
            heads.append(jnp.dot(p.astype(BF16), mv_ref[s, :, sl], preferred_element_type=F32) / l)
        segs.append(jnp.concatenate(heads, axis=-1))
    att = jnp.concatenate(segs, axis=0).astype(BF16)
    h = _layer_norm(ALPHA * h + jnp.dot(att, wmo_ref[...], preferred_element_type=F32), g2_ref[...], b2_ref[...])

    hb = h.astype(BF16)
    row = lax.broadcasted_iota(I32, (tm, 1), 0)
    first = (i % tiles_per_batch) == 0
    for c in range(D_FF // FF_CHUNK):
        cs = slice(c * FF_CHUNK, (c + 1) * FF_CHUNK)
        u = jnp.dot(hb, wu_ref[:, cs], preferred_element_type=F32)
        gt = jnp.dot(hb, wu_ref[:, D_FF + c * FF_CHUNK:D_FF + (c + 1) * FF_CHUNK], preferred_element_type=F32)
        p1 = pltpu.roll(gt, 1, 0)
        p2 = pltpu.roll(gt, 2, 0)
        for s in range(nseg):
            hist = hist_ref[s, :, cs]
            if tiles_per_batch > 1:
                hist = jnp.where(first, hist, carry_ref[:, cs])
            p1 = jnp.where(row == s * seg, hist[7:8, :], p1)
            p2 = jnp.where(row == s * seg, hist[6:7, :], p2)
            p2 = jnp.where(row == s * seg + 1, hist[7:8, :], p2)
        gc = bc_ref[:, cs] + ((wc_ref[0:1, cs] * p2 + wc_ref[1:2, cs] * p1) + wc_ref[2:3, cs] * gt)
        act_ref[:, cs] = (u * jax.nn.gelu(gc)).astype(BF16)
        for s in range(nseg):
            tail_ref[s, :, cs] = gt[(s + 1) * seg - 8:(s + 1) * seg, :]
        carry_ref[:, cs] = gt[tm - 8:tm, :]
    f = jnp.dot(act_ref[...], wd_ref[...], preferred_element_type=F32)
    o_ref[...] = _layer_norm(ALPHA * h + f, g3_ref[...], b3_ref[...])


def _layer_tail(x2d, oa, ob, mk, mv, hist, weights, tm, tiles_per_batch, seg):
    r = x2d.shape[0]
    nseg = tm // seg
    n_stream = r // (tm * tiles_per_batch) * nseg
    row = lambda n: pl.BlockSpec((tm, n), lambda i: (i, 0))
    per_stream = lambda a, b: pl.BlockSpec((nseg, a, b), lambda i: (i // tiles_per_batch, 0, 0))
    const = lambda a: pl.BlockSpec(a.shape, lambda i: (0,) * a.ndim, pipeline_mode=pl.Buffered(1))
    return pl.pallas_call(
        functools.partial(_tail_kernel, tiles_per_batch=tiles_per_batch, seg=seg),
        grid=(r // tm,),
        in_specs=[row(D_MODEL), row(A_WIDTH), row(B_WIDTH), per_stream(MEM_LEN, MEM_WIDTH),
                  per_stream(MEM_LEN, MEM_WIDTH), per_stream(8, D_FF)] + [const(w) for w in weights],
        out_specs=[row(D_MODEL), per_stream(8, D_FF)],
        out_shape=[jax.ShapeDtypeStruct((r, D_MODEL), F32), jax.ShapeDtypeStruct((n_stream, 8, D_FF), F32)],
        scratch_shapes=[pltpu.VMEM((8, D_FF), F32), pltpu.VMEM((tm, D_FF), BF16)],
        compiler_params=_params("arbitrary"),
        name="layer_tail",
    )(x2d, oa, ob, mk, mv, hist, *weights)


def _pad_rows(a, n):
    return jnp.pad(a, ((0, 0), (0, n - a.shape[1])) + ((0, 0),) * (a.ndim - 2))


def _hist8(g_hist):
    return jnp.pad(g_hist, ((0, 0), (8 - g_hist.shape[1], 0), (0, 0)))


def kernel(x_prompt, x_sample, cache_a_k, cache_a_v, cache_b_k, cache_b_v, cache_b_kidx, cache_mem_k, cache_mem_v, state_ffn_conv, mem_prompt, w_in, a_rel_bias, t5_bias, w_o, ln1_g, ln1_b, w_mq, w_mk, w_mv, w_mo, ln2_g, ln2_b, w_up, w_conv, b_conv, w_down, ln3_g, ln3_b):
    bp, tp = x_prompt.shape[:2]
    bs, ts = x_sample.shape[:2]
    l = 0
    vec = lambda a: a[l].reshape(1, -1)
    w_in_p = _prep_w_in(w_in[l])
    w_o_b = w_o[l].astype(BF16)
    w_mq_b, w_mk_b, w_mv_b, w_mo_b = (w[l].astype(BF16) for w in (w_mq, w_mk, w_mv, w_mo))
    w_up_b, w_down_b = w_up[l].astype(BF16), w_down[l].astype(BF16)
    band_row = _band_bias_row(a_rel_bias[l])
    dsa_row = _dsa_bias_row(t5_bias)
    tail_w = (w_o_b, vec(ln1_g), vec(ln1_b), w_mq_b, w_mo_b, vec(ln2_g), vec(ln2_b),
              w_up_b, w_conv[l], vec(b_conv), w_down_b, vec(ln3_g), vec(ln3_b))

    tm = 512
    a_keep = min(N_PREV_CHUNKS * CHUNK, tp)
    (qa, ka, va, qb, qi, kb, vb, ki, kb_b, vb_b, ki_b, wi, ka_tail, va_tail) = _in_proj(
        x_prompt.reshape(bp * tp, D_MODEL), w_in_p, tm, tp // tm)
    r3 = lambda a: a.reshape(bp, tp, a.shape[-1])
    oa = _band_attn(r3(qa), r3(ka), r3(va), band_row, 0, tp)
    ob = _dsa_attn(r3(qi), r3(wi), r3(qb), r3(ki_b), r3(kb_b), r3(vb_b), dsa_row, 0, tp, TILE, TILE)
    mk, mv, mk_b, mv_b = _mem_kv(mem_prompt.reshape(bp * MEM_LEN, D_MODEL), w_mk_b, w_mv_b)
    xp, p_tail = _layer_tail(x_prompt.reshape(bp * tp, D_MODEL), oa.reshape(bp * tp, A_WIDTH),
                             ob.reshape(bp * tp, B_WIDTH), mk_b.reshape(bp, MEM_LEN, MEM_WIDTH),
                             mv_b.reshape(bp, MEM_LEN, MEM_WIDTH), jnp.zeros((bp, 8, D_FF), F32), tail_w,
                             tm, tp // tm, tm)
    prompt_state = (
        ka_tail.reshape(bp, tm, A_HEADS, HEAD_DIM)[:, tm - a_keep:][None],
        va_tail.reshape(bp, tm, A_HEADS, HEAD_DIM)[:, tm - a_keep:][None],
        kb.reshape(1, bp, tp, B_KV_HEADS, HEAD_DIM), vb.reshape(1, bp, tp, B_KV_HEADS, HEAD_DIM),
        ki.reshape(1, bp, tp, IDX_DIM),
        mk.reshape(1, bp, MEM_LEN, MEM_HEADS, MEM_HEAD_DIM), mv.reshape(1, bp, MEM_LEN, MEM_HEADS, MEM_HEAD_DIM),
        p_tail[:, 8 - (CONV_W - 1):][None])

    rs = bs * ts
    (qa, ka, va, qb, qi, kb, vb, ki, kb_b, vb_b, ki_b, wi, ka_new, va_new) = _in_proj(
        x_sample.reshape(rs, D_MODEL), w_in_p, rs, 1)
    s3 = lambda a: a.reshape(bs, ts, -1)
    qpad = lambda a, n: _pad_rows(s3(a), n)

    past_a = cache_a_k.shape[2]
    n_a = past_a + ts
    t_a = -(-n_a // TILE) * TILE
    seq_a = lambda cache, new: _pad_rows(
        jnp.concatenate([cache[l].reshape(bs, past_a, A_WIDTH), s3(new)], axis=1), t_a).astype(BF16)
    oa = _band_attn(qpad(qa, TILE), seq_a(cache_a_k, ka_new), seq_a(cache_a_v, va_new), band_row,
                    past_a // TILE, n_a)

    past_b = cache_b_k.shape[2]
    n_b = past_b + ts
    t_b = -(-n_b // (2 * TILE)) * 2 * TILE
    seq_b = lambda cache, new: _pad_rows(
        jnp.concatenate([cache[l].reshape(bs, past_b, -1), s3(new)], axis=1), t_b).astype(BF16)
    ob = _dsa_attn_merged(s3(qi), s3(wi), s3(qb), seq_b(cache_b_kidx, ki), seq_b(cache_b_k, kb),
                          seq_b(cache_b_v, vb), dsa_row, past_b // TILE, n_b)

    xs, s_tail = _layer_tail(x_sample.reshape(rs, D_MODEL), oa[:, :ts].reshape(rs, A_WIDTH),
                             ob[:, :ts].reshape(rs, B_WIDTH),
                             cache_mem_k[l].reshape(bs, MEM_LEN, MEM_WIDTH).astype(BF16),
                             cache_mem_v[l].reshape(bs, MEM_LEN, MEM_WIDTH).astype(BF16),
                             _hist8(state_ffn_conv[l]), tail_w, rs, 1, ts)
    sample_state = (
        ka_new.reshape(1, bs, ts, A_HEADS, HEAD_DIM), va_new.reshape(1, bs, ts, A_HEADS, HEAD_DIM),
        kb.reshape(1, bs, ts, B_KV_HEADS, HEAD_DIM), vb.reshape(1, bs, ts, B_KV_HEADS, HEAD_DIM),
        ki.reshape(1, bs, ts, IDX_DIM), s_tail[:, 8 - (CONV_W - 1):][None])

    return (xp.reshape(bp, tp, D_MODEL), xs.reshape(bs, ts, D_MODEL)) + prompt_state + sample_state
```

```python
import functools
import math

import jax
import jax.numpy as jnp
from jax import lax
from jax.experimental import pallas as pl
from jax.experimental.pallas import tpu as pltpu

F32 = jnp.float32
BF16 = jnp.bfloat16
I32 = jnp.int32
I16 = jnp.int16

D_MODEL = 1024
CHUNK = 64
N_PREV_CHUNKS = 8
HEAD_DIM = 64
A_HEADS = 8
A_WIDTH = A_HEADS * HEAD_DIM
A_MAX_REL = 64
B_HEADS = 8
B_KV_HEADS = 2
B_GROUP = B_HEADS // B_KV_HEADS
B_WIDTH = B_HEADS * HEAD_DIM
B_KV_WIDTH = B_KV_HEADS * HEAD_DIM
IDX_HEADS = 8
IDX_DIM = 64
TOPK_MAX = 256
N_BUCKETS = 32
T5_MAX_DIST = 128
MEM_LEN = 256
MEM_HEADS = 4
MEM_HEAD_DIM = 128
MEM_WIDTH = MEM_HEADS * MEM_HEAD_DIM
D_FF = 2816
CONV_W = 3
IN_SIZES = (A_WIDTH, A_WIDTH, A_WIDTH, B_WIDTH, B_KV_WIDTH, B_KV_WIDTH, IDX_HEADS * IDX_DIM, IDX_DIM, IDX_HEADS)
DEPTH = 1
ALPHA = (2 * DEPTH) ** 0.25
LN_EPS = 1e-5
ATTN_SCALE = HEAD_DIM ** -0.5
NEG = -1e30

LANES = 128
TILE = 256
BAND_TILES = 1 + (N_PREV_CHUNKS * CHUNK) // TILE
VMEM_LIMIT = 56 * 1024 * 1024

_C_QA, _C_KA, _C_VA, _C_QB = 0, 512, 1024, 1536
_C_KB, _C_VB, _C_QI, _C_KI, _C_WI = 2048, 2176, 2304, 2816, 2944
IN_PAD = 3072

MIN16 = -32768
HI_NEG_INF = -32641

_NT = (((1,), (1,)), ((), ()))


def _params(*sem):
    return pltpu.CompilerParams(dimension_semantics=sem, vmem_limit_bytes=VMEM_LIMIT)


def _layer_norm(z, g, b):
    mu = jnp.mean(z, axis=-1, keepdims=True)
    d = z - mu
    var = jnp.mean(d * d, axis=-1, keepdims=True)
    return d * lax.rsqrt(var + LN_EPS) * g + b


def _toeplitz(g_row, rows, width):
    return pltpu.roll(jnp.broadcast_to(g_row, (rows, width)), 0, 1, stride=1, stride_axis=0)


def _in_proj_kernel(x_ref, w_ref, qa_o, ka_o, va_o, qb_o, qi_o, kb_o, vb_o, ki_o, kbb_o, vbb_o, kib_o,
                    wi_o, kat_o, vat_o, *, tiles_per_batch, wi_scale):
    i = pl.program_id(0)
    xb = x_ref[...].astype(BF16)

    def mm(c0, n):
        return jnp.dot(xb, w_ref[:, c0:c0 + n], preferred_element_type=F32)

    kb = mm(_C_KB, 128)
    vb = mm(_C_VB, 128)
    for g in range(B_KV_HEADS):
        kb_o[:, g, :] = kb[:, g * HEAD_DIM:(g + 1) * HEAD_DIM]
        vb_o[:, g, :] = vb[:, g * HEAD_DIM:(g + 1) * HEAD_DIM]
    kbb_o[...] = kb.astype(BF16)
    vbb_o[...] = vb.astype(BF16)
    ki = mm(_C_KI, 128)[:, :IDX_DIM]
    ki_o[...] = ki
    kib_o[...] = ki.astype(BF16)
    wi_o[...] = mm(_C_WI, 128) * wi_scale
    qa_o[...] = mm(_C_QA, 512).astype(BF16)
    ka = mm(_C_KA, 512)
    va = mm(_C_VA, 512)
    ka_o[...] = ka.astype(BF16)
    va_o[...] = va.astype(BF16)
    qb_o[...] = mm(_C_QB, 512).astype(BF16)
    qi_o[...] = mm(_C_QI, 512).astype(BF16)

    @pl.when(i % tiles_per_batch == tiles_per_batch - 1)
    def _():
        kat_o[...] = ka
        vat_o[...] = va


def _prep_w_in(w):
    parts, off = [], 0
    for n in IN_SIZES:
        parts.append(w[:, off:off + n])
        off += n
    qa, ka, va, qb, kb, vb, qi, ki, wi = parts
    pad = lambda a, n: jnp.pad(a, ((0, 0), (0, n - a.shape[1])))
    cols = [qa * ATTN_SCALE, ka, va, qb * ATTN_SCALE, kb, vb, qi * IDX_DIM ** -0.5, pad(ki, 128), pad(wi, 128)]
    return jnp.concatenate(cols, axis=1).astype(BF16)


def _in_proj(x2d, w_pad, tm, tiles_per_batch):
    r = x2d.shape[0]
    n_tiles = r // tm
    n_batch = n_tiles // tiles_per_batch
    row = lambda n: pl.BlockSpec((tm, n), lambda i: (i, 0))
    tail = pl.BlockSpec((tm, 512), lambda i: (i // tiles_per_batch, 0))
    sds = jax.ShapeDtypeStruct
    kv_state = pl.BlockSpec((tm, B_KV_HEADS, HEAD_DIM), lambda i: (i, 0, 0))
    kv_shape = sds((r, B_KV_HEADS, HEAD_DIM), F32)
    out_shape = [sds((r, 512), BF16)] * 5 + [kv_shape, kv_shape, sds((r, IDX_DIM), F32),
                                              sds((r, 128), BF16), sds((r, 128), BF16), sds((r, IDX_DIM), BF16),
                                              sds((r, 128), F32),
                                              sds((n_batch * tm, 512), F32), sds((n_batch * tm, 512), F32)]
    out_specs = [row(512)] * 5 + [kv_state, kv_state, row(IDX_DIM), row(128), row(128), row(IDX_DIM), row(128),
                                  tail, tail]
    return pl.pallas_call(
        functools.partial(_in_proj_kernel, tiles_per_batch=tiles_per_batch, wi_scale=IDX_HEADS ** -0.5),
        grid=(n_tiles,),
        in_specs=[pl.BlockSpec((tm, D_MODEL), lambda i: (i, 0)),
                  pl.BlockSpec((D_MODEL, IN_PAD), lambda i: (0, 0))],
        out_specs=out_specs,
        out_shape=out_shape,
        compiler_params=_params("arbitrary"),
        name="in_proj",
    )(x2d, w_pad)


BAND_COLS = BAND_TILES * TILE
BAND_ROLL = BAND_COLS + TILE
BAND_SLAB = 4


def _band_bias_row(table):
    idx = jnp.arange(BAND_ROLL)
    d = jnp.where(idx < TILE, idx, idx - BAND_ROLL)
    rel = (BAND_TILES - 1) * TILE + d
    return table[jnp.clip(rel, -A_MAX_REL, A_MAX_REL) + A_MAX_REL].T.astype(F32)


def _band_kernel(g_ref, q_ref, k0, k1, k2, v0, v1, v2, o_ref, bias_ref, vt_ref, s_ref, *, off, valid_len):
    i = pl.program_id(1)
    kt = i + off
    krefs, vrefs = (k0, k1, k2), (v0, v1, v2)

    @pl.when((pl.program_id(0) == 0) & (i == 0))
    def _():
        c = lax.broadcasted_iota(I32, (BAND_COLS, TILE), 0) // CHUNK
        r = lax.broadcasted_iota(I32, (BAND_COLS, TILE), 1) // CHUNK
        ok = (c >= r) & (c <= r + N_PREV_CHUNKS)
        for h in range(A_HEADS):
            bias_ref[h] = jnp.where(ok, _toeplitz(g_ref[h:h + 1, :], BAND_COLS, BAND_ROLL)[:, :TILE], NEG)
        ones = jnp.ones((HEAD_DIM, TILE), BF16)
        for s in range(BAND_TILES):
            for h in range(A_HEADS):
                vt_ref[s, h, HEAD_DIM:2 * HEAD_DIM, :] = ones

    def put(slot, vref):
        vt = vref[0].astype(F32).T
        for h in range(A_HEADS):
            vt_ref[slot, h, 0:HEAD_DIM, :] = vt[h * HEAD_DIM:(h + 1) * HEAD_DIM].astype(BF16)

    slots = [(kt + 1 + j) % BAND_TILES for j in range(BAND_TILES)]

    @pl.when(i == 0)
    def _():
        for j in range(BAND_TILES - 1):
            put(slots[j], vrefs[j])

    put(slots[BAND_TILES - 1], vrefs[BAND_TILES - 1])

    base = (kt - (BAND_TILES - 1)) * TILE

    n_slab = BAND_SLAB
    slab_w = n_slab * HEAD_DIM
    lane_head = lax.broadcasted_iota(I32, (TILE, slab_w), 1) // HEAD_DIM

    def attend(masked):
        outs = []
        for g in range(A_HEADS // n_slab):
            gs = slice(g * slab_w, (g + 1) * slab_w)
            q_slab = q_ref[0, :, gs].astype(F32)
            q_bd = jnp.concatenate([jnp.where(lane_head == hh, q_slab, 0.0) for hh in range(n_slab)],
                                   axis=0).astype(BF16)
            pm = None
            for j in range(BAND_TILES):
                st = lax.dot_general(krefs[j][0, :, gs], q_bd, _NT, preferred_element_type=F32)
                parts = [st[:, hh * TILE:(hh + 1) * TILE] + bias_ref[g * n_slab + hh, j * TILE:(j + 1) * TILE, :]
                         for hh in range(n_slab)]
                if masked:
                    kpos = base + j * TILE + lax.broadcasted_iota(I32, (TILE, TILE), 0)
                    ok = (kpos >= 0) & (kpos < valid_len)
                    parts = [jnp.where(ok, x, NEG) for x in parts]
                sj = jnp.concatenate(parts, axis=1)
                s_ref[g * BAND_TILES + j] = sj
                m8 = sj.reshape(TILE // 8, 8, n_slab * TILE).max(axis=0)
                pm = m8 if pm is None else jnp.maximum(pm, m8)
            m = pm.max(axis=0, keepdims=True)
            p = [jnp.exp(s_ref[g * BAND_TILES + j] - m).astype(BF16) for j in range(BAND_TILES)]
            for hh in range(n_slab):
                acc = None
                for j in range(BAND_TILES):
                    d = jnp.dot(vt_ref[slots[j], g * n_slab + hh], p[j][:, hh * TILE:(hh + 1) * TILE],
                                preferred_element_type=F32)
                    acc = d if acc is None else acc + d
                blk = acc.T
                outs.append(blk[:, 0:HEAD_DIM] / blk[:, HEAD_DIM:HEAD_DIM + 1])
        o_ref[0] = jnp.concatenate(outs, axis=-1).astype(BF16)

    needs_mask = (base < 0) | (base + BAND_COLS > valid_len)

    @pl.when(needs_mask)
    def _():
        attend(True)

    @pl.when(jnp.logical_not(needs_mask))
    def _():
        attend(False)


def _band_attn(q, k, v, g_row, off, valid_len):
    b, tq = q.shape[:2]
    nq = tq // TILE
    qspec = pl.BlockSpec((1, TILE, A_WIDTH), lambda bb, i: (bb, i, 0))
    kspec = lambda d: pl.BlockSpec((1, TILE, A_WIDTH), lambda bb, i: (bb, jnp.maximum(i + off - d, 0), 0))
    return pl.pallas_call(
        functools.partial(_band_kernel, off=off, valid_len=valid_len),
        grid=(b, nq),
        in_specs=[pl.BlockSpec((A_HEADS, BAND_ROLL), lambda bb, i: (0, 0)),
                  qspec, kspec(2), kspec(1), kspec(0), kspec(2), kspec(1), kspec(0)],
        out_specs=pl.BlockSpec((1, TILE, A_WIDTH), lambda bb, i: (bb, i, 0)),
        out_shape=jax.ShapeDtypeStruct((b, tq, A_WIDTH), BF16),
        scratch_shapes=[pltpu.VMEM((A_HEADS, BAND_COLS, TILE), F32),
                        pltpu.VMEM((BAND_TILES, A_HEADS, 2 * HEAD_DIM, TILE), BF16),
                        pltpu.VMEM((A_HEADS // BAND_SLAB * BAND_TILES, TILE, BAND_SLAB * TILE), F32)],
        compiler_params=_params("arbitrary", "arbitrary"),
        name="band_attn",
    )(g_row, q, k, k, k, v, v, v)


DSA_ROLL = 3 * TILE
FAR_STEP = 4


def _t5_bucket(rel):
    half = N_BUCKETS // 2
    max_exact = half // 2
    n = jnp.abs(rel)
    log_ratio = jnp.log(jnp.maximum(n, 1).astype(jnp.float32) / max_exact) / math.log(T5_MAX_DIST / max_exact)
    large = jnp.minimum(max_exact + (log_ratio * (half - max_exact)).astype(jnp.int32), half - 1)
    return jnp.where(rel < 0, half, 0) + jnp.where(n < max_exact, n, large)


def _dsa_bias_row(t5_table):
    idx = jnp.arange(DSA_ROLL)
    d = jnp.where(idx < TILE, idx, idx - DSA_ROLL)
    far = t5_table[_t5_bucket(jnp.full((1,), 2 * TILE + 1, I32))]
    return (t5_table[_t5_bucket(TILE + d)] - far).T.astype(F32)


def _dsa_kernel(g_ref, qi_ref, wi_ref, qb_ref, ki_ref, kb_ref, vb_ref, o_ref,
                sc_ref, scb_ref, keep_ref, vt_ref, qis_ref, qbs_ref, m_ref, acc_ref, bias_ref, s_ref,
                *, off, valid_len, q_valid, n_kt, tq, n_merge):
    i = pl.program_id(1)
    qt = i + off
    q0 = qt * TILE
    nk = qt + 1
    key_rows = lambda j: pl.ds(pl.multiple_of(j * TILE, TILE), TILE)

    @pl.when((pl.program_id(0) == 0) & (i == 0))
    def _():
        for h in range(B_HEADS):
            tile = _toeplitz(g_ref[h:h + 1, :], 2 * TILE, DSA_ROLL)[:, :tq]
            if n_merge:
                ts = tq // n_merge
                stream = lax.broadcasted_iota(I32, (1, tq), 1) // ts
                first = tile
                for b in range(1, n_merge):
                    tile = jnp.where(stream == b, pltpu.roll(first, b * ts, 1), tile)
            bias_ref[h] = tile

    @pl.when(i == 0)
    def _():
        def body(j, c):
            for b in range(max(n_merge, 1)):
                vt = vb_ref[b, key_rows(j), :].astype(F32).T
                for g in range(B_KV_HEADS):
                    vt_ref[g, j, 0:HEAD_DIM, b * TILE:(b + 1) * TILE] = vt[g * HEAD_DIM:(g + 1) * HEAD_DIM].astype(BF16)
            for g in range(B_KV_HEADS):
                vt_ref[g, j, HEAD_DIM:2 * HEAD_DIM, :] = jnp.ones((HEAD_DIM, vt_ref.shape[-1]), BF16)
            return c

        lax.fori_loop(0, n_kt, body, 0)

    colq = lax.broadcasted_iota(I32, (1, tq), 1)
    rowk = lax.broadcasted_iota(I32, (TILE, 1), 0)

    if n_merge:
        width = n_merge * HEAD_DIM
        own = (lax.broadcasted_iota(I32, (tq, width), 0) // (tq // n_merge)
               == lax.broadcasted_iota(I32, (tq, width), 1) // HEAD_DIM)

        def block_diag(x):
            return jnp.where(own, jnp.concatenate([x.astype(F32)] * n_merge, axis=1), 0.0).astype(BF16)

        qi, qb = qi_ref[...], qb_ref[...]
        for h in range(B_HEADS):
            qis_ref[h] = block_diag(qi[:, h * IDX_DIM:(h + 1) * IDX_DIM])
            qbs_ref[h] = block_diag(qb[:, h * HEAD_DIM:(h + 1) * HEAD_DIM])
        side_by_side = lambda ref, j, cols: jnp.concatenate(
            [ref[b, key_rows(j), cols] for b in range(n_merge)], axis=1)
        wi_t = wi_ref[...].T
        lim = jnp.full((1, tq), valid_len, I32)
    else:
        qi = qi_ref[0]
        for h in range(IDX_HEADS):
            qis_ref[h * tq:(h + 1) * tq, :] = qi[:, h * IDX_DIM:(h + 1) * IDX_DIM]
        qb = qb_ref[0]
        for g in range(B_KV_HEADS):
            for hh in range(B_GROUP):
                h = g * B_GROUP + hh
                qbs_ref[g, hh * tq:(hh + 1) * tq, :] = qb[:, h * HEAD_DIM:(h + 1) * HEAD_DIM]
        wi_t = wi_ref[0].T
        lim = jnp.minimum(q0 + (colq // CHUNK + 1) * CHUNK, valid_len)

    def score_tile(j, masked):
        if n_merge:
            kt = side_by_side(ki_ref, j, slice(None))
            lg = jnp.concatenate([lax.dot_general(kt, qis_ref[h], _NT, preferred_element_type=F32)
                                  for h in range(IDX_HEADS)], axis=1)
        else:
            kt = ki_ref[0, key_rows(j), :]
            lg = lax.dot_general(kt, qis_ref[...], _NT, preferred_element_type=F32)
        sc = wi_t[0:1, :] * jnp.maximum(lg[:, 0:tq], 0.0)
        for h in range(1, IDX_HEADS):
            sc = sc + wi_t[h:h + 1, :] * jnp.maximum(lg[:, h * tq:(h + 1) * tq], 0.0)
        if masked:
            sc = jnp.where(j * TILE + rowk < lim, sc, -jnp.inf)
        sc_ref[j] = sc
        scb_ref[j] = sc.astype(BF16)
        rows8 = lambda hit: jnp.where(hit, 1, 0).reshape(TILE // 8, 8, tq).sum(axis=0)
        return rows8(sc > 0.0), rows8(sc >= 0.0)

    def score_pair(jj, c, masked):
        pos_a, nn_a = score_tile(2 * jj, masked)
        pos_b, nn_b = score_tile(2 * jj + 1, masked)
        return c[0] + pos_a + pos_b, c[1] + nn_a + nn_b

    n_open_pairs = (nk - 1) // 2
    counts = lax.fori_loop(0, n_open_pairs, functools.partial(score_pair, masked=False),
                           (jnp.zeros((8, tq), I32), jnp.zeros((8, tq), I32)))
    n_pos, n_nonneg = lax.fori_loop(n_open_pairs, (nk + 1) // 2, functools.partial(score_pair, masked=True), counts)

    def f32_of_key(k):
        return pltpu.bitcast(jnp.where(k < 0, k ^ 0x7FFFFFFF, k), F32)

    def bf16_of_key(k):
        bits = jnp.where(k < 0, k ^ 0x7FFF, k) & 0xFFFF
        return pltpu.bitcast(lax.shift_left(bits, 16), F32).astype(BF16)

    def count_bf16(cand):
        def body(jj, acc):
            for j in (2 * jj, 2 * jj + 1):
                ge = jnp.where(scb_ref[j] >= cand, jnp.int16(1), jnp.int16(0)).reshape(TILE // 16, 16, tq)
                part = ge[0]
                for r in range(1, TILE // 16):
                    part = part + ge[r]
                acc = acc + part
            return acc

        acc = lax.fori_loop(0, (nk + 1) // 2, body, jnp.zeros((16, tq), I16))
        return acc.astype(I32).sum(axis=0, keepdims=True)

    def count_f32(cand, strict=False):
        def body(j, acc):
            blk = sc_ref[j]
            hit = (blk > cand) if strict else (blk >= cand)
            return acc + jnp.where(hit, 1, 0).reshape(TILE // 8, 8, tq).sum(axis=0)

        return lax.fori_loop(0, nk, body, jnp.zeros((8, tq), I32)).sum(axis=0, keepdims=True)

    c_pos = n_pos.sum(axis=0, keepdims=True)
    zero_tie = (c_pos < TOPK_MAX) & (n_nonneg.sum(axis=0, keepdims=True) >= TOPK_MAX)
    skip1 = (colq >= q_valid) | zero_tie

    def level1(it, t):
        cand = t + lax.shift_left(jnp.int32(1), 15 - it)
        return jnp.where(count_bf16(bf16_of_key(cand)) >= TOPK_MAX, cand, t)

    t1 = lax.fori_loop(0, 16, level1, jnp.full((1, tq), MIN16, I32))
    settled1 = skip1 | (t1 <= HI_NEG_INF)

    def level2(st):
        lo, hi, thr_key, done = st
        live = (done == 0) & (hi - lo > 1)
        mid = lo + lax.shift_right_arithmetic(hi - lo, 1)
        c = count_f32(f32_of_key(mid))
        hit = live & (c == TOPK_MAX)
        return (jnp.where(live & (c >= TOPK_MAX), mid, lo), jnp.where(live & (c < TOPK_MAX), mid, hi),
                jnp.where(hit, mid, thr_key), jnp.where(hit, 1, done))

    def n_live(st):
        lo, hi, _, done = st
        return jnp.sum(jnp.where((done == 0) & (hi - lo > 1), 1, 0))

    def level2_pair(carry):
        st = level2(level2(carry[0]))
        return st, n_live(st)

    def key32_of_key16(k):
        return lax.shift_left(k, 16) | jnp.where(k < 0, 0xFFFF, 0)

    key_t1 = key32_of_key16(t1)
    st0 = (key_t1 - 0x8000, key32_of_key16(t1 + 1), key_t1, jnp.where(settled1, 1, 0))
    (lo, _, thr_key, done2), _ = lax.while_loop(lambda carry: carry[1] > 0, level2_pair, (st0, n_live(st0)))
    open2 = done2 == 0
    thr = jnp.where(open2, f32_of_key(lo), f32_of_key(thr_key))
    thr = jnp.where(zero_tie, 0.0, thr)
    thr = jnp.where(settled1 & jnp.logical_not(zero_tie), -jnp.inf, thr)
    thr = jnp.maximum(thr, float(jnp.finfo(F32).min))

    keep_ref[...] = jnp.where(zero_tie, TOPK_MAX - c_pos, 2 ** 30).astype(F32)

    @pl.when(jnp.sum(jnp.where(open2, 1, 0)) > 0)
    def _():
        above = count_f32(thr, strict=True)
        keep_ref[...] = jnp.where(open2, (TOPK_MAX - above).astype(F32), keep_ref[...])

    @pl.when(jnp.sum(jnp.where(open2 | zero_tie, 1, 0)) > 0)
    def _():
        keep = keep_ref[...]
        lower = (lax.broadcasted_iota(I32, (TILE, TILE), 0) > lax.broadcasted_iota(I32, (TILE, TILE), 1))
        lower = jnp.where(lower, 1.0, 0.0).astype(BF16)

        def body(jj, run):
            for j in (2 * jj, 2 * jj + 1):
                blk = sc_ref[j]
                eq = blk == thr
                eq_f = jnp.where(eq, 1.0, 0.0)
                before = jnp.dot(lower, eq_f.astype(BF16), preferred_element_type=F32)
                sc_ref[j] = jnp.where(eq & (run + before >= keep), -jnp.inf, blk)
                run = run + eq_f.reshape(TILE // 8, 8, tq).sum(axis=0).sum(axis=0, keepdims=True)
            return run

        lax.fori_loop(0, (nk + 1) // 2, body, jnp.zeros((1, tq), F32))

    for g in range(B_KV_HEADS):
        m_ref[g] = jnp.full((1, B_GROUP * tq), NEG, F32)
        acc_ref[g] = jnp.zeros((2 * HEAD_DIM, B_GROUP * tq), F32)

    def attend(tiles):
        sels = [sc_ref[j] >= thr for j, _ in tiles]
        part_max = []
        for g in range(B_KV_HEADS):
            pm = None
            for t, ((j, near), sel) in enumerate(zip(tiles, sels)):
                if n_merge:
                    kt = side_by_side(kb_ref, j, slice(g * HEAD_DIM, (g + 1) * HEAD_DIM))
                    st = jnp.concatenate([lax.dot_general(kt, qbs_ref[g * B_GROUP + hh], _NT,
                                                          preferred_element_type=F32)
                                          for hh in range(B_GROUP)], axis=1)
                else:
                    kt = kb_ref[0, key_rows(j), g * HEAD_DIM:(g + 1) * HEAD_DIM]
                    st = lax.dot_general(kt, qbs_ref[g], _NT, preferred_element_type=F32)
                parts = []
                for hh in range(B_GROUP):
                    s_h = st[:, hh * tq:(hh + 1) * tq]
                    if near is not None:
                        s_h = s_h + bias_ref[g * B_GROUP + hh, near * TILE:(near + 1) * TILE, :]
                    parts.append(jnp.where(sel, s_h, -jnp.inf))
                s = jnp.concatenate(parts, axis=1)
                s_ref[g * len(tiles) + t] = s
                m8 = s.reshape(TILE // 8, 8, B_GROUP * tq).max(axis=0)
                pm = m8 if pm is None else jnp.maximum(pm, m8)
            part_max.append(pm)
        for g in range(B_KV_HEADS):
            m_old = m_ref[g]
            m_new = jnp.maximum(m_old, part_max[g].max(axis=0, keepdims=True))
            pv = None
            for t, (j, _) in enumerate(tiles):
                p = jnp.exp(s_ref[g * len(tiles) + t] - m_new)
                if n_merge:
                    lane_stream = colq // (tq // n_merge)
                    d = jnp.concatenate(
                        [jnp.dot(vt_ref[g, j],
                                 jnp.concatenate([jnp.where(lane_stream == b, p[:, hh * tq:(hh + 1) * tq], 0.0)
                                                  for b in range(n_merge)], axis=0).astype(BF16),
                                 preferred_element_type=F32) for hh in range(B_GROUP)], axis=1)
                else:
                    d = jnp.dot(vt_ref[g, j], p.astype(BF16), preferred_element_type=F32)
                pv = d if pv is None else pv + d
            acc_ref[g] = jnp.exp(m_old - m_new) * acc_ref[g] + pv
            m_ref[g] = m_new

    n_far = jnp.maximum(nk - 2, 0)

    def far_step(jj, c):
        attend([(FAR_STEP * jj + t, None) for t in range(FAR_STEP)])
        return c

    lax.fori_loop(0, n_far // FAR_STEP, far_step, 0)
    rem = n_far % FAR_STEP

    @pl.when(rem >= 2)
    def _():
        attend([(n_far - rem, None), (n_far - rem + 1, None)])

    @pl.when(rem % 2 == 1)
    def _():
        attend([(n_far - 1, None)])

    @pl.when(nk >= 2)
    def _():
        attend([(nk - 2, 0), (nk - 1, 1)])

    @pl.when(nk < 2)
    def _():
        attend([(nk - 1, 1)])

    outs = []
    for g in range(B_KV_HEADS):
        for hh in range(B_GROUP):
            blk = acc_ref[g, :, hh * tq:(hh + 1) * tq].T
            outs.append(blk[:, 0:HEAD_DIM] / blk[:, HEAD_DIM:HEAD_DIM + 1])
    o_ref[0] = jnp.concatenate(outs, axis=-1).astype(BF16)


def _dsa_attn(qi, wi, qb, ki, kb, vb, g_row, off, valid_len, q_valid, tq):
    b, t_q = qi.shape[:2]
    tk = ki.shape[1]
    nq, n_kt = t_q // tq, tk // TILE
    assert tq == TILE or nq == 1
    assert n_kt % 2 == 0 and tk == n_kt * TILE
    qspec = lambda n: pl.BlockSpec((1, tq, n), lambda bb, i: (bb, i, 0))
    kspec = lambda n: pl.BlockSpec((1, tk, n), lambda bb, i: (bb, 0, 0))
    in_specs = [pl.BlockSpec((B_HEADS, DSA_ROLL), lambda bb, i: (0, 0)),
                qspec(512), qspec(128), qspec(512), kspec(IDX_DIM), kspec(128), kspec(128)]
    return _dsa_call((g_row, qi, wi, qb, ki, kb, vb), in_specs, (b, nq), t_q, tq, n_kt,
                     dict(off=off, valid_len=valid_len, q_valid=q_valid, n_merge=0))


def _dsa_attn_merged(qi, wi, qb, ki, kb, vb, g_row, off, valid_len):
    s, ts = qi.shape[:2]
    tk = ki.shape[1]
    tq, n_kt = s * ts, tk // TILE
    assert tq % LANES == 0 and ts <= CHUNK and n_kt % 2 == 0 and tk == n_kt * TILE
    flat = lambda a: a.reshape(tq, a.shape[-1])
    operands = (g_row, flat(qi), flat(wi), flat(qb), ki, kb, vb)
    whole = lambda a: pl.BlockSpec(a.shape, lambda bb, i: (0,) * a.ndim, pipeline_mode=pl.Buffered(1))
    out = _dsa_call(operands, [whole(a) for a in operands], (1, 1), tq, tq, n_kt,
                    dict(off=off, valid_len=valid_len, q_valid=tq, n_merge=s))
    return out.reshape(s, ts, B_WIDTH)


def _dsa_call(operands, in_specs, grid, t_q, tq, n_kt, static):
    n_merge = static["n_merge"]
    assert static["valid_len"] >= (static["off"] + grid[1] - 1) * TILE
    if n_merge:
        q_scratch = [pltpu.VMEM((IDX_HEADS, tq, n_merge * IDX_DIM), BF16),
                     pltpu.VMEM((B_HEADS, tq, n_merge * HEAD_DIM), BF16)]
    else:
        q_scratch = [pltpu.VMEM((IDX_HEADS * tq, IDX_DIM), BF16),
                     pltpu.VMEM((B_KV_HEADS, B_GROUP * tq, HEAD_DIM), BF16)]
    return pl.pallas_call(
        functools.partial(_dsa_kernel, n_kt=n_kt, tq=tq, **static),
        grid=grid,
        in_specs=in_specs,
        out_specs=pl.BlockSpec((1, tq, B_WIDTH), lambda bb, i: (bb, i, 0)),
        out_shape=jax.ShapeDtypeStruct((grid[0], t_q, B_WIDTH), BF16),
        scratch_shapes=[pltpu.VMEM((n_kt, TILE, tq), F32),
                        pltpu.VMEM((n_kt, TILE, tq), BF16),
                        pltpu.VMEM((1, tq), F32),
                        pltpu.VMEM((B_KV_HEADS, n_kt, 2 * HEAD_DIM, max(n_merge, 1) * TILE), BF16),
                        *q_scratch,
                        pltpu.VMEM((B_KV_HEADS, 1, B_GROUP * tq), F32),
                        pltpu.VMEM((B_KV_HEADS, 2 * HEAD_DIM, B_GROUP * tq), F32),
                        pltpu.VMEM((B_HEADS, 2 * TILE, tq), F32),
                        pltpu.VMEM((B_KV_HEADS * FAR_STEP, TILE, B_GROUP * tq), F32)],
        compiler_params=_params("arbitrary", "arbitrary"),
        name="dsa_attn",
    )(*operands)


def _mem_kv_kernel(m_ref, wk_ref, wv_ref, k_o, v_o, kb_o, vb_o):
    mb = m_ref[...].astype(BF16)
    k = jnp.dot(mb, wk_ref[...], preferred_element_type=F32)
    v = jnp.dot(mb, wv_ref[...], preferred_element_type=F32)
    k_o[...] = k
    v_o[...] = v
    kb_o[...] = k.astype(BF16)
    vb_o[...] = v.astype(BF16)


def _mem_kv(mem2d, wk, wv):
    r = mem2d.shape[0]
    tm = MEM_LEN
    row = lambda n: pl.BlockSpec((tm, n), lambda i: (i, 0))
    const = lambda s: pl.BlockSpec(s, lambda i: (0, 0))
    sds = jax.ShapeDtypeStruct
    return pl.pallas_call(
        _mem_kv_kernel,
        grid=(r // tm,),
        in_specs=[row(D_MODEL), const((D_MODEL, MEM_WIDTH)), const((D_MODEL, MEM_WIDTH))],
        out_specs=[row(MEM_WIDTH)] * 4,
        out_shape=[sds((r, MEM_WIDTH), F32), sds((r, MEM_WIDTH), F32),
                   sds((r, MEM_WIDTH), BF16), sds((r, MEM_WIDTH), BF16)],
        compiler_params=_params("arbitrary"),
        name="mem_kv",
    )(mem2d, wk, wv)


FF_CHUNK = 256


def _tail_kernel(x_ref, oa_ref, ob_ref, mk_ref, mv_ref, hist_ref,
                 wo_ref, g1_ref, b1_ref, wq_ref, wmo_ref, g2_ref, b2_ref,
                 wu_ref, wc_ref, bc_ref, wd_ref, g3_ref, b3_ref,
                 o_ref, tail_ref, carry_ref, act_ref, *, tiles_per_batch, seg):
    i = pl.program_id(0)
    tm = x_ref.shape[0]
    nseg = tm // seg

    mix = jnp.concatenate([oa_ref[...], ob_ref[...]], axis=-1)
    h = _layer_norm(ALPHA * x_ref[...] + jnp.dot(mix, wo_ref[...], preferred_element_type=F32),
                    g1_ref[...], b1_ref[...])

    q = jnp.dot(h.astype(BF16), wq_ref[...], preferred_element_type=F32).astype(BF16)
    segs = []
    for s in range(nseg):
        qs = q[s * seg:(s + 1) * seg]
        heads = []
        for hd in range(MEM_HEADS):
            sl = slice(hd * MEM_HEAD_DIM, (hd + 1) * MEM_HEAD_DIM)
            sc = lax.dot_general(qs[:, sl], mk_ref[s, :, sl], _NT, preferred_element_type=F32) * MEM_HEAD_DIM ** -0.5
            p = jnp.exp(sc - sc.max(-1, keepdims=True))
            l = p.sum(-1, keepdims=True)
            heads.append(jnp.dot(p.astype(BF16), mv_ref[s, :, sl], preferred_element_type=F32) / l)
        segs.append(jnp.concatenate(heads, axis=-1))
    att = jnp.concatenate(segs, axis=0).astype(BF16)
    h = _layer_norm(ALPHA * h + jnp.dot(att, wmo_ref[...], preferred_element_type=F32), g2_ref[...], b2_ref[...])

    hb = h.astype(BF16)
    row = lax.broadcasted_iota(I32, (tm, 1), 0)
    first = (i % tiles_per_batch) == 0
    for c in range(D_FF // FF_CHUNK):
        cs = slice(c * FF_CHUNK, (c + 1) * FF_CHUNK)
        u = jnp.dot(hb, wu_ref[:, cs], preferred_element_type=F32)
        gt = jnp.dot(hb, wu_ref[:, D_FF + c * FF_CHUNK:D_FF + (c + 1) * FF_CHUNK], preferred_element_type=F32)
        p1 = pltpu.roll(gt, 1, 0)
        p2 = pltpu.roll(gt, 2, 0)
        for s in range(nseg):
            hist = hist_ref[s, :, cs]
            if tiles_per_batch > 1:
                hist = jnp.where(first, hist, carry_ref[:, cs])
            p1 = jnp.where(row == s * seg, hist[7:8, :], p1)
            p2 = jnp.where(row == s * seg, hist[6:7, :], p2)
            p2 = jnp.where(row == s * seg + 1, hist[7:8, :], p2)
        gc = bc_ref[:, cs] + ((wc_ref[0:1, cs] * p2 + wc_ref[1:2, cs] * p1) + wc_ref[2:3, cs] * gt)
        act_ref[:, cs] = (u * jax.nn.gelu(gc)).astype(BF16)
        for s in range(nseg):
            tail_ref[s, :, cs] = gt[(s + 1) * seg - 8:(s + 1) * seg, :]
        carry_ref[:, cs] = gt[tm - 8:tm, :]
    f = jnp.dot(act_ref[...], wd_ref[...], preferred_element_type=F32)
    o_ref[...] = _layer_norm(ALPHA * h + f, g3_ref[...], b3_ref[...])


def _layer_tail(x2d, oa, ob, mk, mv, hist, weights, tm, tiles_per_batch, seg):
    r = x2d.shape[0]
    nseg = tm // seg
    n_stream = r // (tm * tiles_per_batch) * nseg
    row = lambda n: pl.BlockSpec((tm, n), lambda i: (i, 0))
    per_stream = lambda a, b: pl.BlockSpec((nseg, a, b), lambda i: (i // tiles_per_batch, 0, 0))
    const = lambda a: pl.BlockSpec(a.shape, lambda i: (0,) * a.ndim, pipeline_mode=pl.Buffered(1))
    return pl.pallas_call(
        functools.partial(_tail_kernel, tiles_per_batch=tiles_per_batch, seg=seg),
        grid=(r // tm,),
        in_specs=[row(D_MODEL), row(A_WIDTH), row(B_WIDTH), per_stream(MEM_LEN, MEM_WIDTH),
                  per_stream(MEM_LEN, MEM_WIDTH), per_stream(8, D_FF)] + [const(w) for w in weights],
        out_specs=[row(D_MODEL), per_stream(8, D_FF)],
        out_shape=[jax.ShapeDtypeStruct((r, D_MODEL), F32), jax.ShapeDtypeStruct((n_stream, 8, D_FF), F32)],
        scratch_shapes=[pltpu.VMEM((8, D_FF), F32), pltpu.VMEM((tm, D_FF), BF16)],
        compiler_params=_params("arbitrary"),
        name="layer_tail",
    )(x2d, oa, ob, mk, mv, hist, *weights)


def _pad_rows(a, n):
    return jnp.pad(a, ((0, 0), (0, n - a.shape[1])) + ((0, 0),) * (a.ndim - 2))


def _hist8(g_hist):
    return jnp.pad(g_hist, ((0, 0), (8 - g_hist.shape[1], 0), (0, 0)))


def kernel(x_prompt, x_sample, cache_a_k, cache_a_v, cache_b_k, cache_b_v, cache_b_kidx, cache_mem_k, cache_mem_v, state_ffn_conv, mem_prompt, w_in, a_rel_bias, t5_bias, w_o, ln1_g, ln1_b, w_mq, w_mk, w_mv, w_mo, ln2_g, ln2_b, w_up, w_conv, b_conv, w_down, ln3_g, ln3_b):
    bp, tp = x_prompt.shape[:2]
    bs, ts = x_sample.shape[:2]
    l = 0
    vec = lambda a: a[l].reshape(1, -1)
    w_in_p = _prep_w_in(w_in[l])
    w_o_b = w_o[l].astype(BF16)
    w_mq_b, w_mk_b, w_mv_b, w_mo_b = (w[l].astype(BF16) for w in (w_mq, w_mk, w_mv, w_mo))
    w_up_b, w_down_b = w_up[l].astype(BF16), w_down[l].astype(BF16)
    band_row = _band_bias_row(a_rel_bias[l])
    dsa_row = _dsa_bias_row(t5_bias)
    tail_w = (w_o_b, vec(ln1_g), vec(ln1_b), w_mq_b, w_mo_b, vec(ln2_g), vec(ln2_b),
              w_up_b, w_conv[l], vec(b_conv), w_down_b, vec(ln3_g), vec(ln3_b))

    tm = 512
    a_keep = min(N_PREV_CHUNKS * CHUNK, tp)
    (qa, ka, va, qb, qi, kb, vb, ki, kb_b, vb_b, ki_b, wi, ka_tail, va_tail) = _in_proj(
        x_prompt.reshape(bp * tp, D_MODEL), w_in_p, tm, tp // tm)
    r3 = lambda a: a.reshape(bp, tp, a.shape[-1])
    oa = _band_attn(r3(qa), r3(ka), r3(va), band_row, 0, tp)
    ob = _dsa_attn(r3(qi), r3(wi), r3(qb), r3(ki_b), r3(kb_b), r3(vb_b), dsa_row, 0, tp, TILE, TILE)
    mk, mv, mk_b, mv_b = _mem_kv(mem_prompt.reshape(bp * MEM_LEN, D_MODEL), w_mk_b, w_mv_b)
    xp, p_tail = _layer_tail(x_prompt.reshape(bp * tp, D_MODEL), oa.reshape(bp * tp, A_WIDTH),
                             ob.reshape(bp * tp, B_WIDTH), mk_b.reshape(bp, MEM_LEN, MEM_WIDTH),
                             mv_b.reshape(bp, MEM_LEN, MEM_WIDTH), jnp.zeros((bp, 8, D_FF), F32), tail_w,
                             tm, tp // tm, tm)
    prompt_state = (
        ka_tail.reshape(bp, tm, A_HEADS, HEAD_DIM)[:, tm - a_keep:][None],
        va_tail.reshape(bp, tm, A_HEADS, HEAD_DIM)[:, tm - a_keep:][None],
        kb.reshape(1, bp, tp, B_KV_HEADS, HEAD_DIM), vb.reshape(1, bp, tp, B_KV_HEADS, HEAD_DIM),
        ki.reshape(1, bp, tp, IDX_DIM),
        mk.reshape(1, bp, MEM_LEN, MEM_HEADS, MEM_HEAD_DIM), mv.reshape(1, bp, MEM_LEN, MEM_HEADS, MEM_HEAD_DIM),
        p_tail[:, 8 - (CONV_W - 1):][None])

    rs = bs * ts
    (qa, ka, va, qb, qi, kb, vb, ki, kb_b, vb_b, ki_b, wi, ka_new, va_new) = _in_proj(
        x_sample.reshape(rs, D_MODEL), w_in_p, rs, 1)
    s3 = lambda a: a.reshape(bs, ts, -1)
    qpad = lambda a, n: _pad_rows(s3(a), n)

    past_a = cache_a_k.shape[2]
    n_a = past_a + ts
    t_a = -(-n_a // TILE) * TILE
    seq_a = lambda cache, new: _pad_rows(
        jnp.concatenate([cache[l].reshape(bs, past_a, A_WIDTH), s3(new)], axis=1), t_a).astype(BF16)
    oa = _band_attn(qpad(qa, TILE), seq_a(cache_a_k, ka_new), seq_a(cache_a_v, va_new), band_row,
                    past_a // TILE, n_a)

    past_b = cache_b_k.shape[2]
    n_b = past_b + ts
    t_b = -(-n_b // (2 * TILE)) * 2 * TILE
    seq_b = lambda cache, new: _pad_rows(
        jnp.concatenate([cache[l].reshape(bs, past_b, -1), s3(new)], axis=1), t_b).astype(BF16)
    ob = _dsa_attn_merged(s3(qi), s3(wi), s3(qb), seq_b(cache_b_kidx, ki), seq_b(cache_b_k, kb),
                          seq_b(cache_b_v, vb), dsa_row, past_b // TILE, n_b)

    xs, s_tail = _layer_tail(x_sample.reshape(rs, D_MODEL), oa[:, :ts].reshape(rs, A_WIDTH),
                             ob[:, :ts].reshape(rs, B_WIDTH),
                             cache_mem_k[l].reshape(bs, MEM_LEN, MEM_WIDTH).astype(BF16),
                             cache_mem_v[l].reshape(bs, MEM_LEN, MEM_WIDTH).astype(BF16),
                             _hist8(state_ffn_conv[l]), tail_w, rs, 1, ts)
    sample_state = (
        ka_new.reshape(1, bs, ts, A_HEADS, HEAD_DIM), va_new.reshape(1, bs, ts, A_HEADS, HEAD_DIM),
        kb.reshape(1, bs, ts, B_KV_HEADS, HEAD_DIM), vb.reshape(1, bs, ts, B_KV_HEADS, HEAD_DIM),
        ki.reshape(1, bs, ts, IDX_DIM), s_tail[:, 8 - (CONV_W - 1):][None])

    return (xp.reshape(bp, tp, D_MODEL), xs.reshape(bs, ts, D_MODEL)) + prompt_state + sample_state
```

```python
import functools
import math

import jax
import jax.numpy as jnp
from jax import lax
from jax.experimental import pallas as pl
from jax.experimental.pallas import tpu as pltpu

F32 = jnp.float32
BF16 = jnp.bfloat16
I32 = jnp.int32
I16 = jnp.int16

D_MODEL = 1024
CHUNK = 64
N_PREV_CHUNKS = 8
HEAD_DIM = 64
A_HEADS = 8
A_WIDTH = A_HEADS * HEAD_DIM
A_MAX_REL = 64
B_HEADS = 8
B_KV_HEADS = 2
B_GROUP = B_HEADS // B_KV_HEADS
B_WIDTH = B_HEADS * HEAD_DIM
B_KV_WIDTH = B_KV_HEADS * HEAD_DIM
IDX_HEADS = 8
IDX_DIM = 64
TOPK_MAX = 256
N_BUCKETS = 32
T5_MAX_DIST = 128
MEM_LEN = 256
MEM_HEADS = 4
MEM_HEAD_DIM = 128
MEM_WIDTH = MEM_HEADS * MEM_HEAD_DIM
D_FF = 2816
CONV_W = 3
IN_SIZES = (A_WIDTH, A_WIDTH, A_WIDTH, B_WIDTH, B_KV_WIDTH, B_KV_WIDTH, IDX_HEADS * IDX_DIM, IDX_DIM, IDX_HEADS)
DEPTH = 1
ALPHA = (2 * DEPTH) ** 0.25
LN_EPS = 1e-5
ATTN_SCALE = HEAD_DIM ** -0.5
NEG = -1e30

LANES = 128
TILE = 256
BAND_TILES = 1 + (N_PREV_CHUNKS * CHUNK) // TILE
VMEM_LIMIT = 56 * 1024 * 1024

_C_QA, _C_KA, _C_VA, _C_QB = 0, 512, 1024, 1536
_C_KB, _C_VB, _C_QI, _C_KI, _C_WI = 2048, 2176, 2304, 2816, 2944
IN_PAD = 3072

MIN16 = -32768
HI_NEG_INF = -32641

_NT = (((1,), (1,)), ((), ()))


def _params(*sem):
    return pltpu.CompilerParams(dimension_semantics=sem, vmem_limit_bytes=VMEM_LIMIT)


def _layer_norm(z, g, b):
    mu = jnp.mean(z, axis=-1, keepdims=True)
    d = z - mu
    var = jnp.mean(d * d, axis=-1, keepdims=True)
    return d * lax.rsqrt(var + LN_EPS) * g + b


def _toeplitz(g_row, rows, width):
    return pltpu.roll(jnp.broadcast_to(g_row, (rows, width)), 0, 1, stride=1, stride_axis=0)


def _in_proj_kernel(x_ref, w_ref, qa_o, ka_o, va_o, qb_o, qi_o, kb_o, vb_o, ki_o, kbb_o, vbb_o, kib_o,
                    wi_o, kat_o, vat_o, *, tiles_per_batch, wi_scale):
    i = pl.program_id(0)
    xb = x_ref[...].astype(BF16)

    def mm(c0, n):
        return jnp.dot(xb, w_ref[:, c0:c0 + n], preferred_element_type=F32)

    kb = mm(_C_KB, 128)
    vb = mm(_C_VB, 128)
    for g in range(B_KV_HEADS):
        kb_o[:, g, :] = kb[:, g * HEAD_DIM:(g + 1) * HEAD_DIM]
        vb_o[:, g, :] = vb[:, g * HEAD_DIM:(g + 1) * HEAD_DIM]
    kbb_o[...] = kb.astype(BF16)
    vbb_o[...] = vb.astype(BF16)
    ki = mm(_C_KI, 128)[:, :IDX_DIM]
    ki_o[...] = ki
    kib_o[...] = ki.astype(BF16)
    wi_o[...] = mm(_C_WI, 128) * wi_scale
    qa_o[...] = mm(_C_QA, 512).astype(BF16)
    ka = mm(_C_KA, 512)
    va = mm(_C_VA, 512)
    ka_o[...] = ka.astype(BF16)
    va_o[...] = va.astype(BF16)
    qb_o[...] = mm(_C_QB, 512).astype(BF16)
    qi_o[...] = mm(_C_QI, 512).astype(BF16)

    @pl.when(i % tiles_per_batch == tiles_per_batch - 1)
    def _():
        for h in range(A_HEADS):
            kat_o[:, h, :] = ka[:, h * HEAD_DIM:(h + 1) * HEAD_DIM]
            vat_o[:, h, :] = va[:, h * HEAD_DIM:(h + 1) * HEAD_DIM]


def _prep_w_in(w):
    parts, off = [], 0
    for n in IN_SIZES:
        parts.append(w[:, off:off + n])
        off += n
    qa, ka, va, qb, kb, vb, qi, ki, wi = parts
    pad = lambda a, n: jnp.pad(a, ((0, 0), (0, n - a.shape[1])))
    cols = [qa * ATTN_SCALE, ka, va, qb * ATTN_SCALE, kb, vb, qi * IDX_DIM ** -0.5, pad(ki, 128), pad(wi, 128)]
    return jnp.concatenate(cols, axis=1).astype(BF16)


def _in_proj(x2d, w_pad, tm, tiles_per_batch):
    r = x2d.shape[0]
    n_tiles = r // tm
    n_batch = n_tiles // tiles_per_batch
    row = lambda n: pl.BlockSpec((tm, n), lambda i: (i, 0))
    tail = pl.BlockSpec((tm, A_HEADS, HEAD_DIM), lambda i: (i // tiles_per_batch, 0, 0))
    sds = jax.ShapeDtypeStruct
    kv_state = pl.BlockSpec((tm, B_KV_HEADS, HEAD_DIM), lambda i: (i, 0, 0))
    kv_shape = sds((r, B_KV_HEADS, HEAD_DIM), F32)
    out_shape = [sds((r, 512), BF16)] * 5 + [kv_shape, kv_shape, sds((r, IDX_DIM), F32),
                                              sds((r, 128), BF16), sds((r, 128), BF16), sds((r, IDX_DIM), BF16),
                                              sds((r, 128), F32),
                                              sds((n_batch * tm, A_HEADS, HEAD_DIM), F32),
                                              sds((n_batch * tm, A_HEADS, HEAD_DIM), F32)]
    out_specs = [row(512)] * 5 + [kv_state, kv_state, row(IDX_DIM), row(128), row(128), row(IDX_DIM), row(128),
                                  tail, tail]
    return pl.pallas_call(
        functools.partial(_in_proj_kernel, tiles_per_batch=tiles_per_batch, wi_scale=IDX_HEADS ** -0.5),
        grid=(n_tiles,),
        in_specs=[pl.BlockSpec((tm, D_MODEL), lambda i: (i, 0)),
                  pl.BlockSpec((D_MODEL, IN_PAD), lambda i: (0, 0))],
        out_specs=out_specs,
        out_shape=out_shape,
        compiler_params=_params("arbitrary"),
        name="in_proj",
    )(x2d, w_pad)


BAND_COLS = BAND_TILES * TILE
BAND_ROLL = BAND_COLS + TILE
BAND_SLAB = 4


def _band_bias_row(table):
    idx = jnp.arange(BAND_ROLL)
    d = jnp.where(idx < TILE, idx, idx - BAND_ROLL)
    rel = (BAND_TILES - 1) * TILE + d
    return table[jnp.clip(rel, -A_MAX_REL, A_MAX_REL) + A_MAX_REL].T.astype(F32)


def _band_kernel(g_ref, q_ref, k0, k1, k2, v0, v1, v2, o_ref, bias_ref, vt_ref, s_ref, *, off, valid_len,
                 n_state):
    i = pl.program_id(1)
    kt = i + off
    krefs, vrefs = (k0, k1, k2), (v0, v1, v2)

    def state_heads(ref, heads):
        return jnp.concatenate([ref[0, 0, :, h, :] for h in heads], axis=1)

    @pl.when((pl.program_id(0) == 0) & (i == 0))
    def _():
        c = lax.broadcasted_iota(I32, (BAND_COLS, TILE), 0) // CHUNK
        r = lax.broadcasted_iota(I32, (BAND_COLS, TILE), 1) // CHUNK
        ok = (c >= r) & (c <= r + N_PREV_CHUNKS)
        for h in range(A_HEADS):
            bias_ref[h] = jnp.where(ok, _toeplitz(g_ref[h:h + 1, :], BAND_COLS, BAND_ROLL)[:, :TILE], NEG)
        ones = jnp.ones((HEAD_DIM, TILE), BF16)
        for s in range(BAND_TILES):
            for h in range(A_HEADS):
                vt_ref[s, h, HEAD_DIM:2 * HEAD_DIM, :] = ones

    def put(slot, j):
        if j < n_state:
            for hp in range(A_HEADS // 2):
                vt = state_heads(vrefs[j], (2 * hp, 2 * hp + 1)).T
                vt_ref[slot, 2 * hp, 0:HEAD_DIM, :] = vt[0:HEAD_DIM].astype(BF16)
                vt_ref[slot, 2 * hp + 1, 0:HEAD_DIM, :] = vt[HEAD_DIM:2 * HEAD_DIM].astype(BF16)
        else:
            vt = vrefs[j][0].astype(F32).T
            for h in range(A_HEADS):
                vt_ref[slot, h, 0:HEAD_DIM, :] = vt[h * HEAD_DIM:(h + 1) * HEAD_DIM].astype(BF16)

    slots = [(kt + 1 + j) % BAND_TILES for j in range(BAND_TILES)]

    @pl.when(i == 0)
    def _():
        for j in range(BAND_TILES - 1):
            put(slots[j], j)

    put(slots[BAND_TILES - 1], BAND_TILES - 1)

    base = (kt - (BAND_TILES - 1)) * TILE

    n_slab = BAND_SLAB
    slab_w = n_slab * HEAD_DIM
    lane_head = lax.broadcasted_iota(I32, (TILE, slab_w), 1) // HEAD_DIM

    def attend(masked):
        outs = []
        for g in range(A_HEADS // n_slab):
            gs = slice(g * slab_w, (g + 1) * slab_w)
            q_slab = q_ref[0, :, gs].astype(F32)
            q_bd = jnp.concatenate([jnp.where(lane_head == hh, q_slab, 0.0) for hh in range(n_slab)],
                                   axis=0).astype(BF16)
            pm = None
            for j in range(BAND_TILES):
                if j < n_state:
                    k_slab = state_heads(krefs[j], range(g * n_slab, (g + 1) * n_slab)).astype(BF16)
                else:
                    k_slab = krefs[j][0, :, gs]
                st = lax.dot_general(k_slab, q_bd, _NT, preferred_element_type=F32)
                parts = [st[:, hh * TILE:(hh + 1) * TILE] + bias_ref[g * n_slab + hh, j * TILE:(j + 1) * TILE, :]
                         for hh in range(n_slab)]
                if masked:
                    kpos = base + j * TILE + lax.broadcasted_iota(I32, (TILE, TILE), 0)
                    ok = (kpos >= 0) & (kpos < valid_len)
                    parts = [jnp.where(ok, x, NEG) for x in parts]
                sj = jnp.concatenate(parts, axis=1)
                s_ref[g * BAND_TILES + j] = sj
                m8 = sj.reshape(TILE // 8, 8, n_slab * TILE).max(axis=0)
                pm = m8 if pm is None else jnp.maximum(pm, m8)
            m = pm.max(axis=0, keepdims=True)
            p = [jnp.exp(s_ref[g * BAND_TILES + j] - m).astype(BF16) for j in range(BAND_TILES)]
            for hh in range(n_slab):
                acc = None
                for j in range(BAND_TILES):
                    d = jnp.dot(vt_ref[slots[j], g * n_slab + hh], p[j][:, hh * TILE:(hh + 1) * TILE],
                                preferred_element_type=F32)
                    acc = d if acc is None else acc + d
                blk = acc.T
                outs.append(blk[:, 0:HEAD_DIM] / blk[:, HEAD_DIM:HEAD_DIM + 1])
        o_ref[0] = jnp.concatenate(outs, axis=-1).astype(BF16)

    needs_mask = (base < 0) | (base + BAND_COLS > valid_len)

    @pl.when(needs_mask)
    def _():
        attend(True)

    @pl.when(jnp.logical_not(needs_mask))
    def _():
        attend(False)


def _band_attn(q, k, v, g_row, off, valid_len, state_k=None, state_v=None):
    b, tq = q.shape[:2]
    nq = tq // TILE
    qspec = pl.BlockSpec((1, TILE, A_WIDTH), lambda bb, i: (bb, i, 0))
    if state_k is None:
        n_state = 0
        kspec = lambda d: pl.BlockSpec((1, TILE, A_WIDTH), lambda bb, i: (bb, jnp.maximum(i + off - d, 0), 0))
        k_ops, v_ops = (k, k, k), (v, v, v)
        kv_specs = [kspec(2), kspec(1), kspec(0)] * 2
    else:
        n_state = BAND_TILES - 1
        assert nq == 1 and off == n_state and state_k.shape[2] == n_state * TILE
        sspec = lambda t: pl.BlockSpec((1, 1, TILE, A_HEADS, HEAD_DIM), lambda bb, i: (0, bb, t, 0, 0))
        nspec = pl.BlockSpec((1, TILE, A_WIDTH), lambda bb, i: (bb, 0, 0))
        k_ops, v_ops = (state_k, state_k, k), (state_v, state_v, v)
        kv_specs = [sspec(0), sspec(1), nspec] * 2
    return pl.pallas_call(
        functools.partial(_band_kernel, off=off, valid_len=valid_len, n_state=n_state),
        grid=(b, nq),
        in_specs=[pl.BlockSpec((A_HEADS, BAND_ROLL), lambda bb, i: (0, 0)), qspec] + kv_specs,
        out_specs=pl.BlockSpec((1, TILE, A_WIDTH), lambda bb, i: (bb, i, 0)),
        out_shape=jax.ShapeDtypeStruct((b, tq, A_WIDTH), BF16),
        scratch_shapes=[pltpu.VMEM((A_HEADS, BAND_COLS, TILE), F32),
                        pltpu.VMEM((BAND_TILES, A_HEADS, 2 * HEAD_DIM, TILE), BF16),
                        pltpu.VMEM((A_HEADS // BAND_SLAB * BAND_TILES, TILE, BAND_SLAB * TILE), F32)],
        compiler_params=_params("arbitrary", "arbitrary"),
        name="band_attn",
    )(g_row, q, *k_ops, *v_ops)


DSA_ROLL = 3 * TILE
FAR_STEP = 4


def _t5_bucket(rel):
    half = N_BUCKETS // 2
    max_exact = half // 2
    n = jnp.abs(rel)
    log_ratio = jnp.log(jnp.maximum(n, 1).astype(jnp.float32) / max_exact) / math.log(T5_MAX_DIST / max_exact)
    large = jnp.minimum(max_exact + (log_ratio * (half - max_exact)).astype(jnp.int32), half - 1)
    return jnp.where(rel < 0, half, 0) + jnp.where(n < max_exact, n, large)


def _dsa_bias_row(t5_table):
    idx = jnp.arange(DSA_ROLL)
    d = jnp.where(idx < TILE, idx, idx - DSA_ROLL)
    far = t5_table[_t5_bucket(jnp.full((1,), 2 * TILE + 1, I32))]
    return (t5_table[_t5_bucket(TILE + d)] - far).T.astype(F32)


def _dsa_kernel(g_ref, qi_ref, wi_ref, qb_ref, ki_ref, kb_ref, vb_ref, o_ref,
                sc_ref, scb_ref, keep_ref, vt_ref, qis_ref, qbs_ref, m_ref, acc_ref, bias_ref, s_ref,
                *, off, valid_len, q_valid, n_kt, tq, n_merge):
    i = pl.program_id(1)
    qt = i + off
    q0 = qt * TILE
    nk = qt + 1
    key_rows = lambda j: pl.ds(pl.multiple_of(j * TILE, TILE), TILE)

    @pl.when((pl.program_id(0) == 0) & (i == 0))
    def _():
        for h in range(B_HEADS):
            tile = _toeplitz(g_ref[h:h + 1, :], 2 * TILE, DSA_ROLL)[:, :tq]
            if n_merge:
                ts = tq // n_merge
                stream = lax.broadcasted_iota(I32, (1, tq), 1) // ts
                first = tile
                for b in range(1, n_merge):
                    tile = jnp.where(stream == b, pltpu.roll(first, b * ts, 1), tile)
            bias_ref[h] = tile

    @pl.when(i == 0)
    def _():
        def body(j, c):
            for b in range(max(n_merge, 1)):
                vt = vb_ref[b, key_rows(j), :].astype(F32).T
                for g in range(B_KV_HEADS):
                    vt_ref[g, j, 0:HEAD_DIM, b * TILE:(b + 1) * TILE] = vt[g * HEAD_DIM:(g + 1) * HEAD_DIM].astype(BF16)
            for g in range(B_KV_HEADS):
                vt_ref[g, j, HEAD_DIM:2 * HEAD_DIM, :] = jnp.ones((HEAD_DIM, vt_ref.shape[-1]), BF16)
            return c

        lax.fori_loop(0, n_kt, body, 0)

    colq = lax.broadcasted_iota(I32, (1, tq), 1)
    rowk = lax.broadcasted_iota(I32, (TILE, 1), 0)

    if n_merge:
        width = n_merge * HEAD_DIM
        own = (lax.broadcasted_iota(I32, (tq, width), 0) // (tq // n_merge)
               == lax.broadcasted_iota(I32, (tq, width), 1) // HEAD_DIM)

        def block_diag(x):
            return jnp.where(own, jnp.concatenate([x.astype(F32)] * n_merge, axis=1), 0.0).astype(BF16)

        qi, qb = qi_ref[...], qb_ref[...]
        for h in range(B_HEADS):
            qis_ref[h] = block_diag(qi[:, h * IDX_DIM:(h + 1) * IDX_DIM])
            qbs_ref[h] = block_diag(qb[:, h * HEAD_DIM:(h + 1) * HEAD_DIM])
        side_by_side = lambda ref, j, cols: jnp.concatenate(
            [ref[b, key_rows(j), cols] for b in range(n_merge)], axis=1)
        wi_t = wi_ref[...].T
        lim = jnp.full((1, tq), valid_len, I32)
    else:
        qi = qi_ref[0]
        for h in range(IDX_HEADS):
            qis_ref[h * tq:(h + 1) * tq, :] = qi[:, h * IDX_DIM:(h + 1) * IDX_DIM]
        qb = qb_ref[0]
        for g in range(B_KV_HEADS):
            for hh in range(B_GROUP):
                h = g * B_GROUP + hh
                qbs_ref[g, hh * tq:(hh + 1) * tq, :] = qb[:, h * HEAD_DIM:(h + 1) * HEAD_DIM]
        wi_t = wi_ref[0].T
        lim = jnp.minimum(q0 + (colq // CHUNK + 1) * CHUNK, valid_len)

    def score_tile(j, masked):
        if n_merge:
            kt = side_by_side(ki_ref, j, slice(None))
            lg = jnp.concatenate([lax.dot_general(kt, qis_ref[h], _NT, preferred_element_type=F32)
                                  for h in range(IDX_HEADS)], axis=1)
        else:
            kt = ki_ref[0, key_rows(j), :]
            lg = lax.dot_general(kt, qis_ref[...], _NT, preferred_element_type=F32)
        sc = wi_t[0:1, :] * jnp.maximum(lg[:, 0:tq], 0.0)
        for h in range(1, IDX_HEADS):
            sc = sc + wi_t[h:h + 1, :] * jnp.maximum(lg[:, h * tq:(h + 1) * tq], 0.0)
        if masked:
            sc = jnp.where(j * TILE + rowk < lim, sc, -jnp.inf)
        sc_ref[j] = sc
        scb_ref[j] = sc.astype(BF16)
        rows8 = lambda hit: jnp.where(hit, 1, 0).reshape(TILE // 8, 8, tq).sum(axis=0)
        return rows8(sc > 0.0), rows8(sc >= 0.0)

    def score_pair(jj, c, masked):
        pos_a, nn_a = score_tile(2 * jj, masked)
        pos_b, nn_b = score_tile(2 * jj + 1, masked)
        return c[0] + pos_a + pos_b, c[1] + nn_a + nn_b

    n_open_pairs = (nk - 1) // 2
    counts = lax.fori_loop(0, n_open_pairs, functools.partial(score_pair, masked=False),
                           (jnp.zeros((8, tq), I32), jnp.zeros((8, tq), I32)))
    n_pos, n_nonneg = lax.fori_loop(n_open_pairs, (nk + 1) // 2, functools.partial(score_pair, masked=True), counts)

    def f32_of_key(k):
        return pltpu.bitcast(jnp.where(k < 0, k ^ 0x7FFFFFFF, k), F32)

    def bf16_of_key(k):
        bits = jnp.where(k < 0, k ^ 0x7FFF, k) & 0xFFFF
        return pltpu.bitcast(lax.shift_left(bits, 16), F32).astype(BF16)

    def count_bf16(cand):
        def body(jj, acc):
            for j in (2 * jj, 2 * jj + 1):
                ge = jnp.where(scb_ref[j] >= cand, jnp.int16(1), jnp.int16(0)).reshape(TILE // 16, 16, tq)
                part = ge[0]
                for r in range(1, TILE // 16):
                    part = part + ge[r]
                acc = acc + part
            return acc

        acc = lax.fori_loop(0, (nk + 1) // 2, body, jnp.zeros((16, tq), I16))
        return acc.astype(I32).sum(axis=0, keepdims=True)

    def count_f32(cand, strict=False):
        def body(j, acc):
            blk = sc_ref[j]
            hit = (blk > cand) if strict else (blk >= cand)
            return acc + jnp.where(hit, 1, 0).reshape(TILE // 8, 8, tq).sum(axis=0)

        return lax.fori_loop(0, nk, body, jnp.zeros((8, tq), I32)).sum(axis=0, keepdims=True)

    c_pos = n_pos.sum(axis=0, keepdims=True)
    zero_tie = (c_pos < TOPK_MAX) & (n_nonneg.sum(axis=0, keepdims=True) >= TOPK_MAX)
    skip1 = (colq >= q_valid) | zero_tie

    def level1(it, t):
        cand = t + lax.shift_left(jnp.int32(1), 15 - it)
        return jnp.where(count_bf16(bf16_of_key(cand)) >= TOPK_MAX, cand, t)

    t1 = lax.fori_loop(0, 16, level1, jnp.full((1, tq), MIN16, I32))
    settled1 = skip1 | (t1 <= HI_NEG_INF)

    def level2(st):
        lo, hi, thr_key, done = st
        live = (done == 0) & (hi - lo > 1)
        mid = lo + lax.shift_right_arithmetic(hi - lo, 1)
        c = count_f32(f32_of_key(mid))
        hit = live & (c == TOPK_MAX)
        return (jnp.where(live & (c >= TOPK_MAX), mid, lo), jnp.where(live & (c < TOPK_MAX), mid, hi),
                jnp.where(hit, mid, thr_key), jnp.where(hit, 1, done))

    def n_live(st):
        lo, hi, _, done = st
        return jnp.sum(jnp.where((done == 0) & (hi - lo > 1), 1, 0))

    def level2_pair(carry):
        st = level2(level2(carry[0]))
        return st, n_live(st)

    def key32_of_key16(k):
        return lax.shift_left(k, 16) | jnp.where(k < 0, 0xFFFF, 0)

    key_t1 = key32_of_key16(t1)
    st0 = (key_t1 - 0x8000, key32_of_key16(t1 + 1), key_t1, jnp.where(settled1, 1, 0))
    (lo, _, thr_key, done2), _ = lax.while_loop(lambda carry: carry[1] > 0, level2_pair, (st0, n_live(st0)))
    open2 = done2 == 0
    thr = jnp.where(open2, f32_of_key(lo), f32_of_key(thr_key))
    thr = jnp.where(zero_tie, 0.0, thr)
    thr = jnp.where(settled1 & jnp.logical_not(zero_tie), -jnp.inf, thr)
    thr = jnp.maximum(thr, float(jnp.finfo(F32).min))

    keep_ref[...] = jnp.where(zero_tie, TOPK_MAX - c_pos, 2 ** 30).astype(F32)

    @pl.when(jnp.sum(jnp.where(open2, 1, 0)) > 0)
    def _():
        above = count_f32(thr, strict=True)
        keep_ref[...] = jnp.where(open2, (TOPK_MAX - above).astype(F32), keep_ref[...])

    @pl.when(jnp.sum(jnp.where(open2 | zero_tie, 1, 0)) > 0)
    def _():
        keep = keep_ref[...]
        lower = (lax.broadcasted_iota(I32, (TILE, TILE), 0) > lax.broadcasted_iota(I32, (TILE, TILE), 1))
        lower = jnp.where(lower, 1.0, 0.0).astype(BF16)

        def body(jj, run):
            for j in (2 * jj, 2 * jj + 1):
                blk = sc_ref[j]
                eq = blk == thr
                eq_f = jnp.where(eq, 1.0, 0.0)
                before = jnp.dot(lower, eq_f.astype(BF16), preferred_element_type=F32)
                sc_ref[j] = jnp.where(eq & (run + before >= keep), -jnp.inf, blk)
                run = run + eq_f.reshape(TILE // 8, 8, tq).sum(axis=0).sum(axis=0, keepdims=True)
            return run

        lax.fori_loop(0, (nk + 1) // 2, body, jnp.zeros((1, tq), F32))

    for g in range(B_KV_HEADS):
        m_ref[g] = jnp.full((1, B_GROUP * tq), NEG, F32)
        acc_ref[g] = jnp.zeros((2 * HEAD_DIM, B_GROUP * tq), F32)

    def attend(tiles):
        sels = [sc_ref[j] >= thr for j, _ in tiles]
        part_max = []
        for g in range(B_KV_HEADS):
            pm = None
            for t, ((j, near), sel) in enumerate(zip(tiles, sels)):
                if n_merge:
                    kt = side_by_side(kb_ref, j, slice(g * HEAD_DIM, (g + 1) * HEAD_DIM))
                    st = jnp.concatenate([lax.dot_general(kt, qbs_ref[g * B_GROUP + hh], _NT,
                                                          preferred_element_type=F32)
                                          for hh in range(B_GROUP)], axis=1)
                else:
                    kt = kb_ref[0, key_rows(j), g * HEAD_DIM:(g + 1) * HEAD_DIM]
                    st = lax.dot_general(kt, qbs_ref[g], _NT, preferred_element_type=F32)
                parts = []
                for hh in range(B_GROUP):
                    s_h = st[:, hh * tq:(hh + 1) * tq]
                    if near is not None:
                        s_h = s_h + bias_ref[g * B_GROUP + hh, near * TILE:(near + 1) * TILE, :]
                    parts.append(jnp.where(sel, s_h, -jnp.inf))
                s = jnp.concatenate(parts, axis=1)
                s_ref[g * len(tiles) + t] = s
                m8 = s.reshape(TILE // 8, 8, B_GROUP * tq).max(axis=0)
                pm = m8 if pm is None else jnp.maximum(pm, m8)
            part_max.append(pm)
        for g in range(B_KV_HEADS):
            m_old = m_ref[g]
            m_new = jnp.maximum(m_old, part_max[g].max(axis=0, keepdims=True))
            pv = None
            for t, (j, _) in enumerate(tiles):
                p = jnp.exp(s_ref[g * len(tiles) + t] - m_new)
                if n_merge:
                    lane_stream = colq // (tq // n_merge)
                    d = jnp.concatenate(
                        [jnp.dot(vt_ref[g, j],
                                 jnp.concatenate([jnp.where(lane_stream == b, p[:, hh * tq:(hh + 1) * tq], 0.0)
                                                  for b in range(n_merge)], axis=0).astype(BF16),
                                 preferred_element_type=F32) for hh in range(B_GROUP)], axis=1)
                else:
                    d = jnp.dot(vt_ref[g, j], p.astype(BF16), preferred_element_type=F32)
                pv = d if pv is None else pv + d
            acc_ref[g] = jnp.exp(m_old - m_new) * acc_ref[g] + pv
            m_ref[g] = m_new

    n_far = jnp.maximum(nk - 2, 0)

    def far_step(jj, c):
        attend([(FAR_STEP * jj + t, None) for t in range(FAR_STEP)])
        return c

    lax.fori_loop(0, n_far // FAR_STEP, far_step, 0)
    rem = n_far % FAR_STEP

    @pl.when(rem >= 2)
    def _():
        attend([(n_far - rem, None), (n_far - rem + 1, None)])

    @pl.when(rem % 2 == 1)
    def _():
        attend([(n_far - 1, None)])

    @pl.when(nk >= 2)
    def _():
        attend([(nk - 2, 0), (nk - 1, 1)])

    @pl.when(nk < 2)
    def _():
        attend([(nk - 1, 1)])

    outs = []
    for g in range(B_KV_HEADS):
        for hh in range(B_GROUP):
            blk = acc_ref[g, :, hh * tq:(hh + 1) * tq].T
            outs.append(blk[:, 0:HEAD_DIM] / blk[:, HEAD_DIM:HEAD_DIM + 1])
    o_ref[0] = jnp.concatenate(outs, axis=-1).astype(BF16)


def _dsa_attn(qi, wi, qb, ki, kb, vb, g_row, off, valid_len, q_valid, tq):
    b, t_q = qi.shape[:2]
    tk = ki.shape[1]
    nq, n_kt = t_q // tq, tk // TILE
    assert tq == TILE or nq == 1
    assert n_kt % 2 == 0 and tk == n_kt * TILE
    qspec = lambda n: pl.BlockSpec((1, tq, n), lambda bb, i: (bb, i, 0))
    kspec = lambda n: pl.BlockSpec((1, tk, n), lambda bb, i: (bb, 0, 0))
    in_specs = [pl.BlockSpec((B_HEADS, DSA_ROLL), lambda bb, i: (0, 0)),
                qspec(512), qspec(128), qspec(512), kspec(IDX_DIM), kspec(128), kspec(128)]
    return _dsa_call((g_row, qi, wi, qb, ki, kb, vb), in_specs, (b, nq), t_q, tq, n_kt,
                     dict(off=off, valid_len=valid_len, q_valid=q_valid, n_merge=0))


def _dsa_attn_merged(qi, wi, qb, ki, kb, vb, g_row, off, valid_len):
    s, ts = qi.shape[:2]
    tk = ki.shape[1]
    tq, n_kt = s * ts, tk // TILE
    assert tq % LANES == 0 and ts <= CHUNK and n_kt % 2 == 0 and tk == n_kt * TILE
    flat = lambda a: a.reshape(tq, a.shape[-1])
    operands = (g_row, flat(qi), flat(wi), flat(qb), ki, kb, vb)
    whole = lambda a: pl.BlockSpec(a.shape, lambda bb, i: (0,) * a.ndim, pipeline_mode=pl.Buffered(1))
    out = _dsa_call(operands, [whole(a) for a in operands], (1, 1), tq, tq, n_kt,
                    dict(off=off, valid_len=valid_len, q_valid=tq, n_merge=s))
    return out.reshape(s, ts, B_WIDTH)


def _dsa_call(operands, in_specs, grid, t_q, tq, n_kt, static):
    n_merge = static["n_merge"]
    assert static["valid_len"] >= (static["off"] + grid[1] - 1) * TILE
    if n_merge:
        q_scratch = [pltpu.VMEM((IDX_HEADS, tq, n_merge * IDX_DIM), BF16),
                     pltpu.VMEM((B_HEADS, tq, n_merge * HEAD_DIM), BF16)]
    else:
        q_scratch = [pltpu.VMEM((IDX_HEADS * tq, IDX_DIM), BF16),
                     pltpu.VMEM((B_KV_HEADS, B_GROUP * tq, HEAD_DIM), BF16)]
    return pl.pallas_call(
        functools.partial(_dsa_kernel, n_kt=n_kt, tq=tq, **static),
        grid=grid,
        in_specs=in_specs,
        out_specs=pl.BlockSpec((1, tq, B_WIDTH), lambda bb, i: (bb, i, 0)),
        out_shape=jax.ShapeDtypeStruct((grid[0], t_q, B_WIDTH), BF16),
        scratch_shapes=[pltpu.VMEM((n_kt, TILE, tq), F32),
                        pltpu.VMEM((n_kt, TILE, tq), BF16),
                        pltpu.VMEM((1, tq), F32),
                        pltpu.VMEM((B_KV_HEADS, n_kt, 2 * HEAD_DIM, max(n_merge, 1) * TILE), BF16),
                        *q_scratch,
                        pltpu.VMEM((B_KV_HEADS, 1, B_GROUP * tq), F32),
                        pltpu.VMEM((B_KV_HEADS, 2 * HEAD_DIM, B_GROUP * tq), F32),
                        pltpu.VMEM((B_HEADS, 2 * TILE, tq), F32),
                        pltpu.VMEM((B_KV_HEADS * FAR_STEP, TILE, B_GROUP * tq), F32)],
        compiler_params=_params("arbitrary", "arbitrary"),
        name="dsa_attn",
    )(*operands)


def _mem_kv_kernel(m_ref, wk_ref, wv_ref, k_o, v_o):
    mb = m_ref[...].astype(BF16)
    k = jnp.dot(mb, wk_ref[...], preferred_element_type=F32)
    v = jnp.dot(mb, wv_ref[...], preferred_element_type=F32)
    for hd in range(MEM_HEADS):
        sl = slice(hd * MEM_HEAD_DIM, (hd + 1) * MEM_HEAD_DIM)
        k_o[:, hd, :] = k[:, sl]
        v_o[:, hd, :] = v[:, sl]


def _mem_kv(mem2d, wk, wv):
    r = mem2d.shape[0]
    tm = MEM_LEN
    row = lambda n: pl.BlockSpec((tm, n), lambda i: (i, 0))
    const = lambda s: pl.BlockSpec(s, lambda i: (0, 0))
    sds = jax.ShapeDtypeStruct
    return pl.pallas_call(
        _mem_kv_kernel,
        grid=(r // tm,),
        in_specs=[row(D_MODEL), const((D_MODEL, MEM_WIDTH)), const((D_MODEL, MEM_WIDTH))],
        out_specs=[pl.BlockSpec((tm, MEM_HEADS, MEM_HEAD_DIM), lambda i: (i, 0, 0))] * 2,
        out_shape=[sds((r, MEM_HEADS, MEM_HEAD_DIM), F32)] * 2,
        compiler_params=_params("arbitrary"),
        name="mem_kv",
    )(mem2d, wk, wv)


FF_CHUNK = 256


def _tail_kernel(x_ref, oa_ref, ob_ref, mk_ref, mv_ref, hist_ref,
                 wo_ref, g1_ref, b1_ref, wq_ref, wmo_ref, g2_ref, b2_ref,
                 wu_ref, wc_ref, bc_ref, wd_ref, g3_ref, b3_ref,
                 o_ref, tail_ref, carry_ref, act_ref, *, tiles_per_batch, seg):
    i = pl.program_id(0)
    tm = x_ref.shape[0]
    nseg = tm // seg

    mix = jnp.concatenate([oa_ref[...], ob_ref[...]], axis=-1)
    h = _layer_norm(ALPHA * x_ref[...] + jnp.dot(mix, wo_ref[...], preferred_element_type=F32),
                    g1_ref[...], b1_ref[...])

    q = jnp.dot(h.astype(BF16), wq_ref[...], preferred_element_type=F32).astype(BF16)
    segs = []
    for s in range(nseg):
        qs = q[s * seg:(s + 1) * seg]
        heads = []
        for hd in range(MEM_HEADS):
            sl = slice(hd * MEM_HEAD_DIM, (hd + 1) * MEM_HEAD_DIM)
            sc = lax.dot_general(qs[:, sl], mk_ref[s, :, hd, :].astype(BF16), _NT,
                                 preferred_element_type=F32) * MEM_HEAD_DIM ** -0.5
            p = jnp.exp(sc - sc.max(-1, keepdims=True))
            l = p.sum(-1, keepdims=True)
            heads.append(jnp.dot(p.astype(BF16), mv_ref[s, :, hd, :].astype(BF16),
                                 preferred_element_type=F32) / l)
        segs.append(jnp.concatenate(heads, axis=-1))
    att = jnp.concatenate(segs, axis=0).astype(BF16)
    h = _layer_norm(ALPHA * h + jnp.dot(att, wmo_ref[...], preferred_element_type=F32), g2_ref[...], b2_ref[...])

    hb = h.astype(BF16)
    row = lax.broadcasted_iota(I32, (tm, 1), 0)
    first = (i % tiles_per_batch) == 0
    for c in range(D_FF // FF_CHUNK):
        cs = slice(c * FF_CHUNK, (c + 1) * FF_CHUNK)
        u = jnp.dot(hb, wu_ref[:, cs], preferred_element_type=F32)
        gt = jnp.dot(hb, wu_ref[:, D_FF + c * FF_CHUNK:D_FF + (c + 1) * FF_CHUNK], preferred_element_type=F32)
        p1 = pltpu.roll(gt, 1, 0)
        p2 = pltpu.roll(gt, 2, 0)
        for s in range(nseg):
            hist = hist_ref[s, :, cs]
            if tiles_per_batch > 1:
                hist = jnp.where(first, hist, carry_ref[:, cs])
            p1 = jnp.where(row == s * seg, hist[7:8, :], p1)
            p2 = jnp.where(row == s * seg, hist[6:7, :], p2)
            p2 = jnp.where(row == s * seg + 1, hist[7:8, :], p2)
        gc = bc_ref[:, cs] + ((wc_ref[0:1, cs] * p2 + wc_ref[1:2, cs] * p1) + wc_ref[2:3, cs] * gt)
        act_ref[:, cs] = (u * jax.nn.gelu(gc)).astype(BF16)
        for s in range(nseg):
            tail_ref[s, :, cs] = gt[(s + 1) * seg - 8:(s + 1) * seg, :]
        carry_ref[:, cs] = gt[tm - 8:tm, :]
    f = jnp.dot(act_ref[...], wd_ref[...], preferred_element_type=F32)
    o_ref[...] = _layer_norm(ALPHA * h + f, g3_ref[...], b3_ref[...])


def _layer_tail(x2d, oa, ob, mk, mv, hist, weights, tm, tiles_per_batch, seg):
    r = x2d.shape[0]
    nseg = tm // seg
    n_stream = r // (tm * tiles_per_batch) * nseg
    row = lambda n: pl.BlockSpec((tm, n), lambda i: (i, 0))
    per_stream = lambda a, b: pl.BlockSpec((nseg, a, b), lambda i: (i // tiles_per_batch, 0, 0))
    mem = pl.BlockSpec((nseg, MEM_LEN, MEM_HEADS, MEM_HEAD_DIM), lambda i: (i // tiles_per_batch, 0, 0, 0))
    const = lambda a: pl.BlockSpec(a.shape, lambda i: (0,) * a.ndim, pipeline_mode=pl.Buffered(1))
    return pl.pallas_call(
        functools.partial(_tail_kernel, tiles_per_batch=tiles_per_batch, seg=seg),
        grid=(r // tm,),
        in_specs=[row(D_MODEL), row(A_WIDTH), row(B_WIDTH), mem, mem, per_stream(8, D_FF)]
                 + [const(w) for w in weights],
        out_specs=[row(D_MODEL), per_stream(8, D_FF)],
        out_shape=[jax.ShapeDtypeStruct((r, D_MODEL), F32), jax.ShapeDtypeStruct((n_stream, 8, D_FF), F32)],
        scratch_shapes=[pltpu.VMEM((8, D_FF), F32), pltpu.VMEM((tm, D_FF), BF16)],
        compiler_params=_params("arbitrary"),
        name="layer_tail",
    )(x2d, oa, ob, mk, mv, hist, *weights)


def _pad_rows(a, n):
    return jnp.pad(a, ((0, 0), (0, n - a.shape[1])) + ((0, 0),) * (a.ndim - 2))


def _hist8(g_hist):
    return jnp.pad(g_hist, ((0, 0), (8 - g_hist.shape[1], 0), (0, 0)))


def kernel(x_prompt, x_sample, cache_a_k, cache_a_v, cache_b_k, cache_b_v, cache_b_kidx, cache_mem_k, cache_mem_v, state_ffn_conv, mem_prompt, w_in, a_rel_bias, t5_bias, w_o, ln1_g, ln1_b, w_mq, w_mk, w_mv, w_mo, ln2_g, ln2_b, w_up, w_conv, b_conv, w_down, ln3_g, ln3_b):
    bp, tp = x_prompt.shape[:2]
    bs, ts = x_sample.shape[:2]
    l = 0
    vec = lambda a: a[l].reshape(1, -1)
    w_in_p = _prep_w_in(w_in[l])
    w_o_b = w_o[l].astype(BF16)
    w_mq_b, w_mk_b, w_mv_b, w_mo_b = (w[l].astype(BF16) for w in (w_mq, w_mk, w_mv, w_mo))
    w_up_b, w_down_b = w_up[l].astype(BF16), w_down[l].astype(BF16)
    band_row = _band_bias_row(a_rel_bias[l])
    dsa_row = _dsa_bias_row(t5_bias)
    tail_w = (w_o_b, vec(ln1_g), vec(ln1_b), w_mq_b, w_mo_b, vec(ln2_g), vec(ln2_b),
              w_up_b, w_conv[l], vec(b_conv), w_down_b, vec(ln3_g), vec(ln3_b))

    tm = 512
    a_keep = min(N_PREV_CHUNKS * CHUNK, tp)
    (qa, ka, va, qb, qi, kb, vb, ki, kb_b, vb_b, ki_b, wi, ka_tail, va_tail) = _in_proj(
        x_prompt.reshape(bp * tp, D_MODEL), w_in_p, tm, tp // tm)
    r3 = lambda a: a.reshape(bp, tp, a.shape[-1])
    oa = _band_attn(r3(qa), r3(ka), r3(va), band_row, 0, tp)
    ob = _dsa_attn(r3(qi), r3(wi), r3(qb), r3(ki_b), r3(kb_b), r3(vb_b), dsa_row, 0, tp, TILE, TILE)
    mk, mv = (a.reshape(bp, MEM_LEN, MEM_HEADS, MEM_HEAD_DIM)
              for a in _mem_kv(mem_prompt.reshape(bp * MEM_LEN, D_MODEL), w_mk_b, w_mv_b))
    xp, p_tail = _layer_tail(x_prompt.reshape(bp * tp, D_MODEL), oa.reshape(bp * tp, A_WIDTH),
                             ob.reshape(bp * tp, B_WIDTH), mk, mv, jnp.zeros((bp, 8, D_FF), F32), tail_w,
                             tm, tp // tm, tm)
    prompt_state = (
        ka_tail.reshape(bp, tm, A_HEADS, HEAD_DIM)[:, tm - a_keep:][None],
        va_tail.reshape(bp, tm, A_HEADS, HEAD_DIM)[:, tm - a_keep:][None],
        kb.reshape(1, bp, tp, B_KV_HEADS, HEAD_DIM), vb.reshape(1, bp, tp, B_KV_HEADS, HEAD_DIM),
        ki.reshape(1, bp, tp, IDX_DIM),
        mk[None], mv[None], p_tail[:, 8 - (CONV_W - 1):][None])

    rs = bs * ts
    (qa, ka, va, qb, qi, kb, vb, ki, kb_b, vb_b, ki_b, wi, ka_new, va_new) = _in_proj(
        x_sample.reshape(rs, D_MODEL), w_in_p, rs, 1)
    s3 = lambda a: a.reshape(bs, ts, -1)
    qpad = lambda a, n: _pad_rows(s3(a), n)

    past_a = cache_a_k.shape[2]
    n_a = past_a + ts
    new_tile = lambda a: _pad_rows(s3(a), TILE).astype(BF16)
    oa = _band_attn(qpad(qa, TILE), new_tile(ka_new), new_tile(va_new), band_row, past_a // TILE, n_a,
                    state_k=cache_a_k[l:l + 1], state_v=cache_a_v[l:l + 1])

    past_b = cache_b_k.shape[2]
    n_b = past_b + ts
    t_b = -(-n_b // (2 * TILE)) * 2 * TILE
    seq_b = lambda cache, new: _pad_rows(
        jnp.concatenate([cache[l].reshape(bs, past_b, -1), s3(new)], axis=1), t_b).astype(BF16)
    ob = _dsa_attn_merged(s3(qi), s3(wi), s3(qb), seq_b(cache_b_kidx, ki), seq_b(cache_b_k, kb),
                          seq_b(cache_b_v, vb), dsa_row, past_b // TILE, n_b)

    xs, s_tail = _layer_tail(x_sample.reshape(rs, D_MODEL), oa[:, :ts].reshape(rs, A_WIDTH),
                             ob[:, :ts].reshape(rs, B_WIDTH), cache_mem_k[l], cache_mem_v[l],
                             _hist8(state_ffn_conv[l]), tail_w, rs, 1, ts)
    sample_state = (
        ka_new.reshape(1, bs, ts, A_HEADS, HEAD_DIM), va_new.reshape(1, bs, ts, A_HEADS, HEAD_DIM),
        kb.reshape(1, bs, ts, B_KV_HEADS, HEAD_DIM), vb.reshape(1, bs, ts, B_KV_HEADS, HEAD_DIM),
        ki.reshape(1, bs, ts, IDX_DIM), s_tail[:, 8 - (CONV_W - 1):][None])

    return (xp.reshape(bp, tp, D_MODEL), xs.reshape(bs, ts, D_MODEL)) + prompt_state + sample_state
```

```python
import functools
import math

import jax
import jax.numpy as jnp
from jax import lax
from jax.experimental import pallas as pl
from jax.experimental.pallas import tpu as pltpu

F32 = jnp.float32
BF16 = jnp.bfloat16
I32 = jnp.int32
I16 = jnp.int16

D_MODEL = 1024
CHUNK = 64
N_PREV_CHUNKS = 8
HEAD_DIM = 64
A_HEADS = 8
A_WIDTH = A_HEADS * HEAD_DIM
A_MAX_REL = 64
B_HEADS = 8
B_KV_HEADS = 2
B_GROUP = B_HEADS // B_KV_HEADS
B_WIDTH = B_HEADS * HEAD_DIM
B_KV_WIDTH = B_KV_HEADS * HEAD_DIM
IDX_HEADS = 8
IDX_DIM = 64
TOPK_MAX = 256
N_BUCKETS = 32
T5_MAX_DIST = 128
MEM_LEN = 256
MEM_HEADS = 4
MEM_HEAD_DIM = 128
MEM_WIDTH = MEM_HEADS * MEM_HEAD_DIM
D_FF = 2816
CONV_W = 3
IN_SIZES = (A_WIDTH, A_WIDTH, A_WIDTH, B_WIDTH, B_KV_WIDTH, B_KV_WIDTH, IDX_HEADS * IDX_DIM, IDX_DIM, IDX_HEADS)
DEPTH = 1
ALPHA = (2 * DEPTH) ** 0.25
LN_EPS = 1e-5
ATTN_SCALE = HEAD_DIM ** -0.5
NEG = -1e30

LANES = 128
TILE = 256
BAND_TILES = 1 + (N_PREV_CHUNKS * CHUNK) // TILE
VMEM_LIMIT = 56 * 1024 * 1024

_C_QA, _C_KA, _C_VA, _C_QB = 0, 512, 1024, 1536
_C_KB, _C_VB, _C_QI, _C_KI, _C_WI = 2048, 2176, 2304, 2816, 2944
IN_PAD = 3072

MIN16 = -32768
HI_NEG_INF = -32641

_NT = (((1,), (1,)), ((), ()))


def _params(*sem):
    return pltpu.CompilerParams(dimension_semantics=sem, vmem_limit_bytes=VMEM_LIMIT)


def _layer_norm(z, g, b):
    mu = jnp.mean(z, axis=-1, keepdims=True)
    d = z - mu
    var = jnp.mean(d * d, axis=-1, keepdims=True)
    return d * lax.rsqrt(var + LN_EPS) * g + b


def _toeplitz(g_row, rows, width):
    return pltpu.roll(jnp.broadcast_to(g_row, (rows, width)), 0, 1, stride=1, stride_axis=0)


def _in_proj_kernel(x_ref, w_ref, qa_o, ka_o, va_o, qb_o, qi_o, kb_o, vb_o, ki_o, kbb_o, vbb_o, kib_o,
                    wi_o, kat_o, vat_o, *, tiles_per_batch, wi_scale):
    i = pl.program_id(0)
    xb = x_ref[...].astype(BF16)

    def mm(c0, n):
        return jnp.dot(xb, w_ref[:, c0:c0 + n], preferred_element_type=F32)

    kb = mm(_C_KB, 128)
    vb = mm(_C_VB, 128)
    for g in range(B_KV_HEADS):
        kb_o[:, g, :] = kb[:, g * HEAD_DIM:(g + 1) * HEAD_DIM]
        vb_o[:, g, :] = vb[:, g * HEAD_DIM:(g + 1) * HEAD_DIM]
    kbb_o[...] = kb.astype(BF16)
    vbb_o[...] = vb.astype(BF16)
    ki = mm(_C_KI, 128)[:, :IDX_DIM]
    ki_o[...] = ki
    kib_o[...] = ki.astype(BF16)
    wi_o[...] = mm(_C_WI, 128) * wi_scale
    qa_o[...] = mm(_C_QA, 512).astype(BF16)
    ka = mm(_C_KA, 512)
    va = mm(_C_VA, 512)
    ka_o[...] = ka.astype(BF16)
    va_o[...] = va.astype(BF16)
    qb_o[...] = mm(_C_QB, 512).astype(BF16)
    qi_o[...] = mm(_C_QI, 512).astype(BF16)

    @pl.when(i % tiles_per_batch == tiles_per_batch - 1)
    def _():
        kat_o[...] = ka
        vat_o[...] = va


def _prep_w_in(w):
    parts, off = [], 0
    for n in IN_SIZES:
        parts.append(w[:, off:off + n])
        off += n
    qa, ka, va, qb, kb, vb, qi, ki, wi = parts
    pad = lambda a, n: jnp.pad(a, ((0, 0), (0, n - a.shape[1])))
    cols = [qa * ATTN_SCALE, ka, va, qb * ATTN_SCALE, kb, vb, qi * IDX_DIM ** -0.5, pad(ki, 128), pad(wi, 128)]
    return jnp.concatenate(cols, axis=1).astype(BF16)


def _in_proj(x2d, w_pad, tm, tiles_per_batch):
    r = x2d.shape[0]
    n_tiles = r // tm
    n_batch = n_tiles // tiles_per_batch
    row = lambda n: pl.BlockSpec((tm, n), lambda i: (i, 0))
    tail = pl.BlockSpec((tm, 512), lambda i: (i // tiles_per_batch, 0))
    sds = jax.ShapeDtypeStruct
    kv_state = pl.BlockSpec((tm, B_KV_HEADS, HEAD_DIM), lambda i: (i, 0, 0))
    kv_shape = sds((r, B_KV_HEADS, HEAD_DIM), F32)
    out_shape = [sds((r, 512), BF16)] * 5 + [kv_shape, kv_shape, sds((r, IDX_DIM), F32),
                                              sds((r, 128), BF16), sds((r, 128), BF16), sds((r, IDX_DIM), BF16),
                                              sds((r, 128), F32),
                                              sds((n_batch * tm, 512), F32), sds((n_batch * tm, 512), F32)]
    out_specs = [row(512)] * 5 + [kv_state, kv_state, row(IDX_DIM), row(128), row(128), row(IDX_DIM), row(128),
                                  tail, tail]
    return pl.pallas_call(
        functools.partial(_in_proj_kernel, tiles_per_batch=tiles_per_batch, wi_scale=IDX_HEADS ** -0.5),
        grid=(n_tiles,),
        in_specs=[pl.BlockSpec((tm, D_MODEL), lambda i: (i, 0)),
                  pl.BlockSpec((D_MODEL, IN_PAD), lambda i: (0, 0))],
        out_specs=out_specs,
        out_shape=out_shape,
        compiler_params=_params("arbitrary"),
        name="in_proj",
    )(x2d, w_pad)


BAND_COLS = BAND_TILES * TILE
BAND_ROLL = BAND_COLS + TILE
BAND_SLAB = 4


def _band_bias_row(table):
    idx = jnp.arange(BAND_ROLL)
    d = jnp.where(idx < TILE, idx, idx - BAND_ROLL)
    rel = (BAND_TILES - 1) * TILE + d
    return table[jnp.clip(rel, -A_MAX_REL, A_MAX_REL) + A_MAX_REL].T.astype(F32)


def _band_kernel(g_ref, q_ref, k0, k1, k2, v0, v1, v2, o_ref, bias_ref, vt_ref, s_ref, *, off, valid_len):
    i = pl.program_id(1)
    kt = i + off
    krefs, vrefs = (k0, k1, k2), (v0, v1, v2)

    @pl.when((pl.program_id(0) == 0) & (i == 0))
    def _():
        c = lax.broadcasted_iota(I32, (BAND_COLS, TILE), 0) // CHUNK
        r = lax.broadcasted_iota(I32, (BAND_COLS, TILE), 1) // CHUNK
        ok = (c >= r) & (c <= r + N_PREV_CHUNKS)
        for h in range(A_HEADS):
            bias_ref[h] = jnp.where(ok, _toeplitz(g_ref[h:h + 1, :], BAND_COLS, BAND_ROLL)[:, :TILE], NEG)
        ones = jnp.ones((HEAD_DIM, TILE), BF16)
        for s in range(BAND_TILES):
            for h in range(A_HEADS):
                vt_ref[s, h, HEAD_DIM:2 * HEAD_DIM, :] = ones

    def put(slot, vref):
        vt = vref[0].astype(F32).T
        for h in range(A_HEADS):
            vt_ref[slot, h, 0:HEAD_DIM, :] = vt[h * HEAD_DIM:(h + 1) * HEAD_DIM].astype(BF16)

    slots = [(kt + 1 + j) % BAND_TILES for j in range(BAND_TILES)]

    @pl.when(i == 0)
    def _():
        for j in range(BAND_TILES - 1):
            put(slots[j], vrefs[j])

    put(slots[BAND_TILES - 1], vrefs[BAND_TILES - 1])

    base = (kt - (BAND_TILES - 1)) * TILE

    n_slab = BAND_SLAB
    slab_w = n_slab * HEAD_DIM
    lane_head = lax.broadcasted_iota(I32, (TILE, slab_w), 1) // HEAD_DIM

    def attend(masked):
        outs = []
        for g in range(A_HEADS // n_slab):
            gs = slice(g * slab_w, (g + 1) * slab_w)
            q_slab = q_ref[0, :, gs].astype(F32)
            q_bd = jnp.concatenate([jnp.where(lane_head == hh, q_slab, 0.0) for hh in range(n_slab)],
                                   axis=0).astype(BF16)
            pm = None
            for j in range(BAND_TILES):
                st = lax.dot_general(krefs[j][0, :, gs], q_bd, _NT, preferred_element_type=F32)
                parts = [st[:, hh * TILE:(hh + 1) * TILE] + bias_ref[g * n_slab + hh, j * TILE:(j + 1) * TILE, :]
                         for hh in range(n_slab)]
                if masked:
                    kpos = base + j * TILE + lax.broadcasted_iota(I32, (TILE, TILE), 0)
                    ok = (kpos >= 0) & (kpos < valid_len)
                    parts = [jnp.where(ok, x, NEG) for x in parts]
                sj = jnp.concatenate(parts, axis=1)
                s_ref[g * BAND_TILES + j] = sj
                m8 = sj.reshape(TILE // 8, 8, n_slab * TILE).max(axis=0)
                pm = m8 if pm is None else jnp.maximum(pm, m8)
            m = pm.max(axis=0, keepdims=True)
            p = [jnp.exp(s_ref[g * BAND_TILES + j] - m).astype(BF16) for j in range(BAND_TILES)]
            for hh in range(n_slab):
                acc = None
                for j in range(BAND_TILES):
                    d = jnp.dot(vt_ref[slots[j], g * n_slab + hh], p[j][:, hh * TILE:(hh + 1) * TILE],
                                preferred_element_type=F32)
                    acc = d if acc is None else acc + d
                blk = acc.T
                outs.append(blk[:, 0:HEAD_DIM] / blk[:, HEAD_DIM:HEAD_DIM + 1])
        o_ref[0] = jnp.concatenate(outs, axis=-1).astype(BF16)

    needs_mask = (base < 0) | (base + BAND_COLS > valid_len)

    @pl.when(needs_mask)
    def _():
        attend(True)

    @pl.when(jnp.logical_not(needs_mask))
    def _():
        attend(False)


def _band_attn(q, k, v, g_row, off, valid_len):
    b, tq = q.shape[:2]
    nq = tq // TILE
    qspec = pl.BlockSpec((1, TILE, A_WIDTH), lambda bb, i: (bb, i, 0))
    kspec = lambda d: pl.BlockSpec((1, TILE, A_WIDTH), lambda bb, i: (bb, jnp.maximum(i + off - d, 0), 0))
    return pl.pallas_call(
        functools.partial(_band_kernel, off=off, valid_len=valid_len),
        grid=(b, nq),
        in_specs=[pl.BlockSpec((A_HEADS, BAND_ROLL), lambda bb, i: (0, 0)),
                  qspec, kspec(2), kspec(1), kspec(0), kspec(2), kspec(1), kspec(0)],
        out_specs=pl.BlockSpec((1, TILE, A_WIDTH), lambda bb, i: (bb, i, 0)),
        out_shape=jax.ShapeDtypeStruct((b, tq, A_WIDTH), BF16),
        scratch_shapes=[pltpu.VMEM((A_HEADS, BAND_COLS, TILE), F32),
                        pltpu.VMEM((BAND_TILES, A_HEADS, 2 * HEAD_DIM, TILE), BF16),
                        pltpu.VMEM((A_HEADS // BAND_SLAB * BAND_TILES, TILE, BAND_SLAB * TILE), F32)],
        compiler_params=_params("arbitrary", "arbitrary"),
        name="band_attn",
    )(g_row, q, k, k, k, v, v, v)


DSA_ROLL = 3 * TILE
FAR_STEP = 4


def _t5_bucket(rel):
    half = N_BUCKETS // 2
    max_exact = half // 2
    n = jnp.abs(rel)
    log_ratio = jnp.log(jnp.maximum(n, 1).astype(jnp.float32) / max_exact) / math.log(T5_MAX_DIST / max_exact)
    large = jnp.minimum(max_exact + (log_ratio * (half - max_exact)).astype(jnp.int32), half - 1)
    return jnp.where(rel < 0, half, 0) + jnp.where(n < max_exact, n, large)


def _dsa_bias_row(t5_table):
    idx = jnp.arange(DSA_ROLL)
    d = jnp.where(idx < TILE, idx, idx - DSA_ROLL)
    far = t5_table[_t5_bucket(jnp.full((1,), 2 * TILE + 1, I32))]
    return (t5_table[_t5_bucket(TILE + d)] - far).T.astype(F32)


def _dsa_kernel(g_ref, qi_ref, wi_ref, qb_ref, ki_ref, kb_ref, vb_ref, o_ref,
                sc_ref, scb_ref, keep_ref, vt_ref, qis_ref, qbs_ref, m_ref, acc_ref, bias_ref, s_ref,
                *, off, valid_len, q_valid, n_kt, tq, n_merge):
    i = pl.program_id(1)
    qt = i + off
    q0 = qt * TILE
    nk = qt + 1
    key_rows = lambda j: pl.ds(pl.multiple_of(j * TILE, TILE), TILE)

    @pl.when((pl.program_id(0) == 0) & (i == 0))
    def _():
        for h in range(B_HEADS):
            tile = _toeplitz(g_ref[h:h + 1, :], 2 * TILE, DSA_ROLL)[:, :tq]
            if n_merge:
                ts = tq // n_merge
                stream = lax.broadcasted_iota(I32, (1, tq), 1) // ts
                first = tile
                for b in range(1, n_merge):
                    tile = jnp.where(stream == b, pltpu.roll(first, b * ts, 1), tile)
            bias_ref[h] = tile

    @pl.when(i == 0)
    def _():
        def body(j, c):
            for b in range(max(n_merge, 1)):
                vt = vb_ref[b, key_rows(j), :].astype(F32).T
                for g in range(B_KV_HEADS):
                    vt_ref[g, j, 0:HEAD_DIM, b * TILE:(b + 1) * TILE] = vt[g * HEAD_DIM:(g + 1) * HEAD_DIM].astype(BF16)
            for g in range(B_KV_HEADS):
                vt_ref[g, j, HEAD_DIM:2 * HEAD_DIM, :] = jnp.ones((HEAD_DIM, vt_ref.shape[-1]), BF16)
            return c

        lax.fori_loop(0, n_kt, body, 0)

    colq = lax.broadcasted_iota(I32, (1, tq), 1)
    rowk = lax.broadcasted_iota(I32, (TILE, 1), 0)

    if n_merge:
        width = n_merge * HEAD_DIM
        own = (lax.broadcasted_iota(I32, (tq, width), 0) // (tq // n_merge)
               == lax.broadcasted_iota(I32, (tq, width), 1) // HEAD_DIM)

        def block_diag(x):
            return jnp.where(own, jnp.concatenate([x.astype(F32)] * n_merge, axis=1), 0.0).astype(BF16)

        qi, qb = qi_ref[...], qb_ref[...]
        for h in range(B_HEADS):
            qis_ref[h] = block_diag(qi[:, h * IDX_DIM:(h + 1) * IDX_DIM])
            qbs_ref[h] = block_diag(qb[:, h * HEAD_DIM:(h + 1) * HEAD_DIM])
        side_by_side = lambda ref, j, cols: jnp.concatenate(
            [ref[b, key_rows(j), cols] for b in range(n_merge)], axis=1)
        wi_t = wi_ref[...].T
        lim = jnp.full((1, tq), valid_len, I32)
    else:
        qi = qi_ref[0]
        for h in range(IDX_HEADS):
            qis_ref[h * tq:(h + 1) * tq, :] = qi[:, h * IDX_DIM:(h + 1) * IDX_DIM]
        qb = qb_ref[0]
        for g in range(B_KV_HEADS):
            for hh in range(B_GROUP):
                h = g * B_GROUP + hh
                qbs_ref[g, hh * tq:(hh + 1) * tq, :] = qb[:, h * HEAD_DIM:(h + 1) * HEAD_DIM]
        wi_t = wi_ref[0].T
        lim = jnp.minimum(q0 + (colq // CHUNK + 1) * CHUNK, valid_len)

    def score_tile(j, masked):
        if n_merge:
            kt = side_by_side(ki_ref, j, slice(None))
            lg = jnp.concatenate([lax.dot_general(kt, qis_ref[h], _NT, preferred_element_type=F32)
                                  for h in range(IDX_HEADS)], axis=1)
        else:
            kt = ki_ref[0, key_rows(j), :]
            lg = lax.dot_general(kt, qis_ref[...], _NT, preferred_element_type=F32)
        sc = wi_t[0:1, :] * jnp.maximum(lg[:, 0:tq], 0.0)
        for h in range(1, IDX_HEADS):
            sc = sc + wi_t[h:h + 1, :] * jnp.maximum(lg[:, h * tq:(h + 1) * tq], 0.0)
        if masked:
            sc = jnp.where(j * TILE + rowk < lim, sc, -jnp.inf)
        sc_ref[j] = sc
        scb_ref[j] = sc.astype(BF16)
        rows8 = lambda hit: jnp.where(hit, 1, 0).reshape(TILE // 8, 8, tq).sum(axis=0)
        return rows8(sc > 0.0), rows8(sc >= 0.0)

    def score_pair(jj, c, masked):
        pos_a, nn_a = score_tile(2 * jj, masked)
        pos_b, nn_b = score_tile(2 * jj + 1, masked)
        return c[0] + pos_a + pos_b, c[1] + nn_a + nn_b

    n_open_pairs = (nk - 1) // 2
    counts = lax.fori_loop(0, n_open_pairs, functools.partial(score_pair, masked=False),
                           (jnp.zeros((8, tq), I32), jnp.zeros((8, tq), I32)))
    n_pos, n_nonneg = lax.fori_loop(n_open_pairs, (nk + 1) // 2, functools.partial(score_pair, masked=True), counts)

    def f32_of_key(k):
        return pltpu.bitcast(jnp.where(k < 0, k ^ 0x7FFFFFFF, k), F32)

    def bf16_of_key(k):
        bits = jnp.where(k < 0, k ^ 0x7FFF, k) & 0xFFFF
        return pltpu.bitcast(lax.shift_left(bits, 16), F32).astype(BF16)

    def count_bf16(cand):
        def body(jj, acc):
            for j in (2 * jj, 2 * jj + 1):
                ge = jnp.where(scb_ref[j] >= cand, jnp.int16(1), jnp.int16(0)).reshape(TILE // 16, 16, tq)
                part = ge[0]
                for r in range(1, TILE // 16):
                    part = part + ge[r]
                acc = acc + part
            return acc

        acc = lax.fori_loop(0, (nk + 1) // 2, body, jnp.zeros((16, tq), I16))
        return acc.astype(I32).sum(axis=0, keepdims=True)

    def count_f32(cand, strict=False):
        def body(j, acc):
            blk = sc_ref[j]
            hit = (blk > cand) if strict else (blk >= cand)
            return acc + jnp.where(hit, 1, 0).reshape(TILE // 8, 8, tq).sum(axis=0)

        return lax.fori_loop(0, nk, body, jnp.zeros((8, tq), I32)).sum(axis=0, keepdims=True)

    c_pos = n_pos.sum(axis=0, keepdims=True)
    zero_tie = (c_pos < TOPK_MAX) & (n_nonneg.sum(axis=0, keepdims=True) >= TOPK_MAX)
    skip1 = (colq >= q_valid) | zero_tie

    def level1(it, t):
        cand = t + lax.shift_left(jnp.int32(1), 15 - it)
        return jnp.where(count_bf16(bf16_of_key(cand)) >= TOPK_MAX, cand, t)

    t1 = lax.fori_loop(0, 16, level1, jnp.full((1, tq), MIN16, I32))
    settled1 = skip1 | (t1 <= HI_NEG_INF)

    def level2(st):
        lo, hi, thr_key, done = st
        live = (done == 0) & (hi - lo > 1)
        mid = lo + lax.shift_right_arithmetic(hi - lo, 1)
        c = count_f32(f32_of_key(mid))
        hit = live & (c == TOPK_MAX)
        return (jnp.where(live & (c >= TOPK_MAX), mid, lo), jnp.where(live & (c < TOPK_MAX), mid, hi),
                jnp.where(hit, mid, thr_key), jnp.where(hit, 1, done))

    def n_live(st):
        lo, hi, _, done = st
        return jnp.sum(jnp.where((done == 0) & (hi - lo > 1), 1, 0))

    def level2_pair(carry):
        st = level2(level2(carry[0]))
        return st, n_live(st)

    def key32_of_key16(k):
        return lax.shift_left(k, 16) | jnp.where(k < 0, 0xFFFF, 0)

    key_t1 = key32_of_key16(t1)
    st0 = (key_t1 - 0x8000, key32_of_key16(t1 + 1), key_t1, jnp.where(settled1, 1, 0))
    (lo, _, thr_key, done2), _ = lax.while_loop(lambda carry: carry[1] > 0, level2_pair, (st0, n_live(st0)))
    open2 = done2 == 0
    thr = jnp.where(open2, f32_of_key(lo), f32_of_key(thr_key))
    thr = jnp.where(zero_tie, 0.0, thr)
    thr = jnp.where(settled1 & jnp.logical_not(zero_tie), -jnp.inf, thr)
    thr = jnp.maximum(thr, float(jnp.finfo(F32).min))

    keep_ref[...] = jnp.where(zero_tie, TOPK_MAX - c_pos, 2 ** 30).astype(F32)

    @pl.when(jnp.sum(jnp.where(open2, 1, 0)) > 0)
    def _():
        above = count_f32(thr, strict=True)
        keep_ref[...] = jnp.where(open2, (TOPK_MAX - above).astype(F32), keep_ref[...])

    @pl.when(jnp.sum(jnp.where(open2 | zero_tie, 1, 0)) > 0)
    def _():
        keep = keep_ref[...]
        lower = (lax.broadcasted_iota(I32, (TILE, TILE), 0) > lax.broadcasted_iota(I32, (TILE, TILE), 1))
        lower = jnp.where(lower, 1.0, 0.0).astype(BF16)

        def body(jj, run):
            for j in (2 * jj, 2 * jj + 1):
                blk = sc_ref[j]
                eq = blk == thr
                eq_f = jnp.where(eq, 1.0, 0.0)
                before = jnp.dot(lower, eq_f.astype(BF16), preferred_element_type=F32)
                sc_ref[j] = jnp.where(eq & (run + before >= keep), -jnp.inf, blk)
                run = run + eq_f.reshape(TILE // 8, 8, tq).sum(axis=0).sum(axis=0, keepdims=True)
            return run

        lax.fori_loop(0, (nk + 1) // 2, body, jnp.zeros((1, tq), F32))

    for g in range(B_KV_HEADS):
        m_ref[g] = jnp.full((1, B_GROUP * tq), NEG, F32)
        acc_ref[g] = jnp.zeros((2 * HEAD_DIM, B_GROUP * tq), F32)

    def attend(tiles):
        sels = [sc_ref[j] >= thr for j, _ in tiles]
        part_max = []
        for g in range(B_KV_HEADS):
            pm = None
            for t, ((j, near), sel) in enumerate(zip(tiles, sels)):
                if n_merge:
                    kt = side_by_side(kb_ref, j, slice(g * HEAD_DIM, (g + 1) * HEAD_DIM))
                    st = jnp.concatenate([lax.dot_general(kt, qbs_ref[g * B_GROUP + hh], _NT,
                                                          preferred_element_type=F32)
                                          for hh in range(B_GROUP)], axis=1)
                else:
                    kt = kb_ref[0, key_rows(j), g * HEAD_DIM:(g + 1) * HEAD_DIM]
                    st = lax.dot_general(kt, qbs_ref[g], _NT, preferred_element_type=F32)
                parts = []
                for hh in range(B_GROUP):
                    s_h = st[:, hh * tq:(hh + 1) * tq]
                    if near is not None:
                        s_h = s_h + bias_ref[g * B_GROUP + hh, near * TILE:(near + 1) * TILE, :]
                    parts.append(jnp.where(sel, s_h, -jnp.inf))
                s = jnp.concatenate(parts, axis=1)
                s_ref[g * len(tiles) + t] = s
                m8 = s.reshape(TILE // 8, 8, B_GROUP * tq).max(axis=0)
                pm = m8 if pm is None else jnp.maximum(pm, m8)
            part_max.append(pm)
        for g in range(B_KV_HEADS):
            m_old = m_ref[g]
            m_new = jnp.maximum(m_old, part_max[g].max(axis=0, keepdims=True))
            pv = None
            for t, (j, _) in enumerate(tiles):
                p = jnp.exp(s_ref[g * len(tiles) + t] - m_new)
                if n_merge:
                    lane_stream = colq // (tq // n_merge)
                    d = jnp.concatenate(
                        [jnp.dot(vt_ref[g, j],
                                 jnp.concatenate([jnp.where(lane_stream == b, p[:, hh * tq:(hh + 1) * tq], 0.0)
                                                  for b in range(n_merge)], axis=0).astype(BF16),
                                 preferred_element_type=F32) for hh in range(B_GROUP)], axis=1)
                else:
                    d = jnp.dot(vt_ref[g, j], p.astype(BF16), preferred_element_type=F32)
                pv = d if pv is None else pv + d
            acc_ref[g] = jnp.exp(m_old - m_new) * acc_ref[g] + pv
            m_ref[g] = m_new

    n_far = jnp.maximum(nk - 2, 0)

    def far_step(jj, c):
        attend([(FAR_STEP * jj + t, None) for t in range(FAR_STEP)])
        return c

    lax.fori_loop(0, n_far // FAR_STEP, far_step, 0)
    rem = n_far % FAR_STEP

    for r in range(FAR_STEP):
        @pl.when((rem == r) & (nk >= 2))
        def _(r=r):
            attend([(n_far - r + t, None) for t in range(r)] + [(nk - 2, 0), (nk - 1, 1)])

    @pl.when(nk < 2)
    def _():
        attend([(nk - 1, 1)])

    outs = []
    for g in range(B_KV_HEADS):
        for hh in range(B_GROUP):
            blk = acc_ref[g, :, hh * tq:(hh + 1) * tq].T
            outs.append(blk[:, 0:HEAD_DIM] / blk[:, HEAD_DIM:HEAD_DIM + 1])
    o_ref[0] = jnp.concatenate(outs, axis=-1).astype(BF16)


def _dsa_attn(qi, wi, qb, ki, kb, vb, g_row, off, valid_len, q_valid, tq):
    b, t_q = qi.shape[:2]
    tk = ki.shape[1]
    nq, n_kt = t_q // tq, tk // TILE
    assert tq == TILE or nq == 1
    assert n_kt % 2 == 0 and tk == n_kt * TILE
    qspec = lambda n: pl.BlockSpec((1, tq, n), lambda bb, i: (bb, i, 0))
    kspec = lambda n: pl.BlockSpec((1, tk, n), lambda bb, i: (bb, 0, 0))
    in_specs = [pl.BlockSpec((B_HEADS, DSA_ROLL), lambda bb, i: (0, 0)),
                qspec(512), qspec(128), qspec(512), kspec(IDX_DIM), kspec(128), kspec(128)]
    return _dsa_call((g_row, qi, wi, qb, ki, kb, vb), in_specs, (b, nq), t_q, tq, n_kt,
                     dict(off=off, valid_len=valid_len, q_valid=q_valid, n_merge=0))


def _dsa_attn_merged(qi, wi, qb, ki, kb, vb, g_row, off, valid_len):
    s, ts = qi.shape[:2]
    tk = ki.shape[1]
    tq, n_kt = s * ts, tk // TILE
    assert tq % LANES == 0 and ts <= CHUNK and n_kt % 2 == 0 and tk == n_kt * TILE
    flat = lambda a: a.reshape(tq, a.shape[-1])
    operands = (g_row, flat(qi), flat(wi), flat(qb), ki, kb, vb)
    whole = lambda a: pl.BlockSpec(a.shape, lambda bb, i: (0,) * a.ndim, pipeline_mode=pl.Buffered(1))
    out = _dsa_call(operands, [whole(a) for a in operands], (1, 1), tq, tq, n_kt,
                    dict(off=off, valid_len=valid_len, q_valid=tq, n_merge=s))
    return out.reshape(s, ts, B_WIDTH)


def _dsa_call(operands, in_specs, grid, t_q, tq, n_kt, static):
    n_merge = static["n_merge"]
    assert static["valid_len"] >= (static["off"] + grid[1] - 1) * TILE
    if n_merge:
        q_scratch = [pltpu.VMEM((IDX_HEADS, tq, n_merge * IDX_DIM), BF16),
                     pltpu.VMEM((B_HEADS, tq, n_merge * HEAD_DIM), BF16)]
    else:
        q_scratch = [pltpu.VMEM((IDX_HEADS * tq, IDX_DIM), BF16),
                     pltpu.VMEM((B_KV_HEADS, B_GROUP * tq, HEAD_DIM), BF16)]
    return pl.pallas_call(
        functools.partial(_dsa_kernel, n_kt=n_kt, tq=tq, **static),
        grid=grid,
        in_specs=in_specs,
        out_specs=pl.BlockSpec((1, tq, B_WIDTH), lambda bb, i: (bb, i, 0)),
        out_shape=jax.ShapeDtypeStruct((grid[0], t_q, B_WIDTH), BF16),
        scratch_shapes=[pltpu.VMEM((n_kt, TILE, tq), F32),
                        pltpu.VMEM((n_kt, TILE, tq), BF16),
                        pltpu.VMEM((1, tq), F32),
                        pltpu.VMEM((B_KV_HEADS, n_kt, 2 * HEAD_DIM, max(n_merge, 1) * TILE), BF16),
                        *q_scratch,
                        pltpu.VMEM((B_KV_HEADS, 1, B_GROUP * tq), F32),
                        pltpu.VMEM((B_KV_HEADS, 2 * HEAD_DIM, B_GROUP * tq), F32),
                        pltpu.VMEM((B_HEADS, 2 * TILE, tq), F32),
                        pltpu.VMEM((B_KV_HEADS * (FAR_STEP + 1), TILE, B_GROUP * tq), F32)],
        compiler_params=_params("arbitrary", "arbitrary"),
        name="dsa_attn",
    )(*operands)


def _mem_kv_kernel(m_ref, wk_ref, wv_ref, k_o, v_o, kb_o, vb_o):
    mb = m_ref[...].astype(BF16)
    k = jnp.dot(mb, wk_ref[...], preferred_element_type=F32)
    v = jnp.dot(mb, wv_ref[...], preferred_element_type=F32)
    k_o[...] = k
    v_o[...] = v
    kb_o[...] = k.astype(BF16)
    vb_o[...] = v.astype(BF16)


def _mem_kv(mem2d, wk, wv):
    r = mem2d.shape[0]
    tm = MEM_LEN
    row = lambda n: pl.BlockSpec((tm, n), lambda i: (i, 0))
    const = lambda s: pl.BlockSpec(s, lambda i: (0, 0))
    sds = jax.ShapeDtypeStruct
    return pl.pallas_call(
        _mem_kv_kernel,
        grid=(r // tm,),
        in_specs=[row(D_MODEL), const((D_MODEL, MEM_WIDTH)), const((D_MODEL, MEM_WIDTH))],
        out_specs=[row(MEM_WIDTH)] * 4,
        out_shape=[sds((r, MEM_WIDTH), F32), sds((r, MEM_WIDTH), F32),
                   sds((r, MEM_WIDTH), BF16), sds((r, MEM_WIDTH), BF16)],
        compiler_params=_params("arbitrary"),
        name="mem_kv",
    )(mem2d, wk, wv)


FF_CHUNK = 256


def _tail_kernel(x_ref, oa_ref, ob_ref, mk_ref, mv_ref, hist_ref,
                 wo_ref, g1_ref, b1_ref, wq_ref, wmo_ref, g2_ref, b2_ref,
                 wu_ref, wc_ref, bc_ref, wd_ref, g3_ref, b3_ref,
                 o_ref, tail_ref, carry_ref, act_ref, *, tiles_per_batch, seg):
    i = pl.program_id(0)
    tm = x_ref.shape[0]
    nseg = tm // seg

    mix = jnp.concatenate([oa_ref[...], ob_ref[...]], axis=-1)
    h = _layer_norm(ALPHA * x_ref[...] + jnp.dot(mix, wo_ref[...], preferred_element_type=F32),
                    g1_ref[...], b1_ref[...])

    q = jnp.dot(h.astype(BF16), wq_ref[...], preferred_element_type=F32).astype(BF16)
    segs = []
    for s in range(nseg):
        qs = q[s * seg:(s + 1) * seg]
        heads = []
        for hd in range(MEM_HEADS):
            sl = slice(hd * MEM_HEAD_DIM, (hd + 1) * MEM_HEAD_DIM)
            sc = lax.dot_general(qs[:, sl], mk_ref[s, :, sl], _NT, preferred_element_type=F32) * MEM_HEAD_DIM ** -0.5
            p = jnp.exp(sc - sc.max(-1, keepdims=True))
            l = p.sum(-1, keepdims=True)
            heads.append(jnp.dot(p.astype(BF16), mv_ref[s, :, sl], preferred_element_type=F32) / l)
        segs.append(jnp.concatenate(heads, axis=-1))
    att = jnp.concatenate(segs, axis=0).astype(BF16)
    h = _layer_norm(ALPHA * h + jnp.dot(att, wmo_ref[...], preferred_element_type=F32), g2_ref[...], b2_ref[...])

    hb = h.astype(BF16)
    row = lax.broadcasted_iota(I32, (tm, 1), 0)
    first = (i % tiles_per_batch) == 0
    for c in range(D_FF // FF_CHUNK):
        cs = slice(c * FF_CHUNK, (c + 1) * FF_CHUNK)
        u = jnp.dot(hb, wu_ref[:, cs], preferred_element_type=F32)
        gt = jnp.dot(hb, wu_ref[:, D_FF + c * FF_CHUNK:D_FF + (c + 1) * FF_CHUNK], preferred_element_type=F32)
        p1 = pltpu.roll(gt, 1, 0)
        p2 = pltpu.roll(gt, 2, 0)
        for s in range(nseg):
            hist = hist_ref[s, :, cs]
            if tiles_per_batch > 1:
                hist = jnp.where(first, hist, carry_ref[:, cs])
            p1 = jnp.where(row == s * seg, hist[7:8, :], p1)
            p2 = jnp.where(row == s * seg, hist[6:7, :], p2)
            p2 = jnp.where(row == s * seg + 1, hist[7:8, :], p2)
        gc = bc_ref[:, cs] + ((wc_ref[0:1, cs] * p2 + wc_ref[1:2, cs] * p1) + wc_ref[2:3, cs] * gt)
        act_ref[:, cs] = (u * jax.nn.gelu(gc)).astype(BF16)
        for s in range(nseg):
            tail_ref[s, :, cs] = gt[(s + 1) * seg - 8:(s + 1) * seg, :]
        carry_ref[:, cs] = gt[tm - 8:tm, :]
    f = jnp.dot(act_ref[...], wd_ref[...], preferred_element_type=F32)
    o_ref[...] = _layer_norm(ALPHA * h + f, g3_ref[...], b3_ref[...])


def _layer_tail(x2d, oa, ob, mk, mv, hist, weights, tm, tiles_per_batch, seg):
    r = x2d.shape[0]
    nseg = tm // seg
    n_stream = r // (tm * tiles_per_batch) * nseg
    row = lambda n: pl.BlockSpec((tm, n), lambda i: (i, 0))
    per_stream = lambda a, b: pl.BlockSpec((nseg, a, b), lambda i: (i // tiles_per_batch, 0, 0))
    const = lambda a: pl.BlockSpec(a.shape, lambda i: (0,) * a.ndim, pipeline_mode=pl.Buffered(1))
    return pl.pallas_call(
        functools.partial(_tail_kernel, tiles_per_batch=tiles_per_batch, seg=seg),
        grid=(r // tm,),
        in_specs=[row(D_MODEL), row(A_WIDTH), row(B_WIDTH), per_stream(MEM_LEN, MEM_WIDTH),
                  per_stream(MEM_LEN, MEM_WIDTH), per_stream(8, D_FF)] + [const(w) for w in weights],
        out_specs=[row(D_MODEL), per_stream(8, D_FF)],
        out_shape=[jax.ShapeDtypeStruct((r, D_MODEL), F32), jax.ShapeDtypeStruct((n_stream, 8, D_FF), F32)],
        scratch_shapes=[pltpu.VMEM((8, D_FF), F32), pltpu.VMEM((tm, D_FF), BF16)],
        compiler_params=_params("arbitrary"),
        name="layer_tail",
    )(x2d, oa, ob, mk, mv, hist, *weights)


def _pad_rows(a, n):
    return jnp.pad(a, ((0, 0), (0, n - a.shape[1])) + ((0, 0),) * (a.ndim - 2))


def _hist8(g_hist):
    return jnp.pad(g_hist, ((0, 0), (8 - g_hist.shape[1], 0), (0, 0)))


def kernel(x_prompt, x_sample, cache_a_k, cache_a_v, cache_b_k, cache_b_v, cache_b_kidx, cache_mem_k, cache_mem_v, state_ffn_conv, mem_prompt, w_in, a_rel_bias, t5_bias, w_o, ln1_g, ln1_b, w_mq, w_mk, w_mv, w_mo, ln2_g, ln2_b, w_up, w_conv, b_conv, w_down, ln3_g, ln3_b):
    bp, tp = x_prompt.shape[:2]
    bs, ts = x_sample.shape[:2]
    l = 0
    vec = lambda a: a[l].reshape(1, -1)
    w_in_p = _prep_w_in(w_in[l])
    w_o_b = w_o[l].astype(BF16)
    w_mq_b, w_mk_b, w_mv_b, w_mo_b = (w[l].astype(BF16) for w in (w_mq, w_mk, w_mv, w_mo))
    w_up_b, w_down_b = w_up[l].astype(BF16), w_down[l].astype(BF16)
    band_row = _band_bias_row(a_rel_bias[l])
    dsa_row = _dsa_bias_row(t5_bias)
    tail_w = (w_o_b, vec(ln1_g), vec(ln1_b), w_mq_b, w_mo_b, vec(ln2_g), vec(ln2_b),
              w_up_b, w_conv[l], vec(b_conv), w_down_b, vec(ln3_g), vec(ln3_b))

    tm = 512
    a_keep = min(N_PREV_CHUNKS * CHUNK, tp)
    (qa, ka, va, qb, qi, kb, vb, ki, kb_b, vb_b, ki_b, wi, ka_tail, va_tail) = _in_proj(
        x_prompt.reshape(bp * tp, D_MODEL), w_in_p, tm, tp // tm)
    r3 = lambda a: a.reshape(bp, tp, a.shape[-1])
    oa = _band_attn(r3(qa), r3(ka), r3(va), band_row, 0, tp)
    ob = _dsa_attn(r3(qi), r3(wi), r3(qb), r3(ki_b), r3(kb_b), r3(vb_b), dsa_row, 0, tp, TILE, TILE)
    mk, mv, mk_b, mv_b = _mem_kv(mem_prompt.reshape(bp * MEM_LEN, D_MODEL), w_mk_b, w_mv_b)
    xp, p_tail = _layer_tail(x_prompt.reshape(bp * tp, D_MODEL), oa.reshape(bp * tp, A_WIDTH),
                             ob.reshape(bp * tp, B_WIDTH), mk_b.reshape(bp, MEM_LEN, MEM_WIDTH),
                             mv_b.reshape(bp, MEM_LEN, MEM_WIDTH), jnp.zeros((bp, 8, D_FF), F32), tail_w,
                             tm, tp // tm, tm)
    prompt_state = (
        ka_tail.reshape(bp, tm, A_HEADS, HEAD_DIM)[:, tm - a_keep:][None],
        va_tail.reshape(bp, tm, A_HEADS, HEAD_DIM)[:, tm - a_keep:][None],
        kb.reshape(1, bp, tp, B_KV_HEADS, HEAD_DIM), vb.reshape(1, bp, tp, B_KV_HEADS, HEAD_DIM),
        ki.reshape(1, bp, tp, IDX_DIM),
        mk.reshape(1, bp, MEM_LEN, MEM_HEADS, MEM_HEAD_DIM), mv.reshape(1, bp, MEM_LEN, MEM_HEADS, MEM_HEAD_DIM),
        p_tail[:, 8 - (CONV_W - 1):][None])

    rs = bs * ts
    (qa, ka, va, qb, qi, kb, vb, ki, kb_b, vb_b, ki_b, wi, ka_new, va_new) = _in_proj(
        x_sample.reshape(rs, D_MODEL), w_in_p, rs, 1)
    s3 = lambda a: a.reshape(bs, ts, -1)
    qpad = lambda a, n: _pad_rows(s3(a), n)

    past_a = cache_a_k.shape[2]
    n_a = past_a + ts
    t_a = -(-n_a // TILE) * TILE
    seq_a = lambda cache, new: _pad_rows(
        jnp.concatenate([cache[l].reshape(bs, past_a, A_WIDTH), s3(new)], axis=1), t_a).astype(BF16)
    oa = _band_attn(qpad(qa, TILE), seq_a(cache_a_k, ka_new), seq_a(cache_a_v, va_new), band_row,
                    past_a // TILE, n_a)

    past_b = cache_b_k.shape[2]
    n_b = past_b + ts
    t_b = -(-n_b // (2 * TILE)) * 2 * TILE
    seq_b = lambda cache, new: _pad_rows(
        jnp.concatenate([cache[l].reshape(bs, past_b, -1), s3(new)], axis=1), t_b).astype(BF16)
    ob = _dsa_attn_merged(s3(qi), s3(wi), s3(qb), seq_b(cache_b_kidx, ki), seq_b(cache_b_k, kb),
                          seq_b(cache_b_v, vb), dsa_row, past_b // TILE, n_b)

    xs, s_tail = _layer_tail(x_sample.reshape(rs, D_MODEL), oa[:, :ts].reshape(rs, A_WIDTH),
                             ob[:, :ts].reshape(rs, B_WIDTH),
                             cache_mem_k[l].reshape(bs, MEM_LEN, MEM_WIDTH).astype(BF16),
                             cache_mem_v[l].reshape(bs, MEM_LEN, MEM_WIDTH).astype(BF16),
                             _hist8(state_ffn_conv[l]), tail_w, rs, 1, ts)
    sample_state = (
        ka_new.reshape(1, bs, ts, A_HEADS, HEAD_DIM), va_new.reshape(1, bs, ts, A_HEADS, HEAD_DIM),
        kb.reshape(1, bs, ts, B_KV_HEADS, HEAD_DIM), vb.reshape(1, bs, ts, B_KV_HEADS, HEAD_DIM),
        ki.reshape(1, bs, ts, IDX_DIM), s_tail[:, 8 - (CONV_W - 1):][None])

    return (xp.reshape(bp, tp, D_MODEL), xs.reshape(bs, ts, D_MODEL)) + prompt_state + sample_state
```

```python
import functools
import math

import jax
import jax.numpy as jnp
from jax import lax
from jax.experimental import pallas as pl
from jax.experimental.pallas import tpu as pltpu

F32 = jnp.float32
BF16 = jnp.bfloat16
I32 = jnp.int32
I16 = jnp.int16

D_MODEL = 1024
CHUNK = 64
N_PREV_CHUNKS = 8
HEAD_DIM = 64
A_HEADS = 8
A_WIDTH = A_HEADS * HEAD_DIM
A_MAX_REL = 64
B_HEADS = 8
B_KV_HEADS = 2
B_GROUP = B_HEADS // B_KV_HEADS
B_WIDTH = B_HEADS * HEAD_DIM
B_KV_WIDTH = B_KV_HEADS * HEAD_DIM
IDX_HEADS = 8
IDX_DIM = 64
TOPK_MAX = 256
N_BUCKETS = 32
T5_MAX_DIST = 128
MEM_LEN = 256
MEM_HEADS = 4
MEM_HEAD_DIM = 128
MEM_WIDTH = MEM_HEADS * MEM_HEAD_DIM
D_FF = 2816
CONV_W = 3
IN_SIZES = (A_WIDTH, A_WIDTH, A_WIDTH, B_WIDTH, B_KV_WIDTH, B_KV_WIDTH, IDX_HEADS * IDX_DIM, IDX_DIM, IDX_HEADS)
DEPTH = 1
ALPHA = (2 * DEPTH) ** 0.25
LN_EPS = 1e-5
ATTN_SCALE = HEAD_DIM ** -0.5
NEG = -1e30

LANES = 128
TILE = 256
BAND_TILES = 1 + (N_PREV_CHUNKS * CHUNK) // TILE
VMEM_LIMIT = 56 * 1024 * 1024

_C_QA, _C_KA, _C_VA, _C_QB = 0, 512, 1024, 1536
_C_KB, _C_VB, _C_QI, _C_KI, _C_WI = 2048, 2176, 2304, 2816, 2944
IN_PAD = 3072

MIN16 = -32768
HI_NEG_INF = -32641

_NT = (((1,), (1,)), ((), ()))


def _params(*sem):
    return pltpu.CompilerParams(dimension_semantics=sem, vmem_limit_bytes=VMEM_LIMIT)


def _layer_norm(z, g, b):
    mu = jnp.mean(z, axis=-1, keepdims=True)
    d = z - mu
    var = jnp.mean(d * d, axis=-1, keepdims=True)
    return d * lax.rsqrt(var + LN_EPS) * g + b


def _toeplitz(g_row, rows, width):
    return pltpu.roll(jnp.broadcast_to(g_row, (rows, width)), 0, 1, stride=1, stride_axis=0)


def _in_proj_kernel(x_ref, w_ref, qa_o, ka_o, va_o, qb_o, qi_o, kb_o, vb_o, ki_o, kbb_o, vbb_o, kib_o,
                    wi_o, kat_o, vat_o, *, tiles_per_batch, wi_scale):
    i = pl.program_id(0)
    xb = x_ref[...].astype(BF16)

    def mm(c0, n):
        return jnp.dot(xb, w_ref[:, c0:c0 + n], preferred_element_type=F32)

    kb = mm(_C_KB, 128)
    vb = mm(_C_VB, 128)
    for g in range(B_KV_HEADS):
        kb_o[:, g, :] = kb[:, g * HEAD_DIM:(g + 1) * HEAD_DIM]
        vb_o[:, g, :] = vb[:, g * HEAD_DIM:(g + 1) * HEAD_DIM]
    kbb_o[...] = kb.astype(BF16)
    vbb_o[...] = vb.astype(BF16)
    ki = mm(_C_KI, 128)[:, :IDX_DIM]
    ki_o[...] = ki
    kib_o[...] = ki.astype(BF16)
    wi_o[...] = mm(_C_WI, 128) * wi_scale
    qa_o[...] = mm(_C_QA, 512).astype(BF16)
    ka = mm(_C_KA, 512)
    va = mm(_C_VA, 512)
    ka_o[...] = ka.astype(BF16)
    va_o[...] = va.astype(BF16)
    qb_o[...] = mm(_C_QB, 512).astype(BF16)
    qi_o[...] = mm(_C_QI, 512).astype(BF16)

    @pl.when(i % tiles_per_batch == tiles_per_batch - 1)
    def _():
        kat_o[...] = ka
        vat_o[...] = va


def _prep_w_in(w):
    parts, off = [], 0
    for n in IN_SIZES:
        parts.append(w[:, off:off + n])
        off += n
    qa, ka, va, qb, kb, vb, qi, ki, wi = parts
    pad = lambda a, n: jnp.pad(a, ((0, 0), (0, n - a.shape[1])))
    cols = [qa * ATTN_SCALE, ka, va, qb * ATTN_SCALE, kb, vb, qi * IDX_DIM ** -0.5, pad(ki, 128), pad(wi, 128)]
    return jnp.concatenate(cols, axis=1).astype(BF16)


def _in_proj(x2d, w_pad, tm, tiles_per_batch):
    r = x2d.shape[0]
    n_tiles = r // tm
    n_batch = n_tiles // tiles_per_batch
    row = lambda n: pl.BlockSpec((tm, n), lambda i: (i, 0))
    tail = pl.BlockSpec((tm, 512), lambda i: (i // tiles_per_batch, 0))
    sds = jax.ShapeDtypeStruct
    kv_state = pl.BlockSpec((tm, B_KV_HEADS, HEAD_DIM), lambda i: (i, 0, 0))
    kv_shape = sds((r, B_KV_HEADS, HEAD_DIM), F32)
    out_shape = [sds((r, 512), BF16)] * 5 + [kv_shape, kv_shape, sds((r, IDX_DIM), F32),
                                              sds((r, 128), BF16), sds((r, 128), BF16), sds((r, IDX_DIM), BF16),
                                              sds((r, 128), F32),
                                              sds((n_batch * tm, 512), F32), sds((n_batch * tm, 512), F32)]
    out_specs = [row(512)] * 5 + [kv_state, kv_state, row(IDX_DIM), row(128), row(128), row(IDX_DIM), row(128),
                                  tail, tail]
    return pl.pallas_call(
        functools.partial(_in_proj_kernel, tiles_per_batch=tiles_per_batch, wi_scale=IDX_HEADS ** -0.5),
        grid=(n_tiles,),
        in_specs=[pl.BlockSpec((tm, D_MODEL), lambda i: (i, 0)),
                  pl.BlockSpec((D_MODEL, IN_PAD), lambda i: (0, 0))],
        out_specs=out_specs,
        out_shape=out_shape,
        compiler_params=_params("arbitrary"),
        name="in_proj",
    )(x2d, w_pad)


BAND_COLS = BAND_TILES * TILE
BAND_ROLL = BAND_COLS + TILE
BAND_SLAB = 4


def _band_bias_row(table):
    idx = jnp.arange(BAND_ROLL)
    d = jnp.where(idx < TILE, idx, idx - BAND_ROLL)
    rel = (BAND_TILES - 1) * TILE + d
    return table[jnp.clip(rel, -A_MAX_REL, A_MAX_REL) + A_MAX_REL].T.astype(F32)


def _band_kernel(g_ref, q_ref, k0, k1, k2, v0, v1, v2, o_ref, bias_ref, vt_ref, s_ref, *, off, valid_len):
    i = pl.program_id(1)
    kt = i + off
    krefs, vrefs = (k0, k1, k2), (v0, v1, v2)

    @pl.when((pl.program_id(0) == 0) & (i == 0))
    def _():
        c = lax.broadcasted_iota(I32, (BAND_COLS, TILE), 0) // CHUNK
        r = lax.broadcasted_iota(I32, (BAND_COLS, TILE), 1) // CHUNK
        ok = (c >= r) & (c <= r + N_PREV_CHUNKS)
        for h in range(A_HEADS):
            bias_ref[h] = jnp.where(ok, _toeplitz(g_ref[h:h + 1, :], BAND_COLS, BAND_ROLL)[:, :TILE], NEG)
        ones = jnp.ones((HEAD_DIM, TILE), BF16)
        for s in range(BAND_TILES):
            for h in range(A_HEADS):
                vt_ref[s, h, HEAD_DIM:2 * HEAD_DIM, :] = ones

    def put(slot, vref):
        vt = vref[0].astype(F32).T
        for h in range(A_HEADS):
            vt_ref[slot, h, 0:HEAD_DIM, :] = vt[h * HEAD_DIM:(h + 1) * HEAD_DIM].astype(BF16)

    slots = [(kt + 1 + j) % BAND_TILES for j in range(BAND_TILES)]

    @pl.when(i == 0)
    def _():
        for j in range(BAND_TILES - 1):
            put(slots[j], vrefs[j])

    put(slots[BAND_TILES - 1], vrefs[BAND_TILES - 1])

    base = (kt - (BAND_TILES - 1)) * TILE

    n_slab = BAND_SLAB
    slab_w = n_slab * HEAD_DIM
    lane_head = lax.broadcasted_iota(I32, (TILE, slab_w), 1) // HEAD_DIM

    def attend(masked):
        outs, maxes = [], []
        for g in range(A_HEADS // n_slab):
            gs = slice(g * slab_w, (g + 1) * slab_w)
            q_slab = q_ref[0, :, gs].astype(F32)
            q_bd = jnp.concatenate([jnp.where(lane_head == hh, q_slab, 0.0) for hh in range(n_slab)],
                                   axis=0).astype(BF16)
            pm = None
            for j in range(BAND_TILES):
                st = lax.dot_general(krefs[j][0, :, gs], q_bd, _NT, preferred_element_type=F32)
                parts = [st[:, hh * TILE:(hh + 1) * TILE] + bias_ref[g * n_slab + hh, j * TILE:(j + 1) * TILE, :]
                         for hh in range(n_slab)]
                if masked:
                    kpos = base + j * TILE + lax.broadcasted_iota(I32, (TILE, TILE), 0)
                    ok = (kpos >= 0) & (kpos < valid_len)
                    parts = [jnp.where(ok, x, NEG) for x in parts]
                sj = jnp.concatenate(parts, axis=1)
                s_ref[g * BAND_TILES + j] = sj
                m8 = sj.reshape(TILE // 8, 8, n_slab * TILE).max(axis=0)
                pm = m8 if pm is None else jnp.maximum(pm, m8)
            maxes.append(pm.max(axis=0, keepdims=True))
        for g, m in enumerate(maxes):
            p = [jnp.exp(s_ref[g * BAND_TILES + j] - m).astype(BF16) for j in range(BAND_TILES)]
            for hh in range(n_slab):
                acc = None
                for j in range(BAND_TILES):
                    d = jnp.dot(vt_ref[slots[j], g * n_slab + hh], p[j][:, hh * TILE:(hh + 1) * TILE],
                                preferred_element_type=F32)
                    acc = d if acc is None else acc + d
                blk = acc.T
                outs.append(blk[:, 0:HEAD_DIM] / blk[:, HEAD_DIM:HEAD_DIM + 1])
        o_ref[0] = jnp.concatenate(outs, axis=-1).astype(BF16)

    needs_mask = (base < 0) | (base + BAND_COLS > valid_len)

    @pl.when(needs_mask)
    def _():
        attend(True)

    @pl.when(jnp.logical_not(needs_mask))
    def _():
        attend(False)


def _band_attn(q, k, v, g_row, off, valid_len):
    b, tq = q.shape[:2]
    nq = tq // TILE
    qspec = pl.BlockSpec((1, TILE, A_WIDTH), lambda bb, i: (bb, i, 0))
    kspec = lambda d: pl.BlockSpec((1, TILE, A_WIDTH), lambda bb, i: (bb, jnp.maximum(i + off - d, 0), 0))
    return pl.pallas_call(
        functools.partial(_band_kernel, off=off, valid_len=valid_len),
        grid=(b, nq),
        in_specs=[pl.BlockSpec((A_HEADS, BAND_ROLL), lambda bb, i: (0, 0)),
                  qspec, kspec(2), kspec(1), kspec(0), kspec(2), kspec(1), kspec(0)],
        out_specs=pl.BlockSpec((1, TILE, A_WIDTH), lambda bb, i: (bb, i, 0)),
        out_shape=jax.ShapeDtypeStruct((b, tq, A_WIDTH), BF16),
        scratch_shapes=[pltpu.VMEM((A_HEADS, BAND_COLS, TILE), F32),
                        pltpu.VMEM((BAND_TILES, A_HEADS, 2 * HEAD_DIM, TILE), BF16),
                        pltpu.VMEM((A_HEADS // BAND_SLAB * BAND_TILES, TILE, BAND_SLAB * TILE), F32)],
        compiler_params=_params("arbitrary", "arbitrary"),
        name="band_attn",
    )(g_row, q, k, k, k, v, v, v)


DSA_ROLL = 3 * TILE
FAR_STEP = 4


def _t5_bucket(rel):
    half = N_BUCKETS // 2
    max_exact = half // 2
    n = jnp.abs(rel)
    log_ratio = jnp.log(jnp.maximum(n, 1).astype(jnp.float32) / max_exact) / math.log(T5_MAX_DIST / max_exact)
    large = jnp.minimum(max_exact + (log_ratio * (half - max_exact)).astype(jnp.int32), half - 1)
    return jnp.where(rel < 0, half, 0) + jnp.where(n < max_exact, n, large)


def _dsa_bias_row(t5_table):
    idx = jnp.arange(DSA_ROLL)
    d = jnp.where(idx < TILE, idx, idx - DSA_ROLL)
    far = t5_table[_t5_bucket(jnp.full((1,), 2 * TILE + 1, I32))]
    return (t5_table[_t5_bucket(TILE + d)] - far).T.astype(F32)


def _dsa_kernel(g_ref, qi_ref, wi_ref, qb_ref, ki_ref, kb_ref, vb_ref, o_ref,
                sc_ref, scb_ref, keep_ref, vt_ref, qis_ref, qbs_ref, m_ref, acc_ref, bias_ref, s_ref,
                *, off, valid_len, q_valid, n_kt, tq, n_merge):
    i = pl.program_id(1)
    qt = i + off
    q0 = qt * TILE
    nk = qt + 1
    key_rows = lambda j: pl.ds(pl.multiple_of(j * TILE, TILE), TILE)

    @pl.when((pl.program_id(0) == 0) & (i == 0))
    def _():
        for h in range(B_HEADS):
            tile = _toeplitz(g_ref[h:h + 1, :], 2 * TILE, DSA_ROLL)[:, :tq]
            if n_merge:
                ts = tq // n_merge
                stream = lax.broadcasted_iota(I32, (1, tq), 1) // ts
                first = tile
                for b in range(1, n_merge):
                    tile = jnp.where(stream == b, pltpu.roll(first, b * ts, 1), tile)
            bias_ref[h] = tile

    @pl.when(i == 0)
    def _():
        def body(j, c):
            for b in range(max(n_merge, 1)):
                vt = vb_ref[b, key_rows(j), :].astype(F32).T
                for g in range(B_KV_HEADS):
                    vt_ref[g, j, 0:HEAD_DIM, b * TILE:(b + 1) * TILE] = vt[g * HEAD_DIM:(g + 1) * HEAD_DIM].astype(BF16)
            for g in range(B_KV_HEADS):
                vt_ref[g, j, HEAD_DIM:2 * HEAD_DIM, :] = jnp.ones((HEAD_DIM, vt_ref.shape[-1]), BF16)
            return c

        lax.fori_loop(0, n_kt, body, 0)

    colq = lax.broadcasted_iota(I32, (1, tq), 1)
    rowk = lax.broadcasted_iota(I32, (TILE, 1), 0)

    if n_merge:
        width = n_merge * HEAD_DIM
        own = (lax.broadcasted_iota(I32, (tq, width), 0) // (tq // n_merge)
               == lax.broadcasted_iota(I32, (tq, width), 1) // HEAD_DIM)

        def block_diag(x):
            return jnp.where(own, jnp.concatenate([x.astype(F32)] * n_merge, axis=1), 0.0).astype(BF16)

        qi, qb = qi_ref[...], qb_ref[...]
        for h in range(B_HEADS):
            qis_ref[h] = block_diag(qi[:, h * IDX_DIM:(h + 1) * IDX_DIM])
            qbs_ref[h] = block_diag(qb[:, h * HEAD_DIM:(h + 1) * HEAD_DIM])
        side_by_side = lambda ref, j, cols: jnp.concatenate(
            [ref[b, key_rows(j), cols] for b in range(n_merge)], axis=1)
        wi_t = wi_ref[...].T
        lim = jnp.full((1, tq), valid_len, I32)
    else:
        qi = qi_ref[0]
        for h in range(IDX_HEADS):
            qis_ref[h * tq:(h + 1) * tq, :] = qi[:, h * IDX_DIM:(h + 1) * IDX_DIM]
        qb = qb_ref[0]
        for g in range(B_KV_HEADS):
            for hh in range(B_GROUP):
                h = g * B_GROUP + hh
                qbs_ref[g, hh * tq:(hh + 1) * tq, :] = qb[:, h * HEAD_DIM:(h + 1) * HEAD_DIM]
        wi_t = wi_ref[0].T
        lim = jnp.minimum(q0 + (colq // CHUNK + 1) * CHUNK, valid_len)

    def score_tile(j, masked):
        if n_merge:
            kt = side_by_side(ki_ref, j, slice(None))
            lg = jnp.concatenate([lax.dot_general(kt, qis_ref[h], _NT, preferred_element_type=F32)
                                  for h in range(IDX_HEADS)], axis=1)
        else:
            kt = ki_ref[0, key_rows(j), :]
            lg = lax.dot_general(kt, qis_ref[...], _NT, preferred_element_type=F32)
        sc = wi_t[0:1, :] * jnp.maximum(lg[:, 0:tq], 0.0)
        for h in range(1, IDX_HEADS):
            sc = sc + wi_t[h:h + 1, :] * jnp.maximum(lg[:, h * tq:(h + 1) * tq], 0.0)
        if masked:
            sc = jnp.where(j * TILE + rowk < lim, sc, -jnp.inf)
        sc_ref[j] = sc
        scb_ref[j] = sc.astype(BF16)
        rows8 = lambda hit: jnp.where(hit, 1, 0).reshape(TILE // 8, 8, tq).sum(axis=0)
        return rows8(sc > 0.0), rows8(sc >= 0.0)

    def score_pair(jj, c, masked):
        pos_a, nn_a = score_tile(2 * jj, masked)
        pos_b, nn_b = score_tile(2 * jj + 1, masked)
        return c[0] + pos_a + pos_b, c[1] + nn_a + nn_b

    def score_quad(jj, c):
        for t in range(4):
            pos, nn = score_tile(4 * jj + t, False)
            c = (c[0] + pos, c[1] + nn)
        return c

    n_open_quads = (nk - 1) // 4
    counts = lax.fori_loop(0, n_open_quads, score_quad, (jnp.zeros((8, tq), I32), jnp.zeros((8, tq), I32)))
    n_pos, n_nonneg = lax.fori_loop(2 * n_open_quads, (nk + 1) // 2, functools.partial(score_pair, masked=True),
                                    counts)

    def f32_of_key(k):
        return pltpu.bitcast(jnp.where(k < 0, k ^ 0x7FFFFFFF, k), F32)

    def bf16_of_key(k):
        bits = jnp.where(k < 0, k ^ 0x7FFF, k) & 0xFFFF
        return pltpu.bitcast(lax.shift_left(bits, 16), F32).astype(BF16)

    def count_bf16(cand):
        def body(jj, acc):
            for j in (2 * jj, 2 * jj + 1):
                ge = jnp.where(scb_ref[j] >= cand, jnp.int16(1), jnp.int16(0)).reshape(TILE // 16, 16, tq)
                part = ge[0]
                for r in range(1, TILE // 16):
                    part = part + ge[r]
                acc = acc + part
            return acc

        acc = lax.fori_loop(0, (nk + 1) // 2, body, jnp.zeros((16, tq), I16))
        return acc.astype(I32).sum(axis=0, keepdims=True)

    def count_f32(cand, strict=False):
        def body(j, acc):
            blk = sc_ref[j]
            hit = (blk > cand) if strict else (blk >= cand)
            return acc + jnp.where(hit, 1, 0).reshape(TILE // 8, 8, tq).sum(axis=0)

        return lax.fori_loop(0, nk, body, jnp.zeros((8, tq), I32)).sum(axis=0, keepdims=True)

    c_pos = n_pos.sum(axis=0, keepdims=True)
    zero_tie = (c_pos < TOPK_MAX) & (n_nonneg.sum(axis=0, keepdims=True) >= TOPK_MAX)
    skip1 = (colq >= q_valid) | zero_tie

    def level1(it, t):
        cand = t + lax.shift_left(jnp.int32(1), 15 - it)
        return jnp.where(count_bf16(bf16_of_key(cand)) >= TOPK_MAX, cand, t)

    t1 = lax.fori_loop(0, 16, level1, jnp.full((1, tq), MIN16, I32))
    settled1 = skip1 | (t1 <= HI_NEG_INF)

    def level2(st):
        lo, hi, thr_key, done = st
        live = (done == 0) & (hi - lo > 1)
        mid = lo + lax.shift_right_arithmetic(hi - lo, 1)
        c = count_f32(f32_of_key(mid))
        hit = live & (c == TOPK_MAX)
        return (jnp.where(live & (c >= TOPK_MAX), mid, lo), jnp.where(live & (c < TOPK_MAX), mid, hi),
                jnp.where(hit, mid, thr_key), jnp.where(hit, 1, done))

    def n_live(st):
        lo, hi, _, done = st
        return jnp.sum(jnp.where((done == 0) & (hi - lo > 1), 1, 0))

    def level2_pair(carry):
        st = level2(level2(carry[0]))
        return st, n_live(st)

    def key32_of_key16(k):
        return lax.shift_left(k, 16) | jnp.where(k < 0, 0xFFFF, 0)

    key_t1 = key32_of_key16(t1)
    st0 = (key_t1 - 0x8000, key32_of_key16(t1 + 1), key_t1, jnp.where(settled1, 1, 0))
    (lo, _, thr_key, done2), _ = lax.while_loop(lambda carry: carry[1] > 0, level2_pair, (st0, n_live(st0)))
    open2 = done2 == 0
    thr = jnp.where(open2, f32_of_key(lo), f32_of_key(thr_key))
    thr = jnp.where(zero_tie, 0.0, thr)
    thr = jnp.where(settled1 & jnp.logical_not(zero_tie), -jnp.inf, thr)
    thr = jnp.maximum(thr, float(jnp.finfo(F32).min))

    keep_ref[...] = jnp.where(zero_tie, TOPK_MAX - c_pos, 2 ** 30).astype(F32)

    @pl.when(jnp.sum(jnp.where(open2, 1, 0)) > 0)
    def _():
        above = count_f32(thr, strict=True)
        keep_ref[...] = jnp.where(open2, (TOPK_MAX - above).astype(F32), keep_ref[...])

    @pl.when(jnp.sum(jnp.where(open2 | zero_tie, 1, 0)) > 0)
    def _():
        keep = keep_ref[...]
        lower = (lax.broadcasted_iota(I32, (TILE, TILE), 0) > lax.broadcasted_iota(I32, (TILE, TILE), 1))
        lower = jnp.where(lower, 1.0, 0.0).astype(BF16)

        def body(jj, run):
            for j in (2 * jj, 2 * jj + 1):
                blk = sc_ref[j]
                eq = blk == thr
                eq_f = jnp.where(eq, 1.0, 0.0)
                before = jnp.dot(lower, eq_f.astype(BF16), preferred_element_type=F32)
                sc_ref[j] = jnp.where(eq & (run + before >= keep), -jnp.inf, blk)
                run = run + eq_f.reshape(TILE // 8, 8, tq).sum(axis=0).sum(axis=0, keepdims=True)
            return run

        lax.fori_loop(0, (nk + 1) // 2, body, jnp.zeros((1, tq), F32))

    for g in range(B_KV_HEADS):
        m_ref[g] = jnp.full((1, B_GROUP * tq), NEG, F32)
        acc_ref[g] = jnp.zeros((2 * HEAD_DIM, B_GROUP * tq), F32)

    def attend(tiles):
        sels = [sc_ref[j] >= thr for j, _ in tiles]
        part_max = []
        for g in range(B_KV_HEADS):
            pm = None
            for t, ((j, near), sel) in enumerate(zip(tiles, sels)):
                if n_merge:
                    kt = side_by_side(kb_ref, j, slice(g * HEAD_DIM, (g + 1) * HEAD_DIM))
                    st = jnp.concatenate([lax.dot_general(kt, qbs_ref[g * B_GROUP + hh], _NT,
                                                          preferred_element_type=F32)
                                          for hh in range(B_GROUP)], axis=1)
                else:
                    kt = kb_ref[0, key_rows(j), g * HEAD_DIM:(g + 1) * HEAD_DIM]
                    st = lax.dot_general(kt, qbs_ref[g], _NT, preferred_element_type=F32)
                parts = []
                for hh in range(B_GROUP):
                    s_h = st[:, hh * tq:(hh + 1) * tq]
                    if near is not None:
                        s_h = s_h + bias_ref[g * B_GROUP + hh, near * TILE:(near + 1) * TILE, :]
                    parts.append(jnp.where(sel, s_h, -jnp.inf))
                s = jnp.concatenate(parts, axis=1)
                s_ref[g * len(tiles) + t] = s
                m8 = s.reshape(TILE // 8, 8, B_GROUP * tq).max(axis=0)
                pm = m8 if pm is None else jnp.maximum(pm, m8)
            part_max.append(pm)
        for g in range(B_KV_HEADS):
            m_old = m_ref[g]
            m_new = jnp.maximum(m_old, part_max[g].max(axis=0, keepdims=True))
            pv = None
            for t, (j, _) in enumerate(tiles):
                p = jnp.exp(s_ref[g * len(tiles) + t] - m_new)
                if n_merge:
                    lane_stream = colq // (tq // n_merge)
                    d = jnp.concatenate(
                        [jnp.dot(vt_ref[g, j],
                                 jnp.concatenate([jnp.where(lane_stream == b, p[:, hh * tq:(hh + 1) * tq], 0.0)
                                                  for b in range(n_merge)], axis=0).astype(BF16),
                                 preferred_element_type=F32) for hh in range(B_GROUP)], axis=1)
                else:
                    d = jnp.dot(vt_ref[g, j], p.astype(BF16), preferred_element_type=F32)
                pv = d if pv is None else pv + d
            acc_ref[g] = jnp.exp(m_old - m_new) * acc_ref[g] + pv
            m_ref[g] = m_new

    n_far = jnp.maximum(nk - 2, 0)

    def far_step(jj, c):
        attend([(FAR_STEP * jj + t, None) for t in range(FAR_STEP)])
        return c

    lax.fori_loop(0, n_far // FAR_STEP, far_step, 0)
    rem = n_far % FAR_STEP

    for r in range(FAR_STEP):
        @pl.when((rem == r) & (nk >= 2))
        def _(r=r):
            attend([(n_far - r + t, None) for t in range(r)] + [(nk - 2, 0), (nk - 1, 1)])

    @pl.when(nk < 2)
    def _():
        attend([(nk - 1, 1)])

    outs = []
    for g in range(B_KV_HEADS):
        for hh in range(B_GROUP):
            blk = acc_ref[g, :, hh * tq:(hh + 1) * tq].T
            outs.append(blk[:, 0:HEAD_DIM] / blk[:, HEAD_DIM:HEAD_DIM + 1])
    o_ref[0] = jnp.concatenate(outs, axis=-1).astype(BF16)


def _dsa_attn(qi, wi, qb, ki, kb, vb, g_row, off, valid_len, q_valid, tq):
    b, t_q = qi.shape[:2]
    tk = ki.shape[1]
    nq, n_kt = t_q // tq, tk // TILE
    assert tq == TILE or nq == 1
    assert n_kt % 2 == 0 and tk == n_kt * TILE
    qspec = lambda n: pl.BlockSpec((1, tq, n), lambda bb, i: (bb, i, 0))
    kspec = lambda n: pl.BlockSpec((1, tk, n), lambda bb, i: (bb, 0, 0))
    in_specs = [pl.BlockSpec((B_HEADS, DSA_ROLL), lambda bb, i: (0, 0)),
                qspec(512), qspec(128), qspec(512), kspec(IDX_DIM), kspec(128), kspec(128)]
    return _dsa_call((g_row, qi, wi, qb, ki, kb, vb), in_specs, (b, nq), t_q, tq, n_kt,
                     dict(off=off, valid_len=valid_len, q_valid=q_valid, n_merge=0))


def _dsa_attn_merged(qi, wi, qb, ki, kb, vb, g_row, off, valid_len):
    s, ts = qi.shape[:2]
    tk = ki.shape[1]
    tq, n_kt = s * ts, tk // TILE
    assert tq % LANES == 0 and ts <= CHUNK and n_kt % 2 == 0 and tk == n_kt * TILE
    flat = lambda a: a.reshape(tq, a.shape[-1])
    operands = (g_row, flat(qi), flat(wi), flat(qb), ki, kb, vb)
    whole = lambda a: pl.BlockSpec(a.shape, lambda bb, i: (0,) * a.ndim, pipeline_mode=pl.Buffered(1))
    out = _dsa_call(operands, [whole(a) for a in operands], (1, 1), tq, tq, n_kt,
                    dict(off=off, valid_len=valid_len, q_valid=tq, n_merge=s))
    return out.reshape(s, ts, B_WIDTH)


def _dsa_call(operands, in_specs, grid, t_q, tq, n_kt, static):
    n_merge = static["n_merge"]
    assert static["valid_len"] >= (static["off"] + grid[1] - 1) * TILE
    if n_merge:
        q_scratch = [pltpu.VMEM((IDX_HEADS, tq, n_merge * IDX_DIM), BF16),
                     pltpu.VMEM((B_HEADS, tq, n_merge * HEAD_DIM), BF16)]
    else:
        q_scratch = [pltpu.VMEM((IDX_HEADS * tq, IDX_DIM), BF16),
                     pltpu.VMEM((B_KV_HEADS, B_GROUP * tq, HEAD_DIM), BF16)]
    return pl.pallas_call(
        functools.partial(_dsa_kernel, n_kt=n_kt, tq=tq, **static),
        grid=grid,
        in_specs=in_specs,
        out_specs=pl.BlockSpec((1, tq, B_WIDTH), lambda bb, i: (bb, i, 0)),
        out_shape=jax.ShapeDtypeStruct((grid[0], t_q, B_WIDTH), BF16),
        scratch_shapes=[pltpu.VMEM((n_kt, TILE, tq), F32),
                        pltpu.VMEM((n_kt, TILE, tq), BF16),
                        pltpu.VMEM((1, tq), F32),
                        pltpu.VMEM((B_KV_HEADS, n_kt, 2 * HEAD_DIM, max(n_merge, 1) * TILE), BF16),
                        *q_scratch,
                        pltpu.VMEM((B_KV_HEADS, 1, B_GROUP * tq), F32),
                        pltpu.VMEM((B_KV_HEADS, 2 * HEAD_DIM, B_GROUP * tq), F32),
                        pltpu.VMEM((B_HEADS, 2 * TILE, tq), F32),
                        pltpu.VMEM((B_KV_HEADS * (FAR_STEP + 1), TILE, B_GROUP * tq), F32)],
        compiler_params=_params("arbitrary", "arbitrary"),
        name="dsa_attn",
    )(*operands)


def _mem_kv_kernel(m_ref, wk_ref, wv_ref, k_o, v_o, kb_o, vb_o):
    mb = m_ref[...].astype(BF16)
    k = jnp.dot(mb, wk_ref[...], preferred_element_type=F32)
    v = jnp.dot(mb, wv_ref[...], preferred_element_type=F32)
    k_o[...] = k
    v_o[...] = v
    kb_o[...] = k.astype(BF16)
    vb_o[...] = v.astype(BF16)


def _mem_kv(mem2d, wk, wv):
    r = mem2d.shape[0]
    tm = MEM_LEN
    row = lambda n: pl.BlockSpec((tm, n), lambda i: (i, 0))
    const = lambda s: pl.BlockSpec(s, lambda i: (0, 0))
    sds = jax.ShapeDtypeStruct
    return pl.pallas_call(
        _mem_kv_kernel,
        grid=(r // tm,),
        in_specs=[row(D_MODEL), const((D_MODEL, MEM_WIDTH)), const((D_MODEL, MEM_WIDTH))],
        out_specs=[row(MEM_WIDTH)] * 4,
        out_shape=[sds((r, MEM_WIDTH), F32), sds((r, MEM_WIDTH), F32),
                   sds((r, MEM_WIDTH), BF16), sds((r, MEM_WIDTH), BF16)],
        compiler_params=_params("arbitrary"),
        name="mem_kv",
    )(mem2d, wk, wv)


FF_CHUNK = 256


def _tail_kernel(x_ref, oa_ref, ob_ref, mk_ref, mv_ref, hist_ref,
                 wo_ref, g1_ref, b1_ref, wq_ref, wmo_ref, g2_ref, b2_ref,
                 wu_ref, wc_ref, bc_ref, wd_ref, g3_ref, b3_ref,
                 o_ref, tail_ref, carry_ref, act_ref, *, tiles_per_batch, seg):
    i = pl.program_id(0)
    tm = x_ref.shape[0]
    nseg = tm // seg

    mix = jnp.concatenate([oa_ref[...], ob_ref[...]], axis=-1)
    h = _layer_norm(ALPHA * x_ref[...] + jnp.dot(mix, wo_ref[...], preferred_element_type=F32),
                    g1_ref[...], b1_ref[...])

    q = jnp.dot(h.astype(BF16), wq_ref[...], preferred_element_type=F32).astype(BF16)
    segs = []
    for s in range(nseg):
        qs = q[s * seg:(s + 1) * seg]
        heads = []
        for hd in range(MEM_HEADS):
            sl = slice(hd * MEM_HEAD_DIM, (hd + 1) * MEM_HEAD_DIM)
            sc = lax.dot_general(qs[:, sl], mk_ref[s, :, sl], _NT, preferred_element_type=F32) * MEM_HEAD_DIM ** -0.5
            p = jnp.exp(sc - sc.max(-1, keepdims=True))
            l = p.sum(-1, keepdims=True)
            heads.append(jnp.dot(p.astype(BF16), mv_ref[s, :, sl], preferred_element_type=F32) / l)
        segs.append(jnp.concatenate(heads, axis=-1))
    att = jnp.concatenate(segs, axis=0).astype(BF16)
    h = _layer_norm(ALPHA * h + jnp.dot(att, wmo_ref[...], preferred_element_type=F32), g2_ref[...], b2_ref[...])

    hb = h.astype(BF16)
    row = lax.broadcasted_iota(I32, (tm, 1), 0)
    first = (i % tiles_per_batch) == 0
    for c in range(D_FF // FF_CHUNK):
        cs = slice(c * FF_CHUNK, (c + 1) * FF_CHUNK)
        u = jnp.dot(hb, wu_ref[:, cs], preferred_element_type=F32)
        gt = jnp.dot(hb, wu_ref[:, D_FF + c * FF_CHUNK:D_FF + (c + 1) * FF_CHUNK], preferred_element_type=F32)
        p1 = pltpu.roll(gt, 1, 0)
        p2 = pltpu.roll(gt, 2, 0)
        for s in range(nseg):
            hist = hist_ref[s, :, cs]
            if tiles_per_batch > 1:
                hist = jnp.where(first, hist, carry_ref[:, cs])
            p1 = jnp.where(row == s * seg, hist[7:8, :], p1)
            p2 = jnp.where(row == s * seg, hist[6:7, :], p2)
            p2 = jnp.where(row == s * seg + 1, hist[7:8, :], p2)
        gc = bc_ref[:, cs] + ((wc_ref[0:1, cs] * p2 + wc_ref[1:2, cs] * p1) + wc_ref[2:3, cs] * gt)
        act_ref[:, cs] = (u * jax.nn.gelu(gc)).astype(BF16)
        for s in range(nseg):
            tail_ref[s, :, cs] = gt[(s + 1) * seg - 8:(s + 1) * seg, :]
        carry_ref[:, cs] = gt[tm - 8:tm, :]
    f = jnp.dot(act_ref[...], wd_ref[...], preferred_element_type=F32)
    o_ref[...] = _layer_norm(ALPHA * h + f, g3_ref[...], b3_ref[...])


def _layer_tail(x2d, oa, ob, mk, mv, hist, weights, tm, tiles_per_batch, seg):
    r = x2d.shape[0]
    nseg = tm // seg
    n_stream = r // (tm * tiles_per_batch) * nseg
    row = lambda n: pl.BlockSpec((tm, n), lambda i: (i, 0))
    per_stream = lambda a, b: pl.BlockSpec((nseg, a, b), lambda i: (i // tiles_per_batch, 0, 0))
    const = lambda a: pl.BlockSpec(a.shape, lambda i: (0,) * a.ndim, pipeline_mode=pl.Buffered(1))
    return pl.pallas_call(
        functools.partial(_tail_kernel, tiles_per_batch=tiles_per_batch, seg=seg),
        grid=(r // tm,),
        in_specs=[row(D_MODEL), row(A_WIDTH), row(B_WIDTH), per_stream(MEM_LEN, MEM_WIDTH),
                  per_stream(MEM_LEN, MEM_WIDTH), per_stream(8, D_FF)] + [const(w) for w in weights],
        out_specs=[row(D_MODEL), per_stream(8, D_FF)],
        out_shape=[jax.ShapeDtypeStruct((r, D_MODEL), F32), jax.ShapeDtypeStruct((n_stream, 8, D_FF), F32)],
        scratch_shapes=[pltpu.VMEM((8, D_FF), F32), pltpu.VMEM((tm, D_FF), BF16)],
        compiler_params=_params("arbitrary"),
        name="layer_tail",
    )(x2d, oa, ob, mk, mv, hist, *weights)


def _pad_rows(a, n):
    return jnp.pad(a, ((0, 0), (0, n - a.shape[1])) + ((0, 0),) * (a.ndim - 2))


def _hist8(g_hist):
    return jnp.pad(g_hist, ((0, 0), (8 - g_hist.shape[1], 0), (0, 0)))


def kernel(x_prompt, x_sample, cache_a_k, cache_a_v, cache_b_k, cache_b_v, cache_b_kidx, cache_mem_k, cache_mem_v, state_ffn_conv, mem_prompt, w_in, a_rel_bias, t5_bias, w_o, ln1_g, ln1_b, w_mq, w_mk, w_mv, w_mo, ln2_g, ln2_b, w_up, w_conv, b_conv, w_down, ln3_g, ln3_b):
    bp, tp = x_prompt.shape[:2]
    bs, ts = x_sample.shape[:2]
    l = 0
    vec = lambda a: a[l].reshape(1, -1)
    w_in_p = _prep_w_in(w_in[l])
    w_o_b = w_o[l].astype(BF16)
    w_mq_b, w_mk_b, w_mv_b, w_mo_b = (w[l].astype(BF16) for w in (w_mq, w_mk, w_mv, w_mo))
    w_up_b, w_down_b = w_up[l].astype(BF16), w_down[l].astype(BF16)
    band_row = _band_bias_row(a_rel_bias[l])
    dsa_row = _dsa_bias_row(t5_bias)
    tail_w = (w_o_b, vec(ln1_g), vec(ln1_b), w_mq_b, w_mo_b, vec(ln2_g), vec(ln2_b),
              w_up_b, w_conv[l], vec(b_conv), w_down_b, vec(ln3_g), vec(ln3_b))

    tm = 512
    a_keep = min(N_PREV_CHUNKS * CHUNK, tp)
    (qa, ka, va, qb, qi, kb, vb, ki, kb_b, vb_b, ki_b, wi, ka_tail, va_tail) = _in_proj(
        x_prompt.reshape(bp * tp, D_MODEL), w_in_p, tm, tp // tm)
    r3 = lambda a: a.reshape(bp, tp, a.shape[-1])
    oa = _band_attn(r3(qa), r3(ka), r3(va), band_row, 0, tp)
    ob = _dsa_attn(r3(qi), r3(wi), r3(qb), r3(ki_b), r3(kb_b), r3(vb_b), dsa_row, 0, tp, TILE, TILE)
    mk, mv, mk_b, mv_b = _mem_kv(mem_prompt.reshape(bp * MEM_LEN, D_MODEL), w_mk_b, w_mv_b)
    xp, p_tail = _layer_tail(x_prompt.reshape(bp * tp, D_MODEL), oa.reshape(bp * tp, A_WIDTH),
                             ob.reshape(bp * tp, B_WIDTH), mk_b.reshape(bp, MEM_LEN, MEM_WIDTH),
                             mv_b.reshape(bp, MEM_LEN, MEM_WIDTH), jnp.zeros((bp, 8, D_FF), F32), tail_w,
                             tm, tp // tm, tm)
    prompt_state = (
        ka_tail.reshape(bp, tm, A_HEADS, HEAD_DIM)[:, tm - a_keep:][None],
        va_tail.reshape(bp, tm, A_HEADS, HEAD_DIM)[:, tm - a_keep:][None],
        kb.reshape(1, bp, tp, B_KV_HEADS, HEAD_DIM), vb.reshape(1, bp, tp, B_KV_HEADS, HEAD_DIM),
        ki.reshape(1, bp, tp, IDX_DIM),
        mk.reshape(1, bp, MEM_LEN, MEM_HEADS, MEM_HEAD_DIM), mv.reshape(1, bp, MEM_LEN, MEM_HEADS, MEM_HEAD_DIM),
        p_tail[:, 8 - (CONV_W - 1):][None])

    rs = bs * ts
    (qa, ka, va, qb, qi, kb, vb, ki, kb_b, vb_b, ki_b, wi, ka_new, va_new) = _in_proj(
        x_sample.reshape(rs, D_MODEL), w_in_p, rs, 1)
    s3 = lambda a: a.reshape(bs, ts, -1)
    qpad = lambda a, n: _pad_rows(s3(a), n)

    past_a = cache_a_k.shape[2]
    n_a = past_a + ts
    t_a = -(-n_a // TILE) * TILE
    seq_a = lambda cache, new: _pad_rows(
        jnp.concatenate([cache[l].reshape(bs, past_a, A_WIDTH), s3(new)], axis=1), t_a).astype(BF16)
    oa = _band_attn(qpad(qa, TILE), seq_a(cache_a_k, ka_new), seq_a(cache_a_v, va_new), band_row,
                    past_a // TILE, n_a)

    past_b = cache_b_k.shape[2]
    n_b = past_b + ts
    t_b = -(-n_b // (2 * TILE)) * 2 * TILE
    seq_b = lambda cache, new: _pad_rows(
        jnp.concatenate([cache[l].reshape(bs, past_b, -1), s3(new)], axis=1), t_b).astype(BF16)
    ob = _dsa_attn_merged(s3(qi), s3(wi), s3(qb), seq_b(cache_b_kidx, ki), seq_b(cache_b_k, kb),
                          seq_b(cache_b_v, vb), dsa_row, past_b // TILE, n_b)

    xs, s_tail = _layer_tail(x_sample.reshape(rs, D_MODEL), oa[:, :ts].reshape(rs, A_WIDTH),
                             ob[:, :ts].reshape(rs, B_WIDTH),
                             cache_mem_k[l].reshape(bs, MEM_LEN, MEM_WIDTH).astype(BF16),
                             cache_mem_v[l].reshape(bs, MEM_LEN, MEM_WIDTH).astype(BF16),
                             _hist8(state_ffn_conv[l]), tail_w, rs, 1, ts)
    sample_state = (
        ka_new.reshape(1, bs, ts, A_HEADS, HEAD_DIM), va_new.reshape(1, bs, ts, A_HEADS, HEAD_DIM),
        kb.reshape(1, bs, ts, B_KV_HEADS, HEAD_DIM), vb.reshape(1, bs, ts, B_KV_HEADS, HEAD_DIM),
        ki.reshape(1, bs, ts, IDX_DIM), s_tail[:, 8 - (CONV_W - 1):][None])

    return (xp.reshape(bp, tp, D_MODEL), xs.reshape(bs, ts, D_MODEL)) + prompt_state + sample_state
```

```python
import functools
import math

import jax
import jax.numpy as jnp
from jax import lax
from jax.experimental import pallas as pl
from jax.experimental.pallas import tpu as pltpu

F32 = jnp.float32
BF16 = jnp.bfloat16
I32 = jnp.int32
I16 = jnp.int16

D_MODEL = 1024
CHUNK = 64
N_PREV_CHUNKS = 8
HEAD_DIM = 64
A_HEADS = 8
A_WIDTH = A_HEADS * HEAD_DIM
A_MAX_REL = 64
B_HEADS = 8
B_KV_HEADS = 2
B_GROUP = B_HEADS // B_KV_HEADS
B_WIDTH = B_HEADS * HEAD_DIM
B_KV_WIDTH = B_KV_HEADS * HEAD_DIM
IDX_HEADS = 8
IDX_DIM = 64
TOPK_MAX = 256
N_BUCKETS = 32
T5_MAX_DIST = 128
MEM_LEN = 256
MEM_HEADS = 4
MEM_HEAD_DIM = 128
MEM_WIDTH = MEM_HEADS * MEM_HEAD_DIM
D_FF = 2816
CONV_W = 3
IN_SIZES = (A_WIDTH, A_WIDTH, A_WIDTH, B_WIDTH, B_KV_WIDTH, B_KV_WIDTH, IDX_HEADS * IDX_DIM, IDX_DIM, IDX_HEADS)
DEPTH = 1
ALPHA = (2 * DEPTH) ** 0.25
LN_EPS = 1e-5
ATTN_SCALE = HEAD_DIM ** -0.5
NEG = -1e30

LANES = 128
TILE = 256
BAND_TILES = 1 + (N_PREV_CHUNKS * CHUNK) // TILE
VMEM_LIMIT = 56 * 1024 * 1024

_C_QA, _C_KA, _C_VA, _C_QB = 0, 512, 1024, 1536
_C_KB, _C_VB, _C_QI, _C_KI, _C_WI = 2048, 2176, 2304, 2816, 2944
IN_PAD = 3072

MIN16 = -32768
HI_NEG_INF = -32641

_NT = (((1,), (1,)), ((), ()))


def _params(*sem):
    return pltpu.CompilerParams(dimension_semantics=sem, vmem_limit_bytes=VMEM_LIMIT)


def _layer_norm(z, g, b):
    mu = jnp.mean(z, axis=-1, keepdims=True)
    d = z - mu
    var = jnp.mean(d * d, axis=-1, keepdims=True)
    return d * lax.rsqrt(var + LN_EPS) * g + b


def _toeplitz(g_row, rows, width):
    return pltpu.roll(jnp.broadcast_to(g_row, (rows, width)), 0, 1, stride=1, stride_axis=0)


def _in_proj_kernel(x_ref, w_ref, qa_o, ka_o, va_o, qb_o, qi_o, kb_o, vb_o, ki_o, kbb_o, vbb_o, kib_o,
                    wi_o, kat_o, vat_o, *, tiles_per_batch, wi_scale):
    i = pl.program_id(0)
    xb = x_ref[...].astype(BF16)

    def mm(c0, n):
        return jnp.dot(xb, w_ref[:, c0:c0 + n], preferred_element_type=F32)

    kb = mm(_C_KB, 128)
    vb = mm(_C_VB, 128)
    for g in range(B_KV_HEADS):
        kb_o[:, g, :] = kb[:, g * HEAD_DIM:(g + 1) * HEAD_DIM]
        vb_o[:, g, :] = vb[:, g * HEAD_DIM:(g + 1) * HEAD_DIM]
    kbb_o[...] = kb.astype(BF16)
    vbb_o[...] = vb.astype(BF16)
    ki = mm(_C_KI, 128)[:, :IDX_DIM]
    ki_o[...] = ki
    kib_o[...] = ki.astype(BF16)
    wi_o[...] = mm(_C_WI, 128) * wi_scale
    qa_o[...] = mm(_C_QA, 512).astype(BF16)
    ka = mm(_C_KA, 512)
    va = mm(_C_VA, 512)
    ka_o[...] = ka.astype(BF16)
    va_o[...] = va.astype(BF16)
    qb_o[...] = mm(_C_QB, 512).astype(BF16)
    qi_o[...] = mm(_C_QI, 512).astype(BF16)

    @pl.when(i % tiles_per_batch == tiles_per_batch - 1)
    def _():
        kat_o[...] = ka
        vat_o[...] = va


def _prep_w_in(w):
    parts, off = [], 0
    for n in IN_SIZES:
        parts.append(w[:, off:off + n])
        off += n
    qa, ka, va, qb, kb, vb, qi, ki, wi = parts
    pad = lambda a, n: jnp.pad(a, ((0, 0), (0, n - a.shape[1])))
    cols = [qa * ATTN_SCALE, ka, va, qb * ATTN_SCALE, kb, vb, qi * IDX_DIM ** -0.5, pad(ki, 128), pad(wi, 128)]
    return jnp.concatenate(cols, axis=1).astype(BF16)


def _in_proj(x2d, w_pad, tm, tiles_per_batch):
    r = x2d.shape[0]
    n_tiles = r // tm
    n_batch = n_tiles // tiles_per_batch
    row = lambda n: pl.BlockSpec((tm, n), lambda i: (i, 0))
    tail = pl.BlockSpec((tm, 512), lambda i: (i // tiles_per_batch, 0))
    sds = jax.ShapeDtypeStruct
    kv_state = pl.BlockSpec((tm, B_KV_HEADS, HEAD_DIM), lambda i: (i, 0, 0))
    kv_shape = sds((r, B_KV_HEADS, HEAD_DIM), F32)
    out_shape = [sds((r, 512), BF16)] * 5 + [kv_shape, kv_shape, sds((r, IDX_DIM), F32),
                                              sds((r, 128), BF16), sds((r, 128), BF16), sds((r, IDX_DIM), BF16),
                                              sds((r, 128), F32),
                                              sds((n_batch * tm, 512), F32), sds((n_batch * tm, 512), F32)]
    out_specs = [row(512)] * 5 + [kv_state, kv_state, row(IDX_DIM), row(128), row(128), row(IDX_DIM), row(128),
                                  tail, tail]
    return pl.pallas_call(
        functools.partial(_in_proj_kernel, tiles_per_batch=tiles_per_batch, wi_scale=IDX_HEADS ** -0.5),
        grid=(n_tiles,),
        in_specs=[pl.BlockSpec((tm, D_MODEL), lambda i: (i, 0)),
                  pl.BlockSpec((D_MODEL, IN_PAD), lambda i: (0, 0))],
        out_specs=out_specs,
        out_shape=out_shape,
        compiler_params=_params("arbitrary"),
        name="in_proj",
    )(x2d, w_pad)


BAND_COLS = BAND_TILES * TILE
BAND_ROLL = BAND_COLS + TILE
BAND_SLAB = 4


def _band_bias_row(table):
    idx = jnp.arange(BAND_ROLL)
    d = jnp.where(idx < TILE, idx, idx - BAND_ROLL)
    rel = (BAND_TILES - 1) * TILE + d
    return table[jnp.clip(rel, -A_MAX_REL, A_MAX_REL) + A_MAX_REL].T.astype(F32)


def _band_kernel(g_ref, q_ref, k0, k1, k2, v0, v1, v2, o_ref, bias_ref, vt_ref, s_ref, *, off, valid_len):
    i = pl.program_id(1)
    kt = i + off
    krefs, vrefs = (k0, k1, k2), (v0, v1, v2)

    @pl.when((pl.program_id(0) == 0) & (i == 0))
    def _():
        c = lax.broadcasted_iota(I32, (BAND_COLS, TILE), 0) // CHUNK
        r = lax.broadcasted_iota(I32, (BAND_COLS, TILE), 1) // CHUNK
        ok = (c >= r) & (c <= r + N_PREV_CHUNKS)
        for h in range(A_HEADS):
            bias_ref[h] = jnp.where(ok, _toeplitz(g_ref[h:h + 1, :], BAND_COLS, BAND_ROLL)[:, :TILE], NEG)
        ones = jnp.ones((HEAD_DIM, TILE), BF16)
        for s in range(BAND_TILES):
            for h in range(A_HEADS):
                vt_ref[s, h, HEAD_DIM:2 * HEAD_DIM, :] = ones

    def put(slot, vref):
        vt = vref[0].astype(F32).T
        for h in range(A_HEADS):
            vt_ref[slot, h, 0:HEAD_DIM, :] = vt[h * HEAD_DIM:(h + 1) * HEAD_DIM].astype(BF16)

    slots = [(kt + 1 + j) % BAND_TILES for j in range(BAND_TILES)]

    @pl.when(i == 0)
    def _():
        for j in range(BAND_TILES - 1):
            put(slots[j], vrefs[j])

    put(slots[BAND_TILES - 1], vrefs[BAND_TILES - 1])

    base = (kt - (BAND_TILES - 1)) * TILE

    n_slab = BAND_SLAB
    slab_w = n_slab * HEAD_DIM
    lane_head = lax.broadcasted_iota(I32, (TILE, slab_w), 1) // HEAD_DIM

    def attend(masked):
        outs, maxes = [], []
        for g in range(A_HEADS // n_slab):
            gs = slice(g * slab_w, (g + 1) * slab_w)
            q_slab = q_ref[0, :, gs].astype(F32)
            q_bd = jnp.concatenate([jnp.where(lane_head == hh, q_slab, 0.0) for hh in range(n_slab)],
                                   axis=0).astype(BF16)
            pm = None
            for j in range(BAND_TILES):
                st = lax.dot_general(krefs[j][0, :, gs], q_bd, _NT, preferred_element_type=F32)
                parts = [st[:, hh * TILE:(hh + 1) * TILE] + bias_ref[g * n_slab + hh, j * TILE:(j + 1) * TILE, :]
                         for hh in range(n_slab)]
                if masked:
                    kpos = base + j * TILE + lax.broadcasted_iota(I32, (TILE, TILE), 0)
                    ok = (kpos >= 0) & (kpos < valid_len)
                    parts = [jnp.where(ok, x, NEG) for x in parts]
                sj = jnp.concatenate(parts, axis=1)
                s_ref[g * BAND_TILES + j] = sj
                m8 = sj.reshape(TILE // 8, 8, n_slab * TILE).max(axis=0)
                pm = m8 if pm is None else jnp.maximum(pm, m8)
            maxes.append(pm.max(axis=0, keepdims=True))
        for g, m in enumerate(maxes):
            p = [jnp.exp(s_ref[g * BAND_TILES + j] - m).astype(BF16) for j in range(BAND_TILES)]
            for hh in range(n_slab):
                acc = None
                for j in range(BAND_TILES):
                    d = jnp.dot(vt_ref[slots[j], g * n_slab + hh], p[j][:, hh * TILE:(hh + 1) * TILE],
                                preferred_element_type=F32)
                    acc = d if acc is None else acc + d
                blk = acc.T
                outs.append(blk[:, 0:HEAD_DIM] / blk[:, HEAD_DIM:HEAD_DIM + 1])
        o_ref[0] = jnp.concatenate(outs, axis=-1).astype(BF16)

    needs_mask = (base < 0) | (base + BAND_COLS > valid_len)

    @pl.when(needs_mask)
    def _():
        attend(True)

    @pl.when(jnp.logical_not(needs_mask))
    def _():
        attend(False)


def _band_attn(q, k, v, g_row, off, valid_len):
    b, tq = q.shape[:2]
    nq = tq // TILE
    qspec = pl.BlockSpec((1, TILE, A_WIDTH), lambda bb, i: (bb, i, 0))
    kspec = lambda d: pl.BlockSpec((1, TILE, A_WIDTH), lambda bb, i: (bb, jnp.maximum(i + off - d, 0), 0))
    return pl.pallas_call(
        functools.partial(_band_kernel, off=off, valid_len=valid_len),
        grid=(b, nq),
        in_specs=[pl.BlockSpec((A_HEADS, BAND_ROLL), lambda bb, i: (0, 0)),
                  qspec, kspec(2), kspec(1), kspec(0), kspec(2), kspec(1), kspec(0)],
        out_specs=pl.BlockSpec((1, TILE, A_WIDTH), lambda bb, i: (bb, i, 0)),
        out_shape=jax.ShapeDtypeStruct((b, tq, A_WIDTH), BF16),
        scratch_shapes=[pltpu.VMEM((A_HEADS, BAND_COLS, TILE), F32),
                        pltpu.VMEM((BAND_TILES, A_HEADS, 2 * HEAD_DIM, TILE), BF16),
                        pltpu.VMEM((A_HEADS // BAND_SLAB * BAND_TILES, TILE, BAND_SLAB * TILE), F32)],
        compiler_params=_params("arbitrary", "arbitrary"),
        name="band_attn",
    )(g_row, q, k, k, k, v, v, v)


DSA_ROLL = 3 * TILE
FAR_STEP = 4


def _t5_bucket(rel):
    half = N_BUCKETS // 2
    max_exact = half // 2
    n = jnp.abs(rel)
    log_ratio = jnp.log(jnp.maximum(n, 1).astype(jnp.float32) / max_exact) / math.log(T5_MAX_DIST / max_exact)
    large = jnp.minimum(max_exact + (log_ratio * (half - max_exact)).astype(jnp.int32), half - 1)
    return jnp.where(rel < 0, half, 0) + jnp.where(n < max_exact, n, large)


def _dsa_bias_row(t5_table):
    idx = jnp.arange(DSA_ROLL)
    d = jnp.where(idx < TILE, idx, idx - DSA_ROLL)
    far = t5_table[_t5_bucket(jnp.full((1,), 2 * TILE + 1, I32))]
    return (t5_table[_t5_bucket(TILE + d)] - far).T.astype(F32)


def _dsa_kernel(g_ref, qi_ref, wi_ref, qb_ref, ki_ref, kb_ref, vb_ref, o_ref,
                sc_ref, scb_ref, keep_ref, vt_ref, qis_ref, qbs_ref, m_ref, acc_ref, bias_ref, s_ref,
                *, off, valid_len, q_valid, n_kt, tq, n_merge):
    i = pl.program_id(1)
    qt = i + off
    q0 = qt * TILE
    nk = qt + 1
    key_rows = lambda j: pl.ds(pl.multiple_of(j * TILE, TILE), TILE)

    @pl.when((pl.program_id(0) == 0) & (i == 0))
    def _():
        for h in range(B_HEADS):
            tile = _toeplitz(g_ref[h:h + 1, :], 2 * TILE, DSA_ROLL)[:, :tq]
            if n_merge:
                ts = tq // n_merge
                stream = lax.broadcasted_iota(I32, (1, tq), 1) // ts
                first = tile
                for b in range(1, n_merge):
                    tile = jnp.where(stream == b, pltpu.roll(first, b * ts, 1), tile)
            bias_ref[h] = tile

    @pl.when(i == 0)
    def _():
        def body(j, c):
            for b in range(max(n_merge, 1)):
                vt = vb_ref[b, key_rows(j), :].astype(F32).T
                for g in range(B_KV_HEADS):
                    vt_ref[g, j, 0:HEAD_DIM, b * TILE:(b + 1) * TILE] = vt[g * HEAD_DIM:(g + 1) * HEAD_DIM].astype(BF16)
            for g in range(B_KV_HEADS):
                vt_ref[g, j, HEAD_DIM:2 * HEAD_DIM, :] = jnp.ones((HEAD_DIM, vt_ref.shape[-1]), BF16)
            return c

        lax.fori_loop(0, n_kt, body, 0)

    colq = lax.broadcasted_iota(I32, (1, tq), 1)
    rowk = lax.broadcasted_iota(I32, (TILE, 1), 0)

    if n_merge:
        width = n_merge * HEAD_DIM
        own = (lax.broadcasted_iota(I32, (tq, width), 0) // (tq // n_merge)
               == lax.broadcasted_iota(I32, (tq, width), 1) // HEAD_DIM)

        def block_diag(x):
            return jnp.where(own, jnp.concatenate([x.astype(F32)] * n_merge, axis=1), 0.0).astype(BF16)

        qi, qb = qi_ref[...], qb_ref[...]
        for h in range(B_HEADS):
            qis_ref[h] = block_diag(qi[:, h * IDX_DIM:(h + 1) * IDX_DIM])
            qbs_ref[h] = block_diag(qb[:, h * HEAD_DIM:(h + 1) * HEAD_DIM])
        side_by_side = lambda ref, j, cols: jnp.concatenate(
            [ref[b, key_rows(j), cols] for b in range(n_merge)], axis=1)
        wi_t = wi_ref[...].T
        lim = jnp.full((1, tq), valid_len, I32)
    else:
        qi = qi_ref[0]
        for h in range(IDX_HEADS):
            qis_ref[h * tq:(h + 1) * tq, :] = qi[:, h * IDX_DIM:(h + 1) * IDX_DIM]
        qb = qb_ref[0]
        for g in range(B_KV_HEADS):
            for hh in range(B_GROUP):
                h = g * B_GROUP + hh
                qbs_ref[g, hh * tq:(hh + 1) * tq, :] = qb[:, h * HEAD_DIM:(h + 1) * HEAD_DIM]
        wi_t = wi_ref[0].T
        lim = jnp.minimum(q0 + (colq // CHUNK + 1) * CHUNK, valid_len)

    def score_tile(j, masked):
        if n_merge:
            kt = side_by_side(ki_ref, j, slice(None))
            lg = jnp.concatenate([lax.dot_general(kt, qis_ref[h], _NT, preferred_element_type=F32)
                                  for h in range(IDX_HEADS)], axis=1)
        else:
            kt = ki_ref[0, key_rows(j), :]
            lg = lax.dot_general(kt, qis_ref[...], _NT, preferred_element_type=F32)
        sc = wi_t[0:1, :] * jnp.maximum(lg[:, 0:tq], 0.0)
        for h in range(1, IDX_HEADS):
            sc = sc + wi_t[h:h + 1, :] * jnp.maximum(lg[:, h * tq:(h + 1) * tq], 0.0)
        if masked:
            sc = jnp.where(j * TILE + rowk < lim, sc, -jnp.inf)
        sc_ref[j] = sc
        scb_ref[j] = sc.astype(BF16)
        rows8 = lambda hit: jnp.where(hit, 1, 0).reshape(TILE // 8, 8, tq).sum(axis=0)
        return rows8(sc > 0.0), rows8(sc >= 0.0)

    def score_pair(jj, c, masked):
        pos_a, nn_a = score_tile(2 * jj, masked)
        pos_b, nn_b = score_tile(2 * jj + 1, masked)
        return c[0] + pos_a + pos_b, c[1] + nn_a + nn_b

    def score_quad(jj, c):
        for t in range(4):
            pos, nn = score_tile(4 * jj + t, False)
            c = (c[0] + pos, c[1] + nn)
        return c

    n_open_quads = (nk - 1) // 4
    counts = lax.fori_loop(0, n_open_quads, score_quad, (jnp.zeros((8, tq), I32), jnp.zeros((8, tq), I32)))
    n_pos, n_nonneg = lax.fori_loop(2 * n_open_quads, (nk + 1) // 2, functools.partial(score_pair, masked=True),
                                    counts)

    def f32_of_key(k):
        return pltpu.bitcast(jnp.where(k < 0, k ^ 0x7FFFFFFF, k), F32)

    def bf16_of_key(k):
        bits = jnp.where(k < 0, k ^ 0x7FFF, k) & 0xFFFF
        return pltpu.bitcast(lax.shift_left(bits, 16), F32).astype(BF16)

    def count_bf16(cand):
        def body(jj, acc):
            for j in (2 * jj, 2 * jj + 1):
                ge = jnp.where(scb_ref[j] >= cand, jnp.int16(1), jnp.int16(0)).reshape(TILE // 16, 16, tq)
                part = ge[0]
                for r in range(1, TILE // 16):
                    part = part + ge[r]
                acc = acc + part
            return acc

        acc = lax.fori_loop(0, (nk + 1) // 2, body, jnp.zeros((16, tq), I16))
        return acc.astype(I32).sum(axis=0, keepdims=True)

    def count_f32(cand, strict=False):
        def body(j, acc):
            blk = sc_ref[j]
            hit = (blk > cand) if strict else (blk >= cand)
            return acc + jnp.where(hit, 1, 0).reshape(TILE // 8, 8, tq).sum(axis=0)

        return lax.fori_loop(0, nk, body, jnp.zeros((8, tq), I32)).sum(axis=0, keepdims=True)

    c_pos = n_pos.sum(axis=0, keepdims=True)
    zero_tie = (c_pos < TOPK_MAX) & (n_nonneg.sum(axis=0, keepdims=True) >= TOPK_MAX)
    skip1 = (colq >= q_valid) | zero_tie

    def level1(it, t):
        cand = t + lax.shift_left(jnp.int32(1), 15 - it)
        return jnp.where(count_bf16(bf16_of_key(cand)) >= TOPK_MAX, cand, t)

    t1 = lax.fori_loop(0, 16, level1, jnp.full((1, tq), MIN16, I32))
    settled1 = skip1 | (t1 <= HI_NEG_INF)

    def level2(st):
        lo, hi, thr_key, done = st
        live = (done == 0) & (hi - lo > 1)
        mid = lo + lax.shift_right_arithmetic(hi - lo, 1)
        c = count_f32(f32_of_key(mid))
        hit = live & (c == TOPK_MAX)
        return (jnp.where(live & (c >= TOPK_MAX), mid, lo), jnp.where(live & (c < TOPK_MAX), mid, hi),
                jnp.where(hit, mid, thr_key), jnp.where(hit, 1, done))

    def n_live(st):
        lo, hi, _, done = st
        return jnp.sum(jnp.where((done == 0) & (hi - lo > 1), 1, 0))

    def level2_pair(carry):
        st = level2(level2(carry[0]))
        return st, n_live(st)

    def key32_of_key16(k):
        return lax.shift_left(k, 16) | jnp.where(k < 0, 0xFFFF, 0)

    key_t1 = key32_of_key16(t1)
    st0 = (key_t1 - 0x8000, key32_of_key16(t1 + 1), key_t1, jnp.where(settled1, 1, 0))
    (lo, _, thr_key, done2), _ = lax.while_loop(lambda carry: carry[1] > 0, level2_pair, (st0, n_live(st0)))
    open2 = done2 == 0
    thr = jnp.where(open2, f32_of_key(lo), f32_of_key(thr_key))
    thr = jnp.where(zero_tie, 0.0, thr)
    thr = jnp.where(settled1 & jnp.logical_not(zero_tie), -jnp.inf, thr)
    thr = jnp.maximum(thr, float(jnp.finfo(F32).min))

    keep_ref[...] = jnp.where(zero_tie, TOPK_MAX - c_pos, 2 ** 30).astype(F32)

    @pl.when(jnp.sum(jnp.where(open2, 1, 0)) > 0)
    def _():
        above = count_f32(thr, strict=True)
        keep_ref[...] = jnp.where(open2, (TOPK_MAX - above).astype(F32), keep_ref[...])

    @pl.when(jnp.sum(jnp.where(open2 | zero_tie, 1, 0)) > 0)
    def _():
        keep = keep_ref[...]
        lower = (lax.broadcasted_iota(I32, (TILE, TILE), 0) > lax.broadcasted_iota(I32, (TILE, TILE), 1))
        lower = jnp.where(lower, 1.0, 0.0).astype(BF16)

        def body(jj, run):
            for j in (2 * jj, 2 * jj + 1):
                blk = sc_ref[j]
                eq = blk == thr
                eq_f = jnp.where(eq, 1.0, 0.0)
                before = jnp.dot(lower, eq_f.astype(BF16), preferred_element_type=F32)
                sc_ref[j] = jnp.where(eq & (run + before >= keep), -jnp.inf, blk)
                run = run + eq_f.reshape(TILE // 8, 8, tq).sum(axis=0).sum(axis=0, keepdims=True)
            return run

        lax.fori_loop(0, (nk + 1) // 2, body, jnp.zeros((1, tq), F32))

    for g in range(B_KV_HEADS):
        m_ref[g] = jnp.full((1, B_GROUP * tq), NEG, F32)
        acc_ref[g] = jnp.zeros((2 * HEAD_DIM, B_GROUP * tq), F32)

    def attend(tiles):
        sels = [sc_ref[j] >= thr for j, _ in tiles]
        part_max = []
        for g in range(B_KV_HEADS):
            pm = None
            for t, ((j, near), sel) in enumerate(zip(tiles, sels)):
                if n_merge:
                    kt = side_by_side(kb_ref, j, slice(g * HEAD_DIM, (g + 1) * HEAD_DIM))
                    st = jnp.concatenate([lax.dot_general(kt, qbs_ref[g * B_GROUP + hh], _NT,
                                                          preferred_element_type=F32)
                                          for hh in range(B_GROUP)], axis=1)
                else:
                    kt = kb_ref[0, key_rows(j), g * HEAD_DIM:(g + 1) * HEAD_DIM]
                    st = lax.dot_general(kt, qbs_ref[g], _NT, preferred_element_type=F32)
                parts = []
                for hh in range(B_GROUP):
                    s_h = st[:, hh * tq:(hh + 1) * tq]
                    if near is not None:
                        s_h = s_h + bias_ref[g * B_GROUP + hh, near * TILE:(near + 1) * TILE, :]
                    parts.append(jnp.where(sel, s_h, -jnp.inf))
                s = jnp.concatenate(parts, axis=1)
                s_ref[g * len(tiles) + t] = s
                m8 = s.reshape(TILE // 8, 8, B_GROUP * tq).max(axis=0)
                pm = m8 if pm is None else jnp.maximum(pm, m8)
            part_max.append(pm)
        for g in range(B_KV_HEADS):
            m_old = m_ref[g]
            m_new = jnp.maximum(m_old, part_max[g].max(axis=0, keepdims=True))
            pv = None
            for t, (j, _) in enumerate(tiles):
                p = jnp.exp(s_ref[g * len(tiles) + t] - m_new)
                if n_merge:
                    lane_stream = colq // (tq // n_merge)
                    d = jnp.concatenate(
                        [jnp.dot(vt_ref[g, j],
                                 jnp.concatenate([jnp.where(lane_stream == b, p[:, hh * tq:(hh + 1) * tq], 0.0)
                                                  for b in range(n_merge)], axis=0).astype(BF16),
                                 preferred_element_type=F32) for hh in range(B_GROUP)], axis=1)
                else:
                    d = jnp.dot(vt_ref[g, j], p.astype(BF16), preferred_element_type=F32)
                pv = d if pv is None else pv + d
            acc_ref[g] = jnp.exp(m_old - m_new) * acc_ref[g] + pv
            m_ref[g] = m_new

    n_far = jnp.maximum(nk - 2, 0)

    def far_step(jj, c):
        attend([(FAR_STEP * jj + t, None) for t in range(FAR_STEP)])
        return c

    lax.fori_loop(0, n_far // FAR_STEP, far_step, 0)
    rem = n_far % FAR_STEP

    for r in range(FAR_STEP):
        @pl.when((rem == r) & (nk >= 2))
        def _(r=r):
            attend([(n_far - r + t, None) for t in range(r)] + [(nk - 2, 0), (nk - 1, 1)])

    @pl.when(nk < 2)
    def _():
        attend([(nk - 1, 1)])

    outs = []
    for g in range(B_KV_HEADS):
        for hh in range(B_GROUP):
            blk = acc_ref[g, :, hh * tq:(hh + 1) * tq].T
            outs.append(blk[:, 0:HEAD_DIM] / blk[:, HEAD_DIM:HEAD_DIM + 1])
    o_ref[0] = jnp.concatenate(outs, axis=-1).astype(BF16)


def _dsa_attn(qi, wi, qb, ki, kb, vb, g_row, off, valid_len, q_valid, tq):
    b, t_q = qi.shape[:2]
    tk = ki.shape[1]
    nq, n_kt = t_q // tq, tk // TILE
    assert tq == TILE or nq == 1
    assert n_kt % 2 == 0 and tk == n_kt * TILE
    qspec = lambda n: pl.BlockSpec((1, tq, n), lambda bb, i: (bb, i, 0))
    kspec = lambda n: pl.BlockSpec((1, tk, n), lambda bb, i: (bb, 0, 0))
    in_specs = [pl.BlockSpec((B_HEADS, DSA_ROLL), lambda bb, i: (0, 0)),
                qspec(512), qspec(128), qspec(512), kspec(IDX_DIM), kspec(128), kspec(128)]
    return _dsa_call((g_row, qi, wi, qb, ki, kb, vb), in_specs, (b, nq), t_q, tq, n_kt,
                     dict(off=off, valid_len=valid_len, q_valid=q_valid, n_merge=0))


def _dsa_attn_merged(qi, wi, qb, ki, kb, vb, g_row, off, valid_len):
    s, ts = qi.shape[:2]
    tk = ki.shape[1]
    tq, n_kt = s * ts, tk // TILE
    assert tq % LANES == 0 and ts <= CHUNK and n_kt % 2 == 0 and tk == n_kt * TILE
    flat = lambda a: a.reshape(tq, a.shape[-1])
    operands = (g_row, flat(qi), flat(wi), flat(qb), ki, kb, vb)
    whole = lambda a: pl.BlockSpec(a.shape, lambda bb, i: (0,) * a.ndim, pipeline_mode=pl.Buffered(1))
    out = _dsa_call(operands, [whole(a) for a in operands], (1, 1), tq, tq, n_kt,
                    dict(off=off, valid_len=valid_len, q_valid=tq, n_merge=s))
    return out.reshape(s, ts, B_WIDTH)


def _dsa_call(operands, in_specs, grid, t_q, tq, n_kt, static):
    n_merge = static["n_merge"]
    assert static["valid_len"] >= (static["off"] + grid[1] - 1) * TILE
    if n_merge:
        q_scratch = [pltpu.VMEM((IDX_HEADS, tq, n_merge * IDX_DIM), BF16),
                     pltpu.VMEM((B_HEADS, tq, n_merge * HEAD_DIM), BF16)]
    else:
        q_scratch = [pltpu.VMEM((IDX_HEADS * tq, IDX_DIM), BF16),
                     pltpu.VMEM((B_KV_HEADS, B_GROUP * tq, HEAD_DIM), BF16)]
    return pl.pallas_call(
        functools.partial(_dsa_kernel, n_kt=n_kt, tq=tq, **static),
        grid=grid,
        in_specs=in_specs,
        out_specs=pl.BlockSpec((1, tq, B_WIDTH), lambda bb, i: (bb, i, 0)),
        out_shape=jax.ShapeDtypeStruct((grid[0], t_q, B_WIDTH), BF16),
        scratch_shapes=[pltpu.VMEM((n_kt, TILE, tq), F32),
                        pltpu.VMEM((n_kt, TILE, tq), BF16),
                        pltpu.VMEM((1, tq), F32),
                        pltpu.VMEM((B_KV_HEADS, n_kt, 2 * HEAD_DIM, max(n_merge, 1) * TILE), BF16),
                        *q_scratch,
                        pltpu.VMEM((B_KV_HEADS, 1, B_GROUP * tq), F32),
                        pltpu.VMEM((B_KV_HEADS, 2 * HEAD_DIM, B_GROUP * tq), F32),
                        pltpu.VMEM((B_HEADS, 2 * TILE, tq), F32),
                        pltpu.VMEM((B_KV_HEADS * (FAR_STEP + 1), TILE, B_GROUP * tq), F32)],
        compiler_params=_params("arbitrary", "arbitrary"),
        name="dsa_attn",
    )(*operands)


def _mem_kv_kernel(m_ref, wk_ref, wv_ref, k_o, v_o, kb_o, vb_o):
    mb = m_ref[...].astype(BF16)
    k = jnp.dot(mb, wk_ref[...], preferred_element_type=F32)
    v = jnp.dot(mb, wv_ref[...], preferred_element_type=F32)
    k_o[...] = k
    v_o[...] = v
    kb_o[...] = k.astype(BF16)
    vb_o[...] = v.astype(BF16)


def _mem_kv(mem2d, wk, wv):
    r = mem2d.shape[0]
    tm = MEM_LEN
    row = lambda n: pl.BlockSpec((tm, n), lambda i: (i, 0))
    const = lambda s: pl.BlockSpec(s, lambda i: (0, 0))
    sds = jax.ShapeDtypeStruct
    return pl.pallas_call(
        _mem_kv_kernel,
        grid=(r // tm,),
        in_specs=[row(D_MODEL), const((D_MODEL, MEM_WIDTH)), const((D_MODEL, MEM_WIDTH))],
        out_specs=[row(MEM_WIDTH)] * 4,
        out_shape=[sds((r, MEM_WIDTH), F32), sds((r, MEM_WIDTH), F32),
                   sds((r, MEM_WIDTH), BF16), sds((r, MEM_WIDTH), BF16)],
        compiler_params=_params("arbitrary"),
        name="mem_kv",
    )(mem2d, wk, wv)


FF_CHUNK = 256


STAGE_ELEMS = 256 * 1024
N_MATS = 5


def _stage_rows(cols):
    return 1 << ((STAGE_ELEMS // cols).bit_length() - 1)


def _load_cast(src, dst, stage, sem):
    rows = stage.shape[1]
    n = src.shape[0] // rows
    copy = lambda k: pltpu.make_async_copy(src.at[pl.ds(k * rows, rows)], stage.at[k % 2], sem.at[k % 2])
    copy(0).start()
    for k in range(n):
        if k + 1 < n:
            copy(k + 1).start()
        copy(k).wait()
        dst[k * rows:(k + 1) * rows, :] = stage[k % 2].astype(BF16)


def _tail_kernel(x_ref, oa_ref, ob_ref, mk_ref, mv_ref, hist_ref,
                 wo_ref, g1_ref, b1_ref, wq_ref, wmo_ref, g2_ref, b2_ref,
                 wu_ref, wc_ref, bc_ref, wd_ref, g3_ref, b3_ref,
                 o_ref, tail_ref, *rest, tiles_per_batch, seg, cast_weights):
    i = pl.program_id(0)
    tm = x_ref.shape[0]
    nseg = tm // seg
    if cast_weights:
        mats_out, (carry_ref, act_ref), rest = rest[:N_MATS], rest[N_MATS:N_MATS + 2], rest[N_MATS + 2:]
        mats, stages, (sem_in, sem_out) = rest[:N_MATS], rest[N_MATS:-2], rest[-2:]
        stage_of = {s.shape[2]: s for s in stages}

        @pl.when(i == 0)
        def _():
            for j, src in enumerate((wo_ref, wq_ref, wmo_ref, wu_ref, wd_ref)):
                _load_cast(src, mats[j], stage_of[src.shape[1]], sem_in)
                pltpu.make_async_copy(mats[j], mats_out[j], sem_out.at[j]).start()

        wo_ref, wq_ref, wmo_ref, wu_ref, wd_ref = mats
    else:
        carry_ref, act_ref = rest

    mix = jnp.concatenate([oa_ref[...], ob_ref[...]], axis=-1)
    h = _layer_norm(ALPHA * x_ref[...] + jnp.dot(mix, wo_ref[...], preferred_element_type=F32),
                    g1_ref[...], b1_ref[...])

    q = jnp.dot(h.astype(BF16), wq_ref[...], preferred_element_type=F32).astype(BF16)
    segs = []
    for s in range(nseg):
        qs = q[s * seg:(s + 1) * seg]
        heads = []
        for hd in range(MEM_HEADS):
            sl = slice(hd * MEM_HEAD_DIM, (hd + 1) * MEM_HEAD_DIM)
            sc = lax.dot_general(qs[:, sl], mk_ref[s, :, sl], _NT, preferred_element_type=F32) * MEM_HEAD_DIM ** -0.5
            p = jnp.exp(sc - sc.max(-1, keepdims=True))
            l = p.sum(-1, keepdims=True)
            heads.append(jnp.dot(p.astype(BF16), mv_ref[s, :, sl], preferred_element_type=F32) / l)
        segs.append(jnp.concatenate(heads, axis=-1))
    att = jnp.concatenate(segs, axis=0).astype(BF16)
    h = _layer_norm(ALPHA * h + jnp.dot(att, wmo_ref[...], preferred_element_type=F32), g2_ref[...], b2_ref[...])

    hb = h.astype(BF16)
    row = lax.broadcasted_iota(I32, (tm, 1), 0)
    first = (i % tiles_per_batch) == 0
    for c in range(D_FF // FF_CHUNK):
        cs = slice(c * FF_CHUNK, (c + 1) * FF_CHUNK)
        u = jnp.dot(hb, wu_ref[:, cs], preferred_element_type=F32)
        gt = jnp.dot(hb, wu_ref[:, D_FF + c * FF_CHUNK:D_FF + (c + 1) * FF_CHUNK], preferred_element_type=F32)
        p1 = pltpu.roll(gt, 1, 0)
        p2 = pltpu.roll(gt, 2, 0)
        for s in range(nseg):
            hist = hist_ref[s, :, cs]
            if tiles_per_batch > 1:
                hist = jnp.where(first, hist, carry_ref[:, cs])
            p1 = jnp.where(row == s * seg, hist[7:8, :], p1)
            p2 = jnp.where(row == s * seg, hist[6:7, :], p2)
            p2 = jnp.where(row == s * seg + 1, hist[7:8, :], p2)
        gc = bc_ref[:, cs] + ((wc_ref[0:1, cs] * p2 + wc_ref[1:2, cs] * p1) + wc_ref[2:3, cs] * gt)
        act_ref[:, cs] = (u * jax.nn.gelu(gc)).astype(BF16)
        for s in range(nseg):
            tail_ref[s, :, cs] = gt[(s + 1) * seg - 8:(s + 1) * seg, :]
        carry_ref[:, cs] = gt[tm - 8:tm, :]
    f = jnp.dot(act_ref[...], wd_ref[...], preferred_element_type=F32)
    o_ref[...] = _layer_norm(ALPHA * h + f, g3_ref[...], b3_ref[...])

    if cast_weights:
        @pl.when(i == 0)
        def _():
            for j in range(N_MATS):
                pltpu.make_async_copy(mats[j], mats_out[j], sem_out.at[j]).wait()


def _layer_tail(x2d, oa, ob, mk, mv, hist, weights, tm, tiles_per_batch, seg):
    r = x2d.shape[0]
    nseg = tm // seg
    n_stream = r // (tm * tiles_per_batch) * nseg
    mats = [w for w in weights if w.shape[0] > CONV_W]
    cast_weights = mats[0].dtype == F32
    assert len(mats) == N_MATS and all((w.dtype == F32) == cast_weights for w in mats)
    row = lambda n: pl.BlockSpec((tm, n), lambda i: (i, 0))
    per_stream = lambda a, b: pl.BlockSpec((nseg, a, b), lambda i: (i // tiles_per_batch, 0, 0))
    const = lambda a: pl.BlockSpec(a.shape, lambda i: (0,) * a.ndim, pipeline_mode=pl.Buffered(1))
    in_hbm = pl.BlockSpec(memory_space=pl.ANY)
    w_spec = lambda w: in_hbm if cast_weights and w.shape[0] > CONV_W else const(w)
    out_specs = [row(D_MODEL), per_stream(8, D_FF)]
    out_shape = [jax.ShapeDtypeStruct((r, D_MODEL), F32), jax.ShapeDtypeStruct((n_stream, 8, D_FF), F32)]
    scratch = [pltpu.VMEM((8, D_FF), F32), pltpu.VMEM((tm, D_FF), BF16)]
    if cast_weights:
        out_specs += [in_hbm] * N_MATS
        out_shape += [jax.ShapeDtypeStruct(w.shape, BF16) for w in mats]
        scratch += [pltpu.VMEM(w.shape, BF16) for w in mats]
        scratch += [pltpu.VMEM((2, _stage_rows(c), c), F32) for c in sorted({w.shape[1] for w in mats})]
        scratch += [pltpu.SemaphoreType.DMA((2,)), pltpu.SemaphoreType.DMA((N_MATS,))]
    return pl.pallas_call(
        functools.partial(_tail_kernel, tiles_per_batch=tiles_per_batch, seg=seg, cast_weights=cast_weights),
        grid=(r // tm,),
        in_specs=[row(D_MODEL), row(A_WIDTH), row(B_WIDTH), per_stream(MEM_LEN, MEM_WIDTH),
                  per_stream(MEM_LEN, MEM_WIDTH), per_stream(8, D_FF)] + [w_spec(w) for w in weights],
        out_specs=out_specs,
        out_shape=out_shape,
        scratch_shapes=scratch,
        compiler_params=_params("arbitrary"),
        name="layer_tail",
    )(x2d, oa, ob, mk, mv, hist, *weights)


def _pad_rows(a, n):
    return jnp.pad(a, ((0, 0), (0, n - a.shape[1])) + ((0, 0),) * (a.ndim - 2))


def _hist8(g_hist):
    return jnp.pad(g_hist, ((0, 0), (8 - g_hist.shape[1], 0), (0, 0)))


def kernel(x_prompt, x_sample, cache_a_k, cache_a_v, cache_b_k, cache_b_v, cache_b_kidx, cache_mem_k, cache_mem_v, state_ffn_conv, mem_prompt, w_in, a_rel_bias, t5_bias, w_o, ln1_g, ln1_b, w_mq, w_mk, w_mv, w_mo, ln2_g, ln2_b, w_up, w_conv, b_conv, w_down, ln3_g, ln3_b):
    bp, tp = x_prompt.shape[:2]
    bs, ts = x_sample.shape[:2]
    l = 0
    vec = lambda a: a[l].reshape(1, -1)
    w_in_p = _prep_w_in(w_in[l])
    w_mk_b, w_mv_b = w_mk[l].astype(BF16), w_mv[l].astype(BF16)
    band_row = _band_bias_row(a_rel_bias[l])
    dsa_row = _dsa_bias_row(t5_bias)
    tail_weights = lambda wo, wq, wmo, wu, wd: (wo, vec(ln1_g), vec(ln1_b), wq, wmo, vec(ln2_g), vec(ln2_b),
                                                wu, w_conv[l], vec(b_conv), wd, vec(ln3_g), vec(ln3_b))

    tm = 512
    a_keep = min(N_PREV_CHUNKS * CHUNK, tp)
    (qa, ka, va, qb, qi, kb, vb, ki, kb_b, vb_b, ki_b, wi, ka_tail, va_tail) = _in_proj(
        x_prompt.reshape(bp * tp, D_MODEL), w_in_p, tm, tp // tm)
    r3 = lambda a: a.reshape(bp, tp, a.shape[-1])
    oa = _band_attn(r3(qa), r3(ka), r3(va), band_row, 0, tp)
    ob = _dsa_attn(r3(qi), r3(wi), r3(qb), r3(ki_b), r3(kb_b), r3(vb_b), dsa_row, 0, tp, TILE, TILE)
    mk, mv, mk_b, mv_b = _mem_kv(mem_prompt.reshape(bp * MEM_LEN, D_MODEL), w_mk_b, w_mv_b)
    xp, p_tail, *mats_b = _layer_tail(x_prompt.reshape(bp * tp, D_MODEL), oa.reshape(bp * tp, A_WIDTH),
                                      ob.reshape(bp * tp, B_WIDTH), mk_b.reshape(bp, MEM_LEN, MEM_WIDTH),
                                      mv_b.reshape(bp, MEM_LEN, MEM_WIDTH), jnp.zeros((bp, 8, D_FF), F32),
                                      tail_weights(w_o[l], w_mq[l], w_mo[l], w_up[l], w_down[l]), tm, tp // tm, tm)
    prompt_state = (
        ka_tail.reshape(bp, tm, A_HEADS, HEAD_DIM)[:, tm - a_keep:][None],
        va_tail.reshape(bp, tm, A_HEADS, HEAD_DIM)[:, tm - a_keep:][None],
        kb.reshape(1, bp, tp, B_KV_HEADS, HEAD_DIM), vb.reshape(1, bp, tp, B_KV_HEADS, HEAD_DIM),
        ki.reshape(1, bp, tp, IDX_DIM),
        mk.reshape(1, bp, MEM_LEN, MEM_HEADS, MEM_HEAD_DIM), mv.reshape(1, bp, MEM_LEN, MEM_HEADS, MEM_HEAD_DIM),
        p_tail[:, 8 - (CONV_W - 1):][None])

    rs = bs * ts
    (qa, ka, va, qb, qi, kb, vb, ki, kb_b, vb_b, ki_b, wi, ka_new, va_new) = _in_proj(
        x_sample.reshape(rs, D_MODEL), w_in_p, rs, 1)
    s3 = lambda a: a.reshape(bs, ts, -1)
    qpad = lambda a, n: _pad_rows(s3(a), n)

    past_a = cache_a_k.shape[2]
    n_a = past_a + ts
    t_a = -(-n_a // TILE) * TILE
    seq_a = lambda cache, new: _pad_rows(
        jnp.concatenate([cache[l].reshape(bs, past_a, A_WIDTH), s3(new)], axis=1), t_a).astype(BF16)
    oa = _band_attn(qpad(qa, TILE), seq_a(cache_a_k, ka_new), seq_a(cache_a_v, va_new), band_row,
                    past_a // TILE, n_a)

    past_b = cache_b_k.shape[2]
    n_b = past_b + ts
    t_b = -(-n_b // (2 * TILE)) * 2 * TILE
    seq_b = lambda cache, new: _pad_rows(
        jnp.concatenate([cache[l].reshape(bs, past_b, -1), s3(new)], axis=1), t_b).astype(BF16)
    ob = _dsa_attn_merged(s3(qi), s3(wi), s3(qb), seq_b(cache_b_kidx, ki), seq_b(cache_b_k, kb),
                          seq_b(cache_b_v, vb), dsa_row, past_b // TILE, n_b)

    xs, s_tail = _layer_tail(x_sample.reshape(rs, D_MODEL), oa[:, :ts].reshape(rs, A_WIDTH),
                             ob[:, :ts].reshape(rs, B_WIDTH),
                             cache_mem_k[l].reshape(bs, MEM_LEN, MEM_WIDTH).astype(BF16),
                             cache_mem_v[l].reshape(bs, MEM_LEN, MEM_WIDTH).astype(BF16),
                             _hist8(state_ffn_conv[l]), tail_weights(*mats_b), rs, 1, ts)
    sample_state = (
        ka_new.reshape(1, bs, ts, A_HEADS, HEAD_DIM), va_new.reshape(1, bs, ts, A_HEADS, HEAD_DIM),
        kb.reshape(1, bs, ts, B_KV_HEADS, HEAD_DIM), vb.reshape(1, bs, ts, B_KV_HEADS, HEAD_DIM),
        ki.reshape(1, bs, ts, IDX_DIM), s_tail[:, 8 - (CONV_W - 1):][None])

    return (xp.reshape(bp, tp, D_MODEL), xs.reshape(bs, ts, D_MODEL)) + prompt_state + sample_state
```

```python
import functools
import math

import jax
import jax.numpy as jnp
from jax import lax
from jax.experimental import pallas as pl
from jax.experimental.pallas import tpu as pltpu

F32 = jnp.float32
BF16 = jnp.bfloat16
I32 = jnp.int32
I16 = jnp.int16

D_MODEL = 1024
CHUNK = 64
N_PREV_CHUNKS = 8
HEAD_DIM = 64
A_HEADS = 8
A_WIDTH = A_HEADS * HEAD_DIM
A_MAX_REL = 64
B_HEADS = 8
B_KV_HEADS = 2
B_GROUP = B_HEADS // B_KV_HEADS
B_WIDTH = B_HEADS * HEAD_DIM
B_KV_WIDTH = B_KV_HEADS * HEAD_DIM
IDX_HEADS = 8
IDX_DIM = 64
TOPK_MAX = 256
N_BUCKETS = 32
T5_MAX_DIST = 128
MEM_LEN = 256
MEM_HEADS = 4
MEM_HEAD_DIM = 128
MEM_WIDTH = MEM_HEADS * MEM_HEAD_DIM
D_FF = 2816
CONV_W = 3
IN_SIZES = (A_WIDTH, A_WIDTH, A_WIDTH, B_WIDTH, B_KV_WIDTH, B_KV_WIDTH, IDX_HEADS * IDX_DIM, IDX_DIM, IDX_HEADS)
DEPTH = 1
ALPHA = (2 * DEPTH) ** 0.25
LN_EPS = 1e-5
ATTN_SCALE = HEAD_DIM ** -0.5
NEG = -1e30

LANES = 128
TILE = 256
BAND_TILES = 1 + (N_PREV_CHUNKS * CHUNK) // TILE
VMEM_LIMIT = 56 * 1024 * 1024

_C_QA, _C_KA, _C_VA, _C_QB = 0, 512, 1024, 1536
_C_KB, _C_VB, _C_QI, _C_KI, _C_WI = 2048, 2176, 2304, 2816, 2944
IN_PAD = 3072

MIN16 = -32768
HI_NEG_INF = -32641

_NT = (((1,), (1,)), ((), ()))


def _params(*sem):
    return pltpu.CompilerParams(dimension_semantics=sem, vmem_limit_bytes=VMEM_LIMIT)


def _layer_norm(z, g, b):
    mu = jnp.mean(z, axis=-1, keepdims=True)
    d = z - mu
    var = jnp.mean(d * d, axis=-1, keepdims=True)
    return d * lax.rsqrt(var + LN_EPS) * g + b


def _toeplitz(g_row, rows, width):
    return pltpu.roll(jnp.broadcast_to(g_row, (rows, width)), 0, 1, stride=1, stride_axis=0)


def _in_proj_kernel(x_ref, w_ref, qa_o, ka_o, va_o, qb_o, qi_o, kb_o, vb_o, ki_o, kbb_o, vbb_o, kib_o,
                    wi_o, kat_o, vat_o, *, tiles_per_batch, wi_scale):
    i = pl.program_id(0)
    xb = x_ref[...].astype(BF16)

    def mm(c0, n):
        return jnp.dot(xb, w_ref[:, c0:c0 + n], preferred_element_type=F32)

    kb = mm(_C_KB, 128)
    vb = mm(_C_VB, 128)
    for g in range(B_KV_HEADS):
        kb_o[:, g, :] = kb[:, g * HEAD_DIM:(g + 1) * HEAD_DIM]
        vb_o[:, g, :] = vb[:, g * HEAD_DIM:(g + 1) * HEAD_DIM]
    kbb_o[...] = kb.astype(BF16)
    vbb_o[...] = vb.astype(BF16)
    ki = mm(_C_KI, 128)[:, :IDX_DIM]
    ki_o[...] = ki
    kib_o[...] = ki.astype(BF16)
    wi_o[...] = mm(_C_WI, 128) * wi_scale
    qa_o[...] = mm(_C_QA, 512).astype(BF16)
    ka = mm(_C_KA, 512)
    va = mm(_C_VA, 512)
    ka_o[...] = ka.astype(BF16)
    va_o[...] = va.astype(BF16)
    qb_o[...] = mm(_C_QB, 512).astype(BF16)
    qi_o[...] = mm(_C_QI, 512).astype(BF16)

    @pl.when(i % tiles_per_batch == tiles_per_batch - 1)
    def _():
        kat_o[...] = ka
        vat_o[...] = va


def _prep_w_in(w):
    parts, off = [], 0
    for n in IN_SIZES:
        parts.append(w[:, off:off + n])
        off += n
    qa, ka, va, qb, kb, vb, qi, ki, wi = parts
    pad = lambda a, n: jnp.pad(a, ((0, 0), (0, n - a.shape[1])))
    cols = [qa * ATTN_SCALE, ka, va, qb * ATTN_SCALE, kb, vb, qi * IDX_DIM ** -0.5, pad(ki, 128), pad(wi, 128)]
    return jnp.concatenate(cols, axis=1).astype(BF16)


def _in_proj(x2d, w_pad, tm, tiles_per_batch):
    r = x2d.shape[0]
    n_tiles = r // tm
    n_batch = n_tiles // tiles_per_batch
    row = lambda n: pl.BlockSpec((tm, n), lambda i: (i, 0))
    tail = pl.BlockSpec((tm, 512), lambda i: (i // tiles_per_batch, 0))
    sds = jax.ShapeDtypeStruct
    kv_state = pl.BlockSpec((tm, B_KV_HEADS, HEAD_DIM), lambda i: (i, 0, 0))
    kv_shape = sds((r, B_KV_HEADS, HEAD_DIM), F32)
    out_shape = [sds((r, 512), BF16)] * 5 + [kv_shape, kv_shape, sds((r, IDX_DIM), F32),
                                              sds((r, 128), BF16), sds((r, 128), BF16), sds((r, IDX_DIM), BF16),
                                              sds((r, 128), F32),
                                              sds((n_batch * tm, 512), F32), sds((n_batch * tm, 512), F32)]
    out_specs = [row(512)] * 5 + [kv_state, kv_state, row(IDX_DIM), row(128), row(128), row(IDX_DIM), row(128),
                                  tail, tail]
    return pl.pallas_call(
        functools.partial(_in_proj_kernel, tiles_per_batch=tiles_per_batch, wi_scale=IDX_HEADS ** -0.5),
        grid=(n_tiles,),
        in_specs=[pl.BlockSpec((tm, D_MODEL), lambda i: (i, 0)),
                  pl.BlockSpec((D_MODEL, IN_PAD), lambda i: (0, 0))],
        out_specs=out_specs,
        out_shape=out_shape,
        compiler_params=_params("arbitrary"),
        name="in_proj",
    )(x2d, w_pad)


BAND_COLS = BAND_TILES * TILE
BAND_ROLL = BAND_COLS + TILE
BAND_SLAB = 4


def _band_bias_row(table):
    idx = jnp.arange(BAND_ROLL)
    d = jnp.where(idx < TILE, idx, idx - BAND_ROLL)
    rel = (BAND_TILES - 1) * TILE + d
    return table[jnp.clip(rel, -A_MAX_REL, A_MAX_REL) + A_MAX_REL].T.astype(F32)


def _band_kernel(g_ref, q_ref, k0, k1, k2, v0, v1, v2, o_ref, bias_ref, vt_ref, s_ref, *, off, valid_len):
    i = pl.program_id(1)
    kt = i + off
    krefs, vrefs = (k0, k1, k2), (v0, v1, v2)

    @pl.when((pl.program_id(0) == 0) & (i == 0))
    def _():
        c = lax.broadcasted_iota(I32, (BAND_COLS, TILE), 0) // CHUNK
        r = lax.broadcasted_iota(I32, (BAND_COLS, TILE), 1) // CHUNK
        ok = (c >= r) & (c <= r + N_PREV_CHUNKS)
        for h in range(A_HEADS):
            bias_ref[h] = jnp.where(ok, _toeplitz(g_ref[h:h + 1, :], BAND_COLS, BAND_ROLL)[:, :TILE], NEG)
        ones = jnp.ones((HEAD_DIM, TILE), BF16)
        for s in range(BAND_TILES):
            for h in range(A_HEADS):
                vt_ref[s, h, HEAD_DIM:2 * HEAD_DIM, :] = ones

    def put(slot, vref):
        vt = vref[0].astype(F32).T
        for h in range(A_HEADS):
            vt_ref[slot, h, 0:HEAD_DIM, :] = vt[h * HEAD_DIM:(h + 1) * HEAD_DIM].astype(BF16)

    slots = [(kt + 1 + j) % BAND_TILES for j in range(BAND_TILES)]

    @pl.when(i == 0)
    def _():
        for j in range(BAND_TILES - 1):
            put(slots[j], vrefs[j])

    put(slots[BAND_TILES - 1], vrefs[BAND_TILES - 1])

    base = (kt - (BAND_TILES - 1)) * TILE

    n_slab = BAND_SLAB
    slab_w = n_slab * HEAD_DIM
    lane_head = lax.broadcasted_iota(I32, (TILE, slab_w), 1) // HEAD_DIM

    def attend(masked):
        outs, maxes = [], []
        for g in range(A_HEADS // n_slab):
            gs = slice(g * slab_w, (g + 1) * slab_w)
            q_slab = q_ref[0, :, gs].astype(F32)
            q_bd = jnp.concatenate([jnp.where(lane_head == hh, q_slab, 0.0) for hh in range(n_slab)],
                                   axis=0).astype(BF16)
            pm = None
            for j in range(BAND_TILES):
                st = lax.dot_general(krefs[j][0, :, gs], q_bd, _NT, preferred_element_type=F32)
                parts = [st[:, hh * TILE:(hh + 1) * TILE] + bias_ref[g * n_slab + hh, j * TILE:(j + 1) * TILE, :]
                         for hh in range(n_slab)]
                if masked:
                    kpos = base + j * TILE + lax.broadcasted_iota(I32, (TILE, TILE), 0)
                    ok = (kpos >= 0) & (kpos < valid_len)
                    parts = [jnp.where(ok, x, NEG) for x in parts]
                sj = jnp.concatenate(parts, axis=1)
                s_ref[g * BAND_TILES + j] = sj
                m8 = sj.reshape(TILE // 8, 8, n_slab * TILE).max(axis=0)
                pm = m8 if pm is None else jnp.maximum(pm, m8)
            maxes.append(pm.max(axis=0, keepdims=True))
        for g, m in enumerate(maxes):
            p = [jnp.exp(s_ref[g * BAND_TILES + j] - m).astype(BF16) for j in range(BAND_TILES)]
            for hh in range(n_slab):
                acc = None
                for j in range(BAND_TILES):
                    d = jnp.dot(vt_ref[slots[j], g * n_slab + hh], p[j][:, hh * TILE:(hh + 1) * TILE],
                                preferred_element_type=F32)
                    acc = d if acc is None else acc + d
                blk = acc.T
                outs.append(blk[:, 0:HEAD_DIM] / blk[:, HEAD_DIM:HEAD_DIM + 1])
        o_ref[0] = jnp.concatenate(outs, axis=-1).astype(BF16)

    needs_mask = (base < 0) | (base + BAND_COLS > valid_len)

    @pl.when(needs_mask)
    def _():
        attend(True)

    @pl.when(jnp.logical_not(needs_mask))
    def _():
        attend(False)


def _band_attn(q, k, v, g_row, off, valid_len):
    b, tq = q.shape[:2]
    nq = tq // TILE
    qspec = pl.BlockSpec((1, TILE, A_WIDTH), lambda bb, i: (bb, i, 0))
    kspec = lambda d: pl.BlockSpec((1, TILE, A_WIDTH), lambda bb, i: (bb, jnp.maximum(i + off - d, 0), 0))
    return pl.pallas_call(
        functools.partial(_band_kernel, off=off, valid_len=valid_len),
        grid=(b, nq),
        in_specs=[pl.BlockSpec((A_HEADS, BAND_ROLL), lambda bb, i: (0, 0)),
                  qspec, kspec(2), kspec(1), kspec(0), kspec(2), kspec(1), kspec(0)],
        out_specs=pl.BlockSpec((1, TILE, A_WIDTH), lambda bb, i: (bb, i, 0)),
        out_shape=jax.ShapeDtypeStruct((b, tq, A_WIDTH), BF16),
        scratch_shapes=[pltpu.VMEM((A_HEADS, BAND_COLS, TILE), F32),
                        pltpu.VMEM((BAND_TILES, A_HEADS, 2 * HEAD_DIM, TILE), BF16),
                        pltpu.VMEM((A_HEADS // BAND_SLAB * BAND_TILES, TILE, BAND_SLAB * TILE), F32)],
        compiler_params=_params("arbitrary", "arbitrary"),
        name="band_attn",
    )(g_row, q, k, k, k, v, v, v)


DSA_ROLL = 3 * TILE
FAR_STEP = 4


def _t5_bucket(rel):
    half = N_BUCKETS // 2
    max_exact = half // 2
    n = jnp.abs(rel)
    log_ratio = jnp.log(jnp.maximum(n, 1).astype(jnp.float32) / max_exact) / math.log(T5_MAX_DIST / max_exact)
    large = jnp.minimum(max_exact + (log_ratio * (half - max_exact)).astype(jnp.int32), half - 1)
    return jnp.where(rel < 0, half, 0) + jnp.where(n < max_exact, n, large)


def _dsa_bias_row(t5_table):
    idx = jnp.arange(DSA_ROLL)
    d = jnp.where(idx < TILE, idx, idx - DSA_ROLL)
    far = t5_table[_t5_bucket(jnp.full((1,), 2 * TILE + 1, I32))]
    return (t5_table[_t5_bucket(TILE + d)] - far).T.astype(F32)


def _dsa_kernel(g_ref, qi_ref, wi_ref, qb_ref, ki_ref, kb_ref, vb_ref, o_ref,
                sc_ref, scb_ref, keep_ref, vt_ref, qis_ref, qbs_ref, m_ref, acc_ref, bias_ref, s_ref,
                *, off, valid_len, q_valid, n_kt, tq, n_merge):
    i = pl.program_id(1)
    qt = i + off
    q0 = qt * TILE
    nk = qt + 1
    key_rows = lambda j: pl.ds(pl.multiple_of(j * TILE, TILE), TILE)

    @pl.when((pl.program_id(0) == 0) & (i == 0))
    def _():
        for h in range(B_HEADS):
            tile = _toeplitz(g_ref[h:h + 1, :], 2 * TILE, DSA_ROLL)[:, :tq]
            if n_merge:
                ts = tq // n_merge
                stream = lax.broadcasted_iota(I32, (1, tq), 1) // ts
                first = tile
                for b in range(1, n_merge):
                    tile = jnp.where(stream == b, pltpu.roll(first, b * ts, 1), tile)
            bias_ref[h] = tile

    @pl.when(i == 0)
    def _():
        def body(j, c):
            for b in range(max(n_merge, 1)):
                vt = vb_ref[b, key_rows(j), :].astype(F32).T
                for g in range(B_KV_HEADS):
                    vt_ref[g, j, 0:HEAD_DIM, b * TILE:(b + 1) * TILE] = vt[g * HEAD_DIM:(g + 1) * HEAD_DIM].astype(BF16)
            for g in range(B_KV_HEADS):
                vt_ref[g, j, HEAD_DIM:2 * HEAD_DIM, :] = jnp.ones((HEAD_DIM, vt_ref.shape[-1]), BF16)
            return c

        lax.fori_loop(0, n_kt, body, 0)

    colq = lax.broadcasted_iota(I32, (1, tq), 1)
    rowk = lax.broadcasted_iota(I32, (TILE, 1), 0)

    if n_merge:
        width = n_merge * HEAD_DIM
        own = (lax.broadcasted_iota(I32, (tq, width), 0) // (tq // n_merge)
               == lax.broadcasted_iota(I32, (tq, width), 1) // HEAD_DIM)

        def block_diag(x):
            return jnp.where(own, jnp.concatenate([x.astype(F32)] * n_merge, axis=1), 0.0).astype(BF16)

        qi, qb = qi_ref[...], qb_ref[...]
        for h in range(B_HEADS):
            qis_ref[h] = block_diag(qi[:, h * IDX_DIM:(h + 1) * IDX_DIM])
            qbs_ref[h] = block_diag(qb[:, h * HEAD_DIM:(h + 1) * HEAD_DIM])
        side_by_side = lambda ref, j, cols: jnp.concatenate(
            [ref[b, key_rows(j), cols] for b in range(n_merge)], axis=1)
        wi_t = wi_ref[...].T
        lim = jnp.full((1, tq), valid_len, I32)
    else:
        qi = qi_ref[0]
        for h in range(IDX_HEADS):
            qis_ref[h * tq:(h + 1) * tq, :] = qi[:, h * IDX_DIM:(h + 1) * IDX_DIM]
        qb = qb_ref[0]
        for g in range(B_KV_HEADS):
            for hh in range(B_GROUP):
                h = g * B_GROUP + hh
                qbs_ref[g, hh * tq:(hh + 1) * tq, :] = qb[:, h * HEAD_DIM:(h + 1) * HEAD_DIM]
        wi_t = wi_ref[0].T
        lim = jnp.minimum(q0 + (colq // CHUNK + 1) * CHUNK, valid_len)

    def score_tile(j, masked):
        if n_merge:
            kt = side_by_side(ki_ref, j, slice(None))
            lg = jnp.concatenate([lax.dot_general(kt, qis_ref[h], _NT, preferred_element_type=F32)
                                  for h in range(IDX_HEADS)], axis=1)
        else:
            kt = ki_ref[0, key_rows(j), :]
            lg = lax.dot_general(kt, qis_ref[...], _NT, preferred_element_type=F32)
        sc = wi_t[0:1, :] * jnp.maximum(lg[:, 0:tq], 0.0)
        for h in range(1, IDX_HEADS):
            sc = sc + wi_t[h:h + 1, :] * jnp.maximum(lg[:, h * tq:(h + 1) * tq], 0.0)
        if masked:
            sc = jnp.where(j * TILE + rowk < lim, sc, -jnp.inf)
        sc_ref[j] = sc
        scb_ref[j] = sc.astype(BF16)
        rows8 = lambda hit: jnp.where(hit, 1, 0).reshape(TILE // 8, 8, tq).sum(axis=0)
        return rows8(sc > 0.0), rows8(sc >= 0.0)

    def score_pair(jj, c, masked):
        pos_a, nn_a = score_tile(2 * jj, masked)
        pos_b, nn_b = score_tile(2 * jj + 1, masked)
        return c[0] + pos_a + pos_b, c[1] + nn_a + nn_b

    def score_quad(jj, c):
        for t in range(4):
            pos, nn = score_tile(4 * jj + t, False)
            c = (c[0] + pos, c[1] + nn)
        return c

    n_open_quads = (nk - 1) // 4
    counts = lax.fori_loop(0, n_open_quads, score_quad, (jnp.zeros((8, tq), I32), jnp.zeros((8, tq), I32)))
    n_pos, n_nonneg = lax.fori_loop(2 * n_open_quads, (nk + 1) // 2, functools.partial(score_pair, masked=True),
                                    counts)

    def f32_of_key(k):
        return pltpu.bitcast(jnp.where(k < 0, k ^ 0x7FFFFFFF, k), F32)

    def bf16_of_key(k):
        bits = jnp.where(k < 0, k ^ 0x7FFF, k) & 0xFFFF
        return pltpu.bitcast(lax.shift_left(bits, 16), F32).astype(BF16)

    def count_bf16(cand):
        def body(jj, acc):
            for j in (2 * jj, 2 * jj + 1):
                ge = jnp.where(scb_ref[j] >= cand, jnp.int16(1), jnp.int16(0)).reshape(TILE // 16, 16, tq)
                part = ge[0]
                for r in range(1, TILE // 16):
                    part = part + ge[r]
                acc = acc + part
            return acc

        acc = lax.fori_loop(0, (nk + 1) // 2, body, jnp.zeros((16, tq), I16))
        return acc.astype(I32).sum(axis=0, keepdims=True)

    def count_f32(cand, strict=False):
        def body(j, acc):
            blk = sc_ref[j]
            hit = (blk > cand) if strict else (blk >= cand)
            return acc + jnp.where(hit, 1, 0).reshape(TILE // 8, 8, tq).sum(axis=0)

        return lax.fori_loop(0, nk, body, jnp.zeros((8, tq), I32)).sum(axis=0, keepdims=True)

    c_pos = n_pos.sum(axis=0, keepdims=True)
    zero_tie = (c_pos < TOPK_MAX) & (n_nonneg.sum(axis=0, keepdims=True) >= TOPK_MAX)
    skip1 = (colq >= q_valid) | zero_tie

    def level1(it, t):
        cand = t + lax.shift_left(jnp.int32(1), 15 - it)
        return jnp.where(count_bf16(bf16_of_key(cand)) >= TOPK_MAX, cand, t)

    t1 = lax.fori_loop(0, 16, level1, jnp.full((1, tq), MIN16, I32))
    settled1 = skip1 | (t1 <= HI_NEG_INF)

    def level2(st):
        lo, hi, thr_key, done = st
        live = (done == 0) & (hi - lo > 1)
        mid = lo + lax.shift_right_arithmetic(hi - lo, 1)
        c = count_f32(f32_of_key(mid))
        hit = live & (c == TOPK_MAX)
        return (jnp.where(live & (c >= TOPK_MAX), mid, lo), jnp.where(live & (c < TOPK_MAX), mid, hi),
                jnp.where(hit, mid, thr_key), jnp.where(hit, 1, done))

    def n_live(st):
        lo, hi, _, done = st
        return jnp.sum(jnp.where((done == 0) & (hi - lo > 1), 1, 0))

    def level2_pair(carry):
        st = level2(level2(carry[0]))
        return st, n_live(st)

    def key32_of_key16(k):
        return lax.shift_left(k, 16) | jnp.where(k < 0, 0xFFFF, 0)

    key_t1 = key32_of_key16(t1)
    st0 = (key_t1 - 0x8000, key32_of_key16(t1 + 1), key_t1, jnp.where(settled1, 1, 0))
    (lo, _, thr_key, done2), _ = lax.while_loop(lambda carry: carry[1] > 0, level2_pair, (st0, n_live(st0)))
    open2 = done2 == 0
    thr = jnp.where(open2, f32_of_key(lo), f32_of_key(thr_key))
    thr = jnp.where(zero_tie, 0.0, thr)
    thr = jnp.where(settled1 & jnp.logical_not(zero_tie), -jnp.inf, thr)
    thr = jnp.maximum(thr, float(jnp.finfo(F32).min))

    keep_ref[...] = jnp.where(zero_tie, TOPK_MAX - c_pos, 2 ** 30).astype(F32)

    @pl.when(jnp.sum(jnp.where(open2, 1, 0)) > 0)
    def _():
        above = count_f32(thr, strict=True)
        keep_ref[...] = jnp.where(open2, (TOPK_MAX - above).astype(F32), keep_ref[...])

    @pl.when(jnp.sum(jnp.where(open2 | zero_tie, 1, 0)) > 0)
    def _():
        keep = keep_ref[...]
        lower = (lax.broadcasted_iota(I32, (TILE, TILE), 0) > lax.broadcasted_iota(I32, (TILE, TILE), 1))
        lower = jnp.where(lower, 1.0, 0.0).astype(BF16)

        def body(jj, run):
            for j in (2 * jj, 2 * jj + 1):
                blk = sc_ref[j]
                eq = blk == thr
                eq_f = jnp.where(eq, 1.0, 0.0)
                before = jnp.dot(lower, eq_f.astype(BF16), preferred_element_type=F32)
                sc_ref[j] = jnp.where(eq & (run + before >= keep), -jnp.inf, blk)
                run = run + eq_f.reshape(TILE // 8, 8, tq).sum(axis=0).sum(axis=0, keepdims=True)
            return run

        lax.fori_loop(0, (nk + 1) // 2, body, jnp.zeros((1, tq), F32))

    for g in range(B_KV_HEADS):
        m_ref[g] = jnp.full((1, B_GROUP * tq), NEG, F32)
        acc_ref[g] = jnp.zeros((2 * HEAD_DIM, B_GROUP * tq), F32)

    def attend(tiles):
        sels = [sc_ref[j] >= thr for j, _ in tiles]
        part_max = []
        for g in range(B_KV_HEADS):
            pm = None
            for t, ((j, near), sel) in enumerate(zip(tiles, sels)):
                if n_merge:
                    kt = side_by_side(kb_ref, j, slice(g * HEAD_DIM, (g + 1) * HEAD_DIM))
                    st = jnp.concatenate([lax.dot_general(kt, qbs_ref[g * B_GROUP + hh], _NT,
                                                          preferred_element_type=F32)
                                          for hh in range(B_GROUP)], axis=1)
                else:
                    kt = kb_ref[0, key_rows(j), g * HEAD_DIM:(g + 1) * HEAD_DIM]
                    st = lax.dot_general(kt, qbs_ref[g], _NT, preferred_element_type=F32)
                parts = []
                for hh in range(B_GROUP):
                    s_h = st[:, hh * tq:(hh + 1) * tq]
                    if near is not None:
                        s_h = s_h + bias_ref[g * B_GROUP + hh, near * TILE:(near + 1) * TILE, :]
                    parts.append(jnp.where(sel, s_h, -jnp.inf))
                s = jnp.concatenate(parts, axis=1)
                s_ref[g * len(tiles) + t] = s
                m8 = s.reshape(TILE // 8, 8, B_GROUP * tq).max(axis=0)
                pm = m8 if pm is None else jnp.maximum(pm, m8)
            part_max.append(pm)
        for g in range(B_KV_HEADS):
            m_old = m_ref[g]
            m_new = jnp.maximum(m_old, part_max[g].max(axis=0, keepdims=True))
            pv = None
            for t, (j, _) in enumerate(tiles):
                p = jnp.exp(s_ref[g * len(tiles) + t] - m_new)
                if n_merge:
                    lane_stream = colq // (tq // n_merge)
                    d = jnp.concatenate(
                        [jnp.dot(vt_ref[g, j],
                                 jnp.concatenate([jnp.where(lane_stream == b, p[:, hh * tq:(hh + 1) * tq], 0.0)
                                                  for b in range(n_merge)], axis=0).astype(BF16),
                                 preferred_element_type=F32) for hh in range(B_GROUP)], axis=1)
                else:
                    d = jnp.dot(vt_ref[g, j], p.astype(BF16), preferred_element_type=F32)
                pv = d if pv is None else pv + d
            acc_ref[g] = jnp.exp(m_old - m_new) * acc_ref[g] + pv
            m_ref[g] = m_new

    n_far = jnp.maximum(nk - 2, 0)

    def far_step(jj, c):
        attend([(FAR_STEP * jj + t, None) for t in range(FAR_STEP)])
        return c

    lax.fori_loop(0, n_far // FAR_STEP, far_step, 0)
    rem = n_far % FAR_STEP

    for r in range(FAR_STEP):
        @pl.when((rem == r) & (nk >= 2))
        def _(r=r):
            attend([(n_far - r + t, None) for t in range(r)] + [(nk - 2, 0), (nk - 1, 1)])

    @pl.when(nk < 2)
    def _():
        attend([(nk - 1, 1)])

    outs = []
    for g in range(B_KV_HEADS):
        for hh in range(B_GROUP):
            blk = acc_ref[g, :, hh * tq:(hh + 1) * tq].T
            outs.append(blk[:, 0:HEAD_DIM] / blk[:, HEAD_DIM:HEAD_DIM + 1])
    o_ref[0] = jnp.concatenate(outs, axis=-1).astype(BF16)


def _dsa_attn(qi, wi, qb, ki, kb, vb, g_row, off, valid_len, q_valid, tq):
    b, t_q = qi.shape[:2]
    tk = ki.shape[1]
    nq, n_kt = t_q // tq, tk // TILE
    assert tq == TILE or nq == 1
    assert n_kt % 2 == 0 and tk == n_kt * TILE
    qspec = lambda n: pl.BlockSpec((1, tq, n), lambda bb, i: (bb, i, 0))
    kspec = lambda n: pl.BlockSpec((1, tk, n), lambda bb, i: (bb, 0, 0))
    in_specs = [pl.BlockSpec((B_HEADS, DSA_ROLL), lambda bb, i: (0, 0)),
                qspec(512), qspec(128), qspec(512), kspec(IDX_DIM), kspec(128), kspec(128)]
    return _dsa_call((g_row, qi, wi, qb, ki, kb, vb), in_specs, (b, nq), t_q, tq, n_kt,
                     dict(off=off, valid_len=valid_len, q_valid=q_valid, n_merge=0))


def _dsa_attn_merged(qi, wi, qb, ki, kb, vb, g_row, off, valid_len):
    s, ts = qi.shape[:2]
    tk = ki.shape[1]
    tq, n_kt = s * ts, tk // TILE
    assert tq % LANES == 0 and ts <= CHUNK and n_kt % 2 == 0 and tk == n_kt * TILE
    flat = lambda a: a.reshape(tq, a.shape[-1])
    operands = (g_row, flat(qi), flat(wi), flat(qb), ki, kb, vb)
    whole = lambda a: pl.BlockSpec(a.shape, lambda bb, i: (0,) * a.ndim, pipeline_mode=pl.Buffered(1))
    out = _dsa_call(operands, [whole(a) for a in operands], (1, 1), tq, tq, n_kt,
                    dict(off=off, valid_len=valid_len, q_valid=tq, n_merge=s))
    return out.reshape(s, ts, B_WIDTH)


def _dsa_call(operands, in_specs, grid, t_q, tq, n_kt, static):
    n_merge = static["n_merge"]
    assert static["valid_len"] >= (static["off"] + grid[1] - 1) * TILE
    if n_merge:
        q_scratch = [pltpu.VMEM((IDX_HEADS, tq, n_merge * IDX_DIM), BF16),
                     pltpu.VMEM((B_HEADS, tq, n_merge * HEAD_DIM), BF16)]
    else:
        q_scratch = [pltpu.VMEM((IDX_HEADS * tq, IDX_DIM), BF16),
                     pltpu.VMEM((B_KV_HEADS, B_GROUP * tq, HEAD_DIM), BF16)]
    return pl.pallas_call(
        functools.partial(_dsa_kernel, n_kt=n_kt, tq=tq, **static),
        grid=grid,
        in_specs=in_specs,
        out_specs=pl.BlockSpec((1, tq, B_WIDTH), lambda bb, i: (bb, i, 0)),
        out_shape=jax.ShapeDtypeStruct((grid[0], t_q, B_WIDTH), BF16),
        scratch_shapes=[pltpu.VMEM((n_kt, TILE, tq), F32),
                        pltpu.VMEM((n_kt, TILE, tq), BF16),
                        pltpu.VMEM((1, tq), F32),
                        pltpu.VMEM((B_KV_HEADS, n_kt, 2 * HEAD_DIM, max(n_merge, 1) * TILE), BF16),
                        *q_scratch,
                        pltpu.VMEM((B_KV_HEADS, 1, B_GROUP * tq), F32),
                        pltpu.VMEM((B_KV_HEADS, 2 * HEAD_DIM, B_GROUP * tq), F32),
                        pltpu.VMEM((B_HEADS, 2 * TILE, tq), F32),
                        pltpu.VMEM((B_KV_HEADS * (FAR_STEP + 1), TILE, B_GROUP * tq), F32)],
        compiler_params=_params("arbitrary", "arbitrary"),
        name="dsa_attn",
    )(*operands)


def _mem_kv_kernel(m_ref, wk_ref, wv_ref, k_o, v_o, kb_o, vb_o):
    mb = m_ref[...].astype(BF16)
    k = jnp.dot(mb, wk_ref[...], preferred_element_type=F32)
    v = jnp.dot(mb, wv_ref[...], preferred_element_type=F32)
    k_o[...] = k
    v_o[...] = v
    kb_o[...] = k.astype(BF16)
    vb_o[...] = v.astype(BF16)


def _mem_kv(mem2d, wk, wv):
    r = mem2d.shape[0]
    tm = MEM_LEN
    row = lambda n: pl.BlockSpec((tm, n), lambda i: (i, 0))
    const = lambda s: pl.BlockSpec(s, lambda i: (0, 0))
    sds = jax.ShapeDtypeStruct
    return pl.pallas_call(
        _mem_kv_kernel,
        grid=(r // tm,),
        in_specs=[row(D_MODEL), const((D_MODEL, MEM_WIDTH)), const((D_MODEL, MEM_WIDTH))],
        out_specs=[row(MEM_WIDTH)] * 4,
        out_shape=[sds((r, MEM_WIDTH), F32), sds((r, MEM_WIDTH), F32),
                   sds((r, MEM_WIDTH), BF16), sds((r, MEM_WIDTH), BF16)],
        compiler_params=_params("arbitrary"),
        name="mem_kv",
    )(mem2d, wk, wv)


FF_CHUNK = 256


STAGE_ELEMS = 256 * 1024
STAGE_SLOTS = 4
N_MATS = 5


def _stage_rows(cols):
    return 1 << ((STAGE_ELEMS // cols).bit_length() - 1)


def _load_cast(src, dst, stage, sem):
    slots, rows = stage.shape[:2]
    n = src.shape[0] // rows
    copy = lambda k: pltpu.make_async_copy(src.at[pl.ds(k * rows, rows)], stage.at[k % slots], sem.at[k % slots])
    for k in range(min(slots - 1, n)):
        copy(k).start()
    for k in range(n):
        if k + slots - 1 < n:
            copy(k + slots - 1).start()
        copy(k).wait()
        dst[k * rows:(k + 1) * rows, :] = stage[k % slots].astype(BF16)


def _tail_kernel(x_ref, oa_ref, ob_ref, mk_ref, mv_ref, hist_ref,
                 wo_ref, g1_ref, b1_ref, wq_ref, wmo_ref, g2_ref, b2_ref,
                 wu_ref, wc_ref, bc_ref, wd_ref, g3_ref, b3_ref,
                 o_ref, tail_ref, *rest, tiles_per_batch, seg, cast_weights):
    i = pl.program_id(0)
    tm = x_ref.shape[0]
    nseg = tm // seg
    if cast_weights:
        mats_out, (carry_ref, act_ref), rest = rest[:N_MATS], rest[N_MATS:N_MATS + 2], rest[N_MATS + 2:]
        mats, stages, (sem_in, sem_out) = rest[:N_MATS], rest[N_MATS:-2], rest[-2:]
        stage_of = {s.shape[2]: s for s in stages}

        @pl.when(i == 0)
        def _():
            for j, src in enumerate((wo_ref, wq_ref, wmo_ref, wu_ref, wd_ref)):
                _load_cast(src, mats[j], stage_of[src.shape[1]], sem_in)
                pltpu.make_async_copy(mats[j], mats_out[j], sem_out.at[j]).start()

        wo_ref, wq_ref, wmo_ref, wu_ref, wd_ref = mats
    else:
        carry_ref, act_ref = rest

    mix = jnp.concatenate([oa_ref[...], ob_ref[...]], axis=-1)
    h = _layer_norm(ALPHA * x_ref[...] + jnp.dot(mix, wo_ref[...], preferred_element_type=F32),
                    g1_ref[...], b1_ref[...])

    q = jnp.dot(h.astype(BF16), wq_ref[...], preferred_element_type=F32).astype(BF16)
    segs = []
    for s in range(nseg):
        qs = q[s * seg:(s + 1) * seg]
        heads = []
        for hd in range(MEM_HEADS):
            sl = slice(hd * MEM_HEAD_DIM, (hd + 1) * MEM_HEAD_DIM)
            sc = lax.dot_general(qs[:, sl], mk_ref[s, :, sl], _NT, preferred_element_type=F32) * MEM_HEAD_DIM ** -0.5
            p = jnp.exp(sc - sc.max(-1, keepdims=True))
            l = p.sum(-1, keepdims=True)
            heads.append(jnp.dot(p.astype(BF16), mv_ref[s, :, sl], preferred_element_type=F32) / l)
        segs.append(jnp.concatenate(heads, axis=-1))
    att = jnp.concatenate(segs, axis=0).astype(BF16)
    h = _layer_norm(ALPHA * h + jnp.dot(att, wmo_ref[...], preferred_element_type=F32), g2_ref[...], b2_ref[...])

    hb = h.astype(BF16)
    row = lax.broadcasted_iota(I32, (tm, 1), 0)
    first = (i % tiles_per_batch) == 0
    for c in range(D_FF // FF_CHUNK):
        cs = slice(c * FF_CHUNK, (c + 1) * FF_CHUNK)
        u = jnp.dot(hb, wu_ref[:, cs], preferred_element_type=F32)
        gt = jnp.dot(hb, wu_ref[:, D_FF + c * FF_CHUNK:D_FF + (c + 1) * FF_CHUNK], preferred_element_type=F32)
        p1 = pltpu.roll(gt, 1, 0)
        p2 = pltpu.roll(gt, 2, 0)
        for s in range(nseg):
            hist = hist_ref[s, :, cs]
            if tiles_per_batch > 1:
                hist = jnp.where(first, hist, carry_ref[:, cs])
            p1 = jnp.where(row == s * seg, hist[7:8, :], p1)
            p2 = jnp.where(row == s * seg, hist[6:7, :], p2)
            p2 = jnp.where(row == s * seg + 1, hist[7:8, :], p2)
        gc = bc_ref[:, cs] + ((wc_ref[0:1, cs] * p2 + wc_ref[1:2, cs] * p1) + wc_ref[2:3, cs] * gt)
        act_ref[:, cs] = (u * jax.nn.gelu(gc)).astype(BF16)
        for s in range(nseg):
            tail_ref[s, :, cs] = gt[(s + 1) * seg - 8:(s + 1) * seg, :]
        carry_ref[:, cs] = gt[tm - 8:tm, :]
    f = jnp.dot(act_ref[...], wd_ref[...], preferred_element_type=F32)
    o_ref[...] = _layer_norm(ALPHA * h + f, g3_ref[...], b3_ref[...])

    if cast_weights:
        @pl.when(i == 0)
        def _():
            for j in range(N_MATS):
                pltpu.make_async_copy(mats[j], mats_out[j], sem_out.at[j]).wait()


def _layer_tail(x2d, oa, ob, mk, mv, hist, weights, tm, tiles_per_batch, seg):
    r = x2d.shape[0]
    nseg = tm // seg
    n_stream = r // (tm * tiles_per_batch) * nseg
    mats = [w for w in weights if w.shape[0] > CONV_W]
    cast_weights = mats[0].dtype == F32
    assert len(mats) == N_MATS and all((w.dtype == F32) == cast_weights for w in mats)
    row = lambda n: pl.BlockSpec((tm, n), lambda i: (i, 0))
    per_stream = lambda a, b: pl.BlockSpec((nseg, a, b), lambda i: (i // tiles_per_batch, 0, 0))
    const = lambda a: pl.BlockSpec(a.shape, lambda i: (0,) * a.ndim, pipeline_mode=pl.Buffered(1))
    in_hbm = pl.BlockSpec(memory_space=pl.ANY)
    w_spec = lambda w: in_hbm if cast_weights and w.shape[0] > CONV_W else const(w)
    out_specs = [row(D_MODEL), per_stream(8, D_FF)]
    out_shape = [jax.ShapeDtypeStruct((r, D_MODEL), F32), jax.ShapeDtypeStruct((n_stream, 8, D_FF), F32)]
    scratch = [pltpu.VMEM((8, D_FF), F32), pltpu.VMEM((tm, D_FF), BF16)]
    if cast_weights:
        out_specs += [in_hbm] * N_MATS
        out_shape += [jax.ShapeDtypeStruct(w.shape, BF16) for w in mats]
        scratch += [pltpu.VMEM(w.shape, BF16) for w in mats]
        scratch += [pltpu.VMEM((STAGE_SLOTS, _stage_rows(c), c), F32) for c in sorted({w.shape[1] for w in mats})]
        scratch += [pltpu.SemaphoreType.DMA((STAGE_SLOTS,)), pltpu.SemaphoreType.DMA((N_MATS,))]
    return pl.pallas_call(
        functools.partial(_tail_kernel, tiles_per_batch=tiles_per_batch, seg=seg, cast_weights=cast_weights),
        grid=(r // tm,),
        in_specs=[row(D_MODEL), row(A_WIDTH), row(B_WIDTH), per_stream(MEM_LEN, MEM_WIDTH),
                  per_stream(MEM_LEN, MEM_WIDTH), per_stream(8, D_FF)] + [w_spec(w) for w in weights],
        out_specs=out_specs,
        out_shape=out_shape,
        scratch_shapes=scratch,
        compiler_params=_params("arbitrary"),
        name="layer_tail",
    )(x2d, oa, ob, mk, mv, hist, *weights)


def _pad_rows(a, n):
    return jnp.pad(a, ((0, 0), (0, n - a.shape[1])) + ((0, 0),) * (a.ndim - 2))


def _hist8(g_hist):
    return jnp.pad(g_hist, ((0, 0), (8 - g_hist.shape[1], 0), (0, 0)))


def kernel(x_prompt, x_sample, cache_a_k, cache_a_v, cache_b_k, cache_b_v, cache_b_kidx, cache_mem_k, cache_mem_v, state_ffn_conv, mem_prompt, w_in, a_rel_bias, t5_bias, w_o, ln1_g, ln1_b, w_mq, w_mk, w_mv, w_mo, ln2_g, ln2_b, w_up, w_conv, b_conv, w_down, ln3_g, ln3_b):
    bp, tp = x_prompt.shape[:2]
    bs, ts = x_sample.shape[:2]
    l = 0
    vec = lambda a: a[l].reshape(1, -1)
    w_in_p = _prep_w_in(w_in[l])
    w_mk_b, w_mv_b = w_mk[l].astype(BF16), w_mv[l].astype(BF16)
    band_row = _band_bias_row(a_rel_bias[l])
    dsa_row = _dsa_bias_row(t5_bias)
    tail_weights = lambda wo, wq, wmo, wu, wd: (wo, vec(ln1_g), vec(ln1_b), wq, wmo, vec(ln2_g), vec(ln2_b),
                                                wu, w_conv[l], vec(b_conv), wd, vec(ln3_g), vec(ln3_b))

    tm = 512
    a_keep = min(N_PREV_CHUNKS * CHUNK, tp)
    (qa, ka, va, qb, qi, kb, vb, ki, kb_b, vb_b, ki_b, wi, ka_tail, va_tail) = _in_proj(
        x_prompt.reshape(bp * tp, D_MODEL), w_in_p, tm, tp // tm)
    r3 = lambda a: a.reshape(bp, tp, a.shape[-1])
    oa = _band_attn(r3(qa), r3(ka), r3(va), band_row, 0, tp)
    ob = _dsa_attn(r3(qi), r3(wi), r3(qb), r3(ki_b), r3(kb_b), r3(vb_b), dsa_row, 0, tp, TILE, TILE)
    mk, mv, mk_b, mv_b = _mem_kv(mem_prompt.reshape(bp * MEM_LEN, D_MODEL), w_mk_b, w_mv_b)
    xp, p_tail, *mats_b = _layer_tail(x_prompt.reshape(bp * tp, D_MODEL), oa.reshape(bp * tp, A_WIDTH),
                                      ob.reshape(bp * tp, B_WIDTH), mk_b.reshape(bp, MEM_LEN, MEM_WIDTH),
                                      mv_b.reshape(bp, MEM_LEN, MEM_WIDTH), jnp.zeros((bp, 8, D_FF), F32),
                                      tail_weights(w_o[l], w_mq[l], w_mo[l], w_up[l], w_down[l]), tm, tp // tm, tm)
    prompt_state = (
        ka_tail.reshape(bp, tm, A_HEADS, HEAD_DIM)[:, tm - a_keep:][None],
        va_tail.reshape(bp, tm, A_HEADS, HEAD_DIM)[:, tm - a_keep:][None],
        kb.reshape(1, bp, tp, B_KV_HEADS, HEAD_DIM), vb.reshape(1, bp, tp, B_KV_HEADS, HEAD_DIM),
        ki.reshape(1, bp, tp, IDX_DIM),
        mk.reshape(1, bp, MEM_LEN, MEM_HEADS, MEM_HEAD_DIM), mv.reshape(1, bp, MEM_LEN, MEM_HEADS, MEM_HEAD_DIM),
        p_tail[:, 8 - (CONV_W - 1):][None])

    rs = bs * ts
    (qa, ka, va, qb, qi, kb, vb, ki, kb_b, vb_b, ki_b, wi, ka_new, va_new) = _in_proj(
        x_sample.reshape(rs, D_MODEL), w_in_p, rs, 1)
    s3 = lambda a: a.reshape(bs, ts, -1)
    qpad = lambda a, n: _pad_rows(s3(a), n)

    past_a = cache_a_k.shape[2]
    n_a = past_a + ts
    t_a = -(-n_a // TILE) * TILE
    seq_a = lambda cache, new: _pad_rows(
        jnp.concatenate([cache[l].reshape(bs, past_a, A_WIDTH), s3(new)], axis=1), t_a).astype(BF16)
    oa = _band_attn(qpad(qa, TILE), seq_a(cache_a_k, ka_new), seq_a(cache_a_v, va_new), band_row,
                    past_a // TILE, n_a)

    past_b = cache_b_k.shape[2]
    n_b = past_b + ts
    t_b = -(-n_b // (2 * TILE)) * 2 * TILE
    seq_b = lambda cache, new: _pad_rows(
        jnp.concatenate([cache[l].reshape(bs, past_b, -1), s3(new)], axis=1), t_b).astype(BF16)
    ob = _dsa_attn_merged(s3(qi), s3(wi), s3(qb), seq_b(cache_b_kidx, ki), seq_b(cache_b_k, kb),
                          seq_b(cache_b_v, vb), dsa_row, past_b // TILE, n_b)

    xs, s_tail = _layer_tail(x_sample.reshape(rs, D_MODEL), oa[:, :ts].reshape(rs, A_WIDTH),
                             ob[:, :ts].reshape(rs, B_WIDTH),
                             cache_mem_k[l].reshape(bs, MEM_LEN, MEM_WIDTH).astype(BF16),
                             cache_mem_v[l].reshape(bs, MEM_LEN, MEM_WIDTH).astype(BF16),
                             _hist8(state_ffn_conv[l]), tail_weights(*mats_b), rs, 1, ts)
    sample_state = (
        ka_new.reshape(1, bs, ts, A_HEADS, HEAD_DIM), va_new.reshape(1, bs, ts, A_HEADS, HEAD_DIM),
        kb.reshape(1, bs, ts, B_KV_HEADS, HEAD_DIM), vb.reshape(1, bs, ts, B_KV_HEADS, HEAD_DIM),
        ki.reshape(1, bs, ts, IDX_DIM), s_tail[:, 8 - (CONV_W - 1):][None])

    return (xp.reshape(bp, tp, D_MODEL), xs.reshape(bs, ts, D_MODEL)) + prompt_state + sample_state
```

```python
import functools
import math

import jax
import jax.numpy as jnp
from jax import lax
from jax.experimental import pallas as pl
from jax.experimental.pallas import tpu as pltpu

F32 = jnp.float32
BF16 = jnp.bfloat16
I32 = jnp.int32
I16 = jnp.int16

D_MODEL = 1024
CHUNK = 64
N_PREV_CHUNKS = 8
HEAD_DIM = 64
A_HEADS = 8
A_WIDTH = A_HEADS * HEAD_DIM
A_MAX_REL = 64
B_HEADS = 8
B_KV_HEADS = 2
B_GROUP = B_HEADS // B_KV_HEADS
B_WIDTH = B_HEADS * HEAD_DIM
B_KV_WIDTH = B_KV_HEADS * HEAD_DIM
IDX_HEADS = 8
IDX_DIM = 64
TOPK_MAX = 256
N_BUCKETS = 32
T5_MAX_DIST = 128
MEM_LEN = 256
MEM_HEADS = 4
MEM_HEAD_DIM = 128
MEM_WIDTH = MEM_HEADS * MEM_HEAD_DIM
D_FF = 2816
CONV_W = 3
IN_SIZES = (A_WIDTH, A_WIDTH, A_WIDTH, B_WIDTH, B_KV_WIDTH, B_KV_WIDTH, IDX_HEADS * IDX_DIM, IDX_DIM, IDX_HEADS)
DEPTH = 1
ALPHA = (2 * DEPTH) ** 0.25
LN_EPS = 1e-5
ATTN_SCALE = HEAD_DIM ** -0.5
NEG = -1e30

LANES = 128
TILE = 256
BAND_TILES = 1 + (N_PREV_CHUNKS * CHUNK) // TILE
VMEM_LIMIT = 56 * 1024 * 1024

_C_QA, _C_KA, _C_VA, _C_QB = 0, 512, 1024, 1536
_C_KB, _C_VB, _C_QI, _C_KI, _C_WI = 2048, 2176, 2304, 2816, 2944
IN_PAD = 3072

MIN16 = -32768
HI_NEG_INF = -32641

_NT = (((1,), (1,)), ((), ()))


def _params(*sem):
    return pltpu.CompilerParams(dimension_semantics=sem, vmem_limit_bytes=VMEM_LIMIT)


def _layer_norm(z, g, b):
    mu = jnp.mean(z, axis=-1, keepdims=True)
    d = z - mu
    var = jnp.mean(d * d, axis=-1, keepdims=True)
    return d * lax.rsqrt(var + LN_EPS) * g + b


def _toeplitz(g_row, rows, width):
    return pltpu.roll(jnp.broadcast_to(g_row, (rows, width)), 0, 1, stride=1, stride_axis=0)


def _in_proj_kernel(x_ref, w_ref, qa_o, ka_o, va_o, qb_o, qi_o, kb_o, vb_o, ki_o, kbb_o, vbb_o, kib_o,
                    wi_o, kat_o, vat_o, *, tiles_per_batch, wi_scale):
    i = pl.program_id(0)
    xb = x_ref[...].astype(BF16)

    def mm(c0, n):
        return jnp.dot(xb, w_ref[:, c0:c0 + n], preferred_element_type=F32)

    kb = mm(_C_KB, 128)
    vb = mm(_C_VB, 128)
    for g in range(B_KV_HEADS):
        kb_o[:, g, :] = kb[:, g * HEAD_DIM:(g + 1) * HEAD_DIM]
        vb_o[:, g, :] = vb[:, g * HEAD_DIM:(g + 1) * HEAD_DIM]
    kbb_o[...] = kb.astype(BF16)
    vbb_o[...] = vb.astype(BF16)
    ki = mm(_C_KI, 128)[:, :IDX_DIM]
    ki_o[...] = ki
    kib_o[...] = ki.astype(BF16)
    wi_o[...] = mm(_C_WI, 128) * wi_scale
    qa_o[...] = mm(_C_QA, 512).astype(BF16)
    ka = mm(_C_KA, 512)
    va = mm(_C_VA, 512)
    ka_o[...] = ka.astype(BF16)
    va_o[...] = va.astype(BF16)
    qb_o[...] = mm(_C_QB, 512).astype(BF16)
    qi_o[...] = mm(_C_QI, 512).astype(BF16)

    @pl.when(i % tiles_per_batch == tiles_per_batch - 1)
    def _():
        kat_o[...] = ka
        vat_o[...] = va


def _prep_w_in(w):
    parts, off = [], 0
    for n in IN_SIZES:
        parts.append(w[:, off:off + n])
        off += n
    qa, ka, va, qb, kb, vb, qi, ki, wi = parts
    pad = lambda a, n: jnp.pad(a, ((0, 0), (0, n - a.shape[1])))
    cols = [qa * ATTN_SCALE, ka, va, qb * ATTN_SCALE, kb, vb, qi * IDX_DIM ** -0.5, pad(ki, 128), pad(wi, 128)]
    return jnp.concatenate(cols, axis=1).astype(BF16)


def _in_proj(x2d, w_pad, tm, tiles_per_batch):
    r = x2d.shape[0]
    n_tiles = r // tm
    n_batch = n_tiles // tiles_per_batch
    row = lambda n: pl.BlockSpec((tm, n), lambda i: (i, 0))
    tail = pl.BlockSpec((tm, 512), lambda i: (i // tiles_per_batch, 0))
    sds = jax.ShapeDtypeStruct
    kv_state = pl.BlockSpec((tm, B_KV_HEADS, HEAD_DIM), lambda i: (i, 0, 0))
    kv_shape = sds((r, B_KV_HEADS, HEAD_DIM), F32)
    out_shape = [sds((r, 512), BF16)] * 5 + [kv_shape, kv_shape, sds((r, IDX_DIM), F32),
                                              sds((r, 128), BF16), sds((r, 128), BF16), sds((r, IDX_DIM), BF16),
                                              sds((r, 128), F32),
                                              sds((n_batch * tm, 512), F32), sds((n_batch * tm, 512), F32)]
    out_specs = [row(512)] * 5 + [kv_state, kv_state, row(IDX_DIM), row(128), row(128), row(IDX_DIM), row(128),
                                  tail, tail]
    return pl.pallas_call(
        functools.partial(_in_proj_kernel, tiles_per_batch=tiles_per_batch, wi_scale=IDX_HEADS ** -0.5),
        grid=(n_tiles,),
        in_specs=[pl.BlockSpec((tm, D_MODEL), lambda i: (i, 0)),
                  pl.BlockSpec((D_MODEL, IN_PAD), lambda i: (0, 0))],
        out_specs=out_specs,
        out_shape=out_shape,
        compiler_params=_params("arbitrary"),
        name="in_proj",
    )(x2d, w_pad)


BAND_COLS = BAND_TILES * TILE
BAND_ROLL = BAND_COLS + TILE
BAND_SLAB = 4


def _band_bias_row(table):
    idx = jnp.arange(BAND_ROLL)
    d = jnp.where(idx < TILE, idx, idx - BAND_ROLL)
    rel = (BAND_TILES - 1) * TILE + d
    return table[jnp.clip(rel, -A_MAX_REL, A_MAX_REL) + A_MAX_REL].T.astype(F32)


def _band_kernel(g_ref, q_ref, k0, k1, k2, v0, v1, v2, o_ref, bias_ref, vt_ref, s_ref, *, off, valid_len):
    i = pl.program_id(1)
    kt = i + off
    krefs, vrefs = (k0, k1, k2), (v0, v1, v2)

    @pl.when((pl.program_id(0) == 0) & (i == 0))
    def _():
        c = lax.broadcasted_iota(I32, (BAND_COLS, TILE), 0) // CHUNK
        r = lax.broadcasted_iota(I32, (BAND_COLS, TILE), 1) // CHUNK
        ok = (c >= r) & (c <= r + N_PREV_CHUNKS)
        for h in range(A_HEADS):
            bias_ref[h] = jnp.where(ok, _toeplitz(g_ref[h:h + 1, :], BAND_COLS, BAND_ROLL)[:, :TILE], NEG)
        ones = jnp.ones((HEAD_DIM, TILE), BF16)
        for s in range(BAND_TILES):
            for h in range(A_HEADS):
                vt_ref[s, h, HEAD_DIM:2 * HEAD_DIM, :] = ones

    def put(slot, vref):
        vt = vref[0].astype(F32).T
        for h in range(A_HEADS):
            vt_ref[slot, h, 0:HEAD_DIM, :] = vt[h * HEAD_DIM:(h + 1) * HEAD_DIM].astype(BF16)

    slots = [(kt + 1 + j) % BAND_TILES for j in range(BAND_TILES)]

    @pl.when(i == 0)
    def _():
        for j in range(BAND_TILES - 1):
            put(slots[j], vrefs[j])

    put(slots[BAND_TILES - 1], vrefs[BAND_TILES - 1])

    base = (kt - (BAND_TILES - 1)) * TILE

    n_slab = BAND_SLAB
    slab_w = n_slab * HEAD_DIM
    lane_head = lax.broadcasted_iota(I32, (TILE, slab_w), 1) // HEAD_DIM

    def attend(masked):
        outs, maxes = [], []
        for g in range(A_HEADS // n_slab):
            gs = slice(g * slab_w, (g + 1) * slab_w)
            q_slab = q_ref[0, :, gs].astype(F32)
            q_bd = jnp.concatenate([jnp.where(lane_head == hh, q_slab, 0.0) for hh in range(n_slab)],
                                   axis=0).astype(BF16)
            pm = None
            for j in range(BAND_TILES):
                st = lax.dot_general(krefs[j][0, :, gs], q_bd, _NT, preferred_element_type=F32)
                parts = [st[:, hh * TILE:(hh + 1) * TILE] + bias_ref[g * n_slab + hh, j * TILE:(j + 1) * TILE, :]
                         for hh in range(n_slab)]
                if masked:
                    kpos = base + j * TILE + lax.broadcasted_iota(I32, (TILE, TILE), 0)
                    ok = (kpos >= 0) & (kpos < valid_len)
                    parts = [jnp.where(ok, x, NEG) for x in parts]
                sj = jnp.concatenate(parts, axis=1)
                s_ref[g * BAND_TILES + j] = sj
                m8 = sj.reshape(TILE // 8, 8, n_slab * TILE).max(axis=0)
                pm = m8 if pm is None else jnp.maximum(pm, m8)
            maxes.append(pm.max(axis=0, keepdims=True))
        for g, m in enumerate(maxes):
            p = [jnp.exp(s_ref[g * BAND_TILES + j] - m).astype(BF16) for j in range(BAND_TILES)]
            for hh in range(n_slab):
                acc = None
                for j in range(BAND_TILES):
                    d = jnp.dot(vt_ref[slots[j], g * n_slab + hh], p[j][:, hh * TILE:(hh + 1) * TILE],
                                preferred_element_type=F32)
                    acc = d if acc is None else acc + d
                blk = acc.T
                outs.append(blk[:, 0:HEAD_DIM] / blk[:, HEAD_DIM:HEAD_DIM + 1])
        o_ref[0] = jnp.concatenate(outs, axis=-1).astype(BF16)

    needs_mask = (base < 0) | (base + BAND_COLS > valid_len)

    @pl.when(needs_mask)
    def _():
        attend(True)

    @pl.when(jnp.logical_not(needs_mask))
    def _():
        attend(False)


def _band_attn(q, k, v, g_row, off, valid_len):
    b, tq = q.shape[:2]
    nq = tq // TILE
    qspec = pl.BlockSpec((1, TILE, A_WIDTH), lambda bb, i: (bb, i, 0))
    kspec = lambda d: pl.BlockSpec((1, TILE, A_WIDTH), lambda bb, i: (bb, jnp.maximum(i + off - d, 0), 0))
    return pl.pallas_call(
        functools.partial(_band_kernel, off=off, valid_len=valid_len),
        grid=(b, nq),
        in_specs=[pl.BlockSpec((A_HEADS, BAND_ROLL), lambda bb, i: (0, 0)),
                  qspec, kspec(2), kspec(1), kspec(0), kspec(2), kspec(1), kspec(0)],
        out_specs=pl.BlockSpec((1, TILE, A_WIDTH), lambda bb, i: (bb, i, 0)),
        out_shape=jax.ShapeDtypeStruct((b, tq, A_WIDTH), BF16),
        scratch_shapes=[pltpu.VMEM((A_HEADS, BAND_COLS, TILE), F32),
                        pltpu.VMEM((BAND_TILES, A_HEADS, 2 * HEAD_DIM, TILE), BF16),
                        pltpu.VMEM((A_HEADS // BAND_SLAB * BAND_TILES, TILE, BAND_SLAB * TILE), F32)],
        compiler_params=_params("arbitrary", "arbitrary"),
        name="band_attn",
    )(g_row, q, k, k, k, v, v, v)


DSA_ROLL = 3 * TILE
FAR_STEP = 4


def _t5_bucket(rel):
    half = N_BUCKETS // 2
    max_exact = half // 2
    n = jnp.abs(rel)
    log_ratio = jnp.log(jnp.maximum(n, 1).astype(jnp.float32) / max_exact) / math.log(T5_MAX_DIST / max_exact)
    large = jnp.minimum(max_exact + (log_ratio * (half - max_exact)).astype(jnp.int32), half - 1)
    return jnp.where(rel < 0, half, 0) + jnp.where(n < max_exact, n, large)


def _dsa_bias_row(t5_table):
    idx = jnp.arange(DSA_ROLL)
    d = jnp.where(idx < TILE, idx, idx - DSA_ROLL)
    far = t5_table[_t5_bucket(jnp.full((1,), 2 * TILE + 1, I32))]
    return (t5_table[_t5_bucket(TILE + d)] - far).T.astype(F32)


def _dsa_kernel(g_ref, qi_ref, wi_ref, qb_ref, ki_ref, kb_ref, vb_ref, o_ref,
                sc_ref, scb_ref, keep_ref, vt_ref, qis_ref, qbs_ref, m_ref, acc_ref, bias_ref, s_ref,
                *, off, valid_len, q_valid, n_kt, tq, n_merge):
    i = pl.program_id(1)
    qt = i + off
    q0 = qt * TILE
    nk = qt + 1
    key_rows = lambda j: pl.ds(pl.multiple_of(j * TILE, TILE), TILE)

    @pl.when((pl.program_id(0) == 0) & (i == 0))
    def _():
        for h in range(B_HEADS):
            tile = _toeplitz(g_ref[h:h + 1, :], 2 * TILE, DSA_ROLL)[:, :tq]
            if n_merge:
                ts = tq // n_merge
                stream = lax.broadcasted_iota(I32, (1, tq), 1) // ts
                first = tile
                for b in range(1, n_merge):
                    tile = jnp.where(stream == b, pltpu.roll(first, b * ts, 1), tile)
            bias_ref[h] = tile

    @pl.when(i == 0)
    def _():
        def body(j, c):
            for b in range(max(n_merge, 1)):
                vt = vb_ref[b, key_rows(j), :].astype(F32).T
                for g in range(B_KV_HEADS):
                    vt_ref[g, j, 0:HEAD_DIM, b * TILE:(b + 1) * TILE] = vt[g * HEAD_DIM:(g + 1) * HEAD_DIM].astype(BF16)
            for g in range(B_KV_HEADS):
                vt_ref[g, j, HEAD_DIM:2 * HEAD_DIM, :] = jnp.ones((HEAD_DIM, vt_ref.shape[-1]), BF16)
            return c

        lax.fori_loop(0, n_kt, body, 0)

    colq = lax.broadcasted_iota(I32, (1, tq), 1)
    rowk = lax.broadcasted_iota(I32, (TILE, 1), 0)

    if n_merge:
        width = n_merge * HEAD_DIM
        own = (lax.broadcasted_iota(I32, (tq, width), 0) // (tq // n_merge)
               == lax.broadcasted_iota(I32, (tq, width), 1) // HEAD_DIM)

        def block_diag(x):
            return jnp.where(own, jnp.concatenate([x.astype(F32)] * n_merge, axis=1), 0.0).astype(BF16)

        qi, qb = qi_ref[...], qb_ref[...]
        for h in range(B_HEADS):
            qis_ref[h] = block_diag(qi[:, h * IDX_DIM:(h + 1) * IDX_DIM])
            qbs_ref[h] = block_diag(qb[:, h * HEAD_DIM:(h + 1) * HEAD_DIM])
        side_by_side = lambda ref, j, cols: jnp.concatenate(
            [ref[b, key_rows(j), cols] for b in range(n_merge)], axis=1)
        wi_t = wi_ref[...].T
        lim = jnp.full((1, tq), valid_len, I32)
    else:
        qi = qi_ref[0]
        for h in range(IDX_HEADS):
            qis_ref[h * tq:(h + 1) * tq, :] = qi[:, h * IDX_DIM:(h + 1) * IDX_DIM]
        qb = qb_ref[0]
        for g in range(B_KV_HEADS):
            for hh in range(B_GROUP):
                h = g * B_GROUP + hh
                qbs_ref[g, hh * tq:(hh + 1) * tq, :] = qb[:, h * HEAD_DIM:(h + 1) * HEAD_DIM]
        wi_t = wi_ref[0].T
        lim = jnp.minimum(q0 + (colq // CHUNK + 1) * CHUNK, valid_len)

    def score_tile(j, masked):
        if n_merge:
            kt = side_by_side(ki_ref, j, slice(None))
            lg = jnp.concatenate([lax.dot_general(kt, qis_ref[h], _NT, preferred_element_type=F32)
                                  for h in range(IDX_HEADS)], axis=1)
        else:
            kt = ki_ref[0, key_rows(j), :]
            lg = lax.dot_general(kt, qis_ref[...], _NT, preferred_element_type=F32)
        sc = wi_t[0:1, :] * jnp.maximum(lg[:, 0:tq], 0.0)
        for h in range(1, IDX_HEADS):
            sc = sc + wi_t[h:h + 1, :] * jnp.maximum(lg[:, h * tq:(h + 1) * tq], 0.0)
        if masked:
            sc = jnp.where(j * TILE + rowk < lim, sc, -jnp.inf)
        sc_ref[j] = sc
        scb_ref[j] = sc.astype(BF16)
        rows8 = lambda hit: jnp.where(hit, 1, 0).reshape(TILE // 8, 8, tq).sum(axis=0)
        return rows8(sc > 0.0), rows8(sc >= 0.0)

    def score_pair(jj, c, masked):
        pos_a, nn_a = score_tile(2 * jj, masked)
        pos_b, nn_b = score_tile(2 * jj + 1, masked)
        return c[0] + pos_a + pos_b, c[1] + nn_a + nn_b

    def score_quad(jj, c):
        for t in range(4):
            pos, nn = score_tile(4 * jj + t, False)
            c = (c[0] + pos, c[1] + nn)
        return c

    n_open_quads = (nk - 1) // 4
    counts = lax.fori_loop(0, n_open_quads, score_quad, (jnp.zeros((8, tq), I32), jnp.zeros((8, tq), I32)))
    n_pos, n_nonneg = lax.fori_loop(2 * n_open_quads, (nk + 1) // 2, functools.partial(score_pair, masked=True),
                                    counts)

    def f32_of_key(k):
        return pltpu.bitcast(jnp.where(k < 0, k ^ 0x7FFFFFFF, k), F32)

    def bf16_of_key(k):
        bits = jnp.where(k < 0, k ^ 0x7FFF, k) & 0xFFFF
        return pltpu.bitcast(lax.shift_left(bits, 16), F32).astype(BF16)

    def count_bf16(cand):
        def body(jj, acc):
            for j in (2 * jj, 2 * jj + 1):
                ge = jnp.where(scb_ref[j] >= cand, jnp.int16(1), jnp.int16(0)).reshape(TILE // 16, 16, tq)
                part = ge[0]
                for r in range(1, TILE // 16):
                    part = part + ge[r]
                acc = acc + part
            return acc

        acc = lax.fori_loop(0, (nk + 1) // 2, body, jnp.zeros((16, tq), I16))
        return acc.astype(I32).sum(axis=0, keepdims=True)

    def count_f32(cand, strict=False):
        def body(j, acc):
            blk = sc_ref[j]
            hit = (blk > cand) if strict else (blk >= cand)
            return acc + jnp.where(hit, 1, 0).reshape(TILE // 8, 8, tq).sum(axis=0)

        return lax.fori_loop(0, nk, body, jnp.zeros((8, tq), I32)).sum(axis=0, keepdims=True)

    c_pos = n_pos.sum(axis=0, keepdims=True)
    zero_tie = (c_pos < TOPK_MAX) & (n_nonneg.sum(axis=0, keepdims=True) >= TOPK_MAX)
    skip1 = (colq >= q_valid) | zero_tie

    def level1(it, t):
        cand = t + lax.shift_left(jnp.int32(1), 15 - it)
        return jnp.where(count_bf16(bf16_of_key(cand)) >= TOPK_MAX, cand, t)

    t1 = lax.fori_loop(0, 16, level1, jnp.full((1, tq), MIN16, I32))
    settled1 = skip1 | (t1 <= HI_NEG_INF)

    def level2(st):
        lo, hi, thr_key, done = st
        live = (done == 0) & (hi - lo > 1)
        mid = lo + lax.shift_right_arithmetic(hi - lo, 1)
        c = count_f32(f32_of_key(mid))
        hit = live & (c == TOPK_MAX)
        return (jnp.where(live & (c >= TOPK_MAX), mid, lo), jnp.where(live & (c < TOPK_MAX), mid, hi),
                jnp.where(hit, mid, thr_key), jnp.where(hit, 1, done))

    def n_live(st):
        lo, hi, _, done = st
        return jnp.sum(jnp.where((done == 0) & (hi - lo > 1), 1, 0))

    def level2_pair(carry):
        st = level2(level2(carry[0]))
        return st, n_live(st)

    def key32_of_key16(k):
        return lax.shift_left(k, 16) | jnp.where(k < 0, 0xFFFF, 0)

    key_t1 = key32_of_key16(t1)
    st0 = (key_t1 - 0x8000, key32_of_key16(t1 + 1), key_t1, jnp.where(settled1, 1, 0))
    (lo, _, thr_key, done2), _ = lax.while_loop(lambda carry: carry[1] > 0, level2_pair, (st0, n_live(st0)))
    open2 = done2 == 0
    thr = jnp.where(open2, f32_of_key(lo), f32_of_key(thr_key))
    thr = jnp.where(zero_tie, 0.0, thr)
    thr = jnp.where(settled1 & jnp.logical_not(zero_tie), -jnp.inf, thr)
    thr = jnp.maximum(thr, float(jnp.finfo(F32).min))

    keep_ref[...] = jnp.where(zero_tie, TOPK_MAX - c_pos, 2 ** 30).astype(F32)

    @pl.when(jnp.sum(jnp.where(open2, 1, 0)) > 0)
    def _():
        above = count_f32(thr, strict=True)
        keep_ref[...] = jnp.where(open2, (TOPK_MAX - above).astype(F32), keep_ref[...])

    @pl.when(jnp.sum(jnp.where(open2 | zero_tie, 1, 0)) > 0)
    def _():
        keep = keep_ref[...]
        lower = (lax.broadcasted_iota(I32, (TILE, TILE), 0) > lax.broadcasted_iota(I32, (TILE, TILE), 1))
        lower = jnp.where(lower, 1.0, 0.0).astype(BF16)

        def body(jj, run):
            for j in (2 * jj, 2 * jj + 1):
                blk = sc_ref[j]
                eq = blk == thr
                eq_f = jnp.where(eq, 1.0, 0.0)
                before = jnp.dot(lower, eq_f.astype(BF16), preferred_element_type=F32)
                sc_ref[j] = jnp.where(eq & (run + before >= keep), -jnp.inf, blk)
                run = run + eq_f.reshape(TILE // 8, 8, tq).sum(axis=0).sum(axis=0, keepdims=True)
            return run

        lax.fori_loop(0, (nk + 1) // 2, body, jnp.zeros((1, tq), F32))

    for g in range(B_KV_HEADS):
        m_ref[g] = jnp.full((1, B_GROUP * tq), NEG, F32)
        acc_ref[g] = jnp.zeros((2 * HEAD_DIM, B_GROUP * tq), F32)

    def attend(tiles):
        sels = [sc_ref[j] >= thr for j, _ in tiles]
        part_max = []
        for g in range(B_KV_HEADS):
            pm = None
            for t, ((j, near), sel) in enumerate(zip(tiles, sels)):
                if n_merge:
                    kt = side_by_side(kb_ref, j, slice(g * HEAD_DIM, (g + 1) * HEAD_DIM))
                    st = jnp.concatenate([lax.dot_general(kt, qbs_ref[g * B_GROUP + hh], _NT,
                                                          preferred_element_type=F32)
                                          for hh in range(B_GROUP)], axis=1)
                else:
                    kt = kb_ref[0, key_rows(j), g * HEAD_DIM:(g + 1) * HEAD_DIM]
                    st = lax.dot_general(kt, qbs_ref[g], _NT, preferred_element_type=F32)
                parts = []
                for hh in range(B_GROUP):
                    s_h = st[:, hh * tq:(hh + 1) * tq]
                    if near is not None:
                        s_h = s_h + bias_ref[g * B_GROUP + hh, near * TILE:(near + 1) * TILE, :]
                    parts.append(jnp.where(sel, s_h, -jnp.inf))
                s = jnp.concatenate(parts, axis=1)
                s_ref[g * len(tiles) + t] = s
                m8 = s.reshape(TILE // 8, 8, B_GROUP * tq).max(axis=0)
                pm = m8 if pm is None else jnp.maximum(pm, m8)
            part_max.append(pm)
        for g in range(B_KV_HEADS):
            m_old = m_ref[g]
            m_new = jnp.maximum(m_old, part_max[g].max(axis=0, keepdims=True))
            pv = None
            for t, (j, _) in enumerate(tiles):
                p = jnp.exp(s_ref[g * len(tiles) + t] - m_new)
                if n_merge:
                    lane_stream = colq // (tq // n_merge)
                    d = jnp.concatenate(
                        [jnp.dot(vt_ref[g, j],
                                 jnp.concatenate([jnp.where(lane_stream == b, p[:, hh * tq:(hh + 1) * tq], 0.0)
                                                  for b in range(n_merge)], axis=0).astype(BF16),
                                 preferred_element_type=F32) for hh in range(B_GROUP)], axis=1)
                else:
                    d = jnp.dot(vt_ref[g, j], p.astype(BF16), preferred_element_type=F32)
                pv = d if pv is None else pv + d
            acc_ref[g] = jnp.exp(m_old - m_new) * acc_ref[g] + pv
            m_ref[g] = m_new

    n_far = jnp.maximum(nk - 2, 0)

    def far_step(jj, c):
        attend([(FAR_STEP * jj + t, None) for t in range(FAR_STEP)])
        return c

    lax.fori_loop(0, n_far // FAR_STEP, far_step, 0)
    rem = n_far % FAR_STEP

    for r in range(FAR_STEP):
        @pl.when((rem == r) & (nk >= 2))
        def _(r=r):
            attend([(n_far - r + t, None) for t in range(r)] + [(nk - 2, 0), (nk - 1, 1)])

    @pl.when(nk < 2)
    def _():
        attend([(nk - 1, 1)])

    outs = []
    for g in range(B_KV_HEADS):
        for hh in range(B_GROUP):
            blk = acc_ref[g, :, hh * tq:(hh + 1) * tq].T
            outs.append(blk[:, 0:HEAD_DIM] / blk[:, HEAD_DIM:HEAD_DIM + 1])
    o_ref[0] = jnp.concatenate(outs, axis=-1).astype(BF16)


def _dsa_attn(qi, wi, qb, ki, kb, vb, g_row, off, valid_len, q_valid, tq):
    b, t_q = qi.shape[:2]
    tk = ki.shape[1]
    nq, n_kt = t_q // tq, tk // TILE
    assert tq == TILE or nq == 1
    assert n_kt % 2 == 0 and tk == n_kt * TILE
    qspec = lambda n: pl.BlockSpec((1, tq, n), lambda bb, i: (bb, i, 0))
    kspec = lambda n: pl.BlockSpec((1, tk, n), lambda bb, i: (bb, 0, 0))
    in_specs = [pl.BlockSpec((B_HEADS, DSA_ROLL), lambda bb, i: (0, 0)),
                qspec(512), qspec(128), qspec(512), kspec(IDX_DIM), kspec(128), kspec(128)]
    return _dsa_call((g_row, qi, wi, qb, ki, kb, vb), in_specs, (b, nq), t_q, tq, n_kt,
                     dict(off=off, valid_len=valid_len, q_valid=q_valid, n_merge=0))


def _dsa_attn_merged(qi, wi, qb, ki, kb, vb, g_row, off, valid_len):
    s, ts = qi.shape[:2]
    tk = ki.shape[1]
    tq, n_kt = s * ts, tk // TILE
    assert tq % LANES == 0 and ts <= CHUNK and n_kt % 2 == 0 and tk == n_kt * TILE
    flat = lambda a: a.reshape(tq, a.shape[-1])
    operands = (g_row, flat(qi), flat(wi), flat(qb), ki, kb, vb)
    whole = lambda a: pl.BlockSpec(a.shape, lambda bb, i: (0,) * a.ndim, pipeline_mode=pl.Buffered(1))
    out = _dsa_call(operands, [whole(a) for a in operands], (1, 1), tq, tq, n_kt,
                    dict(off=off, valid_len=valid_len, q_valid=tq, n_merge=s))
    return out.reshape(s, ts, B_WIDTH)


def _dsa_call(operands, in_specs, grid, t_q, tq, n_kt, static):
    n_merge = static["n_merge"]
    assert static["valid_len"] >= (static["off"] + grid[1] - 1) * TILE
    if n_merge:
        q_scratch = [pltpu.VMEM((IDX_HEADS, tq, n_merge * IDX_DIM), BF16),
                     pltpu.VMEM((B_HEADS, tq, n_merge * HEAD_DIM), BF16)]
    else:
        q_scratch = [pltpu.VMEM((IDX_HEADS * tq, IDX_DIM), BF16),
                     pltpu.VMEM((B_KV_HEADS, B_GROUP * tq, HEAD_DIM), BF16)]
    return pl.pallas_call(
        functools.partial(_dsa_kernel, n_kt=n_kt, tq=tq, **static),
        grid=grid,
        in_specs=in_specs,
        out_specs=pl.BlockSpec((1, tq, B_WIDTH), lambda bb, i: (bb, i, 0)),
        out_shape=jax.ShapeDtypeStruct((grid[0], t_q, B_WIDTH), BF16),
        scratch_shapes=[pltpu.VMEM((n_kt, TILE, tq), F32),
                        pltpu.VMEM((n_kt, TILE, tq), BF16),
                        pltpu.VMEM((1, tq), F32),
                        pltpu.VMEM((B_KV_HEADS, n_kt, 2 * HEAD_DIM, max(n_merge, 1) * TILE), BF16),
                        *q_scratch,
                        pltpu.VMEM((B_KV_HEADS, 1, B_GROUP * tq), F32),
                        pltpu.VMEM((B_KV_HEADS, 2 * HEAD_DIM, B_GROUP * tq), F32),
                        pltpu.VMEM((B_HEADS, 2 * TILE, tq), F32),
                        pltpu.VMEM((B_KV_HEADS * (FAR_STEP + 1), TILE, B_GROUP * tq), F32)],
        compiler_params=_params("arbitrary", "arbitrary"),
        name="dsa_attn",
    )(*operands)


def _mem_kv_kernel(m_ref, wk_ref, wv_ref, k_o, v_o, kb_o, vb_o):
    mb = m_ref[...].astype(BF16)
    k = jnp.dot(mb, wk_ref[...], preferred_element_type=F32)
    v = jnp.dot(mb, wv_ref[...], preferred_element_type=F32)
    k_o[...] = k
    v_o[...] = v
    kb_o[...] = k.astype(BF16)
    vb_o[...] = v.astype(BF16)


def _mem_kv(mem2d, wk, wv):
    r = mem2d.shape[0]
    tm = MEM_LEN
    row = lambda n: pl.BlockSpec((tm, n), lambda i: (i, 0))
    const = lambda s: pl.BlockSpec(s, lambda i: (0, 0))
    sds = jax.ShapeDtypeStruct
    return pl.pallas_call(
        _mem_kv_kernel,
        grid=(r // tm,),
        in_specs=[row(D_MODEL), const((D_MODEL, MEM_WIDTH)), const((D_MODEL, MEM_WIDTH))],
        out_specs=[row(MEM_WIDTH)] * 4,
        out_shape=[sds((r, MEM_WIDTH), F32), sds((r, MEM_WIDTH), F32),
                   sds((r, MEM_WIDTH), BF16), sds((r, MEM_WIDTH), BF16)],
        compiler_params=_params("arbitrary"),
        name="mem_kv",
    )(mem2d, wk, wv)


FF_CHUNK = 256


STAGE_ELEMS = 128 * 1024
STAGE_SLOTS = 8
N_MATS = 5


def _stage_rows(cols):
    return 1 << ((STAGE_ELEMS // cols).bit_length() - 1)


def _load_cast(src, dst, stage, sem):
    slots, rows = stage.shape[:2]
    n = src.shape[0] // rows
    copy = lambda k: pltpu.make_async_copy(src.at[pl.ds(k * rows, rows)], stage.at[k % slots], sem.at[k % slots])
    for k in range(min(slots - 1, n)):
        copy(k).start()
    for k in range(n):
        if k + slots - 1 < n:
            copy(k + slots - 1).start()
        copy(k).wait()
        dst[k * rows:(k + 1) * rows, :] = stage[k % slots].astype(BF16)


def _tail_kernel(x_ref, oa_ref, ob_ref, mk_ref, mv_ref, hist_ref,
                 wo_ref, g1_ref, b1_ref, wq_ref, wmo_ref, g2_ref, b2_ref,
                 wu_ref, wc_ref, bc_ref, wd_ref, g3_ref, b3_ref,
                 o_ref, tail_ref, *rest, tiles_per_batch, seg, cast_weights):
    i = pl.program_id(0)
    tm = x_ref.shape[0]
    nseg = tm // seg
    if cast_weights:
        mats_out, (carry_ref, act_ref), rest = rest[:N_MATS], rest[N_MATS:N_MATS + 2], rest[N_MATS + 2:]
        mats, stages, (sem_in, sem_out) = rest[:N_MATS], rest[N_MATS:-2], rest[-2:]
        stage_of = {s.shape[2]: s for s in stages}

        @pl.when(i == 0)
        def _():
            for j, src in enumerate((wo_ref, wq_ref, wmo_ref, wu_ref, wd_ref)):
                _load_cast(src, mats[j], stage_of[src.shape[1]], sem_in)
                pltpu.make_async_copy(mats[j], mats_out[j], sem_out.at[j]).start()

        wo_ref, wq_ref, wmo_ref, wu_ref, wd_ref = mats
    else:
        carry_ref, act_ref = rest

    mix = jnp.concatenate([oa_ref[...], ob_ref[...]], axis=-1)
    h = _layer_norm(ALPHA * x_ref[...] + jnp.dot(mix, wo_ref[...], preferred_element_type=F32),
                    g1_ref[...], b1_ref[...])

    q = jnp.dot(h.astype(BF16), wq_ref[...], preferred_element_type=F32).astype(BF16)
    segs = []
    for s in range(nseg):
        qs = q[s * seg:(s + 1) * seg]
        heads = []
        for hd in range(MEM_HEADS):
            sl = slice(hd * MEM_HEAD_DIM, (hd + 1) * MEM_HEAD_DIM)
            sc = lax.dot_general(qs[:, sl], mk_ref[s, :, sl], _NT, preferred_element_type=F32) * MEM_HEAD_DIM ** -0.5
            p = jnp.exp(sc - sc.max(-1, keepdims=True))
            l = p.sum(-1, keepdims=True)
            heads.append(jnp.dot(p.astype(BF16), mv_ref[s, :, sl], preferred_element_type=F32) / l)
        segs.append(jnp.concatenate(heads, axis=-1))
    att = jnp.concatenate(segs, axis=0).astype(BF16)
    h = _layer_norm(ALPHA * h + jnp.dot(att, wmo_ref[...], preferred_element_type=F32), g2_ref[...], b2_ref[...])

    hb = h.astype(BF16)
    row = lax.broadcasted_iota(I32, (tm, 1), 0)
    first = (i % tiles_per_batch) == 0
    for c in range(D_FF // FF_CHUNK):
        cs = slice(c * FF_CHUNK, (c + 1) * FF_CHUNK)
        u = jnp.dot(hb, wu_ref[:, cs], preferred_element_type=F32)
        gt = jnp.dot(hb, wu_ref[:, D_FF + c * FF_CHUNK:D_FF + (c + 1) * FF_CHUNK], preferred_element_type=F32)
        p1 = pltpu.roll(gt, 1, 0)
        p2 = pltpu.roll(gt, 2, 0)
        for s in range(nseg):
            hist = hist_ref[s, :, cs]
            if tiles_per_batch > 1:
                hist = jnp.where(first, hist, carry_ref[:, cs])
            p1 = jnp.where(row == s * seg, hist[7:8, :], p1)
            p2 = jnp.where(row == s * seg, hist[6:7, :], p2)
            p2 = jnp.where(row == s * seg + 1, hist[7:8, :], p2)
        gc = bc_ref[:, cs] + ((wc_ref[0:1, cs] * p2 + wc_ref[1:2, cs] * p1) + wc_ref[2:3, cs] * gt)
        act_ref[:, cs] = (u * jax.nn.gelu(gc)).astype(BF16)
        for s in range(nseg):
            tail_ref[s, :, cs] = gt[(s + 1) * seg - 8:(s + 1) * seg, :]
        carry_ref[:, cs] = gt[tm - 8:tm, :]
    f = jnp.dot(act_ref[...], wd_ref[...], preferred_element_type=F32)
    o_ref[...] = _layer_norm(ALPHA * h + f, g3_ref[...], b3_ref[...])

    if cast_weights:
        @pl.when(i == 0)
        def _():
            for j in range(N_MATS):
                pltpu.make_async_copy(mats[j], mats_out[j], sem_out.at[j]).wait()


def _layer_tail(x2d, oa, ob, mk, mv, hist, weights, tm, tiles_per_batch, seg):
    r = x2d.shape[0]
    nseg = tm // seg
    n_stream = r // (tm * tiles_per_batch) * nseg
    mats = [w for w in weights if w.shape[0] > CONV_W]
    cast_weights = mats[0].dtype == F32
    assert len(mats) == N_MATS and all((w.dtype == F32) == cast_weights for w in mats)
    row = lambda n: pl.BlockSpec((tm, n), lambda i: (i, 0))
    per_stream = lambda a, b: pl.BlockSpec((nseg, a, b), lambda i: (i // tiles_per_batch, 0, 0))
    const = lambda a: pl.BlockSpec(a.shape, lambda i: (0,) * a.ndim, pipeline_mode=pl.Buffered(1))
    in_hbm = pl.BlockSpec(memory_space=pl.ANY)
    w_spec = lambda w: in_hbm if cast_weights and w.shape[0] > CONV_W else const(w)
    out_specs = [row(D_MODEL), per_stream(8, D_FF)]
    out_shape = [jax.ShapeDtypeStruct((r, D_MODEL), F32), jax.ShapeDtypeStruct((n_stream, 8, D_FF), F32)]
    scratch = [pltpu.VMEM((8, D_FF), F32), pltpu.VMEM((tm, D_FF), BF16)]
    if cast_weights:
        out_specs += [in_hbm] * N_MATS
        out_shape += [jax.ShapeDtypeStruct(w.shape, BF16) for w in mats]
        scratch += [pltpu.VMEM(w.shape, BF16) for w in mats]
        scratch += [pltpu.VMEM((STAGE_SLOTS, _stage_rows(c), c), F32) for c in sorted({w.shape[1] for w in mats})]
        scratch += [pltpu.SemaphoreType.DMA((STAGE_SLOTS,)), pltpu.SemaphoreType.DMA((N_MATS,))]
    return pl.pallas_call(
        functools.partial(_tail_kernel, tiles_per_batch=tiles_per_batch, seg=seg, cast_weights=cast_weights),
        grid=(r // tm,),
        in_specs=[row(D_MODEL), row(A_WIDTH), row(B_WIDTH), per_stream(MEM_LEN, MEM_WIDTH),
                  per_stream(MEM_LEN, MEM_WIDTH), per_stream(8, D_FF)] + [w_spec(w) for w in weights],
        out_specs=out_specs,
        out_shape=out_shape,
        scratch_shapes=scratch,
        compiler_params=_params("arbitrary"),
        name="layer_tail",
    )(x2d, oa, ob, mk, mv, hist, *weights)


def _pad_rows(a, n):
    return jnp.pad(a, ((0, 0), (0, n - a.shape[1])) + ((0, 0),) * (a.ndim - 2))


def _hist8(g_hist):
    return jnp.pad(g_hist, ((0, 0), (8 - g_hist.shape[1], 0), (0, 0)))


def kernel(x_prompt, x_sample, cache_a_k, cache_a_v, cache_b_k, cache_b_v, cache_b_kidx, cache_mem_k, cache_mem_v, state_ffn_conv, mem_prompt, w_in, a_rel_bias, t5_bias, w_o, ln1_g, ln1_b, w_mq, w_mk, w_mv, w_mo, ln2_g, ln2_b, w_up, w_conv, b_conv, w_down, ln3_g, ln3_b):
    bp, tp = x_prompt.shape[:2]
    bs, ts = x_sample.shape[:2]
    l = 0
    vec = lambda a: a[l].reshape(1, -1)
    w_in_p = _prep_w_in(w_in[l])
    w_mk_b, w_mv_b = w_mk[l].astype(BF16), w_mv[l].astype(BF16)
    band_row = _band_bias_row(a_rel_bias[l])
    dsa_row = _dsa_bias_row(t5_bias)
    tail_weights = lambda wo, wq, wmo, wu, wd: (wo, vec(ln1_g), vec(ln1_b), wq, wmo, vec(ln2_g), vec(ln2_b),
                                                wu, w_conv[l], vec(b_conv), wd, vec(ln3_g), vec(ln3_b))

    tm = 512
    a_keep = min(N_PREV_CHUNKS * CHUNK, tp)
    (qa, ka, va, qb, qi, kb, vb, ki, kb_b, vb_b, ki_b, wi, ka_tail, va_tail) = _in_proj(
        x_prompt.reshape(bp * tp, D_MODEL), w_in_p, tm, tp // tm)
    r3 = lambda a: a.reshape(bp, tp, a.shape[-1])
    oa = _band_attn(r3(qa), r3(ka), r3(va), band_row, 0, tp)
    ob = _dsa_attn(r3(qi), r3(wi), r3(qb), r3(ki_b), r3(kb_b), r3(vb_b), dsa_row, 0, tp, TILE, TILE)
    mk, mv, mk_b, mv_b = _mem_kv(mem_prompt.reshape(bp * MEM_LEN, D_MODEL), w_mk_b, w_mv_b)
    xp, p_tail, *mats_b = _layer_tail(x_prompt.reshape(bp * tp, D_MODEL), oa.reshape(bp * tp, A_WIDTH),
                                      ob.reshape(bp * tp, B_WIDTH), mk_b.reshape(bp, MEM_LEN, MEM_WIDTH),
                                      mv_b.reshape(bp, MEM_LEN, MEM_WIDTH), jnp.zeros((bp, 8, D_FF), F32),
                                      tail_weights(w_o[l], w_mq[l], w_mo[l], w_up[l], w_down[l]), tm, tp // tm, tm)
    prompt_state = (
        ka_tail.reshape(bp, tm, A_HEADS, HEAD_DIM)[:, tm - a_keep:][None],
        va_tail.reshape(bp, tm, A_HEADS, HEAD_DIM)[:, tm - a_keep:][None],
        kb.reshape(1, bp, tp, B_KV_HEADS, HEAD_DIM), vb.reshape(1, bp, tp, B_KV_HEADS, HEAD_DIM),
        ki.reshape(1, bp, tp, IDX_DIM),
        mk.reshape(1, bp, MEM_LEN, MEM_HEADS, MEM_HEAD_DIM), mv.reshape(1, bp, MEM_LEN, MEM_HEADS, MEM_HEAD_DIM),
        p_tail[:, 8 - (CONV_W - 1):][None])

    rs = bs * ts
    (qa, ka, va, qb, qi, kb, vb, ki, kb_b, vb_b, ki_b, wi, ka_new, va_new) = _in_proj(
        x_sample.reshape(rs, D_MODEL), w_in_p, rs, 1)
    s3 = lambda a: a.reshape(bs, ts, -1)
    qpad = lambda a, n: _pad_rows(s3(a), n)

    past_a = cache_a_k.shape[2]
    n_a = past_a + ts
    t_a = -(-n_a // TILE) * TILE
    seq_a = lambda cache, new: _pad_rows(
        jnp.concatenate([cache[l].reshape(bs, past_a, A_WIDTH), s3(new)], axis=1), t_a).astype(BF16)
    oa = _band_attn(qpad(qa, TILE), seq_a(cache_a_k, ka_new), seq_a(cache_a_v, va_new), band_row,
                    past_a // TILE, n_a)

    past_b = cache_b_k.shape[2]
    n_b = past_b + ts
    t_b = -(-n_b // (2 * TILE)) * 2 * TILE
    seq_b = lambda cache, new: _pad_rows(
        jnp.concatenate([cache[l].reshape(bs, past_b, -1), s3(new)], axis=1), t_b).astype(BF16)
    ob = _dsa_attn_merged(s3(qi), s3(wi), s3(qb), seq_b(cache_b_kidx, ki), seq_b(cache_b_k, kb),
                          seq_b(cache_b_v, vb), dsa_row, past_b // TILE, n_b)

    xs, s_tail = _layer_tail(x_sample.reshape(rs, D_MODEL), oa[:, :ts].reshape(rs, A_WIDTH),
                             ob[:, :ts].reshape(rs, B_WIDTH),
                             cache_mem_k[l].reshape(bs, MEM_LEN, MEM_WIDTH).astype(BF16),
                             cache_mem_v[l].reshape(bs, MEM_LEN, MEM_WIDTH).astype(BF16),
                             _hist8(state_ffn_conv[l]), tail_weights(*mats_b), rs, 1, ts)
    sample_state = (
        ka_new.reshape(1, bs, ts, A_HEADS, HEAD_DIM), va_new.reshape(1, bs, ts, A_HEADS, HEAD_DIM),
        kb.reshape(1, bs, ts, B_KV_HEADS, HEAD_DIM), vb.reshape(1, bs, ts, B_KV_HEADS, HEAD_DIM),
        ki.reshape(1, bs, ts, IDX_DIM), s_tail[:, 8 - (CONV_W - 1):][None])

    return (xp.reshape(bp, tp, D_MODEL), xs.reshape(bs, ts, D_MODEL)) + prompt_state + sample_state
```

```python
import functools
import math

import jax
import jax.numpy as jnp
from jax import lax
from jax.experimental import pallas as pl
from jax.experimental.pallas import tpu as pltpu

F32 = jnp.float32
BF16 = jnp.bfloat16
I32 = jnp.int32
I16 = jnp.int16

D_MODEL = 1024
CHUNK = 64
N_PREV_CHUNKS = 8
HEAD_DIM = 64
A_HEADS = 8
A_WIDTH = A_HEADS * HEAD_DIM
A_MAX_REL = 64
B_HEADS = 8
B_KV_HEADS = 2
B_GROUP = B_HEADS // B_KV_HEADS
B_WIDTH = B_HEADS * HEAD_DIM
B_KV_WIDTH = B_KV_HEADS * HEAD_DIM
IDX_HEADS = 8
IDX_DIM = 64
TOPK_MAX = 256
N_BUCKETS = 32
T5_MAX_DIST = 128
MEM_LEN = 256
MEM_HEADS = 4
MEM_HEAD_DIM = 128
MEM_WIDTH = MEM_HEADS * MEM_HEAD_DIM
D_FF = 2816
CONV_W = 3
IN_SIZES = (A_WIDTH, A_WIDTH, A_WIDTH, B_WIDTH, B_KV_WIDTH, B_KV_WIDTH, IDX_HEADS * IDX_DIM, IDX_DIM, IDX_HEADS)
DEPTH = 1
ALPHA = (2 * DEPTH) ** 0.25
LN_EPS = 1e-5
ATTN_SCALE = HEAD_DIM ** -0.5
NEG = -1e30

LANES = 128
TILE = 256
BAND_TILES = 1 + (N_PREV_CHUNKS * CHUNK) // TILE
VMEM_LIMIT = 56 * 1024 * 1024

_C_QA, _C_KA, _C_VA, _C_QB = 0, 512, 1024, 1536
_C_KB, _C_VB, _C_QI, _C_KI, _C_WI = 2048, 2176, 2304, 2816, 2944
IN_PAD = 3072

MIN16 = -32768
HI_NEG_INF = -32641

_NT = (((1,), (1,)), ((), ()))


def _params(*sem):
    return pltpu.CompilerParams(dimension_semantics=sem, vmem_limit_bytes=VMEM_LIMIT)


def _layer_norm(z, g, b):
    mu = jnp.mean(z, axis=-1, keepdims=True)
    d = z - mu
    var = jnp.mean(d * d, axis=-1, keepdims=True)
    return d * lax.rsqrt(var + LN_EPS) * g + b


def _toeplitz(g_row, rows, width):
    return pltpu.roll(jnp.broadcast_to(g_row, (rows, width)), 0, 1, stride=1, stride_axis=0)


def _in_proj_kernel(x_ref, w_ref, qa_o, ka_o, va_o, qb_o, qi_o, kb_o, vb_o, ki_o, kbb_o, vbb_o, kib_o,
                    wi_o, kat_o, vat_o, *, tiles_per_batch, wi_scale):
    i = pl.program_id(0)
    xb = x_ref[...].astype(BF16)

    def mm(c0, n):
        return jnp.dot(xb, w_ref[:, c0:c0 + n], preferred_element_type=F32)

    kb = mm(_C_KB, 128)
    vb = mm(_C_VB, 128)
    for g in range(B_KV_HEADS):
        kb_o[:, g, :] = kb[:, g * HEAD_DIM:(g + 1) * HEAD_DIM]
        vb_o[:, g, :] = vb[:, g * HEAD_DIM:(g + 1) * HEAD_DIM]
    kbb_o[...] = kb.astype(BF16)
    vbb_o[...] = vb.astype(BF16)
    ki = mm(_C_KI, 128)[:, :IDX_DIM]
    ki_o[...] = ki
    kib_o[...] = ki.astype(BF16)
    wi_o[...] = mm(_C_WI, 128) * wi_scale
    qa_o[...] = mm(_C_QA, 512).astype(BF16)
    ka = mm(_C_KA, 512)
    va = mm(_C_VA, 512)
    ka_o[...] = ka.astype(BF16)
    va_o[...] = va.astype(BF16)
    qb_o[...] = mm(_C_QB, 512).astype(BF16)
    qi_o[...] = mm(_C_QI, 512).astype(BF16)

    @pl.when(i % tiles_per_batch == tiles_per_batch - 1)
    def _():
        kat_o[...] = ka
        vat_o[...] = va


def _prep_w_in(w):
    parts, off = [], 0
    for n in IN_SIZES:
        parts.append(w[:, off:off + n])
        off += n
    qa, ka, va, qb, kb, vb, qi, ki, wi = parts
    pad = lambda a, n: jnp.pad(a, ((0, 0), (0, n - a.shape[1])))
    cols = [qa * ATTN_SCALE, ka, va, qb * ATTN_SCALE, kb, vb, qi * IDX_DIM ** -0.5, pad(ki, 128), pad(wi, 128)]
    return jnp.concatenate(cols, axis=1).astype(BF16)


def _in_proj(x2d, w_pad, tm, tiles_per_batch):
    r = x2d.shape[0]
    n_tiles = r // tm
    n_batch = n_tiles // tiles_per_batch
    row = lambda n: pl.BlockSpec((tm, n), lambda i: (i, 0))
    tail = pl.BlockSpec((tm, 512), lambda i: (i // tiles_per_batch, 0))
    sds = jax.ShapeDtypeStruct
    kv_state = pl.BlockSpec((tm, B_KV_HEADS, HEAD_DIM), lambda i: (i, 0, 0))
    kv_shape = sds((r, B_KV_HEADS, HEAD_DIM), F32)
    out_shape = [sds((r, 512), BF16)] * 5 + [kv_shape, kv_shape, sds((r, IDX_DIM), F32),
                                              sds((r, 128), BF16), sds((r, 128), BF16), sds((r, IDX_DIM), BF16),
                                              sds((r, 128), F32),
                                              sds((n_batch * tm, 512), F32), sds((n_batch * tm, 512), F32)]
    out_specs = [row(512)] * 5 + [kv_state, kv_state, row(IDX_DIM), row(128), row(128), row(IDX_DIM), row(128),
                                  tail, tail]
    return pl.pallas_call(
        functools.partial(_in_proj_kernel, tiles_per_batch=tiles_per_batch, wi_scale=IDX_HEADS ** -0.5),
        grid=(n_tiles,),
        in_specs=[pl.BlockSpec((tm, D_MODEL), lambda i: (i, 0)),
                  pl.BlockSpec((D_MODEL, IN_PAD), lambda i: (0, 0))],
        out_specs=out_specs,
        out_shape=out_shape,
        compiler_params=_params("arbitrary"),
        name="in_proj",
    )(x2d, w_pad)


BAND_COLS = BAND_TILES * TILE
BAND_ROLL = BAND_COLS + TILE
BAND_SLAB = 4


def _band_bias_row(table):
    idx = jnp.arange(BAND_ROLL)
    d = jnp.where(idx < TILE, idx, idx - BAND_ROLL)
    rel = (BAND_TILES - 1) * TILE + d
    return table[jnp.clip(rel, -A_MAX_REL, A_MAX_REL) + A_MAX_REL].T.astype(F32)


def _band_kernel(g_ref, q_ref, k0, k1, k2, v0, v1, v2, o_ref, bias_ref, vt_ref, s_ref, *, off, valid_len):
    i = pl.program_id(1)
    kt = i + off
    krefs, vrefs = (k0, k1, k2), (v0, v1, v2)

    @pl.when((pl.program_id(0) == 0) & (i == 0))
    def _():
        c = lax.broadcasted_iota(I32, (BAND_COLS, TILE), 0) // CHUNK
        r = lax.broadcasted_iota(I32, (BAND_COLS, TILE), 1) // CHUNK
        ok = (c >= r) & (c <= r + N_PREV_CHUNKS)
        for h in range(A_HEADS):
            bias_ref[h] = jnp.where(ok, _toeplitz(g_ref[h:h + 1, :], BAND_COLS, BAND_ROLL)[:, :TILE], NEG)
        ones = jnp.ones((HEAD_DIM, TILE), BF16)
        for s in range(BAND_TILES):
            for h in range(A_HEADS):
                vt_ref[s, h, HEAD_DIM:2 * HEAD_DIM, :] = ones

    def put(slot, vref):
        vt = vref[0].astype(F32).T
        for h in range(A_HEADS):
            vt_ref[slot, h, 0:HEAD_DIM, :] = vt[h * HEAD_DIM:(h + 1) * HEAD_DIM].astype(BF16)

    slots = [(kt + 1 + j) % BAND_TILES for j in range(BAND_TILES)]

    @pl.when(i == 0)
    def _():
        for j in range(BAND_TILES - 1):
            put(slots[j], vrefs[j])

    put(slots[BAND_TILES - 1], vrefs[BAND_TILES - 1])

    base = (kt - (BAND_TILES - 1)) * TILE

    n_slab = BAND_SLAB
    slab_w = n_slab * HEAD_DIM
    lane_head = lax.broadcasted_iota(I32, (TILE, slab_w), 1) // HEAD_DIM

    def attend(masked):
        outs, maxes = [], []
        for g in range(A_HEADS // n_slab):
            gs = slice(g * slab_w, (g + 1) * slab_w)
            q_slab = q_ref[0, :, gs].astype(F32)
            q_bd = jnp.concatenate([jnp.where(lane_head == hh, q_slab, 0.0) for hh in range(n_slab)],
                                   axis=0).astype(BF16)
            pm = None
            for j in range(BAND_TILES):
                st = lax.dot_general(krefs[j][0, :, gs], q_bd, _NT, preferred_element_type=F32)
                parts = [st[:, hh * TILE:(hh + 1) * TILE] + bias_ref[g * n_slab + hh, j * TILE:(j + 1) * TILE, :]
                         for hh in range(n_slab)]
                if masked:
                    kpos = base + j * TILE + lax.broadcasted_iota(I32, (TILE, TILE), 0)
                    ok = (kpos >= 0) & (kpos < valid_len)
                    parts = [jnp.where(ok, x, NEG) for x in parts]
                sj = jnp.concatenate(parts, axis=1)
                s_ref[g * BAND_TILES + j] = sj
                m8 = sj.reshape(TILE // 8, 8, n_slab * TILE).max(axis=0)
                pm = m8 if pm is None else jnp.maximum(pm, m8)
            maxes.append(pm.max(axis=0, keepdims=True))
        for g, m in enumerate(maxes):
            p = [jnp.exp(s_ref[g * BAND_TILES + j] - m).astype(BF16) for j in range(BAND_TILES)]
            for hh in range(n_slab):
                acc = None
                for j in range(BAND_TILES):
                    d = jnp.dot(vt_ref[slots[j], g * n_slab + hh], p[j][:, hh * TILE:(hh + 1) * TILE],
                                preferred_element_type=F32)
                    acc = d if acc is None else acc + d
                blk = acc.T
                outs.append(blk[:, 0:HEAD_DIM] / blk[:, HEAD_DIM:HEAD_DIM + 1])
        o_ref[0] = jnp.concatenate(outs, axis=-1).astype(BF16)

    needs_mask = (base < 0) | (base + BAND_COLS > valid_len)

    @pl.when(needs_mask)
    def _():
        attend(True)

    @pl.when(jnp.logical_not(needs_mask))
    def _():
        attend(False)


def _band_attn(q, k, v, g_row, off, valid_len):
    b, tq = q.shape[:2]
    nq = tq // TILE
    qspec = pl.BlockSpec((1, TILE, A_WIDTH), lambda bb, i: (bb, i, 0))
    kspec = lambda d: pl.BlockSpec((1, TILE, A_WIDTH), lambda bb, i: (bb, jnp.maximum(i + off - d, 0), 0))
    return pl.pallas_call(
        functools.partial(_band_kernel, off=off, valid_len=valid_len),
        grid=(b, nq),
        in_specs=[pl.BlockSpec((A_HEADS, BAND_ROLL), lambda bb, i: (0, 0)),
                  qspec, kspec(2), kspec(1), kspec(0), kspec(2), kspec(1), kspec(0)],
        out_specs=pl.BlockSpec((1, TILE, A_WIDTH), lambda bb, i: (bb, i, 0)),
        out_shape=jax.ShapeDtypeStruct((b, tq, A_WIDTH), BF16),
        scratch_shapes=[pltpu.VMEM((A_HEADS, BAND_COLS, TILE), F32),
                        pltpu.VMEM((BAND_TILES, A_HEADS, 2 * HEAD_DIM, TILE), BF16),
                        pltpu.VMEM((A_HEADS // BAND_SLAB * BAND_TILES, TILE, BAND_SLAB * TILE), F32)],
        compiler_params=_params("arbitrary", "arbitrary"),
        name="band_attn",
    )(g_row, q, k, k, k, v, v, v)


DSA_ROLL = 3 * TILE
FAR_STEP = 4


def _t5_bucket(rel):
    half = N_BUCKETS // 2
    max_exact = half // 2
    n = jnp.abs(rel)
    log_ratio = jnp.log(jnp.maximum(n, 1).astype(jnp.float32) / max_exact) / math.log(T5_MAX_DIST / max_exact)
    large = jnp.minimum(max_exact + (log_ratio * (half - max_exact)).astype(jnp.int32), half - 1)
    return jnp.where(rel < 0, half, 0) + jnp.where(n < max_exact, n, large)


def _dsa_bias_row(t5_table):
    idx = jnp.arange(DSA_ROLL)
    d = jnp.where(idx < TILE, idx, idx - DSA_ROLL)
    far = t5_table[_t5_bucket(jnp.full((1,), 2 * TILE + 1, I32))]
    return (t5_table[_t5_bucket(TILE + d)] - far).T.astype(F32)


def _dsa_kernel(g_ref, qi_ref, wi_ref, qb_ref, ki_ref, kb_ref, vb_ref, o_ref,
                sc_ref, scb_ref, keep_ref, vt_ref, qis_ref, qbs_ref, m_ref, acc_ref, bias_ref, s_ref,
                *, off, valid_len, q_valid, n_kt, tq, n_merge):
    i = pl.program_id(1)
    qt = i + off
    q0 = qt * TILE
    nk = qt + 1
    key_rows = lambda j: pl.ds(pl.multiple_of(j * TILE, TILE), TILE)

    @pl.when((pl.program_id(0) == 0) & (i == 0))
    def _():
        for h in range(B_HEADS):
            tile = _toeplitz(g_ref[h:h + 1, :], 2 * TILE, DSA_ROLL)[:, :tq]
            if n_merge:
                ts = tq // n_merge
                stream = lax.broadcasted_iota(I32, (1, tq), 1) // ts
                first = tile
                for b in range(1, n_merge):
                    tile = jnp.where(stream == b, pltpu.roll(first, b * ts, 1), tile)
            bias_ref[h] = tile

    @pl.when(i == 0)
    def _():
        def body(j, c):
            for b in range(max(n_merge, 1)):
                vt = vb_ref[b, key_rows(j), :].astype(F32).T
                for g in range(B_KV_HEADS):
                    vt_ref[g, j, 0:HEAD_DIM, b * TILE:(b + 1) * TILE] = vt[g * HEAD_DIM:(g + 1) * HEAD_DIM].astype(BF16)
            for g in range(B_KV_HEADS):
                vt_ref[g, j, HEAD_DIM:2 * HEAD_DIM, :] = jnp.ones((HEAD_DIM, vt_ref.shape[-1]), BF16)
            return c

        lax.fori_loop(0, n_kt, body, 0)

    colq = lax.broadcasted_iota(I32, (1, tq), 1)
    rowk = lax.broadcasted_iota(I32, (TILE, 1), 0)

    if n_merge:
        width = n_merge * HEAD_DIM
        own = (lax.broadcasted_iota(I32, (tq, width), 0) // (tq // n_merge)
               == lax.broadcasted_iota(I32, (tq, width), 1) // HEAD_DIM)

        def block_diag(x):
            return jnp.where(own, jnp.concatenate([x.astype(F32)] * n_merge, axis=1), 0.0).astype(BF16)

        qi, qb = qi_ref[...], qb_ref[...]
        for h in range(B_HEADS):
            qis_ref[h] = block_diag(qi[:, h * IDX_DIM:(h + 1) * IDX_DIM])
            qbs_ref[h] = block_diag(qb[:, h * HEAD_DIM:(h + 1) * HEAD_DIM])
        side_by_side = lambda ref, j, cols: jnp.concatenate(
            [ref[b, key_rows(j), cols] for b in range(n_merge)], axis=1)
        wi_t = wi_ref[...].T
        lim = jnp.full((1, tq), valid_len, I32)
    else:
        qi = qi_ref[0]
        for h in range(IDX_HEADS):
            qis_ref[h * tq:(h + 1) * tq, :] = qi[:, h * IDX_DIM:(h + 1) * IDX_DIM]
        qb = qb_ref[0]
        for g in range(B_KV_HEADS):
            for hh in range(B_GROUP):
                h = g * B_GROUP + hh
                qbs_ref[g, hh * tq:(hh + 1) * tq, :] = qb[:, h * HEAD_DIM:(h + 1) * HEAD_DIM]
        wi_t = wi_ref[0].T
        lim = jnp.minimum(q0 + (colq // CHUNK + 1) * CHUNK, valid_len)

    def score_tile(j, masked):
        if n_merge:
            kt = side_by_side(ki_ref, j, slice(None))
            lg = jnp.concatenate([lax.dot_general(kt, qis_ref[h], _NT, preferred_element_type=F32)
                                  for h in range(IDX_HEADS)], axis=1)
        else:
            kt = ki_ref[0, key_rows(j), :]
            lg = lax.dot_general(kt, qis_ref[...], _NT, preferred_element_type=F32)
        sc = wi_t[0:1, :] * jnp.maximum(lg[:, 0:tq], 0.0)
        for h in range(1, IDX_HEADS):
            sc = sc + wi_t[h:h + 1, :] * jnp.maximum(lg[:, h * tq:(h + 1) * tq], 0.0)
        if masked:
            sc = jnp.where(j * TILE + rowk < lim, sc, -jnp.inf)
        sc_ref[j] = sc
        scb_ref[j] = sc.astype(BF16)
        rows8 = lambda hit: jnp.where(hit, 1, 0).reshape(TILE // 8, 8, tq).sum(axis=0)
        return rows8(sc > 0.0), rows8(sc >= 0.0)

    def score_pair(jj, c, masked):
        pos_a, nn_a = score_tile(2 * jj, masked)
        pos_b, nn_b = score_tile(2 * jj + 1, masked)
        return c[0] + pos_a + pos_b, c[1] + nn_a + nn_b

    def score_quad(jj, c):
        for t in range(4):
            pos, nn = score_tile(4 * jj + t, False)
            c = (c[0] + pos, c[1] + nn)
        return c

    n_open_quads = (nk - 1) // 4
    counts = lax.fori_loop(0, n_open_quads, score_quad, (jnp.zeros((8, tq), I32), jnp.zeros((8, tq), I32)))
    n_pos, n_nonneg = lax.fori_loop(2 * n_open_quads, (nk + 1) // 2, functools.partial(score_pair, masked=True),
                                    counts)

    def f32_of_key(k):
        return pltpu.bitcast(jnp.where(k < 0, k ^ 0x7FFFFFFF, k), F32)

    def bf16_of_key(k):
        bits = jnp.where(k < 0, k ^ 0x7FFF, k) & 0xFFFF
        return pltpu.bitcast(lax.shift_left(bits, 16), F32).astype(BF16)

    def count_bf16(cand):
        def body(jj, acc):
            for j in (2 * jj, 2 * jj + 1):
                ge = jnp.where(scb_ref[j] >= cand, jnp.int16(1), jnp.int16(0)).reshape(TILE // 16, 16, tq)
                part = ge[0]
                for r in range(1, TILE // 16):
                    part = part + ge[r]
                acc = acc + part
            return acc

        acc = lax.fori_loop(0, (nk + 1) // 2, body, jnp.zeros((16, tq), I16))
        return acc.astype(I32).sum(axis=0, keepdims=True)

    def count_f32(cand, strict=False):
        def body(j, acc):
            blk = sc_ref[j]
            hit = (blk > cand) if strict else (blk >= cand)
            return acc + jnp.where(hit, 1, 0).reshape(TILE // 8, 8, tq).sum(axis=0)

        return lax.fori_loop(0, nk, body, jnp.zeros((8, tq), I32)).sum(axis=0, keepdims=True)

    c_pos = n_pos.sum(axis=0, keepdims=True)
    zero_tie = (c_pos < TOPK_MAX) & (n_nonneg.sum(axis=0, keepdims=True) >= TOPK_MAX)
    skip1 = (colq >= q_valid) | zero_tie

    def level1(it, t):
        cand = t + lax.shift_left(jnp.int32(1), 15 - it)
        return jnp.where(count_bf16(bf16_of_key(cand)) >= TOPK_MAX, cand, t)

    t1 = lax.fori_loop(0, 16, level1, jnp.full((1, tq), MIN16, I32))
    settled1 = skip1 | (t1 <= HI_NEG_INF)

    def level2(st):
        lo, hi, thr_key, done = st
        live = (done == 0) & (hi - lo > 1)
        mid = lo + lax.shift_right_arithmetic(hi - lo, 1)
        c = count_f32(f32_of_key(mid))
        hit = live & (c == TOPK_MAX)
        return (jnp.where(live & (c >= TOPK_MAX), mid, lo), jnp.where(live & (c < TOPK_MAX), mid, hi),
                jnp.where(hit, mid, thr_key), jnp.where(hit, 1, done))

    def n_live(st):
        lo, hi, _, done = st
        return jnp.sum(jnp.where((done == 0) & (hi - lo > 1), 1, 0))

    def level2_pair(carry):
        st = level2(level2(carry[0]))
        return st, n_live(st)

    def key32_of_key16(k):
        return lax.shift_left(k, 16) | jnp.where(k < 0, 0xFFFF, 0)

    key_t1 = key32_of_key16(t1)
    st0 = (key_t1 - 0x8000, key32_of_key16(t1 + 1), key_t1, jnp.where(settled1, 1, 0))
    (lo, _, thr_key, done2), _ = lax.while_loop(lambda carry: carry[1] > 0, level2_pair, (st0, n_live(st0)))
    open2 = done2 == 0
    thr = jnp.where(open2, f32_of_key(lo), f32_of_key(thr_key))
    thr = jnp.where(zero_tie, 0.0, thr)
    thr = jnp.where(settled1 & jnp.logical_not(zero_tie), -jnp.inf, thr)
    thr = jnp.maximum(thr, float(jnp.finfo(F32).min))

    keep_ref[...] = jnp.where(zero_tie, TOPK_MAX - c_pos, 2 ** 30).astype(F32)

    @pl.when(jnp.sum(jnp.where(open2, 1, 0)) > 0)
    def _():
        above = count_f32(thr, strict=True)
        keep_ref[...] = jnp.where(open2, (TOPK_MAX - above).astype(F32), keep_ref[...])

    @pl.when(jnp.sum(jnp.where(open2 | zero_tie, 1, 0)) > 0)
    def _():
        keep = keep_ref[...]
        lower = (lax.broadcasted_iota(I32, (TILE, TILE), 0) > lax.broadcasted_iota(I32, (TILE, TILE), 1))
        lower = jnp.where(lower, 1.0, 0.0).astype(BF16)

        def body(jj, run):
            for j in (2 * jj, 2 * jj + 1):
                blk = sc_ref[j]
                eq = blk == thr
                eq_f = jnp.where(eq, 1.0, 0.0)
                before = jnp.dot(lower, eq_f.astype(BF16), preferred_element_type=F32)
                sc_ref[j] = jnp.where(eq & (run + before >= keep), -jnp.inf, blk)
                run = run + eq_f.reshape(TILE // 8, 8, tq).sum(axis=0).sum(axis=0, keepdims=True)
            return run

        lax.fori_loop(0, (nk + 1) // 2, body, jnp.zeros((1, tq), F32))

    for g in range(B_KV_HEADS):
        m_ref[g] = jnp.full((1, B_GROUP * tq), NEG, F32)
        acc_ref[g] = jnp.zeros((2 * HEAD_DIM, B_GROUP * tq), F32)

    def attend(tiles):
        sels = [sc_ref[j] >= thr for j, _ in tiles]
        part_max = []
        for g in range(B_KV_HEADS):
            pm = None
            for t, ((j, near), sel) in enumerate(zip(tiles, sels)):
                if n_merge:
                    kt = side_by_side(kb_ref, j, slice(g * HEAD_DIM, (g + 1) * HEAD_DIM))
                    st = jnp.concatenate([lax.dot_general(kt, qbs_ref[g * B_GROUP + hh], _NT,
                                                          preferred_element_type=F32)
                                          for hh in range(B_GROUP)], axis=1)
                else:
                    kt = kb_ref[0, key_rows(j), g * HEAD_DIM:(g + 1) * HEAD_DIM]
                    st = lax.dot_general(kt, qbs_ref[g], _NT, preferred_element_type=F32)
                parts = []
                for hh in range(B_GROUP):
                    s_h = st[:, hh * tq:(hh + 1) * tq]
                    if near is not None:
                        s_h = s_h + bias_ref[g * B_GROUP + hh, near * TILE:(near + 1) * TILE, :]
                    parts.append(jnp.where(sel, s_h, -jnp.inf))
                s = jnp.concatenate(parts, axis=1)
                s_ref[g * len(tiles) + t] = s
                m8 = s.reshape(TILE // 8, 8, B_GROUP * tq).max(axis=0)
                pm = m8 if pm is None else jnp.maximum(pm, m8)
            part_max.append(pm)
        for g in range(B_KV_HEADS):
            m_old = m_ref[g]
            m_new = jnp.maximum(m_old, part_max[g].max(axis=0, keepdims=True))
            pv = None
            for t, (j, _) in enumerate(tiles):
                p = jnp.exp(s_ref[g * len(tiles) + t] - m_new)
                if n_merge:
                    lane_stream = colq // (tq // n_merge)
                    d = jnp.concatenate(
                        [jnp.dot(vt_ref[g, j],
                                 jnp.concatenate([jnp.where(lane_stream == b, p[:, hh * tq:(hh + 1) * tq], 0.0)
                                                  for b in range(n_merge)], axis=0).astype(BF16),
                                 preferred_element_type=F32) for hh in range(B_GROUP)], axis=1)
                else:
                    d = jnp.dot(vt_ref[g, j], p.astype(BF16), preferred_element_type=F32)
                pv = d if pv is None else pv + d
            acc_ref[g] = jnp.exp(m_old - m_new) * acc_ref[g] + pv
            m_ref[g] = m_new

    n_far = jnp.maximum(nk - 2, 0)

    def far_step(jj, c):
        attend([(FAR_STEP * jj + t, None) for t in range(FAR_STEP)])
        return c

    lax.fori_loop(0, n_far // FAR_STEP, far_step, 0)
    rem = n_far % FAR_STEP

    for r in range(FAR_STEP):
        @pl.when((rem == r) & (nk >= 2))
        def _(r=r):
            attend([(n_far - r + t, None) for t in range(r)] + [(nk - 2, 0), (nk - 1, 1)])

    @pl.when(nk < 2)
    def _():
        attend([(nk - 1, 1)])

    outs = []
    for g in range(B_KV_HEADS):
        for hh in range(B_GROUP):
            blk = acc_ref[g, :, hh * tq:(hh + 1) * tq].T
            outs.append(blk[:, 0:HEAD_DIM] / blk[:, HEAD_DIM:HEAD_DIM + 1])
    o_ref[0] = jnp.concatenate(outs, axis=-1).astype(BF16)


def _dsa_attn(qi, wi, qb, ki, kb, vb, g_row, off, valid_len, q_valid, tq):
    b, t_q = qi.shape[:2]
    tk = ki.shape[1]
    nq, n_kt = t_q // tq, tk // TILE
    assert tq == TILE or nq == 1
    assert n_kt % 2 == 0 and tk == n_kt * TILE
    qspec = lambda n: pl.BlockSpec((1, tq, n), lambda bb, i: (bb, i, 0))
    kspec = lambda n: pl.BlockSpec((1, tk, n), lambda bb, i: (bb, 0, 0))
    in_specs = [pl.BlockSpec((B_HEADS, DSA_ROLL), lambda bb, i: (0, 0)),
                qspec(512), qspec(128), qspec(512), kspec(IDX_DIM), kspec(128), kspec(128)]
    return _dsa_call((g_row, qi, wi, qb, ki, kb, vb), in_specs, (b, nq), t_q, tq, n_kt,
                     dict(off=off, valid_len=valid_len, q_valid=q_valid, n_merge=0))


def _dsa_attn_merged(qi, wi, qb, ki, kb, vb, g_row, off, valid_len):
    s, ts = qi.shape[:2]
    tk = ki.shape[1]
    tq, n_kt = s * ts, tk // TILE
    assert tq % LANES == 0 and ts <= CHUNK and n_kt % 2 == 0 and tk == n_kt * TILE
    flat = lambda a: a.reshape(tq, a.shape[-1])
    operands = (g_row, flat(qi), flat(wi), flat(qb), ki, kb, vb)
    whole = lambda a: pl.BlockSpec(a.shape, lambda bb, i: (0,) * a.ndim, pipeline_mode=pl.Buffered(1))
    out = _dsa_call(operands, [whole(a) for a in operands], (1, 1), tq, tq, n_kt,
                    dict(off=off, valid_len=valid_len, q_valid=tq, n_merge=s))
    return out.reshape(s, ts, B_WIDTH)


def _dsa_call(operands, in_specs, grid, t_q, tq, n_kt, static):
    n_merge = static["n_merge"]
    assert static["valid_len"] >= (static["off"] + grid[1] - 1) * TILE
    if n_merge:
        q_scratch = [pltpu.VMEM((IDX_HEADS, tq, n_merge * IDX_DIM), BF16),
                     pltpu.VMEM((B_HEADS, tq, n_merge * HEAD_DIM), BF16)]
    else:
        q_scratch = [pltpu.VMEM((IDX_HEADS * tq, IDX_DIM), BF16),
                     pltpu.VMEM((B_KV_HEADS, B_GROUP * tq, HEAD_DIM), BF16)]
    return pl.pallas_call(
        functools.partial(_dsa_kernel, n_kt=n_kt, tq=tq, **static),
        grid=grid,
        in_specs=in_specs,
        out_specs=pl.BlockSpec((1, tq, B_WIDTH), lambda bb, i: (bb, i, 0)),
        out_shape=jax.ShapeDtypeStruct((grid[0], t_q, B_WIDTH), BF16),
        scratch_shapes=[pltpu.VMEM((n_kt, TILE, tq), F32),
                        pltpu.VMEM((n_kt, TILE, tq), BF16),
                        pltpu.VMEM((1, tq), F32),
                        pltpu.VMEM((B_KV_HEADS, n_kt, 2 * HEAD_DIM, max(n_merge, 1) * TILE), BF16),
                        *q_scratch,
                        pltpu.VMEM((B_KV_HEADS, 1, B_GROUP * tq), F32),
                        pltpu.VMEM((B_KV_HEADS, 2 * HEAD_DIM, B_GROUP * tq), F32),
                        pltpu.VMEM((B_HEADS, 2 * TILE, tq), F32),
                        pltpu.VMEM((B_KV_HEADS * (FAR_STEP + 1), TILE, B_GROUP * tq), F32)],
        compiler_params=_params("arbitrary", "arbitrary"),
        name="dsa_attn",
    )(*operands)


def _mem_kv_kernel(m_ref, wk_ref, wv_ref, k_o, v_o, kb_o, vb_o):
    mb = m_ref[...].astype(BF16)
    k = jnp.dot(mb, wk_ref[...].astype(BF16), preferred_element_type=F32)
    v = jnp.dot(mb, wv_ref[...].astype(BF16), preferred_element_type=F32)
    k_o[...] = k
    v_o[...] = v
    kb_o[...] = k.astype(BF16)
    vb_o[...] = v.astype(BF16)


def _mem_kv(mem2d, wk, wv):
    r = mem2d.shape[0]
    tm = MEM_LEN
    row = lambda n: pl.BlockSpec((tm, n), lambda i: (i, 0))
    const = lambda s: pl.BlockSpec(s, lambda i: (0, 0))
    sds = jax.ShapeDtypeStruct
    return pl.pallas_call(
        _mem_kv_kernel,
        grid=(r // tm,),
        in_specs=[row(D_MODEL), const((D_MODEL, MEM_WIDTH)), const((D_MODEL, MEM_WIDTH))],
        out_specs=[row(MEM_WIDTH)] * 4,
        out_shape=[sds((r, MEM_WIDTH), F32), sds((r, MEM_WIDTH), F32),
                   sds((r, MEM_WIDTH), BF16), sds((r, MEM_WIDTH), BF16)],
        compiler_params=_params("arbitrary"),
        name="mem_kv",
    )(mem2d, wk, wv)


FF_CHUNK = 256


STAGE_ELEMS = 256 * 1024
STAGE_SLOTS = 4
N_MATS = 5


def _stage_rows(cols):
    return 1 << ((STAGE_ELEMS // cols).bit_length() - 1)


def _load_cast(src, dst, stage, sem):
    slots, rows = stage.shape[:2]
    n = src.shape[0] // rows
    copy = lambda k: pltpu.make_async_copy(src.at[pl.ds(k * rows, rows)], stage.at[k % slots], sem.at[k % slots])
    for k in range(min(slots - 1, n)):
        copy(k).start()
    for k in range(n):
        if k + slots - 1 < n:
            copy(k + slots - 1).start()
        copy(k).wait()
        dst[k * rows:(k + 1) * rows, :] = stage[k % slots].astype(BF16)


def _tail_kernel(x_ref, oa_ref, ob_ref, mk_ref, mv_ref, hist_ref,
                 wo_ref, g1_ref, b1_ref, wq_ref, wmo_ref, g2_ref, b2_ref,
                 wu_ref, wc_ref, bc_ref, wd_ref, g3_ref, b3_ref,
                 o_ref, tail_ref, *rest, tiles_per_batch, seg, cast_weights):
    i = pl.program_id(0)
    tm = x_ref.shape[0]
    nseg = tm // seg
    if cast_weights:
        mats_out, (carry_ref, act_ref), rest = rest[:N_MATS], rest[N_MATS:N_MATS + 2], rest[N_MATS + 2:]
        mats, stages, (sem_in, sem_out) = rest[:N_MATS], rest[N_MATS:-2], rest[-2:]
        stage_of = {s.shape[2]: s for s in stages}

        @pl.when(i == 0)
        def _():
            for j, src in enumerate((wo_ref, wq_ref, wmo_ref, wu_ref, wd_ref)):
                _load_cast(src, mats[j], stage_of[src.shape[1]], sem_in)
                pltpu.make_async_copy(mats[j], mats_out[j], sem_out.at[j]).start()

        wo_ref, wq_ref, wmo_ref, wu_ref, wd_ref = mats
    else:
        carry_ref, act_ref = rest

    mix = jnp.concatenate([oa_ref[...], ob_ref[...]], axis=-1)
    h = _layer_norm(ALPHA * x_ref[...] + jnp.dot(mix, wo_ref[...], preferred_element_type=F32),
                    g1_ref[...], b1_ref[...])

    q = jnp.dot(h.astype(BF16), wq_ref[...], preferred_element_type=F32).astype(BF16)
    segs = []
    for s in range(nseg):
        qs = q[s * seg:(s + 1) * seg]
        heads = []
        for hd in range(MEM_HEADS):
            sl = slice(hd * MEM_HEAD_DIM, (hd + 1) * MEM_HEAD_DIM)
            sc = lax.dot_general(qs[:, sl], mk_ref[s, :, sl], _NT, preferred_element_type=F32) * MEM_HEAD_DIM ** -0.5
            p = jnp.exp(sc - sc.max(-1, keepdims=True))
            l = p.sum(-1, keepdims=True)
            heads.append(jnp.dot(p.astype(BF16), mv_ref[s, :, sl], preferred_element_type=F32) / l)
        segs.append(jnp.concatenate(heads, axis=-1))
    att = jnp.concatenate(segs, axis=0).astype(BF16)
    h = _layer_norm(ALPHA * h + jnp.dot(att, wmo_ref[...], preferred_element_type=F32), g2_ref[...], b2_ref[...])

    hb = h.astype(BF16)
    row = lax.broadcasted_iota(I32, (tm, 1), 0)
    first = (i % tiles_per_batch) == 0
    for c in range(D_FF // FF_CHUNK):
        cs = slice(c * FF_CHUNK, (c + 1) * FF_CHUNK)
        u = jnp.dot(hb, wu_ref[:, cs], preferred_element_type=F32)
        gt = jnp.dot(hb, wu_ref[:, D_FF + c * FF_CHUNK:D_FF + (c + 1) * FF_CHUNK], preferred_element_type=F32)
        p1 = pltpu.roll(gt, 1, 0)
        p2 = pltpu.roll(gt, 2, 0)
        for s in range(nseg):
            hist = hist_ref[s, :, cs]
            if tiles_per_batch > 1:
                hist = jnp.where(first, hist, carry_ref[:, cs])
            p1 = jnp.where(row == s * seg, hist[7:8, :], p1)
            p2 = jnp.where(row == s * seg, hist[6:7, :], p2)
            p2 = jnp.where(row == s * seg + 1, hist[7:8, :], p2)
        gc = bc_ref[:, cs] + ((wc_ref[0:1, cs] * p2 + wc_ref[1:2, cs] * p1) + wc_ref[2:3, cs] * gt)
        act_ref[:, cs] = (u * jax.nn.gelu(gc)).astype(BF16)
        for s in range(nseg):
            tail_ref[s, :, cs] = gt[(s + 1) * seg - 8:(s + 1) * seg, :]
        carry_ref[:, cs] = gt[tm - 8:tm, :]
    f = jnp.dot(act_ref[...], wd_ref[...], preferred_element_type=F32)
    o_ref[...] = _layer_norm(ALPHA * h + f, g3_ref[...], b3_ref[...])

    if cast_weights:
        @pl.when(i == 0)
        def _():
            for j in range(N_MATS):
                pltpu.make_async_copy(mats[j], mats_out[j], sem_out.at[j]).wait()


def _layer_tail(x2d, oa, ob, mk, mv, hist, weights, tm, tiles_per_batch, seg):
    r = x2d.shape[0]
    nseg = tm // seg
    n_stream = r // (tm * tiles_per_batch) * nseg
    mats = [w for w in weights if w.shape[0] > CONV_W]
    cast_weights = mats[0].dtype == F32
    assert len(mats) == N_MATS and all((w.dtype == F32) == cast_weights for w in mats)
    row = lambda n: pl.BlockSpec((tm, n), lambda i: (i, 0))
    per_stream = lambda a, b: pl.BlockSpec((nseg, a, b), lambda i: (i // tiles_per_batch, 0, 0))
    const = lambda a: pl.BlockSpec(a.shape, lambda i: (0,) * a.ndim, pipeline_mode=pl.Buffered(1))
    in_hbm = pl.BlockSpec(memory_space=pl.ANY)
    w_spec = lambda w: in_hbm if cast_weights and w.shape[0] > CONV_W else const(w)
    out_specs = [row(D_MODEL), per_stream(8, D_FF)]
    out_shape = [jax.ShapeDtypeStruct((r, D_MODEL), F32), jax.ShapeDtypeStruct((n_stream, 8, D_FF), F32)]
    scratch = [pltpu.VMEM((8, D_FF), F32), pltpu.VMEM((tm, D_FF), BF16)]
    if cast_weights:
        out_specs += [in_hbm] * N_MATS
        out_shape += [jax.ShapeDtypeStruct(w.shape, BF16) for w in mats]
        scratch += [pltpu.VMEM(w.shape, BF16) for w in mats]
        scratch += [pltpu.VMEM((STAGE_SLOTS, _stage_rows(c), c), F32) for c in sorted({w.shape[1] for w in mats})]
        scratch += [pltpu.SemaphoreType.DMA((STAGE_SLOTS,)), pltpu.SemaphoreType.DMA((N_MATS,))]
    return pl.pallas_call(
        functools.partial(_tail_kernel, tiles_per_batch=tiles_per_batch, seg=seg, cast_weights=cast_weights),
        grid=(r // tm,),
        in_specs=[row(D_MODEL), row(A_WIDTH), row(B_WIDTH), per_stream(MEM_LEN, MEM_WIDTH),
                  per_stream(MEM_LEN, MEM_WIDTH), per_stream(8, D_FF)] + [w_spec(w) for w in weights],
        out_specs=out_specs,
        out_shape=out_shape,
        scratch_shapes=scratch,
        compiler_params=_params("arbitrary"),
        name="layer_tail",
    )(x2d, oa, ob, mk, mv, hist, *weights)


def _pad_rows(a, n):
    return jnp.pad(a, ((0, 0), (0, n - a.shape[1])) + ((0, 0),) * (a.ndim - 2))


def _hist8(g_hist):
    return jnp.pad(g_hist, ((0, 0), (8 - g_hist.shape[1], 0), (0, 0)))


def kernel(x_prompt, x_sample, cache_a_k, cache_a_v, cache_b_k, cache_b_v, cache_b_kidx, cache_mem_k, cache_mem_v, state_ffn_conv, mem_prompt, w_in, a_rel_bias, t5_bias, w_o, ln1_g, ln1_b, w_mq, w_mk, w_mv, w_mo, ln2_g, ln2_b, w_up, w_conv, b_conv, w_down, ln3_g, ln3_b):
    bp, tp = x_prompt.shape[:2]
    bs, ts = x_sample.shape[:2]
    l = 0
    vec = lambda a: a[l].reshape(1, -1)
    w_in_p = _prep_w_in(w_in[l])
    w_mk_b, w_mv_b = w_mk[l], w_mv[l]
    band_row = _band_bias_row(a_rel_bias[l])
    dsa_row = _dsa_bias_row(t5_bias)
    tail_weights = lambda wo, wq, wmo, wu, wd: (wo, vec(ln1_g), vec(ln1_b), wq, wmo, vec(ln2_g), vec(ln2_b),
                                                wu, w_conv[l], vec(b_conv), wd, vec(ln3_g), vec(ln3_b))

    tm = 512
    a_keep = min(N_PREV_CHUNKS * CHUNK, tp)
    (qa, ka, va, qb, qi, kb, vb, ki, kb_b, vb_b, ki_b, wi, ka_tail, va_tail) = _in_proj(
        x_prompt.reshape(bp * tp, D_MODEL), w_in_p, tm, tp // tm)
    r3 = lambda a: a.reshape(bp, tp, a.shape[-1])
    oa = _band_attn(r3(qa), r3(ka), r3(va), band_row, 0, tp)
    ob = _dsa_attn(r3(qi), r3(wi), r3(qb), r3(ki_b), r3(kb_b), r3(vb_b), dsa_row, 0, tp, TILE, TILE)
    mk, mv, mk_b, mv_b = _mem_kv(mem_prompt.reshape(bp * MEM_LEN, D_MODEL), w_mk_b, w_mv_b)
    xp, p_tail, *mats_b = _layer_tail(x_prompt.reshape(bp * tp, D_MODEL), oa.reshape(bp * tp, A_WIDTH),
                                      ob.reshape(bp * tp, B_WIDTH), mk_b.reshape(bp, MEM_LEN, MEM_WIDTH),
                                      mv_b.reshape(bp, MEM_LEN, MEM_WIDTH), jnp.zeros((bp, 8, D_FF), F32),
                                      tail_weights(w_o[l], w_mq[l], w_mo[l], w_up[l], w_down[l]), tm, tp // tm, tm)
    prompt_state = (
        ka_tail.reshape(bp, tm, A_HEADS, HEAD_DIM)[:, tm - a_keep:][None],
        va_tail.reshape(bp, tm, A_HEADS, HEAD_DIM)[:, tm - a_keep:][None],
        kb.reshape(1, bp, tp, B_KV_HEADS, HEAD_DIM), vb.reshape(1, bp, tp, B_KV_HEADS, HEAD_DIM),
        ki.reshape(1, bp, tp, IDX_DIM),
        mk.reshape(1, bp, MEM_LEN, MEM_HEADS, MEM_HEAD_DIM), mv.reshape(1, bp, MEM_LEN, MEM_HEADS, MEM_HEAD_DIM),
        p_tail[:, 8 - (CONV_W - 1):][None])

    rs = bs * ts
    (qa, ka, va, qb, qi, kb, vb, ki, kb_b, vb_b, ki_b, wi, ka_new, va_new) = _in_proj(
        x_sample.reshape(rs, D_MODEL), w_in_p, rs, 1)
    s3 = lambda a: a.reshape(bs, ts, -1)
    qpad = lambda a, n: _pad_rows(s3(a), n)

    past_a = cache_a_k.shape[2]
    n_a = past_a + ts
    t_a = -(-n_a // TILE) * TILE
    seq_a = lambda cache, new: _pad_rows(
        jnp.concatenate([cache[l].reshape(bs, past_a, A_WIDTH), s3(new)], axis=1), t_a).astype(BF16)
    oa = _band_attn(qpad(qa, TILE), seq_a(cache_a_k, ka_new), seq_a(cache_a_v, va_new), band_row,
                    past_a // TILE, n_a)

    past_b = cache_b_k.shape[2]
    n_b = past_b + ts
    t_b = -(-n_b // (2 * TILE)) * 2 * TILE
    seq_b = lambda cache, new: _pad_rows(
        jnp.concatenate([cache[l].reshape(bs, past_b, -1), s3(new)], axis=1), t_b).astype(BF16)
    ob = _dsa_attn_merged(s3(qi), s3(wi), s3(qb), seq_b(cache_b_kidx, ki), seq_b(cache_b_k, kb),
                          seq_b(cache_b_v, vb), dsa_row, past_b // TILE, n_b)

    xs, s_tail = _layer_tail(x_sample.reshape(rs, D_MODEL), oa[:, :ts].reshape(rs, A_WIDTH),
                             ob[:, :ts].reshape(rs, B_WIDTH),
                             cache_mem_k[l].reshape(bs, MEM_LEN, MEM_WIDTH).astype(BF16),
                             cache_mem_v[l].reshape(bs, MEM_LEN, MEM_WIDTH).astype(BF16),
                             _hist8(state_ffn_conv[l]), tail_weights(*mats_b), rs, 1, ts)
    sample_state = (
        ka_new.reshape(1, bs, ts, A_HEADS, HEAD_DIM), va_new.reshape(1, bs, ts, A_HEADS, HEAD_DIM),
        kb.reshape(1, bs, ts, B_KV_HEADS, HEAD_DIM), vb.reshape(1, bs, ts, B_KV_HEADS, HEAD_DIM),
        ki.reshape(1, bs, ts, IDX_DIM), s_tail[:, 8 - (CONV_W - 1):][None])

    return (xp.reshape(bp, tp, D_MODEL), xs.reshape(bs, ts, D_MODEL)) + prompt_state + sample_state
```
